```python
import math
import jax, jax.numpy as jnp
from jax import lax
import numpy as np

D_MODEL = 1024
BATCH = 32
SEQ = 2048
DEPTH = 1

N_META = 16
CHUNK = 128
META_PAD = CHUNK - N_META
BRANCH_WIDTH = 2 * D_MODEL
N_BRANCH = 2
RET_QK_DIM = 256
RET_HEADS = D_MODEL // RET_QK_DIM
RET_V_DIM = BRANCH_WIDTH // RET_HEADS
RET_QK = RET_HEADS * RET_QK_DIM
RET_V = BRANCH_WIDTH
ROPE_BASE = 10000.0
SSM_INNER = BRANCH_WIDTH
SSM_HEAD_DIM = 64
SSM_HEADS = SSM_INNER // SSM_HEAD_DIM
SSM_GROUPS = 4
SSM_STATE = 128
CONV_WIDTH = 4
CONV_CH = SSM_INNER + 2 * SSM_GROUPS * SSM_STATE
DT_MIN = 0.001
DT_MAX = 0.1
OFF_K = RET_QK
OFF_V = OFF_K + RET_QK
OFF_GRET = OFF_V + RET_V
OFF_Z = OFF_GRET + RET_V
OFF_XBC = OFF_Z + SSM_INNER
OFF_DT = OFF_XBC + CONV_CH
OFF_GATE = OFF_DT + SSM_HEADS
PROJ_DIM = OFF_GATE + N_BRANCH * D_MODEL
SPLIT_POINTS = (RET_QK, OFF_V, OFF_GRET, OFF_Z, OFF_XBC, OFF_DT, OFF_GATE)
N_EXPERTS = 32
TOP_K = 4
EXPERT_FF = D_MODEL
SWIGLU_ALPHA = 1.702
SWIGLU_LIMIT = 7.0
MOE_BLOCK = 256
NORM_EPS = 1e-6

kernel_name = "hybrid_retention_ssd_moe_meta"


def rmsnorm(t, w):
    tf = t.astype(jnp.float32)
    tf = tf * lax.rsqrt(jnp.mean(tf * tf, axis=-1, keepdims=True) + NORM_EPS)
    return (tf * w.astype(jnp.float32)).astype(t.dtype)


def to_chunks(t):
    b, lp = t.shape[:2]
    return jnp.moveaxis(t.reshape(b, lp // CHUNK, CHUNK, *t.shape[2:]), 1, 0)


def from_chunks(t):
    n, b = t.shape[:2]
    return jnp.moveaxis(t, 0, 1).reshape(b, n * CHUNK, *t.shape[3:])


def pad_front(t):
    return jnp.pad(t, [(0, 0), (META_PAD, 0)] + [(0, 0)] * (t.ndim - 2))


def rotary(t, pos):
    half = t.shape[-1] // 2
    inv_freq = ROPE_BASE ** (-jnp.arange(half, dtype=jnp.float32) / half)
    ang = pos[:, None] * inv_freq[None, :]
    cos = jnp.cos(ang)[None, :, None, :]
    sin = jnp.sin(ang)[None, :, None, :]
    t = t.astype(jnp.float32)
    t1, t2 = t[..., :half], t[..., half:]
    return jnp.concatenate([t1 * cos - t2 * sin, t2 * cos + t1 * sin], axis=-1)


def retention_scan(q, k, v):
    bsz, _, n_heads, dk = q.shape
    dv = v.shape[-1]
    log_gamma = jnp.log(1.0 - 2.0 ** (-5.0 - jnp.arange(n_heads, dtype=jnp.float32)))
    idx = jnp.arange(CHUNK, dtype=jnp.float32)
    dist = idx[:, None] - idx[None, :]
    causal = dist >= 0
    intra = jnp.where(causal[None], jnp.exp(log_gamma[:, None, None] * jnp.where(causal, dist, 0.0)[None]), 0.0)
    q_dec = jnp.exp(log_gamma[None, :] * (idx[:, None] + 1.0))
    k_dec = jnp.exp(log_gamma[None, :] * (CHUNK - 1.0 - idx[:, None]))
    chunk_dec = jnp.exp(log_gamma * CHUNK)

    def step(state, inp):
        qc, kc, vc = inp
        s = jnp.einsum('bihd,bjhd->bhij', qc, kc) * intra[None]
        y = jnp.einsum('bhij,bjhv->bihv', s, vc)
        y = y + jnp.einsum('bihd,bhdv->bihv', qc, state) * q_dec[None, :, :, None]
        state = state * chunk_dec[None, :, None, None] + jnp.einsum('bjhd,bjhv->bhdv', kc * k_dec[None, :, :, None], vc)
        return state, y

    state0 = jnp.zeros((bsz, n_heads, dk, dv), jnp.float32)
    _, ys = lax.scan(step, state0, (to_chunks(q), to_chunks(k), to_chunks(v)))
    return from_chunks(ys)


def ssd_scan(x, a, bm, cm):
    bsz, lp, n_heads, p = x.shape
    hpg = n_heads // SSM_GROUPS
    x = x.reshape(bsz, lp, SSM_GROUPS, hpg, p)
    a = a.reshape(bsz, lp, SSM_GROUPS, hpg)
    causal = jnp.tril(jnp.ones((CHUNK, CHUNK), dtype=bool))

    def step(state, inp):
        xc, ac, bc, cc = inp
        acs = jnp.cumsum(ac, axis=1)
        acs_t = jnp.moveaxis(acs, 1, -1)
        seg = acs_t[..., :, None] - acs_t[..., None, :]
        decay = jnp.exp(jnp.where(causal, seg, -jnp.inf))
        cb = jnp.einsum('bigs,bjgs->bgij', cc, bc)
        y = jnp.einsum('bghij,bjghp->bighp', cb[:, :, None] * decay, xc)
        y = y + jnp.einsum('bigs,bghps->bighp', cc, state) * jnp.exp(acs)[..., None]
        last = acs[:, -1]
        w = jnp.exp(last[:, None] - acs)
        state = state * jnp.exp(last)[..., None, None] + jnp.einsum('bjgs,bjghp->bghps', bc, xc * w[..., None])
        return state, y

    state0 = jnp.zeros((bsz, SSM_GROUPS, hpg, p, SSM_STATE), jnp.float32)
    _, ys = lax.scan(step, state0, (to_chunks(x), to_chunks(a), to_chunks(bm), to_chunks(cm)))
    return from_chunks(ys).reshape(bsz, lp, n_heads, p)


def causal_dwconv(t, w, b):
    ch = t.shape[-1]
    out = lax.conv_general_dilated(t, w[:, None, :].astype(t.dtype), window_strides=(1,),
                                   padding=[(CONV_WIDTH - 1, 0)],
                                   dimension_numbers=('NWC', 'WIO', 'NWC'),
                                   feature_group_count=ch)
    return out + b.astype(t.dtype)


def retention_branch(q, k, v, g, pos):
    bsz, length, _ = q.shape
    q = rotary(q.reshape(bsz, length, RET_HEADS, RET_QK_DIM), pos)
    k = rotary(k.reshape(bsz, length, RET_HEADS, RET_QK_DIM), pos) * (RET_QK_DIM ** -0.5)
    v = v.reshape(bsz, length, RET_HEADS, RET_V_DIM).astype(jnp.float32)
    o = retention_scan(pad_front(q), pad_front(k), pad_front(v))[:, META_PAD:]
    o = o * lax.rsqrt(jnp.mean(o * o, axis=-1, keepdims=True) + NORM_EPS)
    return (jax.nn.silu(g.astype(jnp.float32)) * o.reshape(bsz, length, RET_V)).astype(g.dtype)


def ssd_branch(z, xbc, dt_raw, conv_w, conv_b, dt_bias, a_log, d_skip, norm_w):
    bsz, length, _ = z.shape
    xbc = jax.nn.silu(causal_dwconv(xbc, conv_w, conv_b))
    xs, bm, cm = jnp.split(xbc, [SSM_INNER, SSM_INNER + SSM_GROUPS * SSM_STATE], axis=-1)
    xs = xs.reshape(bsz, length, SSM_HEADS, SSM_HEAD_DIM).astype(jnp.float32)
    bm = bm.reshape(bsz, length, SSM_GROUPS, SSM_STATE).astype(jnp.float32)
    cm = cm.reshape(bsz, length, SSM_GROUPS, SSM_STATE).astype(jnp.float32)
    dt = jax.nn.softplus(dt_raw.astype(jnp.float32) + dt_bias.astype(jnp.float32))
    a = dt * (-jnp.exp(a_log.astype(jnp.float32)))
    y = ssd_scan(pad_front(xs * dt[..., None]), pad_front(a), pad_front(bm), pad_front(cm))[:, META_PAD:]
    y = y + d_skip.astype(jnp.float32)[:, None] * xs
    y = y.reshape(bsz, length, SSM_INNER) * jax.nn.silu(z.astype(jnp.float32))
    y = y.reshape(bsz, length, SSM_GROUPS, SSM_INNER // SSM_GROUPS)
    y = y * lax.rsqrt(jnp.mean(y * y, axis=-1, keepdims=True) + NORM_EPS)
    return (y.reshape(bsz, length, SSM_INNER) * norm_w.astype(jnp.float32)).astype(z.dtype)


def moe_ffn(u, router_w, router_b, w_gate_up, b_gate_up, w_down, b_down):
    n_tok, d = u.shape
    logits = u.astype(jnp.float32) @ router_w.astype(jnp.float32) + router_b.astype(jnp.float32)
    top_logit, top_e = lax.top_k(logits, TOP_K)
    top_w = jax.nn.softmax(top_logit, axis=-1)
    n_pairs = n_tok * TOP_K
    flat_e = top_e.reshape(-1)
    flat_tok = jnp.repeat(jnp.arange(n_tok, dtype=jnp.int32), TOP_K)
    flat_w = top_w.reshape(-1)
    order = jnp.argsort(flat_e)
    e_sorted = flat_e[order]
    counts = jnp.bincount(flat_e, length=N_EXPERTS)
    padded = (counts + MOE_BLOCK - 1) // MOE_BLOCK * MOE_BLOCK
    start = jnp.cumsum(counts) - counts
    pend = jnp.cumsum(padded)
    pstart = pend - padded
    rank = jnp.arange(n_pairs, dtype=jnp.int32) - start[e_sorted]
    dest = pstart[e_sorted] + rank
    n_blocks = -(-(n_pairs + N_EXPERTS * (MOE_BLOCK - 1)) // MOE_BLOCK)
    n_rows = n_blocks * MOE_BLOCK
    row_tok = jnp.full((n_rows,), n_tok, jnp.int32).at[dest].set(flat_tok[order])
    row_w = jnp.zeros((n_rows,), jnp.float32).at[dest].set(flat_w[order])
    block_e = jnp.minimum(jnp.searchsorted(pend, jnp.arange(n_blocks, dtype=jnp.int32) * MOE_BLOCK, side='right'),
                          N_EXPERTS - 1)
    u_pad = jnp.concatenate([u, jnp.zeros((1, d), u.dtype)], axis=0)

    def step(acc, inp):
        tok, wrow, e = inp
        xb = u_pad[tok]
        gu = xb @ w_gate_up[e] + b_gate_up[e]
        gate = jnp.minimum(gu[:, 0::2], SWIGLU_LIMIT)
        up = jnp.clip(gu[:, 1::2], -SWIGLU_LIMIT, SWIGLU_LIMIT)
        act = (up + 1.0) * (gate * jax.nn.sigmoid(SWIGLU_ALPHA * gate))
        y = act @ w_down[e] + b_down[e]
        return acc.at[tok].add((y * wrow[:, None]).astype(acc.dtype)), None

    acc0 = jnp.zeros((n_tok + 1, d), u.dtype)
    acc, _ = lax.scan(step, acc0, (row_tok.reshape(n_blocks, MOE_BLOCK), row_w.reshape(n_blocks, MOE_BLOCK), block_e))
    return acc[:n_tok]


def setup_inputs(seed: int = 0) -> dict:
    key = jax.random.key(seed)
    ks = jax.random.split(key, 20)
    f32 = jnp.float32

    def nrm(k, shape, scale):
        return jax.random.normal(k, shape, f32) * scale

    dt = jnp.exp(jax.random.uniform(ks[6], (DEPTH, SSM_HEADS), f32, math.log(DT_MIN), math.log(DT_MAX)))
    return {
        "x": nrm(ks[0], (BATCH, SEQ, D_MODEL), 1.0),
        "meta_tokens": nrm(ks[1], (N_META, D_MODEL), 1.0),
        "norm_mix": 1.0 + nrm(ks[2], (DEPTH, D_MODEL), 0.02),
        "w_in": nrm(ks[3], (DEPTH, D_MODEL, PROJ_DIM), D_MODEL ** -0.5),
        "conv_w": nrm(ks[4], (DEPTH, CONV_WIDTH, CONV_CH), CONV_WIDTH ** -0.5),
        "conv_b": nrm(ks[5], (DEPTH, CONV_CH), 0.02),
        "dt_bias": dt + jnp.log(-jnp.expm1(-dt)),
        "a_log": jnp.log(jax.random.uniform(ks[7], (DEPTH, SSM_HEADS), f32, 1.0, 16.0)),
        "d_skip": 1.0 + nrm(ks[8], (DEPTH, SSM_HEADS), 0.1),
        "ssm_norm": 1.0 + nrm(ks[9], (DEPTH, SSM_INNER), 0.02),
        "w_branch": nrm(ks[10], (DEPTH, N_BRANCH, BRANCH_WIDTH, D_MODEL), BRANCH_WIDTH ** -0.5),
        "w_out": nrm(ks[11], (DEPTH, D_MODEL, D_MODEL), D_MODEL ** -0.5),
        "norm_ffn": 1.0 + nrm(ks[12], (DEPTH, D_MODEL), 0.02),
        "router_w": nrm(ks[13], (DEPTH, D_MODEL, N_EXPERTS), D_MODEL ** -0.5),
        "router_b": nrm(ks[14], (DEPTH, N_EXPERTS), 0.01),
        "w_gate_up": nrm(ks[15], (DEPTH, N_EXPERTS, D_MODEL, 2 * EXPERT_FF), D_MODEL ** -0.5),
        "b_gate_up": nrm(ks[16], (DEPTH, N_EXPERTS, 2 * EXPERT_FF), 0.01),
        "w_down": nrm(ks[17], (DEPTH, N_EXPERTS, EXPERT_FF, D_MODEL), EXPERT_FF ** -0.5),
        "b_down": nrm(ks[18], (DEPTH, N_EXPERTS, D_MODEL), 0.01),
        "norm_final": 1.0 + nrm(ks[19], (D_MODEL,), 0.02),
    }


def reference(x, meta_tokens, norm_mix, w_in, conv_w, conv_b, dt_bias, a_log, d_skip, ssm_norm,
              w_branch, w_out, norm_ffn, router_w, router_b, w_gate_up, b_gate_up, w_down, b_down,
              norm_final):
    bsz, seq, _ = x.shape
    length = N_META + seq
    h = jnp.concatenate([jnp.broadcast_to(meta_tokens.astype(x.dtype)[None], (bsz, N_META, D_MODEL)), x], axis=1)
    pos = jnp.arange(length, dtype=jnp.float32)
    for layer in range(DEPTH):
        u = rmsnorm(h, norm_mix[layer])
        proj = jnp.einsum('bld,dp->blp', u, w_in[layer])
        q, k, v, g_ret, z, xbc, dt_raw, gate_logits = jnp.split(proj, SPLIT_POINTS, axis=-1)
        o_ret = retention_branch(q, k, v, g_ret, pos)
        o_ssd = ssd_branch(z, xbc, dt_raw, conv_w[layer], conv_b[layer], dt_bias[layer],
                           a_log[layer], d_skip[layer], ssm_norm[layer])
        y_ret = jnp.einsum('blc,cd->bld', o_ret, w_branch[layer, 0])
        y_ssd = jnp.einsum('blc,cd->bld', o_ssd, w_branch[layer, 1])
        gates = jax.nn.sigmoid(gate_logits.astype(jnp.float32)).astype(h.dtype)
        merged = gates[..., :D_MODEL] * y_ret + gates[..., D_MODEL:] * y_ssd
        h = h + jnp.einsum('bld,de->ble', merged, w_out[layer])
        f = moe_ffn(rmsnorm(h, norm_ffn[layer]).reshape(bsz * length, D_MODEL), router_w[layer], router_b[layer],
                    w_gate_up[layer], b_gate_up[layer], w_down[layer], b_down[layer])
        h = h + f.reshape(bsz, length, D_MODEL)
    h = rmsnorm(h, norm_final)
    return h[:, N_META:]
```

```python
import functools

import numpy as np
import jax
import jax.numpy as jnp
from jax import lax
from jax.experimental import pallas as pl
from jax.experimental.pallas import tpu as pltpu

F32 = jnp.float32
BF16 = jnp.bfloat16
I32 = jnp.int32

D_MODEL = 1024
N_META = 16
CHUNK = 128
META_PAD = CHUNK - N_META
NORM_EPS = 1e-6
RET_HEADS = 4
RET_QK_DIM = 256
RET_V_DIM = 512
RET_QK = RET_HEADS * RET_QK_DIM
RET_V = RET_HEADS * RET_V_DIM
ROPE_BASE = 10000.0
SSM_INNER = 2048
SSM_HEAD_DIM = 64
SSM_HEADS = 32
SSM_GROUPS = 4
SSM_STATE = 128
CONV_WIDTH = 4
CONV_CH = SSM_INNER + 2 * SSM_GROUPS * SSM_STATE
HEAD_PAIRS = SSM_HEADS // 2
PAIRS_PER_GROUP = HEAD_PAIRS // SSM_GROUPS
OFF_DT = 2 * RET_QK + 2 * RET_V + SSM_INNER + CONV_CH
N_MAIN = OFF_DT + 2 * D_MODEL
DT_PAD = 128
N_EXPERTS = 32
TOP_K = 4
EXPERT_FF = 1024
SWIGLU_ALPHA = 1.702
SWIGLU_LIMIT = 7.0
MOE_BLOCK = 256

VMEM_LIMIT = 56 * 1024 * 1024
HIGHEST = lax.Precision.HIGHEST


def _params(sem):
    return pltpu.CompilerParams(dimension_semantics=sem, vmem_limit_bytes=VMEM_LIMIT)


def _pick(n, candidates):
    for c in candidates:
        if n % c == 0:
            return c
    raise ValueError(f"no tile for {n} among {candidates}")


def _nt_dot(a, b, **kw):
    return lax.dot_general(a, b, (((1,), (1,)), ((), ())), preferred_element_type=F32, **kw)


def _inproj_body(x_ref, nw_ref, w_ref, wdt_ref, o_ref, dt_ref, xn_ref):
    @pl.when(pl.program_id(1) == 0)
    def _():
        x = x_ref[...]
        ms = jnp.mean(x * x, axis=-1, keepdims=True)
        xn = (x * lax.rsqrt(ms + NORM_EPS) * nw_ref[...]).astype(BF16)
        xn_ref[...] = xn
        dt_ref[...] = jnp.dot(xn, wdt_ref[...], preferred_element_type=F32)

    o_ref[...] = jnp.dot(xn_ref[...], w_ref[...], preferred_element_type=F32).astype(BF16)


def _in_proj(hp, norm_w, w_main, w_dt):
    t = hp.shape[0]
    tm = _pick(t, (1024, 512, 256, 128))
    tn = 1024
    return pl.pallas_call(
        _inproj_body,
        grid=(t // tm, N_MAIN // tn),
        in_specs=[
            pl.BlockSpec((tm, D_MODEL), lambda i, j: (i, 0)),
            pl.BlockSpec((1, D_MODEL), lambda i, j: (0, 0)),
            pl.BlockSpec((D_MODEL, tn), lambda i, j: (0, j)),
            pl.BlockSpec((D_MODEL, DT_PAD), lambda i, j: (0, 0)),
        ],
        out_specs=[
            pl.BlockSpec((tm, tn), lambda i, j: (i, j)),
            pl.BlockSpec((tm, DT_PAD), lambda i, j: (i, 0)),
        ],
        out_shape=[jax.ShapeDtypeStruct((t, N_MAIN), BF16), jax.ShapeDtypeStruct((t, DT_PAD), F32)],
        scratch_shapes=[pltpu.VMEM((tm, D_MODEL), BF16)],
        compiler_params=_params(("parallel", "arbitrary")),
        name="in_proj",
    )(hp, norm_w, w_main, w_dt)


def _ret_tables(lp):
    half = RET_QK_DIM // 2
    inv_freq = ROPE_BASE ** (-np.arange(half, dtype=np.float64) / half)
    pos = np.arange(lp, dtype=np.float64) - META_PAD
    ang = pos[:, None] * inv_freq[None, :]
    log_gamma = np.log(1.0 - 2.0 ** (-5.0 - np.arange(RET_HEADS, dtype=np.float64)))
    idx = np.arange(CHUNK, dtype=np.float64)
    dist = idx[:, None] - idx[None, :]
    intra = np.where(dist >= 0, np.exp(log_gamma[:, None, None] * np.maximum(dist, 0.0)[None]), 0.0)
    q_dec = np.exp(log_gamma[:, None] * (idx[None, :] + 1.0))
    k_dec = np.exp(log_gamma[:, None] * (CHUNK - 1.0 - idx[None, :]))
    q_dec = np.broadcast_to(q_dec[:, :, None], (RET_HEADS, CHUNK, RET_V_DIM))
    k_dec = np.broadcast_to(k_dec[:, :, None], (RET_HEADS, CHUNK, RET_QK_DIM))
    chunk_dec = tuple(float(v) for v in np.exp(log_gamma * CHUNK))
    as32 = lambda a: jnp.asarray(np.ascontiguousarray(a), F32)
    return as32(np.cos(ang)), as32(np.sin(ang)), as32(intra), as32(q_dec), as32(k_dec), chunk_dec


def _ret_body(chunk_dec, q_ref, k_ref, v_ref, g_ref, cos_ref, sin_ref, intra_ref, qd_ref, kd_ref, o_ref, st_ref):
    @pl.when(pl.program_id(1) == 0)
    def _():
        st_ref[...] = jnp.zeros_like(st_ref)

    cos = cos_ref[...]
    sin = sin_ref[...]
    half = RET_QK_DIM // 2

    def rotary(t):
        t1, t2 = t[:, :half], t[:, half:]
        return jnp.concatenate([t1 * cos - t2 * sin, t2 * cos + t1 * sin], axis=1)

    for h in range(RET_HEADS):
        qk = slice(h * RET_QK_DIM, (h + 1) * RET_QK_DIM)
        vv = slice(h * RET_V_DIM, (h + 1) * RET_V_DIM)
        qr = rotary(q_ref[:, qk].astype(F32))
        kr = rotary(k_ref[:, qk].astype(F32)) * (RET_QK_DIM ** -0.5)
        qb = qr.astype(BF16)
        vh = v_ref[:, vv]
        st = st_ref[h]
        s = _nt_dot(qb, kr.astype(BF16)) * intra_ref[h]
        y = jnp.dot(s.astype(BF16), vh, preferred_element_type=F32)
        y = y + jnp.dot(qb, st.astype(BF16), preferred_element_type=F32) * qd_ref[h]
        kdt = jnp.transpose(kr * kd_ref[h]).astype(BF16)
        st_ref[h] = st * chunk_dec[h] + jnp.dot(kdt, vh, preferred_element_type=F32)
        o = y * lax.rsqrt(jnp.mean(y * y, axis=-1, keepdims=True) + NORM_EPS)
        g = g_ref[:, vv].astype(F32)
        o_ref[:, vv] = (g * jax.nn.sigmoid(g) * o).astype(BF16)


def _retention(proj, bsz, nc):
    t = proj.shape[0]
    cos, sin, intra, q_dec, k_dec, chunk_dec = _ret_tables(nc * CHUNK)
    row = lambda b, c: b * nc + c
    const3 = lambda b, c: (0, 0, 0)
    return pl.pallas_call(
        functools.partial(_ret_body, chunk_dec),
        grid=(bsz, nc),
        in_specs=[
            pl.BlockSpec((CHUNK, RET_QK), lambda b, c: (row(b, c), 0)),
            pl.BlockSpec((CHUNK, RET_QK), lambda b, c: (row(b, c), 1)),
            pl.BlockSpec((CHUNK, RET_V), lambda b, c: (row(b, c), 1)),
            pl.BlockSpec((CHUNK, RET_V), lambda b, c: (row(b, c), 2)),
            pl.BlockSpec((CHUNK, RET_QK_DIM // 2), lambda b, c: (c, 0)),
            pl.BlockSpec((CHUNK, RET_QK_DIM // 2), lambda b, c: (c, 0)),
            pl.BlockSpec((RET_HEADS, CHUNK, CHUNK), const3),
            pl.BlockSpec((RET_HEADS, CHUNK, RET_V_DIM), const3),
            pl.BlockSpec((RET_HEADS, CHUNK, RET_QK_DIM), const3),
        ],
        out_specs=pl.BlockSpec((CHUNK, RET_V), lambda b, c: (row(b, c), 0)),
        out_shape=jax.ShapeDtypeStruct((t, RET_V), BF16),
        scratch_shapes=[pltpu.VMEM((RET_HEADS, RET_QK_DIM, RET_V_DIM), F32)],
        compiler_params=_params(("parallel", "arbitrary")),
        name="retention",
    )(proj, proj, proj, proj, cos, sin, intra, q_dec, k_dec)


CONV_COLS = 512


def _ssd_body(z_ref, xs_ref, bc_ref, dt_ref, cw_ref, cb_ref, dtb_ref, alog_ref, dsk_ref, nw_ref,
              o_ref, st_ref, carry_ref, xc_ref, y_ref):
    c = pl.program_id(1)

    @pl.when(c == 0)
    def _():
        st_ref[...] = jnp.zeros_like(st_ref)
        carry_ref[...] = jnp.zeros_like(carry_ref)

    rows = lax.broadcasted_iota(I32, (CHUNK, 1), 0)
    valid = jnp.logical_or(c > 0, rows >= META_PAD)

    for s in range(CONV_CH // CONV_COLS):
        cols = slice(s * CONV_COLS, (s + 1) * CONV_COLS)
        if (s + 1) * CONV_COLS <= SSM_INNER:
            raw = xs_ref[:, cols].astype(F32)
        else:
            raw = bc_ref[:, s * CONV_COLS - SSM_INNER:(s + 1) * CONV_COLS - SSM_INNER].astype(F32)
        full = jnp.concatenate([carry_ref[:, cols], raw], axis=0)
        acc = cb_ref[:, cols]
        for k in range(CONV_WIDTH):
            off = 8 - (CONV_WIDTH - 1) + k
            acc = acc + cw_ref[k:k + 1, cols] * full[off:off + CHUNK]
        carry_ref[:, cols] = raw[CHUNK - 8:]
        xc_ref[:, cols] = jnp.where(valid, acc * jax.nn.sigmoid(acc), 0.0)

    dtv = dt_ref[...] + dtb_ref[...]
    dt = jnp.maximum(dtv, 0.0) + jnp.log1p(jnp.exp(-jnp.abs(dtv)))
    dt = jnp.where(valid, dt, 0.0)
    a = dt * (-jnp.exp(alog_ref[...]))
    ri = lax.broadcasted_iota(I32, (CHUNK, CHUNK), 0)
    ci = lax.broadcasted_iota(I32, (CHUNK, CHUNK), 1)
    causal = ri >= ci
    acs = jnp.dot(causal.astype(F32), a, preferred_element_type=F32, precision=HIGHEST)
    last = acs[CHUNK - 1:CHUNK, :]
    acs_t = jnp.transpose(acs)
    dt_t = jnp.transpose(dt)
    w_t = jnp.transpose(jnp.exp(last - acs) * dt)
    e_last = jnp.exp(last)
    lanes = lax.broadcasted_iota(I32, (1, CHUNK), 1)
    low = lanes < SSM_HEAD_DIM

    for g in range(SSM_GROUPS):
        bm = xc_ref[:, SSM_INNER + g * SSM_STATE:SSM_INNER + (g + 1) * SSM_STATE]
        cm = xc_ref[:, SSM_INNER + (SSM_GROUPS + g) * SSM_STATE:SSM_INNER + (SSM_GROUPS + g + 1) * SSM_STATE]
        cb = _nt_dot(cm.astype(BF16), bm.astype(BF16))
        bm_t = jnp.transpose(bm)
        for pp in range(PAIRS_PER_GROUP):
            m = g * PAIRS_PER_GROUP + pp
            x_pair = xc_ref[:, m * CHUNK:(m + 1) * CHUNK]
            st_pair = st_ref[m]
            y_pair = jnp.zeros((CHUNK, CHUNK), F32)
            upd = jnp.zeros((SSM_STATE, CHUNK), F32)
            for hh in range(2):
                h = 2 * m + hh
                lane_mask = low if hh == 0 else jnp.logical_not(low)
                col = jnp.broadcast_to(acs[:, h:h + 1], (CHUNK, CHUNK))
                seg = col - acs_t[h:h + 1, :]
                dec = jnp.where(causal, jnp.exp(jnp.minimum(seg, 0.0)), 0.0)
                mat = cb * dec * dt_t[h:h + 1, :]
                lhs = jnp.concatenate([mat, cm * jnp.exp(col)], axis=1).astype(BF16)
                xm = jnp.where(lane_mask, x_pair, 0.0).astype(BF16)
                sm = jnp.where(lane_mask, st_pair, 0.0).astype(BF16)
                y_pair = y_pair + jnp.dot(lhs, jnp.concatenate([xm, sm], axis=0), preferred_element_type=F32)
                upd = upd + jnp.dot((bm_t * w_t[h:h + 1, :]).astype(BF16), xm, preferred_element_type=F32)
            decay = jnp.where(low, jnp.broadcast_to(e_last[:, 2 * m:2 * m + 1], (1, CHUNK)),
                              jnp.broadcast_to(e_last[:, 2 * m + 1:2 * m + 2], (1, CHUNK)))
            st_ref[m] = st_pair * decay + upd
            y_ref[:, m * CHUNK:(m + 1) * CHUNK] = y_pair

    gsz = SSM_INNER // SSM_GROUPS
    for g in range(SSM_GROUPS):
        cols = slice(g * gsz, (g + 1) * gsz)
        z = z_ref[:, cols].astype(F32)
        y = (y_ref[:, cols] + dsk_ref[:, cols] * xc_ref[:, cols]) * (z * jax.nn.sigmoid(z))
        y = y * lax.rsqrt(jnp.mean(y * y, axis=-1, keepdims=True) + NORM_EPS)
        o_ref[:, cols] = (y * nw_ref[:, cols]).astype(BF16)


def _ssd(proj, dt_raw, conv_w, conv_b, dt_bias, a_log, d_skip, norm_w, bsz, nc):
    t = proj.shape[0]
    row = lambda b, c: b * nc + c
    const2 = lambda b, c: (0, 0)
    pad = lambda v: jnp.pad(v.astype(F32), (0, DT_PAD - SSM_HEADS)).reshape(1, DT_PAD)
    return pl.pallas_call(
        _ssd_body,
        grid=(bsz, nc),
        in_specs=[
            pl.BlockSpec((CHUNK, SSM_INNER), lambda b, c: (row(b, c), 3)),
            pl.BlockSpec((CHUNK, SSM_INNER), lambda b, c: (row(b, c), 4)),
            pl.BlockSpec((CHUNK, 1024), lambda b, c: (row(b, c), 10)),
            pl.BlockSpec((CHUNK, DT_PAD), lambda b, c: (row(b, c), 0)),
            pl.BlockSpec((CONV_WIDTH, CONV_CH), const2),
            pl.BlockSpec((1, CONV_CH), const2),
            pl.BlockSpec((1, DT_PAD), const2),
            pl.BlockSpec((1, DT_PAD), const2),
            pl.BlockSpec((1, SSM_INNER), const2),
            pl.BlockSpec((1, SSM_INNER), const2),
        ],
        out_specs=pl.BlockSpec((CHUNK, SSM_INNER), lambda b, c: (row(b, c), 0)),
        out_shape=jax.ShapeDtypeStruct((t, SSM_INNER), BF16),
        scratch_shapes=[
            pltpu.VMEM((HEAD_PAIRS, SSM_STATE, CHUNK), F32),
            pltpu.VMEM((8, CONV_CH), F32),
            pltpu.VMEM((CHUNK, CONV_CH), F32),
            pltpu.VMEM((CHUNK, SSM_INNER), F32),
        ],
        compiler_params=_params(("parallel", "arbitrary")),
        name="ssd",
    )(proj, proj, proj, dt_raw, conv_w.astype(F32), conv_b.astype(F32).reshape(1, CONV_CH), pad(dt_bias), pad(a_log),
      jnp.repeat(d_skip.astype(F32), SSM_HEAD_DIM).reshape(1, SSM_INNER), norm_w.astype(F32).reshape(1, SSM_INNER))


def _merge_body(oret_ref, ossd_ref, g0_ref, g1_ref, h_ref, valid_ref, wb0_ref, wb1_ref, wo_ref, nw_ref, rw_ref, rb_ref,
                h1_ref, un_ref, e_ref, w_ref):
    y_ret = jnp.dot(oret_ref[...], wb0_ref[...], preferred_element_type=F32)
    y_ssd = jnp.dot(ossd_ref[...], wb1_ref[...], preferred_element_type=F32)
    merged = (jax.nn.sigmoid(g0_ref[...].astype(F32)) * y_ret + jax.nn.sigmoid(g1_ref[...].astype(F32)) * y_ssd)
    h1 = h_ref[...] + jnp.dot(merged.astype(BF16), wo_ref[...], preferred_element_type=F32)
    h1_ref[...] = h1
    un = h1 * lax.rsqrt(jnp.mean(h1 * h1, axis=-1, keepdims=True) + NORM_EPS) * nw_ref[...]
    un_ref[...] = un

    logits = _nt_dot(rw_ref[...], un, precision=HIGHEST) + rb_ref[...]
    tm = logits.shape[1]
    eidx = lax.broadcasted_iota(I32, (N_EXPERTS, tm), 0)
    vals, ids = [], []
    for _ in range(TOP_K):
        best = jnp.max(logits, axis=0, keepdims=True)
        arg = jnp.min(jnp.where(logits == best, eidx, N_EXPERTS), axis=0, keepdims=True)
        vals.append(best)
        ids.append(arg)
        logits = jnp.where(eidx == arg, -jnp.inf, logits)
    ex = [jnp.exp(v - vals[0]) for v in vals]
    denom = ex[0] + ex[1] + ex[2] + ex[3]
    is_tok = valid_ref[...] > 0
    e_ref[...] = jnp.concatenate([jnp.where(is_tok, i, -1) for i in ids] + [jnp.zeros((8 - TOP_K, tm), I32)], axis=0)
    w_ref[...] = jnp.concatenate([x / denom for x in ex] + [jnp.zeros((8 - TOP_K, tm), F32)], axis=0)


def _merge(o_ret, o_ssd, proj, hp, valid, wb0, wb1, wo, norm_w, router_wt, router_b):
    t = hp.shape[0]
    tm = _pick(t, (256, 128))
    gate_blk = OFF_DT // D_MODEL
    assert gate_blk * D_MODEL == OFF_DT
    const2 = lambda i: (0, 0)
    return pl.pallas_call(
        _merge_body,
        grid=(t // tm,),
        in_specs=[
            pl.BlockSpec((tm, RET_V), lambda i: (i, 0)),
            pl.BlockSpec((tm, SSM_INNER), lambda i: (i, 0)),
            pl.BlockSpec((tm, D_MODEL), lambda i: (i, gate_blk)),
            pl.BlockSpec((tm, D_MODEL), lambda i: (i, gate_blk + 1)),
            pl.BlockSpec((tm, D_MODEL), lambda i: (i, 0)),
            pl.BlockSpec((1, tm), lambda i: (0, i)),
            pl.BlockSpec((RET_V, D_MODEL), const2),
            pl.BlockSpec((SSM_INNER, D_MODEL), const2),
            pl.BlockSpec((D_MODEL, D_MODEL), const2),
            pl.BlockSpec((1, D_MODEL), const2),
            pl.BlockSpec((N_EXPERTS, D_MODEL), const2),
            pl.BlockSpec((N_EXPERTS, 1), const2),
        ],
        out_specs=[
            pl.BlockSpec((tm, D_MODEL), lambda i: (i, 0)),
            pl.BlockSpec((tm, D_MODEL), lambda i: (i, 0)),
            pl.BlockSpec((8, tm), lambda i: (0, i)),
            pl.BlockSpec((8, tm), lambda i: (0, i)),
        ],
        out_shape=[
            jax.ShapeDtypeStruct((t, D_MODEL), F32),
            jax.ShapeDtypeStruct((t, D_MODEL), F32),
            jax.ShapeDtypeStruct((8, t), I32),
            jax.ShapeDtypeStruct((8, t), F32),
        ],
        compiler_params=_params(("parallel",)),
        name="merge_router",
    )(o_ret, o_ssd, proj, proj, hp, valid, wb0, wb1, wo, norm_w, router_wt, router_b)


RANK_TILE = 512


def _rank_body(e_ref, tri_ref, rank_ref, cnt_ref, run_ref):
    @pl.when(pl.program_id(0) == 0)
    def _():
        run_ref[...] = jnp.zeros_like(run_ref)

    eidx = lax.broadcasted_iota(I32, (N_EXPERTS, RANK_TILE), 0)
    ones = jnp.ones((RANK_TILE, RANK_TILE), BF16)
    run = run_ref[...]
    ranks = []
    for k in range(TOP_K):
        hit = eidx == e_ref[k:k + 1, :]
        oh = jnp.where(hit, 1.0, 0.0).astype(BF16)
        cum = jnp.dot(oh, tri_ref[...], preferred_element_type=F32)
        ranks.append(jnp.sum(jnp.where(hit, cum - 1.0 + run, 0.0), axis=0, keepdims=True))
        run = run + jnp.dot(oh, ones, preferred_element_type=F32)
    run_ref[...] = run
    rank_ref[...] = jnp.concatenate(ranks + [jnp.zeros((8 - TOP_K, RANK_TILE), F32)], axis=0).astype(I32)
    cnt_ref[...] = run[:, :128].astype(I32)


def _rank(top_e):
    t = top_e.shape[1]
    assert t % RANK_TILE == 0 or t < RANK_TILE
    tri = jnp.asarray(np.triu(np.ones((RANK_TILE, RANK_TILE), np.float32)), BF16)
    return pl.pallas_call(
        _rank_body,
        grid=(t // RANK_TILE,),
        in_specs=[
            pl.BlockSpec((8, RANK_TILE), lambda i: (0, i)),
            pl.BlockSpec((RANK_TILE, RANK_TILE), lambda i: (0, 0)),
        ],
        out_specs=[
            pl.BlockSpec((8, RANK_TILE), lambda i: (0, i)),
            pl.BlockSpec((N_EXPERTS, 128), lambda i: (0, 0)),
        ],
        out_shape=[jax.ShapeDtypeStruct((8, t), I32), jax.ShapeDtypeStruct((N_EXPERTS, 128), I32)],
        scratch_shapes=[pltpu.VMEM((N_EXPERTS, RANK_TILE), F32)],
        compiler_params=_params(("arbitrary",)),
        name="rank",
    )(top_e, tri)


def _first_row(nc, step):
    return jnp.where(step % nc == 0, META_PAD, 0)


def _dispatch_body(nc, pstart_ref, e_ref, r_ref, un_ref, xs_in_ref, xs_ref, sem):
    del xs_in_ref
    lo = _first_row(nc, pl.program_id(0))

    def row_copy(t, dst):
        return pltpu.make_async_copy(un_ref.at[pl.ds(t, 1), :], xs_ref.at[pl.ds(dst, 1), :], sem)

    def issue(t, carry):
        for k in range(TOP_K):
            row_copy(t, pstart_ref[e_ref[k, t]] + r_ref[k, t]).start()
        return carry

    def drain(t, carry):
        for k in range(TOP_K):
            row_copy(t, 0).wait()
        return carry

    lax.fori_loop(lo, CHUNK, issue, 0)
    lax.fori_loop(lo, CHUNK, drain, 0)


def _dispatch(pstart, top_e, rank, un, n_rows, nc):
    t = un.shape[0]
    xs0 = jnp.zeros((n_rows, D_MODEL), F32)
    smem_blk = pl.BlockSpec((8, CHUNK), lambda i, ps: (0, i), memory_space=pltpu.SMEM)
    return pl.pallas_call(
        functools.partial(_dispatch_body, nc),
        grid_spec=pltpu.PrefetchScalarGridSpec(
            num_scalar_prefetch=1,
            grid=(t // CHUNK,),
            in_specs=[
                smem_blk,
                smem_blk,
                pl.BlockSpec((CHUNK, D_MODEL), lambda i, ps: (i, 0)),
                pl.BlockSpec(memory_space=pl.ANY),
            ],
            out_specs=pl.BlockSpec(memory_space=pl.ANY),
            scratch_shapes=[pltpu.SemaphoreType.DMA(())],
        ),
        out_shape=jax.ShapeDtypeStruct((n_rows, D_MODEL), F32),
        input_output_aliases={4: 0},
        compiler_params=_params(("arbitrary",)),
        name="dispatch",
    )(pstart, top_e, rank, un, xs0)


def _expert_body(be_ref, nu_ref, x_ref, wg_ref, wu_ref, wd_ref, bg_ref, bu_ref, bd_ref, y_ref):
    del be_ref

    @pl.when(pl.program_id(0) < nu_ref[0])
    def _():
        x = x_ref[...].astype(BF16)
        gate = jnp.dot(x, wg_ref[0], preferred_element_type=F32) + bg_ref[0]
        up = jnp.dot(x, wu_ref[0], preferred_element_type=F32) + bu_ref[0]
        gate = jnp.minimum(gate, SWIGLU_LIMIT)
        up = jnp.clip(up, -SWIGLU_LIMIT, SWIGLU_LIMIT)
        act = (up + 1.0) * (gate * jax.nn.sigmoid(SWIGLU_ALPHA * gate))
        y_ref[...] = jnp.dot(act.astype(BF16), wd_ref[0], preferred_element_type=F32) + bd_ref[0]


def _experts(block_e, n_used, xs, wg, wu, wd, bg, bu, bd):
    n_rows = xs.shape[0]
    n_blocks = n_rows // MOE_BLOCK
    wspec = lambda shape: pl.BlockSpec((1,) + shape, lambda i, be, nu: (be[i], 0, 0))
    return pl.pallas_call(
        _expert_body,
        grid_spec=pltpu.PrefetchScalarGridSpec(
            num_scalar_prefetch=2,
            grid=(n_blocks,),
            in_specs=[
                pl.BlockSpec((MOE_BLOCK, D_MODEL), lambda i, be, nu: (i, 0)),
                wspec((D_MODEL, EXPERT_FF)),
                wspec((D_MODEL, EXPERT_FF)),
                wspec((EXPERT_FF, D_MODEL)),
                wspec((1, EXPERT_FF)),
                wspec((1, EXPERT_FF)),
                wspec((1, D_MODEL)),
            ],
            out_specs=pl.BlockSpec((MOE_BLOCK, D_MODEL), lambda i, be, nu: (i, 0)),
        ),
        out_shape=jax.ShapeDtypeStruct((n_rows, D_MODEL), F32),
        compiler_params=_params(("arbitrary",)),
        name="experts",
    )(block_e, n_used, xs, wg, wu, wd, bg, bu, bd)


def _combine_body(pstart_ref, e_ref, r_ref, w_ref, h1_ref, nw_ref, y_hbm, o_ref, ybuf, sem):
    def row_copy(k, t, src):
        return pltpu.make_async_copy(y_hbm.at[pl.ds(src, 1), :], ybuf.at[k, pl.ds(t, 1), :], sem)

    def issue(t, carry):
        for k in range(TOP_K):
            row_copy(k, t, pstart_ref[e_ref[k, t]] + r_ref[k, t]).start()
        return carry

    def drain(t, carry):
        for k in range(TOP_K):
            row_copy(k, t, 0).wait()
        return carry

    lax.fori_loop(0, CHUNK, issue, 0)
    lax.fori_loop(0, CHUNK, drain, 0)

    ri = lax.broadcasted_iota(I32, (CHUNK, CHUNK), 0)
    ci = lax.broadcasted_iota(I32, (CHUNK, CHUNK), 1)
    wcol = _nt_dot((ri == ci).astype(F32), w_ref[...], precision=HIGHEST)
    f = wcol[:, 0:1] * ybuf[0]
    for k in range(1, TOP_K):
        f = f + wcol[:, k:k + 1] * ybuf[k]
    h2 = h1_ref[...] + f
    o_ref[...] = h2 * lax.rsqrt(jnp.mean(h2 * h2, axis=-1, keepdims=True) + NORM_EPS) * nw_ref[...]


def _combine(pstart, top_e, rank, top_w, h1, norm_w, y_sorted, bsz, nc):
    seq_chunks = nc - 1
    src = lambda b, c: b * nc + c + 1
    smem_blk = pl.BlockSpec((8, CHUNK), lambda b, c, ps: (0, src(b, c)), memory_space=pltpu.SMEM)
    return pl.pallas_call(
        _combine_body,
        grid_spec=pltpu.PrefetchScalarGridSpec(
            num_scalar_prefetch=1,
            grid=(bsz, seq_chunks),
            in_specs=[
                smem_blk,
                smem_blk,
                pl.BlockSpec((8, CHUNK), lambda b, c, ps: (0, src(b, c))),
                pl.BlockSpec((CHUNK, D_MODEL), lambda b, c, ps: (src(b, c), 0)),
                pl.BlockSpec((1, D_MODEL), lambda b, c, ps: (0, 0)),
                pl.BlockSpec(memory_space=pl.ANY),
            ],
            out_specs=pl.BlockSpec((CHUNK, D_MODEL), lambda b, c, ps: (b * seq_chunks + c, 0)),
            scratch_shapes=[pltpu.VMEM((TOP_K, CHUNK, D_MODEL), F32), pltpu.SemaphoreType.DMA(())],
        ),
        out_shape=jax.ShapeDtypeStruct((bsz * seq_chunks * CHUNK, D_MODEL), F32),
        compiler_params=_params(("arbitrary", "arbitrary")),
        name="combine",
    )(pstart, top_e, rank, top_w, h1, norm_w, y_sorted)


def kernel(x, meta_tokens, norm_mix, w_in, conv_w, conv_b, dt_bias, a_log, d_skip, ssm_norm, w_branch, w_out, norm_ffn,
           router_w, router_b, w_gate_up, b_gate_up, w_down, b_down, norm_final):
    bsz, seq, _ = x.shape
    assert seq % CHUNK == 0 and norm_mix.shape[0] == 1
    nc = 1 + seq // CHUNK
    lp = nc * CHUNK
    t = bsz * lp

    hp = jnp.concatenate([jnp.zeros((bsz, META_PAD, D_MODEL), x.dtype),
                          jnp.broadcast_to(meta_tokens.astype(x.dtype)[None], (bsz, N_META, D_MODEL)), x], axis=1)
    hp = hp.reshape(t, D_MODEL)
    valid = jnp.asarray(np.tile(np.arange(lp) >= META_PAD, bsz).astype(np.int32).reshape(1, t))

    w_in0 = w_in[0]
    w_main = jnp.concatenate([w_in0[:, :OFF_DT], w_in0[:, OFF_DT + SSM_HEADS:]], axis=1).astype(BF16)
    w_dt = jnp.pad(w_in0[:, OFF_DT:OFF_DT + SSM_HEADS], ((0, 0), (0, DT_PAD - SSM_HEADS))).astype(BF16)
    wg = w_gate_up[0][:, :, 0::2].astype(BF16)
    wu = w_gate_up[0][:, :, 1::2].astype(BF16)
    bg = b_gate_up[0][:, None, 0::2].astype(F32)
    bu = b_gate_up[0][:, None, 1::2].astype(F32)
    wd = w_down[0].astype(BF16)
    bd = b_down[0][:, None, :].astype(F32)

    proj, dt_raw = _in_proj(hp, norm_mix[0].reshape(1, D_MODEL).astype(F32), w_main, w_dt)
    o_ret = _retention(proj, bsz, nc)
    o_ssd = _ssd(proj, dt_raw, conv_w[0], conv_b[0], dt_bias[0], a_log[0], d_skip[0], ssm_norm[0], bsz, nc)
    h1, un, top_e, top_w = _merge(
        o_ret, o_ssd, proj, hp, valid, w_branch[0, 0].astype(BF16), w_branch[0, 1].astype(BF16), w_out[0].astype(BF16),
        norm_ffn[0].reshape(1, D_MODEL).astype(F32), jnp.transpose(router_w[0]).astype(F32),
        router_b[0].reshape(N_EXPERTS, 1).astype(F32))

    rank, counts = _rank(top_e)
    counts = counts[:, 0]
    padded = (counts + MOE_BLOCK - 1) // MOE_BLOCK * MOE_BLOCK
    pend = jnp.cumsum(padded)
    pstart = (pend - padded).astype(I32)
    n_pairs = bsz * (N_META + seq) * TOP_K
    n_blocks = -(-(n_pairs + N_EXPERTS * (MOE_BLOCK - 1)) // MOE_BLOCK)
    blk_row = jnp.arange(n_blocks, dtype=I32) * MOE_BLOCK
    block_e = jnp.minimum(jnp.searchsorted(pend, blk_row, side="right"), N_EXPERTS - 1).astype(I32)
    n_used = (pend[-1:] // MOE_BLOCK).astype(I32)

    xs = _dispatch(pstart, top_e, rank, un, n_blocks * MOE_BLOCK, nc)
    ys = _experts(block_e, n_used, xs, wg, wu, wd, bg, bu, bd)
    out = _combine(pstart, top_e, rank, top_w, h1, norm_final.reshape(1, D_MODEL).astype(F32), ys, bsz, nc)
    return out.reshape(bsz, seq, D_MODEL)
```

```python
import functools

import numpy as np
import jax
import jax.numpy as jnp
from jax import lax
from jax.experimental import pallas as pl
from jax.experimental.pallas import tpu as pltpu

F32 = jnp.float32
BF16 = jnp.bfloat16
I32 = jnp.int32

D_MODEL = 1024
N_META = 16
CHUNK = 128
META_PAD = CHUNK - N_META
NORM_EPS = 1e-6
RET_HEADS = 4
RET_QK_DIM = 256
RET_V_DIM = 512
RET_QK = RET_HEADS * RET_QK_DIM
RET_V = RET_HEADS * RET_V_DIM
ROPE_BASE = 10000.0
SSM_INNER = 2048
SSM_HEAD_DIM = 64
SSM_HEADS = 32
SSM_GROUPS = 4
SSM_STATE = 128
CONV_WIDTH = 4
CONV_CH = SSM_INNER + 2 * SSM_GROUPS * SSM_STATE
HEAD_PAIRS = SSM_HEADS // 2
PAIRS_PER_GROUP = HEAD_PAIRS // SSM_GROUPS
OFF_DT = 2 * RET_QK + 2 * RET_V + SSM_INNER + CONV_CH
N_MAIN = OFF_DT + 2 * D_MODEL
DT_PAD = 128
N_EXPERTS = 32
TOP_K = 4
EXPERT_FF = 1024
SWIGLU_ALPHA = 1.702
SWIGLU_LIMIT = 7.0
MOE_BLOCK = 256

VMEM_LIMIT = 56 * 1024 * 1024
HIGHEST = lax.Precision.HIGHEST


def _params(sem):
    return pltpu.CompilerParams(dimension_semantics=sem, vmem_limit_bytes=VMEM_LIMIT)


def _pick(n, candidates):
    for c in candidates:
        if n % c == 0:
            return c
    raise ValueError(f"no tile for {n} among {candidates}")


def _nt_dot(a, b, **kw):
    return lax.dot_general(a, b, (((1,), (1,)), ((), ())), preferred_element_type=F32, **kw)


def _inproj_body(x_ref, nw_ref, w_ref, wdt_ref, o_ref, dt_ref, xn_ref):
    @pl.when(pl.program_id(1) == 0)
    def _():
        x = x_ref[...]
        ms = jnp.mean(x * x, axis=-1, keepdims=True)
        xn = (x * lax.rsqrt(ms + NORM_EPS) * nw_ref[...]).astype(BF16)
        xn_ref[...] = xn
        dt_ref[...] = jnp.dot(xn, wdt_ref[...], preferred_element_type=F32)

    o_ref[...] = jnp.dot(xn_ref[...], w_ref[...], preferred_element_type=F32).astype(BF16)


def _in_proj(hp, norm_w, w_main, w_dt):
    t = hp.shape[0]
    tm = _pick(t, (1024, 512, 256, 128))
    tn = 1024
    return pl.pallas_call(
        _inproj_body,
        grid=(t // tm, N_MAIN // tn),
        in_specs=[
            pl.BlockSpec((tm, D_MODEL), lambda i, j: (i, 0)),
            pl.BlockSpec((1, D_MODEL), lambda i, j: (0, 0)),
            pl.BlockSpec((D_MODEL, tn), lambda i, j: (0, j)),
            pl.BlockSpec((D_MODEL, DT_PAD), lambda i, j: (0, 0)),
        ],
        out_specs=[
            pl.BlockSpec((tm, tn), lambda i, j: (i, j)),
            pl.BlockSpec((tm, DT_PAD), lambda i, j: (i, 0)),
        ],
        out_shape=[jax.ShapeDtypeStruct((t, N_MAIN), BF16), jax.ShapeDtypeStruct((t, DT_PAD), F32)],
        scratch_shapes=[pltpu.VMEM((tm, D_MODEL), BF16)],
        compiler_params=_params(("parallel", "arbitrary")),
        name="in_proj",
    )(hp, norm_w, w_main, w_dt)


def _ret_tables(lp):
    half = RET_QK_DIM // 2
    inv_freq = ROPE_BASE ** (-np.arange(half, dtype=np.float64) / half)
    pos = np.arange(lp, dtype=np.float64) - META_PAD
    ang = pos[:, None] * inv_freq[None, :]
    log_gamma = np.log(1.0 - 2.0 ** (-5.0 - np.arange(RET_HEADS, dtype=np.float64)))
    idx = np.arange(CHUNK, dtype=np.float64)
    dist = idx[:, None] - idx[None, :]
    intra = np.where(dist >= 0, np.exp(log_gamma[:, None, None] * np.maximum(dist, 0.0)[None]), 0.0)
    q_dec = np.exp(log_gamma[:, None] * (idx[None, :] + 1.0))
    k_dec = np.exp(log_gamma[:, None] * (CHUNK - 1.0 - idx[None, :]))
    q_dec = np.broadcast_to(q_dec[:, :, None], (RET_HEADS, CHUNK, RET_V_DIM))
    k_dec = np.broadcast_to(k_dec[:, :, None], (RET_HEADS, CHUNK, RET_QK_DIM))
    chunk_dec = tuple(float(v) for v in np.exp(log_gamma * CHUNK))
    as32 = lambda a: jnp.asarray(np.ascontiguousarray(a), F32)
    return as32(np.cos(ang)), as32(np.sin(ang)), as32(intra), as32(q_dec), as32(k_dec), chunk_dec


def _ret_body(chunk_dec, q_ref, k_ref, v_ref, g_ref, cos_ref, sin_ref, intra_ref, qd_ref, kd_ref, o_ref, st_ref):
    @pl.when(pl.program_id(1) == 0)
    def _():
        st_ref[...] = jnp.zeros_like(st_ref)

    cos = cos_ref[...]
    sin = sin_ref[...]
    half = RET_QK_DIM // 2

    def rotary(t):
        t1, t2 = t[:, :half], t[:, half:]
        return jnp.concatenate([t1 * cos - t2 * sin, t2 * cos + t1 * sin], axis=1)

    for h in range(RET_HEADS):
        qk = slice(h * RET_QK_DIM, (h + 1) * RET_QK_DIM)
        vv = slice(h * RET_V_DIM, (h + 1) * RET_V_DIM)
        qr = rotary(q_ref[:, qk].astype(F32))
        kr = rotary(k_ref[:, qk].astype(F32)) * (RET_QK_DIM ** -0.5)
        qb = qr.astype(BF16)
        vh = v_ref[:, vv]
        st = st_ref[h]
        s = _nt_dot(qb, kr.astype(BF16)) * intra_ref[h]
        y = jnp.dot(s.astype(BF16), vh, preferred_element_type=F32)
        y = y + jnp.dot(qb, st.astype(BF16), preferred_element_type=F32) * qd_ref[h]
        kdt = jnp.transpose(kr * kd_ref[h]).astype(BF16)
        st_ref[h] = st * chunk_dec[h] + jnp.dot(kdt, vh, preferred_element_type=F32)
        o = y * lax.rsqrt(jnp.mean(y * y, axis=-1, keepdims=True) + NORM_EPS)
        g = g_ref[:, vv].astype(F32)
        o_ref[:, vv] = (g * jax.nn.sigmoid(g) * o).astype(BF16)


def _retention(proj, bsz, nc):
    t = proj.shape[0]
    cos, sin, intra, q_dec, k_dec, chunk_dec = _ret_tables(nc * CHUNK)
    row = lambda b, c: b * nc + c
    const3 = lambda b, c: (0, 0, 0)
    return pl.pallas_call(
        functools.partial(_ret_body, chunk_dec),
        grid=(bsz, nc),
        in_specs=[
            pl.BlockSpec((CHUNK, RET_QK), lambda b, c: (row(b, c), 0)),
            pl.BlockSpec((CHUNK, RET_QK), lambda b, c: (row(b, c), 1)),
            pl.BlockSpec((CHUNK, RET_V), lambda b, c: (row(b, c), 1)),
            pl.BlockSpec((CHUNK, RET_V), lambda b, c: (row(b, c), 2)),
            pl.BlockSpec((CHUNK, RET_QK_DIM // 2), lambda b, c: (c, 0)),
            pl.BlockSpec((CHUNK, RET_QK_DIM // 2), lambda b, c: (c, 0)),
            pl.BlockSpec((RET_HEADS, CHUNK, CHUNK), const3),
            pl.BlockSpec((RET_HEADS, CHUNK, RET_V_DIM), const3),
            pl.BlockSpec((RET_HEADS, CHUNK, RET_QK_DIM), const3),
        ],
        out_specs=pl.BlockSpec((CHUNK, RET_V), lambda b, c: (row(b, c), 0)),
        out_shape=jax.ShapeDtypeStruct((t, RET_V), BF16),
        scratch_shapes=[pltpu.VMEM((RET_HEADS, RET_QK_DIM, RET_V_DIM), F32)],
        compiler_params=_params(("parallel", "arbitrary")),
        name="retention",
    )(proj, proj, proj, proj, cos, sin, intra, q_dec, k_dec)


CONV_COLS = 512


def _ssd_body(z_ref, xs_ref, bc_ref, dt_ref, cw_ref, cb_ref, dtb_ref, alog_ref, dsk_ref, nw_ref,
              o_ref, st_ref, full_ref, xc_ref, y_ref):
    c = pl.program_id(1)

    @pl.when(c == 0)
    def _():
        st_ref[...] = jnp.zeros_like(st_ref)
        full_ref[0:8, :] = jnp.zeros((8, CONV_CH), F32)

    rows = lax.broadcasted_iota(I32, (CHUNK, 1), 0)
    valid = jnp.logical_or(c > 0, rows >= META_PAD)

    full_ref[8:, :SSM_INNER] = xs_ref[...].astype(F32)
    full_ref[8:, SSM_INNER:] = bc_ref[...].astype(F32)
    for s in range(CONV_CH // CONV_COLS):
        cols = slice(s * CONV_COLS, (s + 1) * CONV_COLS)
        acc = cb_ref[:, cols]
        for k in range(CONV_WIDTH):
            off = 8 - (CONV_WIDTH - 1) + k
            acc = acc + cw_ref[k:k + 1, cols] * full_ref[off:off + CHUNK, cols]
        xc_ref[:, cols] = jnp.where(valid, acc * jax.nn.sigmoid(acc), 0.0)
    full_ref[0:8, :] = full_ref[CHUNK:CHUNK + 8, :]

    dtv = dt_ref[...] + dtb_ref[...]
    dt = jnp.maximum(dtv, 0.0) + jnp.log1p(jnp.exp(-jnp.abs(dtv)))
    dt = jnp.where(valid, dt, 0.0)
    a = dt * (-jnp.exp(alog_ref[...]))
    ri = lax.broadcasted_iota(I32, (CHUNK, CHUNK), 0)
    ci = lax.broadcasted_iota(I32, (CHUNK, CHUNK), 1)
    causal = ri >= ci
    acs = jnp.dot(causal.astype(F32), a, preferred_element_type=F32, precision=HIGHEST)
    last = acs[CHUNK - 1:CHUNK, :]
    acs_t = jnp.transpose(acs)
    dt_t = jnp.transpose(dt)
    w_t = jnp.transpose(jnp.exp(last - acs) * dt)
    e_last = jnp.exp(last)
    lanes = lax.broadcasted_iota(I32, (1, CHUNK), 1)
    low = lanes < SSM_HEAD_DIM

    for g in range(SSM_GROUPS):
        bm = xc_ref[:, SSM_INNER + g * SSM_STATE:SSM_INNER + (g + 1) * SSM_STATE]
        cm = xc_ref[:, SSM_INNER + (SSM_GROUPS + g) * SSM_STATE:SSM_INNER + (SSM_GROUPS + g + 1) * SSM_STATE]
        cb = _nt_dot(cm.astype(BF16), bm.astype(BF16))
        bm_t = jnp.transpose(bm)
        for pp in range(PAIRS_PER_GROUP):
            m = g * PAIRS_PER_GROUP + pp
            x_pair = xc_ref[:, m * CHUNK:(m + 1) * CHUNK]
            st_pair = st_ref[m]
            y_pair = jnp.zeros((CHUNK, CHUNK), F32)
            upd = jnp.zeros((SSM_STATE, CHUNK), F32)
            for hh in range(2):
                h = 2 * m + hh
                lane_mask = low if hh == 0 else jnp.logical_not(low)
                col = jnp.broadcast_to(acs[:, h:h + 1], (CHUNK, CHUNK))
                seg = col - acs_t[h:h + 1, :]
                dec = jnp.where(causal, jnp.exp(jnp.minimum(seg, 0.0)), 0.0)
                mat = cb * dec * dt_t[h:h + 1, :]
                lhs = jnp.concatenate([mat, cm * jnp.exp(col)], axis=1).astype(BF16)
                xm = jnp.where(lane_mask, x_pair, 0.0).astype(BF16)
                sm = jnp.where(lane_mask, st_pair, 0.0).astype(BF16)
                y_pair = y_pair + jnp.dot(lhs, jnp.concatenate([xm, sm], axis=0), preferred_element_type=F32)
                upd = upd + jnp.dot((bm_t * w_t[h:h + 1, :]).astype(BF16), xm, preferred_element_type=F32)
            decay = jnp.where(low, jnp.broadcast_to(e_last[:, 2 * m:2 * m + 1], (1, CHUNK)),
                              jnp.broadcast_to(e_last[:, 2 * m + 1:2 * m + 2], (1, CHUNK)))
            st_ref[m] = st_pair * decay + upd
            y_ref[:, m * CHUNK:(m + 1) * CHUNK] = y_pair

    gsz = SSM_INNER // SSM_GROUPS
    for g in range(SSM_GROUPS):
        cols = slice(g * gsz, (g + 1) * gsz)
        z = z_ref[:, cols].astype(F32)
        y = (y_ref[:, cols] + dsk_ref[:, cols] * xc_ref[:, cols]) * (z * jax.nn.sigmoid(z))
        y = y * lax.rsqrt(jnp.mean(y * y, axis=-1, keepdims=True) + NORM_EPS)
        o_ref[:, cols] = (y * nw_ref[:, cols]).astype(BF16)


def _ssd(proj, dt_raw, conv_w, conv_b, dt_bias, a_log, d_skip, norm_w, bsz, nc):
    t = proj.shape[0]
    row = lambda b, c: b * nc + c
    const2 = lambda b, c: (0, 0)
    pad = lambda v: jnp.pad(v.astype(F32), (0, DT_PAD - SSM_HEADS)).reshape(1, DT_PAD)
    return pl.pallas_call(
        _ssd_body,
        grid=(bsz, nc),
        in_specs=[
            pl.BlockSpec((CHUNK, SSM_INNER), lambda b, c: (row(b, c), 3)),
            pl.BlockSpec((CHUNK, SSM_INNER), lambda b, c: (row(b, c), 4)),
            pl.BlockSpec((CHUNK, 1024), lambda b, c: (row(b, c), 10)),
            pl.BlockSpec((CHUNK, DT_PAD), lambda b, c: (row(b, c), 0)),
            pl.BlockSpec((CONV_WIDTH, CONV_CH), const2),
            pl.BlockSpec((1, CONV_CH), const2),
            pl.BlockSpec((1, DT_PAD), const2),
            pl.BlockSpec((1, DT_PAD), const2),
            pl.BlockSpec((1, SSM_INNER), const2),
            pl.BlockSpec((1, SSM_INNER), const2),
        ],
        out_specs=pl.BlockSpec((CHUNK, SSM_INNER), lambda b, c: (row(b, c), 0)),
        out_shape=jax.ShapeDtypeStruct((t, SSM_INNER), BF16),
        scratch_shapes=[
            pltpu.VMEM((HEAD_PAIRS, SSM_STATE, CHUNK), F32),
            pltpu.VMEM((8 + CHUNK, CONV_CH), F32),
            pltpu.VMEM((CHUNK, CONV_CH), F32),
            pltpu.VMEM((CHUNK, SSM_INNER), F32),
        ],
        compiler_params=_params(("parallel", "arbitrary")),
        name="ssd",
    )(proj, proj, proj, dt_raw, conv_w.astype(F32), conv_b.astype(F32).reshape(1, CONV_CH), pad(dt_bias), pad(a_log),
      jnp.repeat(d_skip.astype(F32), SSM_HEAD_DIM).reshape(1, SSM_INNER), norm_w.astype(F32).reshape(1, SSM_INNER))


def _merge_body(oret_ref, ossd_ref, g0_ref, g1_ref, h_ref, valid_ref, wb0_ref, wb1_ref, wo_ref, nw_ref, rw_ref, rb_ref,
                h1_ref, un_ref, e_ref, w_ref):
    y_ret = jnp.dot(oret_ref[...], wb0_ref[...], preferred_element_type=F32)
    y_ssd = jnp.dot(ossd_ref[...], wb1_ref[...], preferred_element_type=F32)
    merged = (jax.nn.sigmoid(g0_ref[...].astype(F32)) * y_ret + jax.nn.sigmoid(g1_ref[...].astype(F32)) * y_ssd)
    h1 = h_ref[...] + jnp.dot(merged.astype(BF16), wo_ref[...], preferred_element_type=F32)
    h1_ref[...] = h1
    un = h1 * lax.rsqrt(jnp.mean(h1 * h1, axis=-1, keepdims=True) + NORM_EPS) * nw_ref[...]
    un_ref[...] = un

    logits = _nt_dot(rw_ref[...], un, precision=HIGHEST) + rb_ref[...]
    tm = logits.shape[1]
    eidx = lax.broadcasted_iota(I32, (N_EXPERTS, tm), 0)
    vals, ids = [], []
    for _ in range(TOP_K):
        best = jnp.max(logits, axis=0, keepdims=True)
        arg = jnp.min(jnp.where(logits == best, eidx, N_EXPERTS), axis=0, keepdims=True)
        vals.append(best)
        ids.append(arg)
        logits = jnp.where(eidx == arg, -jnp.inf, logits)
    ex = [jnp.exp(v - vals[0]) for v in vals]
    denom = ex[0] + ex[1] + ex[2] + ex[3]
    is_tok = valid_ref[...] > 0
    e_ref[...] = jnp.concatenate([jnp.where(is_tok, i, -1) for i in ids] + [jnp.zeros((8 - TOP_K, tm), I32)], axis=0)
    w_ref[...] = jnp.concatenate([x / denom for x in ex] + [jnp.zeros((8 - TOP_K, tm), F32)], axis=0)


def _merge(o_ret, o_ssd, proj, hp, valid, wb0, wb1, wo, norm_w, router_wt, router_b):
    t = hp.shape[0]
    tm = _pick(t, (256, 128))
    gate_blk = OFF_DT // D_MODEL
    assert gate_blk * D_MODEL == OFF_DT
    const2 = lambda i: (0, 0)
    return pl.pallas_call(
        _merge_body,
        grid=(t // tm,),
        in_specs=[
            pl.BlockSpec((tm, RET_V), lambda i: (i, 0)),
            pl.BlockSpec((tm, SSM_INNER), lambda i: (i, 0)),
            pl.BlockSpec((tm, D_MODEL), lambda i: (i, gate_blk)),
            pl.BlockSpec((tm, D_MODEL), lambda i: (i, gate_blk + 1)),
            pl.BlockSpec((tm, D_MODEL), lambda i: (i, 0)),
            pl.BlockSpec((1, tm), lambda i: (0, i)),
            pl.BlockSpec((RET_V, D_MODEL), const2),
            pl.BlockSpec((SSM_INNER, D_MODEL), const2),
            pl.BlockSpec((D_MODEL, D_MODEL), const2),
            pl.BlockSpec((1, D_MODEL), const2),
            pl.BlockSpec((N_EXPERTS, D_MODEL), const2),
            pl.BlockSpec((N_EXPERTS, 1), const2),
        ],
        out_specs=[
            pl.BlockSpec((tm, D_MODEL), lambda i: (i, 0)),
            pl.BlockSpec((tm, D_MODEL), lambda i: (i, 0)),
            pl.BlockSpec((8, tm), lambda i: (0, i)),
            pl.BlockSpec((8, tm), lambda i: (0, i)),
        ],
        out_shape=[
            jax.ShapeDtypeStruct((t, D_MODEL), F32),
            jax.ShapeDtypeStruct((t, D_MODEL), F32),
            jax.ShapeDtypeStruct((8, t), I32),
            jax.ShapeDtypeStruct((8, t), F32),
        ],
        compiler_params=_params(("parallel",)),
        name="merge_router",
    )(o_ret, o_ssd, proj, proj, hp, valid, wb0, wb1, wo, norm_w, router_wt, router_b)


RANK_TILE = 512


def _rank_body(e_ref, tri_ref, rank_ref, cnt_ref, run_ref):
    @pl.when(pl.program_id(0) == 0)
    def _():
        run_ref[...] = jnp.zeros_like(run_ref)

    eidx = lax.broadcasted_iota(I32, (N_EXPERTS, RANK_TILE), 0)
    ones = jnp.ones((RANK_TILE, RANK_TILE), BF16)
    run = run_ref[...]
    ranks = []
    for k in range(TOP_K):
        hit = eidx == e_ref[k:k + 1, :]
        oh = jnp.where(hit, 1.0, 0.0).astype(BF16)
        cum = jnp.dot(oh, tri_ref[...], preferred_element_type=F32)
        ranks.append(jnp.sum(jnp.where(hit, cum - 1.0 + run, 0.0), axis=0, keepdims=True))
        run = run + jnp.dot(oh, ones, preferred_element_type=F32)
    run_ref[...] = run
    rank_ref[...] = jnp.concatenate(ranks + [jnp.zeros((8 - TOP_K, RANK_TILE), F32)], axis=0).astype(I32)
    cnt_ref[...] = run[:, :128].astype(I32)


def _rank(top_e):
    t = top_e.shape[1]
    assert t % RANK_TILE == 0 or t < RANK_TILE
    tri = jnp.asarray(np.triu(np.ones((RANK_TILE, RANK_TILE), np.float32)), BF16)
    return pl.pallas_call(
        _rank_body,
        grid=(t // RANK_TILE,),
        in_specs=[
            pl.BlockSpec((8, RANK_TILE), lambda i: (0, i)),
            pl.BlockSpec((RANK_TILE, RANK_TILE), lambda i: (0, 0)),
        ],
        out_specs=[
            pl.BlockSpec((8, RANK_TILE), lambda i: (0, i)),
            pl.BlockSpec((N_EXPERTS, 128), lambda i: (0, 0)),
        ],
        out_shape=[jax.ShapeDtypeStruct((8, t), I32), jax.ShapeDtypeStruct((N_EXPERTS, 128), I32)],
        scratch_shapes=[pltpu.VMEM((N_EXPERTS, RANK_TILE), F32)],
        compiler_params=_params(("arbitrary",)),
        name="rank",
    )(top_e, tri)


def _first_row(nc, step):
    return jnp.where(step % nc == 0, META_PAD, 0)


def _dispatch_body(nc, pstart_ref, e_ref, r_ref, un_ref, xs_in_ref, xs_ref, sem):
    del xs_in_ref
    lo = _first_row(nc, pl.program_id(0))

    def row_copy(t, dst):
        return pltpu.make_async_copy(un_ref.at[pl.ds(t, 1), :], xs_ref.at[pl.ds(dst, 1), :], sem)

    def issue(t, carry):
        for k in range(TOP_K):
            row_copy(t, pstart_ref[e_ref[k, t]] + r_ref[k, t]).start()
        return carry

    def drain(t, carry):
        for k in range(TOP_K):
            row_copy(t, 0).wait()
        return carry

    lax.fori_loop(lo, CHUNK, issue, 0)
    lax.fori_loop(lo, CHUNK, drain, 0)


def _dispatch(pstart, top_e, rank, un, n_rows, nc):
    t = un.shape[0]
    xs0 = jnp.zeros((n_rows, D_MODEL), F32)
    smem_blk = pl.BlockSpec((8, CHUNK), lambda i, ps: (0, i), memory_space=pltpu.SMEM)
    return pl.pallas_call(
        functools.partial(_dispatch_body, nc),
        grid_spec=pltpu.PrefetchScalarGridSpec(
            num_scalar_prefetch=1,
            grid=(t // CHUNK,),
            in_specs=[
                smem_blk,
                smem_blk,
                pl.BlockSpec((CHUNK, D_MODEL), lambda i, ps: (i, 0)),
                pl.BlockSpec(memory_space=pl.ANY),
            ],
            out_specs=pl.BlockSpec(memory_space=pl.ANY),
            scratch_shapes=[pltpu.SemaphoreType.DMA(())],
        ),
        out_shape=jax.ShapeDtypeStruct((n_rows, D_MODEL), F32),
        input_output_aliases={4: 0},
        compiler_params=_params(("arbitrary",)),
        name="dispatch",
    )(pstart, top_e, rank, un, xs0)


GU_SLAB = 256
GU_HALF = GU_SLAB // 2


def _regroup_body(w_ref, p_ref, o_ref):
    for s in range(2 * EXPERT_FF // GU_SLAB):
        cols = slice(s * GU_SLAB, (s + 1) * GU_SLAB)
        o_ref[0, :, cols] = jnp.dot(w_ref[0, :, cols].astype(BF16), p_ref[...], preferred_element_type=F32).astype(BF16)


def _regroup_gate_up(w_gate_up):
    perm = np.zeros((GU_SLAB, GU_SLAB), np.float32)
    perm[2 * np.arange(GU_HALF), np.arange(GU_HALF)] = 1.0
    perm[2 * np.arange(GU_HALF) + 1, GU_HALF + np.arange(GU_HALF)] = 1.0
    blk = (1, D_MODEL, 2 * EXPERT_FF)
    return pl.pallas_call(
        _regroup_body,
        grid=(N_EXPERTS,),
        in_specs=[pl.BlockSpec(blk, lambda e: (e, 0, 0)), pl.BlockSpec((GU_SLAB, GU_SLAB), lambda e: (0, 0))],
        out_specs=pl.BlockSpec(blk, lambda e: (e, 0, 0)),
        out_shape=jax.ShapeDtypeStruct((N_EXPERTS, D_MODEL, 2 * EXPERT_FF), BF16),
        compiler_params=_params(("parallel",)),
        name="regroup_gate_up",
    )(w_gate_up, jnp.asarray(perm, BF16))


def _regroup_bias(b_gate_up):
    b = b_gate_up.astype(F32).reshape(N_EXPERTS, 2 * EXPERT_FF // GU_SLAB, GU_HALF, 2)
    return jnp.transpose(b, (0, 1, 3, 2)).reshape(N_EXPERTS, 1, 2 * EXPERT_FF)


def _expert_body(be_ref, nu_ref, x_ref, wgu_ref, wd_ref, bgu_ref, bd_ref, y_ref):
    del be_ref

    @pl.when(pl.program_id(0) < nu_ref[0])
    def _():
        x = x_ref[...].astype(BF16)
        gu = jnp.dot(x, wgu_ref[0], preferred_element_type=F32) + bgu_ref[0]
        acts = []
        for s in range(2 * EXPERT_FF // GU_SLAB):
            gate = jnp.minimum(gu[:, s * GU_SLAB:s * GU_SLAB + GU_HALF], SWIGLU_LIMIT)
            up = jnp.clip(gu[:, s * GU_SLAB + GU_HALF:(s + 1) * GU_SLAB], -SWIGLU_LIMIT, SWIGLU_LIMIT)
            acts.append(((up + 1.0) * (gate * jax.nn.sigmoid(SWIGLU_ALPHA * gate))).astype(BF16))
        act = jnp.concatenate(acts, axis=1)
        y_ref[...] = jnp.dot(act, wd_ref[0], preferred_element_type=F32) + bd_ref[0]


def _experts(block_e, n_used, xs, wgu, wd, bgu, bd):
    n_rows = xs.shape[0]
    n_blocks = n_rows // MOE_BLOCK
    wspec = lambda shape: pl.BlockSpec((1,) + shape, lambda i, be, nu: (be[i], 0, 0))
    return pl.pallas_call(
        _expert_body,
        grid_spec=pltpu.PrefetchScalarGridSpec(
            num_scalar_prefetch=2,
            grid=(n_blocks,),
            in_specs=[
                pl.BlockSpec((MOE_BLOCK, D_MODEL), lambda i, be, nu: (i, 0)),
                wspec((D_MODEL, 2 * EXPERT_FF)),
                wspec((EXPERT_FF, D_MODEL)),
                wspec((1, 2 * EXPERT_FF)),
                wspec((1, D_MODEL)),
            ],
            out_specs=pl.BlockSpec((MOE_BLOCK, D_MODEL), lambda i, be, nu: (i, 0)),
        ),
        out_shape=jax.ShapeDtypeStruct((n_rows, D_MODEL), F32),
        compiler_params=_params(("arbitrary",)),
        name="experts",
    )(block_e, n_used, xs, wgu, wd, bgu, bd)


def _combine_body(pstart_ref, e_ref, r_ref, w_ref, h1_ref, nw_ref, y_hbm, o_ref, ybuf, sem):
    def row_copy(k, t, src):
        return pltpu.make_async_copy(y_hbm.at[pl.ds(src, 1), :], ybuf.at[k, pl.ds(t, 1), :], sem)

    def issue(t, carry):
        for k in range(TOP_K):
            row_copy(k, t, pstart_ref[e_ref[k, t]] + r_ref[k, t]).start()
        return carry

    def drain(t, carry):
        for k in range(TOP_K):
            row_copy(k, t, 0).wait()
        return carry

    lax.fori_loop(0, CHUNK, issue, 0)
    lax.fori_loop(0, CHUNK, drain, 0)

    ri = lax.broadcasted_iota(I32, (CHUNK, CHUNK), 0)
    ci = lax.broadcasted_iota(I32, (CHUNK, CHUNK), 1)
    wcol = _nt_dot((ri == ci).astype(F32), w_ref[...], precision=HIGHEST)
    f = wcol[:, 0:1] * ybuf[0]
    for k in range(1, TOP_K):
        f = f + wcol[:, k:k + 1] * ybuf[k]
    h2 = h1_ref[...] + f
    o_ref[...] = h2 * lax.rsqrt(jnp.mean(h2 * h2, axis=-1, keepdims=True) + NORM_EPS) * nw_ref[...]


def _combine(pstart, top_e, rank, top_w, h1, norm_w, y_sorted, bsz, nc):
    seq_chunks = nc - 1
    src = lambda b, c: b * nc + c + 1
    smem_blk = pl.BlockSpec((8, CHUNK), lambda b, c, ps: (0, src(b, c)), memory_space=pltpu.SMEM)
    return pl.pallas_call(
        _combine_body,
        grid_spec=pltpu.PrefetchScalarGridSpec(
            num_scalar_prefetch=1,
            grid=(bsz, seq_chunks),
            in_specs=[
                smem_blk,
                smem_blk,
                pl.BlockSpec((8, CHUNK), lambda b, c, ps: (0, src(b, c))),
                pl.BlockSpec((CHUNK, D_MODEL), lambda b, c, ps: (src(b, c), 0)),
                pl.BlockSpec((1, D_MODEL), lambda b, c, ps: (0, 0)),
                pl.BlockSpec(memory_space=pl.ANY),
            ],
            out_specs=pl.BlockSpec((CHUNK, D_MODEL), lambda b, c, ps: (b * seq_chunks + c, 0)),
            scratch_shapes=[pltpu.VMEM((TOP_K, CHUNK, D_MODEL), F32), pltpu.SemaphoreType.DMA(())],
        ),
        out_shape=jax.ShapeDtypeStruct((bsz * seq_chunks * CHUNK, D_MODEL), F32),
        compiler_params=_params(("arbitrary", "arbitrary")),
        name="combine",
    )(pstart, top_e, rank, top_w, h1, norm_w, y_sorted)


def kernel(x, meta_tokens, norm_mix, w_in, conv_w, conv_b, dt_bias, a_log, d_skip, ssm_norm, w_branch, w_out, norm_ffn,
           router_w, router_b, w_gate_up, b_gate_up, w_down, b_down, norm_final):
    bsz, seq, _ = x.shape
    assert seq % CHUNK == 0 and norm_mix.shape[0] == 1
    nc = 1 + seq // CHUNK
    lp = nc * CHUNK
    t = bsz * lp

    hp = jnp.concatenate([jnp.zeros((bsz, META_PAD, D_MODEL), x.dtype),
                          jnp.broadcast_to(meta_tokens.astype(x.dtype)[None], (bsz, N_META, D_MODEL)), x], axis=1)
    hp = hp.reshape(t, D_MODEL)
    valid = jnp.asarray(np.tile(np.arange(lp) >= META_PAD, bsz).astype(np.int32).reshape(1, t))

    w_in0 = w_in[0]
    w_main = jnp.concatenate([w_in0[:, :OFF_DT], w_in0[:, OFF_DT + SSM_HEADS:]], axis=1).astype(BF16)
    w_dt = jnp.pad(w_in0[:, OFF_DT:OFF_DT + SSM_HEADS], ((0, 0), (0, DT_PAD - SSM_HEADS))).astype(BF16)
    wgu = _regroup_gate_up(w_gate_up[0])
    bgu = _regroup_bias(b_gate_up[0])
    wd = w_down[0].astype(BF16)
    bd = b_down[0][:, None, :].astype(F32)

    proj, dt_raw = _in_proj(hp, norm_mix[0].reshape(1, D_MODEL).astype(F32), w_main, w_dt)
    o_ret = _retention(proj, bsz, nc)
    o_ssd = _ssd(proj, dt_raw, conv_w[0], conv_b[0], dt_bias[0], a_log[0], d_skip[0], ssm_norm[0], bsz, nc)
    h1, un, top_e, top_w = _merge(
        o_ret, o_ssd, proj, hp, valid, w_branch[0, 0].astype(BF16), w_branch[0, 1].astype(BF16), w_out[0].astype(BF16),
        norm_ffn[0].reshape(1, D_MODEL).astype(F32), jnp.transpose(router_w[0]).astype(F32),
        router_b[0].reshape(N_EXPERTS, 1).astype(F32))

    rank, counts = _rank(top_e)
    counts = counts[:, 0]
    padded = (counts + MOE_BLOCK - 1) // MOE_BLOCK * MOE_BLOCK
    pend = jnp.cumsum(padded)
    pstart = (pend - padded).astype(I32)
    n_pairs = bsz * (N_META + seq) * TOP_K
    n_blocks = -(-(n_pairs + N_EXPERTS * (MOE_BLOCK - 1)) // MOE_BLOCK)
    blk_row = jnp.arange(n_blocks, dtype=I32) * MOE_BLOCK
    block_e = jnp.minimum(jnp.sum(pend[None, :] <= blk_row[:, None], axis=1), N_EXPERTS - 1).astype(I32)
    n_used = (pend[-1:] // MOE_BLOCK).astype(I32)

    xs = _dispatch(pstart, top_e, rank, un, n_blocks * MOE_BLOCK, nc)
    ys = _experts(block_e, n_used, xs, wgu, wd, bgu, bd)
    out = _combine(pstart, top_e, rank, top_w, h1, norm_final.reshape(1, D_MODEL).astype(F32), ys, bsz, nc)
    return out.reshape(bsz, seq, D_MODEL)
```

```python
import functools

import numpy as np
import jax
import jax.numpy as jnp
from jax import lax
from jax.experimental import pallas as pl
from jax.experimental.pallas import tpu as pltpu

F32 = jnp.float32
BF16 = jnp.bfloat16
I32 = jnp.int32

D_MODEL = 1024
N_META = 16
CHUNK = 128
META_PAD = CHUNK - N_META
NORM_EPS = 1e-6
RET_HEADS = 4
RET_QK_DIM = 256
RET_V_DIM = 512
RET_QK = RET_HEADS * RET_QK_DIM
RET_V = RET_HEADS * RET_V_DIM
ROPE_BASE = 10000.0
SSM_INNER = 2048
SSM_HEAD_DIM = 64
SSM_HEADS = 32
SSM_GROUPS = 4
SSM_STATE = 128
CONV_WIDTH = 4
CONV_CH = SSM_INNER + 2 * SSM_GROUPS * SSM_STATE
HEAD_PAIRS = SSM_HEADS // 2
PAIRS_PER_GROUP = HEAD_PAIRS // SSM_GROUPS
OFF_DT = 2 * RET_QK + 2 * RET_V + SSM_INNER + CONV_CH
N_MAIN = OFF_DT + 2 * D_MODEL
DT_PAD = 128
N_EXPERTS = 32
TOP_K = 4
EXPERT_FF = 1024
SWIGLU_ALPHA = 1.702
SWIGLU_LIMIT = 7.0
MOE_BLOCK = 256

VMEM_LIMIT = 56 * 1024 * 1024
HIGHEST = lax.Precision.HIGHEST


def _params(sem):
    return pltpu.CompilerParams(dimension_semantics=sem, vmem_limit_bytes=VMEM_LIMIT)


def _pick(n, candidates):
    for c in candidates:
        if n % c == 0:
            return c
    raise ValueError(f"no tile for {n} among {candidates}")


def _nt_dot(a, b, **kw):
    return lax.dot_general(a, b, (((1,), (1,)), ((), ())), preferred_element_type=F32, **kw)


def _inproj_body(x_ref, nw_ref, w_ref, wdt_ref, o_ref, dt_ref, xn_ref):
    @pl.when(pl.program_id(1) == 0)
    def _():
        x = x_ref[...]
        ms = jnp.mean(x * x, axis=-1, keepdims=True)
        xn = (x * lax.rsqrt(ms + NORM_EPS) * nw_ref[...]).astype(BF16)
        xn_ref[...] = xn
        dt_ref[...] = jnp.dot(xn, wdt_ref[...], preferred_element_type=F32)

    o_ref[...] = jnp.dot(xn_ref[...], w_ref[...], preferred_element_type=F32).astype(BF16)


def _in_proj(hp, norm_w, w_main, w_dt):
    t = hp.shape[0]
    tm = _pick(t, (1024, 512, 256, 128))
    tn = 1024
    return pl.pallas_call(
        _inproj_body,
        grid=(t // tm, N_MAIN // tn),
        in_specs=[
            pl.BlockSpec((tm, D_MODEL), lambda i, j: (i, 0)),
            pl.BlockSpec((1, D_MODEL), lambda i, j: (0, 0)),
            pl.BlockSpec((D_MODEL, tn), lambda i, j: (0, j)),
            pl.BlockSpec((D_MODEL, DT_PAD), lambda i, j: (0, 0)),
        ],
        out_specs=[
            pl.BlockSpec((tm, tn), lambda i, j: (i, j)),
            pl.BlockSpec((tm, DT_PAD), lambda i, j: (i, 0)),
        ],
        out_shape=[jax.ShapeDtypeStruct((t, N_MAIN), BF16), jax.ShapeDtypeStruct((t, DT_PAD), F32)],
        scratch_shapes=[pltpu.VMEM((tm, D_MODEL), BF16)],
        compiler_params=_params(("parallel", "arbitrary")),
        name="in_proj",
    )(hp, norm_w, w_main, w_dt)


def _ret_tables(lp):
    half = RET_QK_DIM // 2
    inv_freq = ROPE_BASE ** (-np.arange(half, dtype=np.float64) / half)
    pos = np.arange(lp, dtype=np.float64) - META_PAD
    ang = pos[:, None] * inv_freq[None, :]
    log_gamma = np.log(1.0 - 2.0 ** (-5.0 - np.arange(RET_HEADS, dtype=np.float64)))
    idx = np.arange(CHUNK, dtype=np.float64)
    dist = idx[:, None] - idx[None, :]
    intra = np.where(dist >= 0, np.exp(log_gamma[:, None, None] * np.maximum(dist, 0.0)[None]), 0.0)
    q_dec = np.exp(log_gamma[:, None] * (idx[None, :] + 1.0))
    k_dec = np.exp(log_gamma[:, None] * (CHUNK - 1.0 - idx[None, :]))
    q_dec = np.broadcast_to(q_dec[:, :, None], (RET_HEADS, CHUNK, RET_V_DIM))
    k_dec = np.broadcast_to(k_dec[:, :, None], (RET_HEADS, CHUNK, RET_QK_DIM))
    chunk_dec = tuple(float(v) for v in np.exp(log_gamma * CHUNK))
    as32 = lambda a: jnp.asarray(np.ascontiguousarray(a), F32)
    return as32(np.cos(ang)), as32(np.sin(ang)), as32(intra), as32(q_dec), as32(k_dec), chunk_dec


def _ret_body(chunk_dec, q_ref, k_ref, v_ref, g_ref, cos_ref, sin_ref, intra_ref, qd_ref, kd_ref, o_ref, st_ref):
    @pl.when(pl.program_id(1) == 0)
    def _():
        st_ref[...] = jnp.zeros_like(st_ref)

    cos = cos_ref[...]
    sin = sin_ref[...]
    half = RET_QK_DIM // 2

    def rotary(t):
        t1, t2 = t[:, :half], t[:, half:]
        return jnp.concatenate([t1 * cos - t2 * sin, t2 * cos + t1 * sin], axis=1)

    for h in range(RET_HEADS):
        qk = slice(h * RET_QK_DIM, (h + 1) * RET_QK_DIM)
        vv = slice(h * RET_V_DIM, (h + 1) * RET_V_DIM)
        qr = rotary(q_ref[:, qk].astype(F32))
        kr = rotary(k_ref[:, qk].astype(F32)) * (RET_QK_DIM ** -0.5)
        qb = qr.astype(BF16)
        vh = v_ref[:, vv]
        st = st_ref[h]
        s = _nt_dot(qb, kr.astype(BF16)) * intra_ref[h]
        y = jnp.dot(s.astype(BF16), vh, preferred_element_type=F32)
        y = y + jnp.dot(qb, st.astype(BF16), preferred_element_type=F32) * qd_ref[h]
        kdt = jnp.transpose(kr * kd_ref[h]).astype(BF16)
        st_ref[h] = st * chunk_dec[h] + jnp.dot(kdt, vh, preferred_element_type=F32)
        o = y * lax.rsqrt(jnp.mean(y * y, axis=-1, keepdims=True) + NORM_EPS)
        g = g_ref[:, vv].astype(F32)
        o_ref[:, vv] = (g * jax.nn.sigmoid(g) * o).astype(BF16)


def _retention(proj, bsz, nc):
    t = proj.shape[0]
    cos, sin, intra, q_dec, k_dec, chunk_dec = _ret_tables(nc * CHUNK)
    row = lambda b, c: b * nc + c
    const3 = lambda b, c: (0, 0, 0)
    return pl.pallas_call(
        functools.partial(_ret_body, chunk_dec),
        grid=(bsz, nc),
        in_specs=[
            pl.BlockSpec((CHUNK, RET_QK), lambda b, c: (row(b, c), 0)),
            pl.BlockSpec((CHUNK, RET_QK), lambda b, c: (row(b, c), 1)),
            pl.BlockSpec((CHUNK, RET_V), lambda b, c: (row(b, c), 1)),
            pl.BlockSpec((CHUNK, RET_V), lambda b, c: (row(b, c), 2)),
            pl.BlockSpec((CHUNK, RET_QK_DIM // 2), lambda b, c: (c, 0)),
            pl.BlockSpec((CHUNK, RET_QK_DIM // 2), lambda b, c: (c, 0)),
            pl.BlockSpec((RET_HEADS, CHUNK, CHUNK), const3),
            pl.BlockSpec((RET_HEADS, CHUNK, RET_V_DIM), const3),
            pl.BlockSpec((RET_HEADS, CHUNK, RET_QK_DIM), const3),
        ],
        out_specs=pl.BlockSpec((CHUNK, RET_V), lambda b, c: (row(b, c), 0)),
        out_shape=jax.ShapeDtypeStruct((t, RET_V), BF16),
        scratch_shapes=[pltpu.VMEM((RET_HEADS, RET_QK_DIM, RET_V_DIM), F32)],
        compiler_params=_params(("parallel", "arbitrary")),
        name="retention",
    )(proj, proj, proj, proj, cos, sin, intra, q_dec, k_dec)


CONV_COLS = 512


def _ssd_body(z_ref, xs_ref, bc_ref, dt_ref, cw_ref, cb_ref, dtb_ref, alog_ref, dsk_ref, nw_ref,
              o_ref, st_ref, full_ref, xc_ref, y_ref):
    c = pl.program_id(1)

    @pl.when(c == 0)
    def _():
        st_ref[...] = jnp.zeros_like(st_ref)
        full_ref[0:8, :] = jnp.zeros((8, CONV_CH), F32)

    rows = lax.broadcasted_iota(I32, (CHUNK, 1), 0)
    valid = jnp.logical_or(c > 0, rows >= META_PAD)

    full_ref[8:, :SSM_INNER] = xs_ref[...].astype(F32)
    full_ref[8:, SSM_INNER:] = bc_ref[...].astype(F32)
    for s in range(CONV_CH // CONV_COLS):
        cols = slice(s * CONV_COLS, (s + 1) * CONV_COLS)
        acc = cb_ref[:, cols]
        for k in range(CONV_WIDTH):
            off = 8 - (CONV_WIDTH - 1) + k
            acc = acc + cw_ref[k:k + 1, cols] * full_ref[off:off + CHUNK, cols]
        xc_ref[:, cols] = jnp.where(valid, acc * jax.nn.sigmoid(acc), 0.0)
    full_ref[0:8, :] = full_ref[CHUNK:CHUNK + 8, :]

    dtv = dt_ref[...] + dtb_ref[...]
    dt = jnp.maximum(dtv, 0.0) + jnp.log1p(jnp.exp(-jnp.abs(dtv)))
    dt = jnp.where(valid, dt, 0.0)
    a = dt * (-jnp.exp(alog_ref[...]))
    ri = lax.broadcasted_iota(I32, (CHUNK, CHUNK), 0)
    ci = lax.broadcasted_iota(I32, (CHUNK, CHUNK), 1)
    causal = ri >= ci
    acs = jnp.dot(causal.astype(F32), a, preferred_element_type=F32, precision=HIGHEST)
    last = acs[CHUNK - 1:CHUNK, :]
    acs_t = jnp.transpose(acs)
    dt_t = jnp.transpose(dt)
    w_t = jnp.transpose(jnp.exp(last - acs) * dt)
    e_last = jnp.exp(last)
    lanes = lax.broadcasted_iota(I32, (1, CHUNK), 1)
    low = lanes < SSM_HEAD_DIM

    for g in range(SSM_GROUPS):
        bm = xc_ref[:, SSM_INNER + g * SSM_STATE:SSM_INNER + (g + 1) * SSM_STATE]
        cm = xc_ref[:, SSM_INNER + (SSM_GROUPS + g) * SSM_STATE:SSM_INNER + (SSM_GROUPS + g + 1) * SSM_STATE]
        cb = _nt_dot(cm.astype(BF16), bm.astype(BF16))
        bm_t = jnp.transpose(bm)
        for pp in range(PAIRS_PER_GROUP):
            m = g * PAIRS_PER_GROUP + pp
            x_pair = xc_ref[:, m * CHUNK:(m + 1) * CHUNK]
            st_pair = st_ref[m]
            y_pair = jnp.zeros((CHUNK, CHUNK), F32)
            upd = jnp.zeros((SSM_STATE, CHUNK), F32)
            for hh in range(2):
                h = 2 * m + hh
                lane_mask = low if hh == 0 else jnp.logical_not(low)
                col = jnp.broadcast_to(acs[:, h:h + 1], (CHUNK, CHUNK))
                seg = col - acs_t[h:h + 1, :]
                dec = jnp.where(causal, jnp.exp(jnp.minimum(seg, 0.0)), 0.0)
                mat = cb * dec * dt_t[h:h + 1, :]
                lhs = jnp.concatenate([mat, cm * jnp.exp(col)], axis=1).astype(BF16)
                xm = jnp.where(lane_mask, x_pair, 0.0).astype(BF16)
                sm = jnp.where(lane_mask, st_pair, 0.0).astype(BF16)
                y_pair = y_pair + jnp.dot(lhs, jnp.concatenate([xm, sm], axis=0), preferred_element_type=F32)
                upd = upd + jnp.dot((bm_t * w_t[h:h + 1, :]).astype(BF16), xm, preferred_element_type=F32)
            decay = jnp.where(low, jnp.broadcast_to(e_last[:, 2 * m:2 * m + 1], (1, CHUNK)),
                              jnp.broadcast_to(e_last[:, 2 * m + 1:2 * m + 2], (1, CHUNK)))
            st_ref[m] = st_pair * decay + upd
            y_ref[:, m * CHUNK:(m + 1) * CHUNK] = y_pair

    gsz = SSM_INNER // SSM_GROUPS
    for g in range(SSM_GROUPS):
        cols = slice(g * gsz, (g + 1) * gsz)
        z = z_ref[:, cols].astype(F32)
        y = (y_ref[:, cols] + dsk_ref[:, cols] * xc_ref[:, cols]) * (z * jax.nn.sigmoid(z))
        y = y * lax.rsqrt(jnp.mean(y * y, axis=-1, keepdims=True) + NORM_EPS)
        o_ref[:, cols] = (y * nw_ref[:, cols]).astype(BF16)


def _ssd(proj, dt_raw, conv_w, conv_b, dt_bias, a_log, d_skip, norm_w, bsz, nc):
    t = proj.shape[0]
    row = lambda b, c: b * nc + c
    const2 = lambda b, c: (0, 0)
    pad = lambda v: jnp.pad(v.astype(F32), (0, DT_PAD - SSM_HEADS)).reshape(1, DT_PAD)
    return pl.pallas_call(
        _ssd_body,
        grid=(bsz, nc),
        in_specs=[
            pl.BlockSpec((CHUNK, SSM_INNER), lambda b, c: (row(b, c), 3)),
            pl.BlockSpec((CHUNK, SSM_INNER), lambda b, c: (row(b, c), 4)),
            pl.BlockSpec((CHUNK, 1024), lambda b, c: (row(b, c), 10)),
            pl.BlockSpec((CHUNK, DT_PAD), lambda b, c: (row(b, c), 0)),
            pl.BlockSpec((CONV_WIDTH, CONV_CH), const2),
            pl.BlockSpec((1, CONV_CH), const2),
            pl.BlockSpec((1, DT_PAD), const2),
            pl.BlockSpec((1, DT_PAD), const2),
            pl.BlockSpec((1, SSM_INNER), const2),
            pl.BlockSpec((1, SSM_INNER), const2),
        ],
        out_specs=pl.BlockSpec((CHUNK, SSM_INNER), lambda b, c: (row(b, c), 0)),
        out_shape=jax.ShapeDtypeStruct((t, SSM_INNER), BF16),
        scratch_shapes=[
            pltpu.VMEM((HEAD_PAIRS, SSM_STATE, CHUNK), F32),
            pltpu.VMEM((8 + CHUNK, CONV_CH), F32),
            pltpu.VMEM((CHUNK, CONV_CH), F32),
            pltpu.VMEM((CHUNK, SSM_INNER), F32),
        ],
        compiler_params=_params(("parallel", "arbitrary")),
        name="ssd",
    )(proj, proj, proj, dt_raw, conv_w.astype(F32), conv_b.astype(F32).reshape(1, CONV_CH), pad(dt_bias), pad(a_log),
      jnp.repeat(d_skip.astype(F32), SSM_HEAD_DIM).reshape(1, SSM_INNER), norm_w.astype(F32).reshape(1, SSM_INNER))


def _merge_body(oret_ref, ossd_ref, g0_ref, g1_ref, h_ref, valid_ref, wb0_ref, wb1_ref, wo_ref, nw_ref, rw_ref, rb_ref,
                h1_ref, un_ref, e_ref, w_ref):
    y_ret = jnp.dot(oret_ref[...], wb0_ref[...], preferred_element_type=F32)
    y_ssd = jnp.dot(ossd_ref[...], wb1_ref[...], preferred_element_type=F32)
    merged = (jax.nn.sigmoid(g0_ref[...].astype(F32)) * y_ret + jax.nn.sigmoid(g1_ref[...].astype(F32)) * y_ssd)
    h1 = h_ref[...] + jnp.dot(merged.astype(BF16), wo_ref[...], preferred_element_type=F32)
    h1_ref[...] = h1
    un = h1 * lax.rsqrt(jnp.mean(h1 * h1, axis=-1, keepdims=True) + NORM_EPS) * nw_ref[...]
    un_ref[...] = un.astype(BF16)

    logits = _nt_dot(rw_ref[...], un, precision=HIGHEST) + rb_ref[...]
    tm = logits.shape[1]
    eidx = lax.broadcasted_iota(I32, (N_EXPERTS, tm), 0)
    vals, ids = [], []
    for _ in range(TOP_K):
        best = jnp.max(logits, axis=0, keepdims=True)
        arg = jnp.min(jnp.where(logits == best, eidx, N_EXPERTS), axis=0, keepdims=True)
        vals.append(best)
        ids.append(arg)
        logits = jnp.where(eidx == arg, -jnp.inf, logits)
    ex = [jnp.exp(v - vals[0]) for v in vals]
    denom = ex[0] + ex[1] + ex[2] + ex[3]
    is_tok = valid_ref[...] > 0
    e_ref[...] = jnp.concatenate([jnp.where(is_tok, i, -1) for i in ids] + [jnp.zeros((8 - TOP_K, tm), I32)], axis=0)
    w_ref[...] = jnp.concatenate([x / denom for x in ex] + [jnp.zeros((8 - TOP_K, tm), F32)], axis=0)


def _merge(o_ret, o_ssd, proj, hp, valid, wb0, wb1, wo, norm_w, router_wt, router_b):
    t = hp.shape[0]
    tm = _pick(t, (256, 128))
    gate_blk = OFF_DT // D_MODEL
    assert gate_blk * D_MODEL == OFF_DT
    const2 = lambda i: (0, 0)
    return pl.pallas_call(
        _merge_body,
        grid=(t // tm,),
        in_specs=[
            pl.BlockSpec((tm, RET_V), lambda i: (i, 0)),
            pl.BlockSpec((tm, SSM_INNER), lambda i: (i, 0)),
            pl.BlockSpec((tm, D_MODEL), lambda i: (i, gate_blk)),
            pl.BlockSpec((tm, D_MODEL), lambda i: (i, gate_blk + 1)),
            pl.BlockSpec((tm, D_MODEL), lambda i: (i, 0)),
            pl.BlockSpec((1, tm), lambda i: (0, i)),
            pl.BlockSpec((RET_V, D_MODEL), const2),
            pl.BlockSpec((SSM_INNER, D_MODEL), const2),
            pl.BlockSpec((D_MODEL, D_MODEL), const2),
            pl.BlockSpec((1, D_MODEL), const2),
            pl.BlockSpec((N_EXPERTS, D_MODEL), const2),
            pl.BlockSpec((N_EXPERTS, 1), const2),
        ],
        out_specs=[
            pl.BlockSpec((tm, D_MODEL), lambda i: (i, 0)),
            pl.BlockSpec((tm, D_MODEL), lambda i: (i, 0)),
            pl.BlockSpec((8, tm), lambda i: (0, i)),
            pl.BlockSpec((8, tm), lambda i: (0, i)),
        ],
        out_shape=[
            jax.ShapeDtypeStruct((t, D_MODEL), F32),
            jax.ShapeDtypeStruct((t, D_MODEL), BF16),
            jax.ShapeDtypeStruct((8, t), I32),
            jax.ShapeDtypeStruct((8, t), F32),
        ],
        compiler_params=_params(("parallel",)),
        name="merge_router",
    )(o_ret, o_ssd, proj, proj, hp, valid, wb0, wb1, wo, norm_w, router_wt, router_b)


ROW_ALIGN = 8
PIECE = 16
LOCAL_ROWS = 1024
META_BEFORE, META_COUNT, META_LSTART = 0, 1, 2


def _rank_body(e_ref, slot_ref, meta_ref, cnt_ref, run_ref):
    @pl.when(pl.program_id(0) == 0)
    def _():
        run_ref[...] = jnp.zeros_like(run_ref)

    eidx = lax.broadcasted_iota(I32, (N_EXPERTS, CHUNK), 0)
    lane = lax.broadcasted_iota(I32, (N_EXPERTS, CHUNK), 1)
    ri = lax.broadcasted_iota(I32, (CHUNK, CHUNK), 0)
    ci = lax.broadcasted_iota(I32, (CHUNK, CHUNK), 1)
    upper = jnp.where(ri <= ci, 1.0, 0.0).astype(BF16)
    ones = jnp.ones((CHUNK, CHUNK), BF16)
    seen = jnp.zeros((N_EXPERTS, CHUNK), F32)
    hits, ranks = [], []
    for k in range(TOP_K):
        hit = eidx == e_ref[k:k + 1, :]
        oh = jnp.where(hit, 1.0, 0.0).astype(BF16)
        ranks.append(seen + jnp.dot(oh, upper, preferred_element_type=F32) - 1.0)
        seen = seen + jnp.dot(oh, ones, preferred_element_type=F32)
        hits.append(hit)
    count = seen
    round_up = lambda v, m: jnp.floor((v + (m - 1)) * (1.0 / m)) * m
    region = round_up(count, ROW_ALIGN)
    ei = lax.broadcasted_iota(I32, (N_EXPERTS, N_EXPERTS), 0)
    ej = lax.broadcasted_iota(I32, (N_EXPERTS, N_EXPERTS), 1)
    lstart = jnp.dot(jnp.where(ej < ei, 1.0, 0.0).astype(BF16), round_up(count, PIECE).astype(BF16),
                     preferred_element_type=F32)
    slots = []
    for k in range(TOP_K):
        s = jnp.sum(jnp.where(hits[k], lstart + ranks[k], 0.0), axis=0, keepdims=True)
        slots.append(jnp.where(e_ref[k:k + 1, :] >= 0, s, -1.0))
    slot_ref[...] = jnp.concatenate(slots + [jnp.full((8 - TOP_K, CHUNK), -1.0, F32)], axis=0).astype(I32)

    run = run_ref[...]
    diag = lambda m: jnp.sum(jnp.where(eidx == lane, m, 0.0), axis=0, keepdims=True)
    meta_ref[0] = jnp.concatenate([diag(run), diag(count), diag(lstart), jnp.zeros((5, CHUNK), F32)], axis=0).astype(I32)
    run_ref[...] = run + region
    cnt_ref[...] = (run + region).astype(I32)


def _rank(top_e):
    t = top_e.shape[1]
    n_tiles = t // CHUNK
    return pl.pallas_call(
        _rank_body,
        grid=(n_tiles,),
        in_specs=[pl.BlockSpec((8, CHUNK), lambda i: (0, i))],
        out_specs=[
            pl.BlockSpec((8, CHUNK), lambda i: (0, i)),
            pl.BlockSpec((1, 8, CHUNK), lambda i: (i, 0, 0)),
            pl.BlockSpec((N_EXPERTS, CHUNK), lambda i: (0, 0)),
        ],
        out_shape=[
            jax.ShapeDtypeStruct((8, t), I32),
            jax.ShapeDtypeStruct((n_tiles, 8, CHUNK), I32),
            jax.ShapeDtypeStruct((N_EXPERTS, CHUNK), I32),
        ],
        scratch_shapes=[pltpu.VMEM((N_EXPERTS, CHUNK), F32)],
        compiler_params=_params(("arbitrary",)),
        name="rank",
    )(top_e)


def _pack_halves(v):
    bits = pltpu.bitcast(v, jnp.uint32)
    return (bits[:, :D_MODEL // 2] >> 16) | bits[:, D_MODEL // 2:]


def _unpack_halves(p):
    lo = pltpu.bitcast(p << 16, F32)
    hi = pltpu.bitcast(p & jnp.uint32(0xFFFF0000), F32)
    return jnp.concatenate([lo, hi], axis=1).astype(BF16)


def _piece_copies(pstart_ref, meta_ref, make_copy, act):
    def per_expert(e, carry):
        n_pieces = lax.shift_right_logical(meta_ref[0, META_COUNT, e] + (PIECE - 1), PIECE.bit_length() - 1)
        local0 = meta_ref[0, META_LSTART, e]
        sorted0 = pstart_ref[e] + meta_ref[0, META_BEFORE, e]

        def per_piece(j, c):
            act(make_copy(pl.multiple_of(local0 + j * PIECE, ROW_ALIGN), pl.multiple_of(sorted0 + j * PIECE, ROW_ALIGN)))
            return c

        return lax.fori_loop(0, n_pieces, per_piece, carry)

    lax.fori_loop(0, N_EXPERTS, per_expert, 0)


def _dispatch_body(pstart_ref, meta_ref, slot_ref, un_ref, xs_in_ref, xs_ref, loc_ref, sem):
    del xs_in_ref
    srow = lax.broadcasted_iota(I32, (LOCAL_ROWS, CHUNK), 0)
    onehot = jnp.zeros((LOCAL_ROWS, CHUNK), F32)
    for k in range(TOP_K):
        onehot = onehot + jnp.where(srow == slot_ref[k:k + 1, :], 1.0, 0.0)
    loc_ref[...] = _pack_halves(jnp.dot(onehot.astype(BF16), un_ref[...], preferred_element_type=F32))

    def make_copy(local_row, sorted_row):
        return pltpu.make_async_copy(loc_ref.at[pl.ds(local_row, PIECE), :], xs_ref.at[pl.ds(sorted_row, PIECE), :], sem)

    _piece_copies(pstart_ref, meta_ref, make_copy, lambda cp: cp.start())
    _piece_copies(pstart_ref, meta_ref, make_copy, lambda cp: cp.wait())


def _dispatch(pstart, meta, slot, un, n_rows):
    t = un.shape[0]
    xs0 = jnp.zeros((n_rows, D_MODEL // 2), jnp.uint32)
    return pl.pallas_call(
        _dispatch_body,
        grid_spec=pltpu.PrefetchScalarGridSpec(
            num_scalar_prefetch=1,
            grid=(t // CHUNK,),
            in_specs=[
                pl.BlockSpec((1, 8, CHUNK), lambda i, ps: (i, 0, 0), memory_space=pltpu.SMEM),
                pl.BlockSpec((8, CHUNK), lambda i, ps: (0, i)),
                pl.BlockSpec((CHUNK, D_MODEL), lambda i, ps: (i, 0)),
                pl.BlockSpec(memory_space=pl.ANY),
            ],
            out_specs=pl.BlockSpec(memory_space=pl.ANY),
            scratch_shapes=[pltpu.VMEM((LOCAL_ROWS, D_MODEL // 2), jnp.uint32), pltpu.SemaphoreType.DMA(())],
        ),
        out_shape=jax.ShapeDtypeStruct((n_rows, D_MODEL // 2), jnp.uint32),
        input_output_aliases={4: 0},
        compiler_params=_params(("arbitrary",)),
        name="dispatch",
    )(pstart, meta, slot, un, xs0)


GU_SLAB = 256
GU_HALF = GU_SLAB // 2


def _regroup_body(w_ref, p_ref, o_ref):
    for s in range(2 * EXPERT_FF // GU_SLAB):
        cols = slice(s * GU_SLAB, (s + 1) * GU_SLAB)
        o_ref[0, :, cols] = jnp.dot(w_ref[0, :, cols].astype(BF16), p_ref[...], preferred_element_type=F32).astype(BF16)


def _regroup_gate_up(w_gate_up):
    perm = np.zeros((GU_SLAB, GU_SLAB), np.float32)
    perm[2 * np.arange(GU_HALF), np.arange(GU_HALF)] = 1.0
    perm[2 * np.arange(GU_HALF) + 1, GU_HALF + np.arange(GU_HALF)] = 1.0
    blk = (1, D_MODEL, 2 * EXPERT_FF)
    return pl.pallas_call(
        _regroup_body,
        grid=(N_EXPERTS,),
        in_specs=[pl.BlockSpec(blk, lambda e: (e, 0, 0)), pl.BlockSpec((GU_SLAB, GU_SLAB), lambda e: (0, 0))],
        out_specs=pl.BlockSpec(blk, lambda e: (e, 0, 0)),
        out_shape=jax.ShapeDtypeStruct((N_EXPERTS, D_MODEL, 2 * EXPERT_FF), BF16),
        compiler_params=_params(("parallel",)),
        name="regroup_gate_up",
    )(w_gate_up, jnp.asarray(perm, BF16))


def _regroup_bias(b_gate_up):
    b = b_gate_up.astype(F32).reshape(N_EXPERTS, 2 * EXPERT_FF // GU_SLAB, GU_HALF, 2)
    return jnp.transpose(b, (0, 1, 3, 2)).reshape(N_EXPERTS, 1, 2 * EXPERT_FF)


def _expert_body(be_ref, nu_ref, x_ref, wgu_ref, wd_ref, bgu_ref, bd_ref, y_ref):
    del be_ref

    @pl.when(pl.program_id(0) >= nu_ref[0])
    def _():
        y_ref[...] = jnp.zeros_like(y_ref)

    @pl.when(pl.program_id(0) < nu_ref[0])
    def _():
        x = _unpack_halves(x_ref[...])
        gu = jnp.dot(x, wgu_ref[0], preferred_element_type=F32) + bgu_ref[0]
        acts = []
        for s in range(2 * EXPERT_FF // GU_SLAB):
            gate = jnp.minimum(gu[:, s * GU_SLAB:s * GU_SLAB + GU_HALF], SWIGLU_LIMIT)
            up = jnp.clip(gu[:, s * GU_SLAB + GU_HALF:(s + 1) * GU_SLAB], -SWIGLU_LIMIT, SWIGLU_LIMIT)
            acts.append(((up + 1.0) * (gate * jax.nn.sigmoid(SWIGLU_ALPHA * gate))).astype(BF16))
        act = jnp.concatenate(acts, axis=1)
        y = jnp.dot(act, wd_ref[0], preferred_element_type=F32) + bd_ref[0]
        y_ref[...] = _pack_halves(y.astype(BF16).astype(F32))


def _experts(block_e, n_used, xs, wgu, wd, bgu, bd):
    n_rows = xs.shape[0]
    n_blocks = n_rows // MOE_BLOCK
    wspec = lambda shape: pl.BlockSpec((1,) + shape, lambda i, be, nu: (be[i], 0, 0))
    return pl.pallas_call(
        _expert_body,
        grid_spec=pltpu.PrefetchScalarGridSpec(
            num_scalar_prefetch=2,
            grid=(n_blocks,),
            in_specs=[
                pl.BlockSpec((MOE_BLOCK, D_MODEL // 2), lambda i, be, nu: (i, 0)),
                wspec((D_MODEL, 2 * EXPERT_FF)),
                wspec((EXPERT_FF, D_MODEL)),
                wspec((1, 2 * EXPERT_FF)),
                wspec((1, D_MODEL)),
            ],
            out_specs=pl.BlockSpec((MOE_BLOCK, D_MODEL // 2), lambda i, be, nu: (i, 0)),
        ),
        out_shape=jax.ShapeDtypeStruct((n_rows, D_MODEL // 2), jnp.uint32),
        compiler_params=_params(("arbitrary",)),
        name="experts",
    )(block_e, n_used, xs, wgu, wd, bgu, bd)


def _combine_body(pstart_ref, meta_ref, slot_ref, w_ref, h1_ref, nw_ref, y_hbm, o_ref, ybuf, sem):
    @pl.when(jnp.logical_and(pl.program_id(0) == 0, pl.program_id(1) == 0))
    def _():
        ybuf[...] = jnp.zeros_like(ybuf)

    def make_copy(local_row, sorted_row):
        return pltpu.make_async_copy(y_hbm.at[pl.ds(sorted_row, PIECE), :], ybuf.at[pl.ds(local_row, PIECE), :], sem)

    _piece_copies(pstart_ref, meta_ref, make_copy, lambda cp: cp.start())
    _piece_copies(pstart_ref, meta_ref, make_copy, lambda cp: cp.wait())

    srow = lax.broadcasted_iota(I32, (LOCAL_ROWS, CHUNK), 0)
    wmat = jnp.zeros((LOCAL_ROWS, CHUNK), F32)
    for k in range(TOP_K):
        wmat = wmat + jnp.where(srow == slot_ref[k:k + 1, :], w_ref[k:k + 1, :], 0.0)
    wmat = jnp.transpose(wmat)
    w_hi = wmat.astype(BF16)
    w_lo = (wmat - w_hi.astype(F32)).astype(BF16)
    y = _unpack_halves(ybuf[...])
    f = jnp.dot(w_hi, y, preferred_element_type=F32) + jnp.dot(w_lo, y, preferred_element_type=F32)
    h2 = h1_ref[...] + f
    o_ref[...] = h2 * lax.rsqrt(jnp.mean(h2 * h2, axis=-1, keepdims=True) + NORM_EPS) * nw_ref[...]


def _combine(pstart, meta, slot, top_w, h1, norm_w, y_sorted, bsz, nc):
    seq_chunks = nc - 1
    src = lambda b, c: b * nc + c + 1
    return pl.pallas_call(
        _combine_body,
        grid_spec=pltpu.PrefetchScalarGridSpec(
            num_scalar_prefetch=1,
            grid=(bsz, seq_chunks),
            in_specs=[
                pl.BlockSpec((1, 8, CHUNK), lambda b, c, ps: (src(b, c), 0, 0), memory_space=pltpu.SMEM),
                pl.BlockSpec((8, CHUNK), lambda b, c, ps: (0, src(b, c))),
                pl.BlockSpec((8, CHUNK), lambda b, c, ps: (0, src(b, c))),
                pl.BlockSpec((CHUNK, D_MODEL), lambda b, c, ps: (src(b, c), 0)),
                pl.BlockSpec((1, D_MODEL), lambda b, c, ps: (0, 0)),
                pl.BlockSpec(memory_space=pl.ANY),
            ],
            out_specs=pl.BlockSpec((CHUNK, D_MODEL), lambda b, c, ps: (b * seq_chunks + c, 0)),
            scratch_shapes=[pltpu.VMEM((LOCAL_ROWS, D_MODEL // 2), jnp.uint32), pltpu.SemaphoreType.DMA(())],
        ),
        out_shape=jax.ShapeDtypeStruct((bsz * seq_chunks * CHUNK, D_MODEL), F32),
        compiler_params=_params(("arbitrary", "arbitrary")),
        name="combine",
    )(pstart, meta, slot, top_w, h1, norm_w, y_sorted)


def kernel(x, meta_tokens, norm_mix, w_in, conv_w, conv_b, dt_bias, a_log, d_skip, ssm_norm, w_branch, w_out, norm_ffn,
           router_w, router_b, w_gate_up, b_gate_up, w_down, b_down, norm_final):
    bsz, seq, _ = x.shape
    assert seq % CHUNK == 0 and norm_mix.shape[0] == 1
    nc = 1 + seq // CHUNK
    lp = nc * CHUNK
    t = bsz * lp

    hp = jnp.concatenate([jnp.zeros((bsz, META_PAD, D_MODEL), x.dtype),
                          jnp.broadcast_to(meta_tokens.astype(x.dtype)[None], (bsz, N_META, D_MODEL)), x], axis=1)
    hp = hp.reshape(t, D_MODEL)
    valid = jnp.asarray(np.tile(np.arange(lp) >= META_PAD, bsz).astype(np.int32).reshape(1, t))

    w_in0 = w_in[0]
    w_main = jnp.concatenate([w_in0[:, :OFF_DT], w_in0[:, OFF_DT + SSM_HEADS:]], axis=1).astype(BF16)
    w_dt = jnp.pad(w_in0[:, OFF_DT:OFF_DT + SSM_HEADS], ((0, 0), (0, DT_PAD - SSM_HEADS))).astype(BF16)
    wgu = _regroup_gate_up(w_gate_up[0])
    bgu = _regroup_bias(b_gate_up[0])
    wd = w_down[0].astype(BF16)
    bd = b_down[0][:, None, :].astype(F32)

    proj, dt_raw = _in_proj(hp, norm_mix[0].reshape(1, D_MODEL).astype(F32), w_main, w_dt)
    o_ret = _retention(proj, bsz, nc)
    o_ssd = _ssd(proj, dt_raw, conv_w[0], conv_b[0], dt_bias[0], a_log[0], d_skip[0], ssm_norm[0], bsz, nc)
    h1, un, top_e, top_w = _merge(
        o_ret, o_ssd, proj, hp, valid, w_branch[0, 0].astype(BF16), w_branch[0, 1].astype(BF16), w_out[0].astype(BF16),
        norm_ffn[0].reshape(1, D_MODEL).astype(F32), jnp.transpose(router_w[0]).astype(F32),
        router_b[0].reshape(N_EXPERTS, 1).astype(F32))

    slot, meta, counts = _rank(top_e)
    counts = counts[:, 0]
    slack = MOE_BLOCK - 1 + PIECE - ROW_ALIGN
    padded = (counts + slack) // MOE_BLOCK * MOE_BLOCK
    pend = jnp.cumsum(padded)
    pstart = (pend - padded).astype(I32)
    n_pairs = bsz * (N_META + seq) * TOP_K
    max_rows = n_pairs + bsz * nc * N_EXPERTS * (ROW_ALIGN - 1)
    n_blocks = (max_rows + N_EXPERTS * slack) // MOE_BLOCK
    blk_row = jnp.arange(n_blocks, dtype=I32) * MOE_BLOCK
    block_e = jnp.minimum(jnp.sum(pend[None, :] <= blk_row[:, None], axis=1), N_EXPERTS - 1).astype(I32)
    n_used = (pend[-1:] // MOE_BLOCK).astype(I32)

    xs = _dispatch(pstart, meta, slot, un, n_blocks * MOE_BLOCK)
    ys = _experts(block_e, n_used, xs, wgu, wd, bgu, bd)
    out = _combine(pstart, meta, slot, top_w, h1, norm_final.reshape(1, D_MODEL).astype(F32), ys, bsz, nc)
    return out.reshape(bsz, seq, D_MODEL)
```

```python
import functools

import numpy as np
import jax
import jax.numpy as jnp
from jax import lax
from jax.experimental import pallas as pl
from jax.experimental.pallas import tpu as pltpu

F32 = jnp.float32
BF16 = jnp.bfloat16
I32 = jnp.int32

D_MODEL = 1024
N_META = 16
CHUNK = 128
META_PAD = CHUNK - N_META
NORM_EPS = 1e-6
RET_HEADS = 4
RET_QK_DIM = 256
RET_V_DIM = 512
RET_QK = RET_HEADS * RET_QK_DIM
RET_V = RET_HEADS * RET_V_DIM
ROPE_BASE = 10000.0
SSM_INNER = 2048
SSM_HEAD_DIM = 64
SSM_HEADS = 32
SSM_GROUPS = 4
SSM_STATE = 128
CONV_WIDTH = 4
CONV_CH = SSM_INNER + 2 * SSM_GROUPS * SSM_STATE
HEAD_PAIRS = SSM_HEADS // 2
PAIRS_PER_GROUP = HEAD_PAIRS // SSM_GROUPS
OFF_DT = 2 * RET_QK + 2 * RET_V + SSM_INNER + CONV_CH
N_MAIN = OFF_DT + 2 * D_MODEL
DT_PAD = 128
N_EXPERTS = 32
TOP_K = 4
EXPERT_FF = 1024
SWIGLU_ALPHA = 1.702
SWIGLU_LIMIT = 7.0
MOE_BLOCK = 512

VMEM_LIMIT = 56 * 1024 * 1024
HIGHEST = lax.Precision.HIGHEST


def _params(sem):
    return pltpu.CompilerParams(dimension_semantics=sem, vmem_limit_bytes=VMEM_LIMIT)


def _pick(n, candidates):
    for c in candidates:
        if n % c == 0:
            return c
    raise ValueError(f"no tile for {n} among {candidates}")


def _nt_dot(a, b, **kw):
    return lax.dot_general(a, b, (((1,), (1,)), ((), ())), preferred_element_type=F32, **kw)


def _inproj_body(x_ref, nw_ref, w_ref, wdt_ref, o_ref, dt_ref, xn_ref):
    @pl.when(pl.program_id(1) == 0)
    def _():
        x = x_ref[...]
        ms = jnp.mean(x * x, axis=-1, keepdims=True)
        xn = (x * lax.rsqrt(ms + NORM_EPS) * nw_ref[...]).astype(BF16)
        xn_ref[...] = xn
        dt_ref[...] = jnp.dot(xn, wdt_ref[...], preferred_element_type=F32)

    o_ref[...] = jnp.dot(xn_ref[...], w_ref[...], preferred_element_type=F32).astype(BF16)


def _in_proj(hp, norm_w, w_main, w_dt):
    t = hp.shape[0]
    tm = _pick(t, (1024, 512, 256, 128))
    tn = 1024
    return pl.pallas_call(
        _inproj_body,
        grid=(t // tm, N_MAIN // tn),
        in_specs=[
            pl.BlockSpec((tm, D_MODEL), lambda i, j: (i, 0)),
            pl.BlockSpec((1, D_MODEL), lambda i, j: (0, 0)),
            pl.BlockSpec((D_MODEL, tn), lambda i, j: (0, j)),
            pl.BlockSpec((D_MODEL, DT_PAD), lambda i, j: (0, 0)),
        ],
        out_specs=[
            pl.BlockSpec((tm, tn), lambda i, j: (i, j)),
            pl.BlockSpec((tm, DT_PAD), lambda i, j: (i, 0)),
        ],
        out_shape=[jax.ShapeDtypeStruct((t, N_MAIN), BF16), jax.ShapeDtypeStruct((t, DT_PAD), F32)],
        scratch_shapes=[pltpu.VMEM((tm, D_MODEL), BF16)],
        compiler_params=_params(("parallel", "arbitrary")),
        name="in_proj",
    )(hp, norm_w, w_main, w_dt)


def _ret_tables(lp):
    half = RET_QK_DIM // 2
    inv_freq = ROPE_BASE ** (-np.arange(half, dtype=np.float64) / half)
    pos = np.arange(lp, dtype=np.float64) - META_PAD
    ang = pos[:, None] * inv_freq[None, :]
    log_gamma = np.log(1.0 - 2.0 ** (-5.0 - np.arange(RET_HEADS, dtype=np.float64)))
    idx = np.arange(CHUNK, dtype=np.float64)
    dist = idx[:, None] - idx[None, :]
    intra = np.where(dist >= 0, np.exp(log_gamma[:, None, None] * np.maximum(dist, 0.0)[None]), 0.0)
    q_dec = np.exp(log_gamma[:, None] * (idx[None, :] + 1.0))
    k_dec = np.exp(log_gamma[:, None] * (CHUNK - 1.0 - idx[None, :]))
    q_dec = np.broadcast_to(q_dec[:, :, None], (RET_HEADS, CHUNK, RET_V_DIM))
    k_dec = np.broadcast_to(k_dec[:, :, None], (RET_HEADS, CHUNK, RET_QK_DIM))
    chunk_dec = tuple(float(v) for v in np.exp(log_gamma * CHUNK))
    as32 = lambda a: jnp.asarray(np.ascontiguousarray(a), F32)
    return as32(np.cos(ang)), as32(np.sin(ang)), as32(intra), as32(q_dec), as32(k_dec), chunk_dec


def _ret_body(chunk_dec, q_ref, k_ref, v_ref, g_ref, cos_ref, sin_ref, intra_ref, qd_ref, kd_ref, o_ref, st_ref):
    @pl.when(pl.program_id(1) == 0)
    def _():
        st_ref[...] = jnp.zeros_like(st_ref)

    cos = cos_ref[...]
    sin = sin_ref[...]
    half = RET_QK_DIM // 2

    def rotary(t):
        t1, t2 = t[:, :half], t[:, half:]
        return jnp.concatenate([t1 * cos - t2 * sin, t2 * cos + t1 * sin], axis=1)

    for h in range(RET_HEADS):
        qk = slice(h * RET_QK_DIM, (h + 1) * RET_QK_DIM)
        vv = slice(h * RET_V_DIM, (h + 1) * RET_V_DIM)
        qr = rotary(q_ref[:, qk].astype(F32))
        kr = rotary(k_ref[:, qk].astype(F32)) * (RET_QK_DIM ** -0.5)
        qb = qr.astype(BF16)
        vh = v_ref[:, vv]
        st = st_ref[h]
        s = _nt_dot(qb, kr.astype(BF16)) * intra_ref[h]
        y = jnp.dot(s.astype(BF16), vh, preferred_element_type=F32)
        y = y + jnp.dot(qb, st.astype(BF16), preferred_element_type=F32) * qd_ref[h]
        kdt = jnp.transpose(kr * kd_ref[h]).astype(BF16)
        st_ref[h] = st * chunk_dec[h] + jnp.dot(kdt, vh, preferred_element_type=F32)
        o = y * lax.rsqrt(jnp.mean(y * y, axis=-1, keepdims=True) + NORM_EPS)
        g = g_ref[:, vv].astype(F32)
        o_ref[:, vv] = (g * jax.nn.sigmoid(g) * o).astype(BF16)


def _retention(proj, bsz, nc):
    t = proj.shape[0]
    cos, sin, intra, q_dec, k_dec, chunk_dec = _ret_tables(nc * CHUNK)
    row = lambda b, c: b * nc + c
    const3 = lambda b, c: (0, 0, 0)
    return pl.pallas_call(
        functools.partial(_ret_body, chunk_dec),
        grid=(bsz, nc),
        in_specs=[
            pl.BlockSpec((CHUNK, RET_QK), lambda b, c: (row(b, c), 0)),
            pl.BlockSpec((CHUNK, RET_QK), lambda b, c: (row(b, c), 1)),
            pl.BlockSpec((CHUNK, RET_V), lambda b, c: (row(b, c), 1)),
            pl.BlockSpec((CHUNK, RET_V), lambda b, c: (row(b, c), 2)),
            pl.BlockSpec((CHUNK, RET_QK_DIM // 2), lambda b, c: (c, 0)),
            pl.BlockSpec((CHUNK, RET_QK_DIM // 2), lambda b, c: (c, 0)),
            pl.BlockSpec((RET_HEADS, CHUNK, CHUNK), const3),
            pl.BlockSpec((RET_HEADS, CHUNK, RET_V_DIM), const3),
            pl.BlockSpec((RET_HEADS, CHUNK, RET_QK_DIM), const3),
        ],
        out_specs=pl.BlockSpec((CHUNK, RET_V), lambda b, c: (row(b, c), 0)),
        out_shape=jax.ShapeDtypeStruct((t, RET_V), BF16),
        scratch_shapes=[pltpu.VMEM((RET_HEADS, RET_QK_DIM, RET_V_DIM), F32)],
        compiler_params=_params(("parallel", "arbitrary")),
        name="retention",
    )(proj, proj, proj, proj, cos, sin, intra, q_dec, k_dec)


CONV_COLS = 512


def _ssd_body(z_ref, xs_ref, bc_ref, dt_ref, cw_ref, cb_ref, dtb_ref, alog_ref, dsk_ref, nw_ref,
              o_ref, st_ref, full_ref, xc_ref, y_ref):
    c = pl.program_id(1)

    @pl.when(c == 0)
    def _():
        st_ref[...] = jnp.zeros_like(st_ref)
        full_ref[0:8, :] = jnp.zeros((8, CONV_CH), F32)

    rows = lax.broadcasted_iota(I32, (CHUNK, 1), 0)
    valid = jnp.logical_or(c > 0, rows >= META_PAD)

    full_ref[8:, :SSM_INNER] = xs_ref[...].astype(F32)
    full_ref[8:, SSM_INNER:] = bc_ref[...].astype(F32)
    for s in range(CONV_CH // CONV_COLS):
        cols = slice(s * CONV_COLS, (s + 1) * CONV_COLS)
        acc = cb_ref[:, cols]
        for k in range(CONV_WIDTH):
            off = 8 - (CONV_WIDTH - 1) + k
            acc = acc + cw_ref[k:k + 1, cols] * full_ref[off:off + CHUNK, cols]
        xc_ref[:, cols] = jnp.where(valid, acc * jax.nn.sigmoid(acc), 0.0)
    full_ref[0:8, :] = full_ref[CHUNK:CHUNK + 8, :]

    dtv = dt_ref[...] + dtb_ref[...]
    dt = jnp.maximum(dtv, 0.0) + jnp.log1p(jnp.exp(-jnp.abs(dtv)))
    dt = jnp.where(valid, dt, 0.0)
    a = dt * (-jnp.exp(alog_ref[...]))
    ri = lax.broadcasted_iota(I32, (CHUNK, CHUNK), 0)
    ci = lax.broadcasted_iota(I32, (CHUNK, CHUNK), 1)
    causal = ri >= ci
    acs = jnp.dot(causal.astype(F32), a, preferred_element_type=F32, precision=HIGHEST)
    last = acs[CHUNK - 1:CHUNK, :]
    acs_t = jnp.transpose(acs)
    dt_t = jnp.transpose(dt)
    w_t = jnp.transpose(jnp.exp(last - acs) * dt)
    e_last = jnp.exp(last)
    lanes = lax.broadcasted_iota(I32, (1, CHUNK), 1)
    low = lanes < SSM_HEAD_DIM

    for g in range(SSM_GROUPS):
        bm = xc_ref[:, SSM_INNER + g * SSM_STATE:SSM_INNER + (g + 1) * SSM_STATE]
        cm = xc_ref[:, SSM_INNER + (SSM_GROUPS + g) * SSM_STATE:SSM_INNER + (SSM_GROUPS + g + 1) * SSM_STATE]
        cb = _nt_dot(cm.astype(BF16), bm.astype(BF16))
        bm_t = jnp.transpose(bm)
        for pp in range(PAIRS_PER_GROUP):
            m = g * PAIRS_PER_GROUP + pp
            x_pair = xc_ref[:, m * CHUNK:(m + 1) * CHUNK]
            st_pair = st_ref[m]
            y_pair = jnp.zeros((CHUNK, CHUNK), F32)
            upd = jnp.zeros((SSM_STATE, CHUNK), F32)
            for hh in range(2):
                h = 2 * m + hh
                lane_mask = low if hh == 0 else jnp.logical_not(low)
                col = jnp.broadcast_to(acs[:, h:h + 1], (CHUNK, CHUNK))
                seg = col - acs_t[h:h + 1, :]
                dec = jnp.where(causal, jnp.exp(jnp.minimum(seg, 0.0)), 0.0)
                mat = cb * dec * dt_t[h:h + 1, :]
                lhs = jnp.concatenate([mat, cm * jnp.exp(col)], axis=1).astype(BF16)
                xm = jnp.where(lane_mask, x_pair, 0.0).astype(BF16)
                sm = jnp.where(lane_mask, st_pair, 0.0).astype(BF16)
                y_pair = y_pair + jnp.dot(lhs, jnp.concatenate([xm, sm], axis=0), preferred_element_type=F32)
                upd = upd + jnp.dot((bm_t * w_t[h:h + 1, :]).astype(BF16), xm, preferred_element_type=F32)
            decay = jnp.where(low, jnp.broadcast_to(e_last[:, 2 * m:2 * m + 1], (1, CHUNK)),
                              jnp.broadcast_to(e_last[:, 2 * m + 1:2 * m + 2], (1, CHUNK)))
            st_ref[m] = st_pair * decay + upd
            y_ref[:, m * CHUNK:(m + 1) * CHUNK] = y_pair

    gsz = SSM_INNER // SSM_GROUPS
    for g in range(SSM_GROUPS):
        cols = slice(g * gsz, (g + 1) * gsz)
        z = z_ref[:, cols].astype(F32)
        y = (y_ref[:, cols] + dsk_ref[:, cols] * xc_ref[:, cols]) * (z * jax.nn.sigmoid(z))
        y = y * lax.rsqrt(jnp.mean(y * y, axis=-1, keepdims=True) + NORM_EPS)
        o_ref[:, cols] = (y * nw_ref[:, cols]).astype(BF16)


def _ssd(proj, dt_raw, conv_w, conv_b, dt_bias, a_log, d_skip, norm_w, bsz, nc):
    t = proj.shape[0]
    row = lambda b, c: b * nc + c
    const2 = lambda b, c: (0, 0)
    pad = lambda v: jnp.pad(v.astype(F32), (0, DT_PAD - SSM_HEADS)).reshape(1, DT_PAD)
    return pl.pallas_call(
        _ssd_body,
        grid=(bsz, nc),
        in_specs=[
            pl.BlockSpec((CHUNK, SSM_INNER), lambda b, c: (row(b, c), 3)),
            pl.BlockSpec((CHUNK, SSM_INNER), lambda b, c: (row(b, c), 4)),
            pl.BlockSpec((CHUNK, 1024), lambda b, c: (row(b, c), 10)),
            pl.BlockSpec((CHUNK, DT_PAD), lambda b, c: (row(b, c), 0)),
            pl.BlockSpec((CONV_WIDTH, CONV_CH), const2),
            pl.BlockSpec((1, CONV_CH), const2),
            pl.BlockSpec((1, DT_PAD), const2),
            pl.BlockSpec((1, DT_PAD), const2),
            pl.BlockSpec((1, SSM_INNER), const2),
            pl.BlockSpec((1, SSM_INNER), const2),
        ],
        out_specs=pl.BlockSpec((CHUNK, SSM_INNER), lambda b, c: (row(b, c), 0)),
        out_shape=jax.ShapeDtypeStruct((t, SSM_INNER), BF16),
        scratch_shapes=[
            pltpu.VMEM((HEAD_PAIRS, SSM_STATE, CHUNK), F32),
            pltpu.VMEM((8 + CHUNK, CONV_CH), F32),
            pltpu.VMEM((CHUNK, CONV_CH), F32),
            pltpu.VMEM((CHUNK, SSM_INNER), F32),
        ],
        compiler_params=_params(("parallel", "arbitrary")),
        name="ssd",
    )(proj, proj, proj, dt_raw, conv_w.astype(F32), conv_b.astype(F32).reshape(1, CONV_CH), pad(dt_bias), pad(a_log),
      jnp.repeat(d_skip.astype(F32), SSM_HEAD_DIM).reshape(1, SSM_INNER), norm_w.astype(F32).reshape(1, SSM_INNER))


def _merge_body(oret_ref, ossd_ref, g0_ref, g1_ref, h_ref, valid_ref, wb0_ref, wb1_ref, wo_ref, nw_ref, rw_ref, rb_ref,
                h1_ref, un_ref, e_ref, w_ref):
    y_ret = jnp.dot(oret_ref[...], wb0_ref[...], preferred_element_type=F32)
    y_ssd = jnp.dot(ossd_ref[...], wb1_ref[...], preferred_element_type=F32)
    merged = (jax.nn.sigmoid(g0_ref[...].astype(F32)) * y_ret + jax.nn.sigmoid(g1_ref[...].astype(F32)) * y_ssd)
    h1 = h_ref[...] + jnp.dot(merged.astype(BF16), wo_ref[...], preferred_element_type=F32)
    h1_ref[...] = h1
    un = h1 * lax.rsqrt(jnp.mean(h1 * h1, axis=-1, keepdims=True) + NORM_EPS) * nw_ref[...]
    un_ref[...] = un.astype(BF16)

    logits = _nt_dot(rw_ref[...], un, precision=HIGHEST) + rb_ref[...]
    tm = logits.shape[1]
    eidx = lax.broadcasted_iota(I32, (N_EXPERTS, tm), 0)
    vals, ids = [], []
    for _ in range(TOP_K):
        best = jnp.max(logits, axis=0, keepdims=True)
        arg = jnp.min(jnp.where(logits == best, eidx, N_EXPERTS), axis=0, keepdims=True)
        vals.append(best)
        ids.append(arg)
        logits = jnp.where(eidx == arg, -jnp.inf, logits)
    ex = [jnp.exp(v - vals[0]) for v in vals]
    denom = ex[0] + ex[1] + ex[2] + ex[3]
    is_tok = valid_ref[...] > 0
    e_ref[...] = jnp.concatenate([jnp.where(is_tok, i, -1) for i in ids] + [jnp.zeros((8 - TOP_K, tm), I32)], axis=0)
    w_ref[...] = jnp.concatenate([x / denom for x in ex] + [jnp.zeros((8 - TOP_K, tm), F32)], axis=0)


def _merge(o_ret, o_ssd, proj, hp, valid, wb0, wb1, wo, norm_w, router_wt, router_b):
    t = hp.shape[0]
    tm = _pick(t, (256, 128))
    gate_blk = OFF_DT // D_MODEL
    assert gate_blk * D_MODEL == OFF_DT
    const2 = lambda i: (0, 0)
    return pl.pallas_call(
        _merge_body,
        grid=(t // tm,),
        in_specs=[
            pl.BlockSpec((tm, RET_V), lambda i: (i, 0)),
            pl.BlockSpec((tm, SSM_INNER), lambda i: (i, 0)),
            pl.BlockSpec((tm, D_MODEL), lambda i: (i, gate_blk)),
            pl.BlockSpec((tm, D_MODEL), lambda i: (i, gate_blk + 1)),
            pl.BlockSpec((tm, D_MODEL), lambda i: (i, 0)),
            pl.BlockSpec((1, tm), lambda i: (0, i)),
            pl.BlockSpec((RET_V, D_MODEL), const2),
            pl.BlockSpec((SSM_INNER, D_MODEL), const2),
            pl.BlockSpec((D_MODEL, D_MODEL), const2),
            pl.BlockSpec((1, D_MODEL), const2),
            pl.BlockSpec((N_EXPERTS, D_MODEL), const2),
            pl.BlockSpec((N_EXPERTS, 1), const2),
        ],
        out_specs=[
            pl.BlockSpec((tm, D_MODEL), lambda i: (i, 0)),
            pl.BlockSpec((tm, D_MODEL), lambda i: (i, 0)),
            pl.BlockSpec((8, tm), lambda i: (0, i)),
            pl.BlockSpec((8, tm), lambda i: (0, i)),
        ],
        out_shape=[
            jax.ShapeDtypeStruct((t, D_MODEL), F32),
            jax.ShapeDtypeStruct((t, D_MODEL), BF16),
            jax.ShapeDtypeStruct((8, t), I32),
            jax.ShapeDtypeStruct((8, t), F32),
        ],
        compiler_params=_params(("parallel",)),
        name="merge_router",
    )(o_ret, o_ssd, proj, proj, hp, valid, wb0, wb1, wo, norm_w, router_wt, router_b)


ROW_ALIGN = 8
PIECE = 16
LOCAL_ROWS = 1024
META_BEFORE, META_COUNT, META_LSTART = 0, 1, 2


def _rank_body(tiles, e_ref, slot_ref, meta_ref, cnt_ref, run_ref):
    @pl.when(pl.program_id(0) == 0)
    def _():
        run_ref[...] = jnp.zeros_like(run_ref)

    eidx = lax.broadcasted_iota(I32, (N_EXPERTS, CHUNK), 0)
    lane = lax.broadcasted_iota(I32, (N_EXPERTS, CHUNK), 1)
    ri = lax.broadcasted_iota(I32, (CHUNK, CHUNK), 0)
    ci = lax.broadcasted_iota(I32, (CHUNK, CHUNK), 1)
    upper = jnp.where(ri <= ci, 1.0, 0.0).astype(BF16)
    ones = jnp.ones((CHUNK, CHUNK), BF16)
    ei = lax.broadcasted_iota(I32, (N_EXPERTS, N_EXPERTS), 0)
    ej = lax.broadcasted_iota(I32, (N_EXPERTS, N_EXPERTS), 1)
    lower = jnp.where(ej < ei, 1.0, 0.0).astype(BF16)
    round_up = lambda v, m: jnp.floor((v + (m - 1)) * (1.0 / m)) * m
    diag = lambda m: jnp.sum(jnp.where(eidx == lane, m, 0.0), axis=0, keepdims=True)

    run = run_ref[...]
    for s in range(tiles):
        e_tile = e_ref[:, s * CHUNK:(s + 1) * CHUNK]
        seen = jnp.zeros((N_EXPERTS, CHUNK), F32)
        hits, ranks = [], []
        for k in range(TOP_K):
            hit = eidx == e_tile[k:k + 1, :]
            oh = jnp.where(hit, 1.0, 0.0).astype(BF16)
            ranks.append(seen + jnp.dot(oh, upper, preferred_element_type=F32) - 1.0)
            seen = seen + jnp.dot(oh, ones, preferred_element_type=F32)
            hits.append(hit)
        count = seen
        lstart = jnp.dot(lower, round_up(count, PIECE).astype(BF16), preferred_element_type=F32)
        slots = []
        for k in range(TOP_K):
            slot = jnp.sum(jnp.where(hits[k], lstart + ranks[k], 0.0), axis=0, keepdims=True)
            slots.append(jnp.where(e_tile[k:k + 1, :] >= 0, slot, -1.0))
        slot_ref[:, s * CHUNK:(s + 1) * CHUNK] = jnp.concatenate(
            slots + [jnp.full((8 - TOP_K, CHUNK), -1.0, F32)], axis=0).astype(I32)
        meta_ref[s] = jnp.concatenate(
            [diag(run), diag(count), diag(lstart), jnp.zeros((5, CHUNK), F32)], axis=0).astype(I32)
        run = run + round_up(count, ROW_ALIGN)
    run_ref[...] = run
    cnt_ref[...] = run.astype(I32)


def _rank(top_e):
    t = top_e.shape[1]
    n_tiles = t // CHUNK
    tiles = _pick(n_tiles, (4, 2, 1))
    return pl.pallas_call(
        functools.partial(_rank_body, tiles),
        grid=(n_tiles // tiles,),
        in_specs=[pl.BlockSpec((8, tiles * CHUNK), lambda i: (0, i))],
        out_specs=[
            pl.BlockSpec((8, tiles * CHUNK), lambda i: (0, i)),
            pl.BlockSpec((tiles, 8, CHUNK), lambda i: (i, 0, 0)),
            pl.BlockSpec((N_EXPERTS, CHUNK), lambda i: (0, 0)),
        ],
        out_shape=[
            jax.ShapeDtypeStruct((8, t), I32),
            jax.ShapeDtypeStruct((n_tiles, 8, CHUNK), I32),
            jax.ShapeDtypeStruct((N_EXPERTS, CHUNK), I32),
        ],
        scratch_shapes=[pltpu.VMEM((N_EXPERTS, CHUNK), F32)],
        compiler_params=_params(("arbitrary",)),
        name="rank",
    )(top_e)


def _pack_halves(v):
    bits = pltpu.bitcast(v, jnp.uint32)
    return (bits[:, :D_MODEL // 2] >> 16) | bits[:, D_MODEL // 2:]


def _unpack_halves(p):
    lo = pltpu.bitcast(p << 16, F32)
    hi = pltpu.bitcast(p & jnp.uint32(0xFFFF0000), F32)
    return jnp.concatenate([lo, hi], axis=1).astype(BF16)


def _piece_copies(pstart_ref, meta_ref, make_copy, act):
    def per_expert(e, carry):
        n_pieces = lax.shift_right_logical(meta_ref[0, META_COUNT, e] + (PIECE - 1), PIECE.bit_length() - 1)
        local0 = meta_ref[0, META_LSTART, e]
        sorted0 = pstart_ref[e] + meta_ref[0, META_BEFORE, e]

        def per_piece(j, c):
            act(make_copy(pl.multiple_of(local0 + j * PIECE, ROW_ALIGN), pl.multiple_of(sorted0 + j * PIECE, ROW_ALIGN)))
            return c

        return lax.fori_loop(0, n_pieces, per_piece, carry)

    lax.fori_loop(0, N_EXPERTS, per_expert, 0)


def _dispatch_body(pstart_ref, fill_ref, meta_ref, slot_ref, un_ref, xs_ref, loc_ref, zero_ref, sem):
    srow = lax.broadcasted_iota(I32, (LOCAL_ROWS, CHUNK), 0)
    onehot = jnp.zeros((LOCAL_ROWS, CHUNK), F32)
    for k in range(TOP_K):
        onehot = onehot + jnp.where(srow == slot_ref[k:k + 1, :], 1.0, 0.0)
    loc_ref[...] = _pack_halves(jnp.dot(onehot.astype(BF16), un_ref[...], preferred_element_type=F32))

    def make_copy(local_row, sorted_row):
        return pltpu.make_async_copy(loc_ref.at[pl.ds(local_row, PIECE), :], xs_ref.at[pl.ds(sorted_row, PIECE), :], sem)

    _piece_copies(pstart_ref, meta_ref, make_copy, lambda cp: cp.start())
    _piece_copies(pstart_ref, meta_ref, make_copy, lambda cp: cp.wait())

    @pl.when(pl.program_id(0) == pl.num_programs(0) - 1)
    def _():
        zero_ref[...] = jnp.zeros_like(zero_ref)

        def fill(act):
            def per_expert(e, carry):
                def per_piece(j, c):
                    row = pl.multiple_of(fill_ref[0, e] + j * ROW_ALIGN, ROW_ALIGN)
                    act(pltpu.make_async_copy(zero_ref, xs_ref.at[pl.ds(row, ROW_ALIGN), :], sem))
                    return c
                return lax.fori_loop(0, fill_ref[1, e], per_piece, carry)
            lax.fori_loop(0, N_EXPERTS, per_expert, 0)

        fill(lambda cp: cp.start())
        fill(lambda cp: cp.wait())


def _dispatch(pstart, fill, meta, slot, un, n_rows):
    t = un.shape[0]
    return pl.pallas_call(
        _dispatch_body,
        grid_spec=pltpu.PrefetchScalarGridSpec(
            num_scalar_prefetch=2,
            grid=(t // CHUNK,),
            in_specs=[
                pl.BlockSpec((1, 8, CHUNK), lambda i, ps, fl: (i, 0, 0), memory_space=pltpu.SMEM),
                pl.BlockSpec((8, CHUNK), lambda i, ps, fl: (0, i)),
                pl.BlockSpec((CHUNK, D_MODEL), lambda i, ps, fl: (i, 0)),
            ],
            out_specs=pl.BlockSpec(memory_space=pl.ANY),
            scratch_shapes=[pltpu.VMEM((LOCAL_ROWS, D_MODEL // 2), jnp.uint32),
                            pltpu.VMEM((ROW_ALIGN, D_MODEL // 2), jnp.uint32), pltpu.SemaphoreType.DMA(())],
        ),
        out_shape=jax.ShapeDtypeStruct((n_rows, D_MODEL // 2), jnp.uint32),
        compiler_params=_params(("arbitrary",)),
        name="dispatch",
    )(pstart, fill, meta, slot, un)


GU_SLAB = 256
GU_HALF = GU_SLAB // 2


def _regroup_body(w_ref, p_ref, o_ref):
    for s in range(2 * EXPERT_FF // GU_SLAB):
        cols = slice(s * GU_SLAB, (s + 1) * GU_SLAB)
        o_ref[0, :, cols] = jnp.dot(w_ref[0, :, cols].astype(BF16), p_ref[...], preferred_element_type=F32).astype(BF16)


def _regroup_gate_up(w_gate_up):
    perm = np.zeros((GU_SLAB, GU_SLAB), np.float32)
    perm[2 * np.arange(GU_HALF), np.arange(GU_HALF)] = 1.0
    perm[2 * np.arange(GU_HALF) + 1, GU_HALF + np.arange(GU_HALF)] = 1.0
    blk = (1, D_MODEL, 2 * EXPERT_FF)
    return pl.pallas_call(
        _regroup_body,
        grid=(N_EXPERTS,),
        in_specs=[pl.BlockSpec(blk, lambda e: (e, 0, 0)), pl.BlockSpec((GU_SLAB, GU_SLAB), lambda e: (0, 0))],
        out_specs=pl.BlockSpec(blk, lambda e: (e, 0, 0)),
        out_shape=jax.ShapeDtypeStruct((N_EXPERTS, D_MODEL, 2 * EXPERT_FF), BF16),
        compiler_params=_params(("parallel",)),
        name="regroup_gate_up",
    )(w_gate_up, jnp.asarray(perm, BF16))


def _regroup_bias(b_gate_up):
    b = b_gate_up.astype(F32).reshape(N_EXPERTS, 2 * EXPERT_FF // GU_SLAB, GU_HALF, 2)
    return jnp.transpose(b, (0, 1, 3, 2)).reshape(N_EXPERTS, 1, 2 * EXPERT_FF)


def _expert_body(be_ref, nu_ref, x_ref, wgu_ref, wd_ref, bgu_ref, bd_ref, y_ref):
    del be_ref

    @pl.when(pl.program_id(0) >= nu_ref[0])
    def _():
        y_ref[...] = jnp.zeros_like(y_ref)

    @pl.when(pl.program_id(0) < nu_ref[0])
    def _():
        x = _unpack_halves(x_ref[...])
        gu = jnp.dot(x, wgu_ref[0], preferred_element_type=F32) + bgu_ref[0]
        acts = []
        for s in range(2 * EXPERT_FF // GU_SLAB):
            gate = jnp.minimum(gu[:, s * GU_SLAB:s * GU_SLAB + GU_HALF], SWIGLU_LIMIT)
            up = jnp.clip(gu[:, s * GU_SLAB + GU_HALF:(s + 1) * GU_SLAB], -SWIGLU_LIMIT, SWIGLU_LIMIT)
            acts.append(((up + 1.0) * (gate * jax.nn.sigmoid(SWIGLU_ALPHA * gate))).astype(BF16))
        act = jnp.concatenate(acts, axis=1)
        y = jnp.dot(act, wd_ref[0], preferred_element_type=F32) + bd_ref[0]
        y_ref[...] = _pack_halves(y.astype(BF16).astype(F32))


def _experts(block_e, n_used, xs, wgu, wd, bgu, bd):
    n_rows = xs.shape[0]
    n_blocks = n_rows // MOE_BLOCK
    wspec = lambda shape: pl.BlockSpec((1,) + shape, lambda i, be, nu: (be[i], 0, 0))
    return pl.pallas_call(
        _expert_body,
        grid_spec=pltpu.PrefetchScalarGridSpec(
            num_scalar_prefetch=2,
            grid=(n_blocks,),
            in_specs=[
                pl.BlockSpec((MOE_BLOCK, D_MODEL // 2), lambda i, be, nu: (i, 0)),
                wspec((D_MODEL, 2 * EXPERT_FF)),
                wspec((EXPERT_FF, D_MODEL)),
                wspec((1, 2 * EXPERT_FF)),
                wspec((1, D_MODEL)),
            ],
            out_specs=pl.BlockSpec((MOE_BLOCK, D_MODEL // 2), lambda i, be, nu: (i, 0)),
        ),
        out_shape=jax.ShapeDtypeStruct((n_rows, D_MODEL // 2), jnp.uint32),
        compiler_params=_params(("arbitrary",)),
        name="experts",
    )(block_e, n_used, xs, wgu, wd, bgu, bd)


def _combine_body(pstart_ref, meta_ref, slot_ref, w_ref, h1_ref, nw_ref, y_hbm, o_ref, ybuf, sem):
    @pl.when(jnp.logical_and(pl.program_id(0) == 0, pl.program_id(1) == 0))
    def _():
        ybuf[...] = jnp.zeros_like(ybuf)

    def make_copy(local_row, sorted_row):
        return pltpu.make_async_copy(y_hbm.at[pl.ds(sorted_row, PIECE), :], ybuf.at[pl.ds(local_row, PIECE), :], sem)

    _piece_copies(pstart_ref, meta_ref, make_copy, lambda cp: cp.start())
    _piece_copies(pstart_ref, meta_ref, make_copy, lambda cp: cp.wait())

    srow = lax.broadcasted_iota(I32, (LOCAL_ROWS, CHUNK), 0)
    wmat = jnp.zeros((LOCAL_ROWS, CHUNK), F32)
    for k in range(TOP_K):
        wmat = wmat + jnp.where(srow == slot_ref[k:k + 1, :], w_ref[k:k + 1, :], 0.0)
    wmat = jnp.transpose(wmat)
    w_hi = wmat.astype(BF16)
    w_lo = (wmat - w_hi.astype(F32)).astype(BF16)
    y = _unpack_halves(ybuf[...])
    f = jnp.dot(w_hi, y, preferred_element_type=F32) + jnp.dot(w_lo, y, preferred_element_type=F32)
    h2 = h1_ref[...] + f
    o_ref[...] = h2 * lax.rsqrt(jnp.mean(h2 * h2, axis=-1, keepdims=True) + NORM_EPS) * nw_ref[...]


def _combine(pstart, meta, slot, top_w, h1, norm_w, y_sorted, bsz, nc):
    seq_chunks = nc - 1
    src = lambda b, c: b * nc + c + 1
    return pl.pallas_call(
        _combine_body,
        grid_spec=pltpu.PrefetchScalarGridSpec(
            num_scalar_prefetch=1,
            grid=(bsz, seq_chunks),
            in_specs=[
                pl.BlockSpec((1, 8, CHUNK), lambda b, c, ps: (src(b, c), 0, 0), memory_space=pltpu.SMEM),
                pl.BlockSpec((8, CHUNK), lambda b, c, ps: (0, src(b, c))),
                pl.BlockSpec((8, CHUNK), lambda b, c, ps: (0, src(b, c))),
                pl.BlockSpec((CHUNK, D_MODEL), lambda b, c, ps: (src(b, c), 0)),
                pl.BlockSpec((1, D_MODEL), lambda b, c, ps: (0, 0)),
                pl.BlockSpec(memory_space=pl.ANY),
            ],
            out_specs=pl.BlockSpec((CHUNK, D_MODEL), lambda b, c, ps: (b * seq_chunks + c, 0)),
            scratch_shapes=[pltpu.VMEM((LOCAL_ROWS, D_MODEL // 2), jnp.uint32), pltpu.SemaphoreType.DMA(())],
        ),
        out_shape=jax.ShapeDtypeStruct((bsz * seq_chunks * CHUNK, D_MODEL), F32),
        compiler_params=_params(("arbitrary", "arbitrary")),
        name="combine",
    )(pstart, meta, slot, top_w, h1, norm_w, y_sorted)


def kernel(x, meta_tokens, norm_mix, w_in, conv_w, conv_b, dt_bias, a_log, d_skip, ssm_norm, w_branch, w_out, norm_ffn,
           router_w, router_b, w_gate_up, b_gate_up, w_down, b_down, norm_final):
    bsz, seq, _ = x.shape
    assert seq % CHUNK == 0 and norm_mix.shape[0] == 1
    nc = 1 + seq // CHUNK
    lp = nc * CHUNK
    t = bsz * lp

    hp = jnp.concatenate([jnp.zeros((bsz, META_PAD, D_MODEL), x.dtype),
                          jnp.broadcast_to(meta_tokens.astype(x.dtype)[None], (bsz, N_META, D_MODEL)), x], axis=1)
    hp = hp.reshape(t, D_MODEL)
    valid = jnp.asarray(np.tile(np.arange(lp) >= META_PAD, bsz).astype(np.int32).reshape(1, t))

    w_in0 = w_in[0]
    w_main = jnp.concatenate([w_in0[:, :OFF_DT], w_in0[:, OFF_DT + SSM_HEADS:]], axis=1).astype(BF16)
    w_dt = jnp.pad(w_in0[:, OFF_DT:OFF_DT + SSM_HEADS], ((0, 0), (0, DT_PAD - SSM_HEADS))).astype(BF16)
    wgu = _regroup_gate_up(w_gate_up[0])
    bgu = _regroup_bias(b_gate_up[0])
    wd = w_down[0].astype(BF16)
    bd = b_down[0][:, None, :].astype(F32)

    proj, dt_raw = _in_proj(hp, norm_mix[0].reshape(1, D_MODEL).astype(F32), w_main, w_dt)
    o_ret = _retention(proj, bsz, nc)
    o_ssd = _ssd(proj, dt_raw, conv_w[0], conv_b[0], dt_bias[0], a_log[0], d_skip[0], ssm_norm[0], bsz, nc)
    h1, un, top_e, top_w = _merge(
        o_ret, o_ssd, proj, hp, valid, w_branch[0, 0].astype(BF16), w_branch[0, 1].astype(BF16), w_out[0].astype(BF16),
        norm_ffn[0].reshape(1, D_MODEL).astype(F32), jnp.transpose(router_w[0]).astype(F32),
        router_b[0].reshape(N_EXPERTS, 1).astype(F32))

    slot, meta, counts = _rank(top_e)
    counts = counts[:, 0]
    slack = MOE_BLOCK - 1 + PIECE - ROW_ALIGN
    padded = (counts + slack) // MOE_BLOCK * MOE_BLOCK
    pend = jnp.cumsum(padded)
    pstart = (pend - padded).astype(I32)
    n_pairs = bsz * (N_META + seq) * TOP_K
    max_rows = n_pairs + bsz * nc * N_EXPERTS * (ROW_ALIGN - 1)
    n_blocks = (max_rows + N_EXPERTS * slack) // MOE_BLOCK
    blk_row = jnp.arange(n_blocks, dtype=I32) * MOE_BLOCK
    block_e = jnp.minimum(jnp.sum(pend[None, :] <= blk_row[:, None], axis=1), N_EXPERTS - 1).astype(I32)
    n_used = (pend[-1:] // MOE_BLOCK).astype(I32)
    fill = jnp.stack([pstart + counts, (padded - counts) // ROW_ALIGN]).astype(I32)

    xs = _dispatch(pstart, fill, meta, slot, un, n_blocks * MOE_BLOCK)
    ys = _experts(block_e, n_used, xs, wgu, wd, bgu, bd)
    out = _combine(pstart, meta, slot, top_w, h1, norm_final.reshape(1, D_MODEL).astype(F32), ys, bsz, nc)
    return out.reshape(bsz, seq, D_MODEL)
```

```python
import functools

import numpy as np
import jax
import jax.numpy as jnp
from jax import lax
from jax.experimental import pallas as pl
from jax.experimental.pallas import tpu as pltpu

F32 = jnp.float32
BF16 = jnp.bfloat16
I32 = jnp.int32

D_MODEL = 1024
N_META = 16
CHUNK = 128
META_PAD = CHUNK - N_META
NORM_EPS = 1e-6
RET_HEADS = 4
RET_QK_DIM = 256
RET_V_DIM = 512
RET_QK = RET_HEADS * RET_QK_DIM
RET_V = RET_HEADS * RET_V_DIM
ROPE_BASE = 10000.0
SSM_INNER = 2048
SSM_HEAD_DIM = 64
SSM_HEADS = 32
SSM_GROUPS = 4
SSM_STATE = 128
CONV_WIDTH = 4
CONV_CH = SSM_INNER + 2 * SSM_GROUPS * SSM_STATE
HEAD_PAIRS = SSM_HEADS // 2
PAIRS_PER_GROUP = HEAD_PAIRS // SSM_GROUPS
OFF_DT = 2 * RET_QK + 2 * RET_V + SSM_INNER + CONV_CH
N_MAIN = OFF_DT + 2 * D_MODEL
DT_PAD = 128
N_EXPERTS = 32
TOP_K = 4
EXPERT_FF = 1024
SWIGLU_ALPHA = 1.702
SWIGLU_LIMIT = 7.0
MOE_BLOCK = 512

VMEM_LIMIT = 56 * 1024 * 1024
HIGHEST = lax.Precision.HIGHEST


def _params(sem):
    return pltpu.CompilerParams(dimension_semantics=sem, vmem_limit_bytes=VMEM_LIMIT)


def _pick(n, candidates):
    for c in candidates:
        if n % c == 0:
            return c
    raise ValueError(f"no tile for {n} among {candidates}")


def _nt_dot(a, b, **kw):
    return lax.dot_general(a, b, (((1,), (1,)), ((), ())), preferred_element_type=F32, **kw)


def _inproj_body(x_ref, nw_ref, w_ref, wdt_ref, o_ref, dt_ref, xn_ref):
    @pl.when(pl.program_id(1) == 0)
    def _():
        x = x_ref[...]
        ms = jnp.mean(x * x, axis=-1, keepdims=True)
        xn = (x * lax.rsqrt(ms + NORM_EPS) * nw_ref[...]).astype(BF16)
        xn_ref[...] = xn
        dt_ref[...] = jnp.dot(xn, wdt_ref[...], preferred_element_type=F32)

    o_ref[...] = jnp.dot(xn_ref[...], w_ref[...], preferred_element_type=F32).astype(BF16)


def _in_proj(hp, norm_w, w_main, w_dt):
    t = hp.shape[0]
    tm = _pick(t, (1024, 512, 256, 128))
    tn = 1024
    return pl.pallas_call(
        _inproj_body,
        grid=(t // tm, N_MAIN // tn),
        in_specs=[
            pl.BlockSpec((tm, D_MODEL), lambda i, j: (i, 0)),
            pl.BlockSpec((1, D_MODEL), lambda i, j: (0, 0)),
            pl.BlockSpec((D_MODEL, tn), lambda i, j: (0, j)),
            pl.BlockSpec((D_MODEL, DT_PAD), lambda i, j: (0, 0)),
        ],
        out_specs=[
            pl.BlockSpec((tm, tn), lambda i, j: (i, j)),
            pl.BlockSpec((tm, DT_PAD), lambda i, j: (i, 0)),
        ],
        out_shape=[jax.ShapeDtypeStruct((t, N_MAIN), BF16), jax.ShapeDtypeStruct((t, DT_PAD), F32)],
        scratch_shapes=[pltpu.VMEM((tm, D_MODEL), BF16)],
        compiler_params=_params(("parallel", "arbitrary")),
        name="in_proj",
    )(hp, norm_w, w_main, w_dt)


def _ret_tables(lp):
    half = RET_QK_DIM // 2
    inv_freq = ROPE_BASE ** (-np.arange(half, dtype=np.float64) / half)
    pos = np.arange(lp, dtype=np.float64) - META_PAD
    ang = pos[:, None] * inv_freq[None, :]
    log_gamma = np.log(1.0 - 2.0 ** (-5.0 - np.arange(RET_HEADS, dtype=np.float64)))
    idx = np.arange(CHUNK, dtype=np.float64)
    dist = idx[:, None] - idx[None, :]
    intra = np.where(dist >= 0, np.exp(log_gamma[:, None, None] * np.maximum(dist, 0.0)[None]), 0.0)
    q_dec = np.exp(log_gamma[:, None] * (idx[None, :] + 1.0))
    k_dec = np.exp(log_gamma[:, None] * (CHUNK - 1.0 - idx[None, :]))
    q_dec = np.broadcast_to(q_dec[:, :, None], (RET_HEADS, CHUNK, RET_V_DIM))
    k_dec = np.broadcast_to(k_dec[:, :, None], (RET_HEADS, CHUNK, RET_QK_DIM))
    chunk_dec = tuple(float(v) for v in np.exp(log_gamma * CHUNK))
    as32 = lambda a: jnp.asarray(np.ascontiguousarray(a), F32)
    return as32(np.cos(ang)), as32(np.sin(ang)), as32(intra), as32(q_dec), as32(k_dec), chunk_dec


def _ret_body(chunk_dec, q_ref, k_ref, v_ref, g_ref, cos_ref, sin_ref, intra_ref, qd_ref, kd_ref, o_ref, st_ref):
    @pl.when(pl.program_id(1) == 0)
    def _():
        st_ref[...] = jnp.zeros_like(st_ref)

    cos = cos_ref[...]
    sin = sin_ref[...]
    half = RET_QK_DIM // 2

    def rotary(t):
        t1, t2 = t[:, :half], t[:, half:]
        return jnp.concatenate([t1 * cos - t2 * sin, t2 * cos + t1 * sin], axis=1)

    for h in range(RET_HEADS):
        qk = slice(h * RET_QK_DIM, (h + 1) * RET_QK_DIM)
        vv = slice(h * RET_V_DIM, (h + 1) * RET_V_DIM)
        qr = rotary(q_ref[:, qk].astype(F32))
        kr = rotary(k_ref[:, qk].astype(F32)) * (RET_QK_DIM ** -0.5)
        qb = qr.astype(BF16)
        vh = v_ref[:, vv]
        st = st_ref[h]
        s = _nt_dot(qb, kr.astype(BF16)) * intra_ref[h]
        y = jnp.dot(s.astype(BF16), vh, preferred_element_type=F32)
        y = y + jnp.dot(qb, st.astype(BF16), preferred_element_type=F32) * qd_ref[h]
        kdt = jnp.transpose(kr * kd_ref[h]).astype(BF16)
        st_ref[h] = st * chunk_dec[h] + jnp.dot(kdt, vh, preferred_element_type=F32)
        o = y * lax.rsqrt(jnp.mean(y * y, axis=-1, keepdims=True) + NORM_EPS)
        g = g_ref[:, vv].astype(F32)
        o_ref[:, vv] = (g * jax.nn.sigmoid(g) * o).astype(BF16)


def _retention(proj, bsz, nc):
    t = proj.shape[0]
    cos, sin, intra, q_dec, k_dec, chunk_dec = _ret_tables(nc * CHUNK)
    row = lambda b, c: b * nc + c
    const3 = lambda b, c: (0, 0, 0)
    return pl.pallas_call(
        functools.partial(_ret_body, chunk_dec),
        grid=(bsz, nc),
        in_specs=[
            pl.BlockSpec((CHUNK, RET_QK), lambda b, c: (row(b, c), 0)),
            pl.BlockSpec((CHUNK, RET_QK), lambda b, c: (row(b, c), 1)),
            pl.BlockSpec((CHUNK, RET_V), lambda b, c: (row(b, c), 1)),
            pl.BlockSpec((CHUNK, RET_V), lambda b, c: (row(b, c), 2)),
            pl.BlockSpec((CHUNK, RET_QK_DIM // 2), lambda b, c: (c, 0)),
            pl.BlockSpec((CHUNK, RET_QK_DIM // 2), lambda b, c: (c, 0)),
            pl.BlockSpec((RET_HEADS, CHUNK, CHUNK), const3),
            pl.BlockSpec((RET_HEADS, CHUNK, RET_V_DIM), const3),
            pl.BlockSpec((RET_HEADS, CHUNK, RET_QK_DIM), const3),
        ],
        out_specs=pl.BlockSpec((CHUNK, RET_V), lambda b, c: (row(b, c), 0)),
        out_shape=jax.ShapeDtypeStruct((t, RET_V), BF16),
        scratch_shapes=[pltpu.VMEM((RET_HEADS, RET_QK_DIM, RET_V_DIM), F32)],
        compiler_params=_params(("parallel", "arbitrary")),
        name="retention",
    )(proj, proj, proj, proj, cos, sin, intra, q_dec, k_dec)


CONV_COLS = 512


def _ssd_body(z_ref, xs_ref, bc_ref, dt_ref, cw_ref, cb_ref, dtb_ref, alog_ref, dsk_ref, nw_ref,
              o_ref, st_ref, full_ref, xc_ref, y_ref):
    c = pl.program_id(1)

    @pl.when(c == 0)
    def _():
        st_ref[...] = jnp.zeros_like(st_ref)
        full_ref[0:8, :] = jnp.zeros((8, CONV_CH), F32)

    rows = lax.broadcasted_iota(I32, (CHUNK, 1), 0)
    valid = jnp.logical_or(c > 0, rows >= META_PAD)

    full_ref[8:, :SSM_INNER] = xs_ref[...].astype(F32)
    full_ref[8:, SSM_INNER:] = bc_ref[...].astype(F32)
    for s in range(CONV_CH // CONV_COLS):
        cols = slice(s * CONV_COLS, (s + 1) * CONV_COLS)
        acc = cb_ref[:, cols]
        for k in range(CONV_WIDTH):
            off = 8 - (CONV_WIDTH - 1) + k
            acc = acc + cw_ref[k:k + 1, cols] * full_ref[off:off + CHUNK, cols]
        xc_ref[:, cols] = jnp.where(valid, acc * jax.nn.sigmoid(acc), 0.0)
    full_ref[0:8, :] = full_ref[CHUNK:CHUNK + 8, :]

    dtv = dt_ref[...] + dtb_ref[...]
    dt = jnp.maximum(dtv, 0.0) + jnp.log1p(jnp.exp(-jnp.abs(dtv)))
    dt = jnp.where(valid, dt, 0.0)
    a = dt * (-jnp.exp(alog_ref[...]))
    ri = lax.broadcasted_iota(I32, (CHUNK, CHUNK), 0)
    ci = lax.broadcasted_iota(I32, (CHUNK, CHUNK), 1)
    causal = ri >= ci
    acs = jnp.dot(causal.astype(F32), a, preferred_element_type=F32, precision=HIGHEST)
    last = acs[CHUNK - 1:CHUNK, :]
    acs_t = jnp.transpose(acs)
    dt_t = jnp.transpose(dt)
    w_t = jnp.transpose(jnp.exp(last - acs) * dt)
    e_last = jnp.exp(last)
    lanes = lax.broadcasted_iota(I32, (1, CHUNK), 1)
    low = lanes < SSM_HEAD_DIM

    for g in range(SSM_GROUPS):
        bm = xc_ref[:, SSM_INNER + g * SSM_STATE:SSM_INNER + (g + 1) * SSM_STATE]
        cm = xc_ref[:, SSM_INNER + (SSM_GROUPS + g) * SSM_STATE:SSM_INNER + (SSM_GROUPS + g + 1) * SSM_STATE]
        cb = _nt_dot(cm.astype(BF16), bm.astype(BF16))
        bm_t = jnp.transpose(bm)
        for pp in range(PAIRS_PER_GROUP):
            m = g * PAIRS_PER_GROUP + pp
            x_pair = xc_ref[:, m * CHUNK:(m + 1) * CHUNK]
            st_pair = st_ref[m]
            y_pair = jnp.zeros((CHUNK, CHUNK), F32)
            upd = jnp.zeros((SSM_STATE, CHUNK), F32)
            for hh in range(2):
                h = 2 * m + hh
                lane_mask = low if hh == 0 else jnp.logical_not(low)
                col = jnp.broadcast_to(acs[:, h:h + 1], (CHUNK, CHUNK))
                seg = col - acs_t[h:h + 1, :]
                dec = jnp.where(causal, jnp.exp(jnp.minimum(seg, 0.0)), 0.0)
                mat = cb * dec * dt_t[h:h + 1, :]
                lhs = jnp.concatenate([mat, cm * jnp.exp(col)], axis=1).astype(BF16)
                xm = jnp.where(lane_mask, x_pair, 0.0).astype(BF16)
                sm = jnp.where(lane_mask, st_pair, 0.0).astype(BF16)
                y_pair = y_pair + jnp.dot(lhs, jnp.concatenate([xm, sm], axis=0), preferred_element_type=F32)
                upd = upd + jnp.dot((bm_t * w_t[h:h + 1, :]).astype(BF16), xm, preferred_element_type=F32)
            decay = jnp.where(low, jnp.broadcast_to(e_last[:, 2 * m:2 * m + 1], (1, CHUNK)),
                              jnp.broadcast_to(e_last[:, 2 * m + 1:2 * m + 2], (1, CHUNK)))
            st_ref[m] = st_pair * decay + upd
            y_ref[:, m * CHUNK:(m + 1) * CHUNK] = y_pair

    gsz = SSM_INNER // SSM_GROUPS
    for g in range(SSM_GROUPS):
        cols = slice(g * gsz, (g + 1) * gsz)
        z = z_ref[:, cols].astype(F32)
        y = (y_ref[:, cols] + dsk_ref[:, cols] * xc_ref[:, cols]) * (z * jax.nn.sigmoid(z))
        y = y * lax.rsqrt(jnp.mean(y * y, axis=-1, keepdims=True) + NORM_EPS)
        o_ref[:, cols] = (y * nw_ref[:, cols]).astype(BF16)


def _ssd(proj, dt_raw, conv_w, conv_b, dt_bias, a_log, d_skip, norm_w, bsz, nc):
    t = proj.shape[0]
    row = lambda b, c: b * nc + c
    const2 = lambda b, c: (0, 0)
    pad = lambda v: jnp.pad(v.astype(F32), (0, DT_PAD - SSM_HEADS)).reshape(1, DT_PAD)
    return pl.pallas_call(
        _ssd_body,
        grid=(bsz, nc),
        in_specs=[
            pl.BlockSpec((CHUNK, SSM_INNER), lambda b, c: (row(b, c), 3)),
            pl.BlockSpec((CHUNK, SSM_INNER), lambda b, c: (row(b, c), 4)),
            pl.BlockSpec((CHUNK, 1024), lambda b, c: (row(b, c), 10)),
            pl.BlockSpec((CHUNK, DT_PAD), lambda b, c: (row(b, c), 0)),
            pl.BlockSpec((CONV_WIDTH, CONV_CH), const2),
            pl.BlockSpec((1, CONV_CH), const2),
            pl.BlockSpec((1, DT_PAD), const2),
            pl.BlockSpec((1, DT_PAD), const2),
            pl.BlockSpec((1, SSM_INNER), const2),
            pl.BlockSpec((1, SSM_INNER), const2),
        ],
        out_specs=pl.BlockSpec((CHUNK, SSM_INNER), lambda b, c: (row(b, c), 0)),
        out_shape=jax.ShapeDtypeStruct((t, SSM_INNER), BF16),
        scratch_shapes=[
            pltpu.VMEM((HEAD_PAIRS, SSM_STATE, CHUNK), F32),
            pltpu.VMEM((8 + CHUNK, CONV_CH), F32),
            pltpu.VMEM((CHUNK, CONV_CH), F32),
            pltpu.VMEM((CHUNK, SSM_INNER), F32),
        ],
        compiler_params=_params(("parallel", "arbitrary")),
        name="ssd",
    )(proj, proj, proj, dt_raw, conv_w.astype(F32), conv_b.astype(F32).reshape(1, CONV_CH), pad(dt_bias), pad(a_log),
      jnp.repeat(d_skip.astype(F32), SSM_HEAD_DIM).reshape(1, SSM_INNER), norm_w.astype(F32).reshape(1, SSM_INNER))


def _merge_body(oret_ref, ossd_ref, g0_ref, g1_ref, h_ref, valid_ref, wb0_ref, wb1_ref, wo_ref, nw_ref, rw_ref, rb_ref,
                h1_ref, un_ref, e_ref, w_ref):
    y_ret = jnp.dot(oret_ref[...], wb0_ref[...], preferred_element_type=F32)
    y_ssd = jnp.dot(ossd_ref[...], wb1_ref[...], preferred_element_type=F32)
    merged = (jax.nn.sigmoid(g0_ref[...].astype(F32)) * y_ret + jax.nn.sigmoid(g1_ref[...].astype(F32)) * y_ssd)
    h1 = h_ref[...] + jnp.dot(merged.astype(BF16), wo_ref[...], preferred_element_type=F32)
    h1_ref[...] = h1
    un = h1 * lax.rsqrt(jnp.mean(h1 * h1, axis=-1, keepdims=True) + NORM_EPS) * nw_ref[...]
    un_ref[...] = un.astype(BF16)

    logits = _nt_dot(rw_ref[...], un, precision=HIGHEST) + rb_ref[...]
    tm = logits.shape[1]
    eidx = lax.broadcasted_iota(I32, (N_EXPERTS, tm), 0)
    vals, ids = [], []
    for _ in range(TOP_K):
        best = jnp.max(logits, axis=0, keepdims=True)
        arg = jnp.min(jnp.where(logits == best, eidx, N_EXPERTS), axis=0, keepdims=True)
        vals.append(best)
        ids.append(arg)
        logits = jnp.where(eidx == arg, -jnp.inf, logits)
    ex = [jnp.exp(v - vals[0]) for v in vals]
    denom = ex[0] + ex[1] + ex[2] + ex[3]
    is_tok = valid_ref[...] > 0
    e_ref[...] = jnp.concatenate([jnp.where(is_tok, i, -1) for i in ids] + [jnp.zeros((8 - TOP_K, tm), I32)], axis=0)
    w_ref[...] = jnp.concatenate([x / denom for x in ex] + [jnp.zeros((8 - TOP_K, tm), F32)], axis=0)


def _merge(o_ret, o_ssd, proj, hp, valid, wb0, wb1, wo, norm_w, router_wt, router_b):
    t = hp.shape[0]
    tm = _pick(t, (256, 128))
    gate_blk = OFF_DT // D_MODEL
    assert gate_blk * D_MODEL == OFF_DT
    const2 = lambda i: (0, 0)
    return pl.pallas_call(
        _merge_body,
        grid=(t // tm,),
        in_specs=[
            pl.BlockSpec((tm, RET_V), lambda i: (i, 0)),
            pl.BlockSpec((tm, SSM_INNER), lambda i: (i, 0)),
            pl.BlockSpec((tm, D_MODEL), lambda i: (i, gate_blk)),
            pl.BlockSpec((tm, D_MODEL), lambda i: (i, gate_blk + 1)),
            pl.BlockSpec((tm, D_MODEL), lambda i: (i, 0)),
            pl.BlockSpec((1, tm), lambda i: (0, i)),
            pl.BlockSpec((RET_V, D_MODEL), const2),
            pl.BlockSpec((SSM_INNER, D_MODEL), const2),
            pl.BlockSpec((D_MODEL, D_MODEL), const2),
            pl.BlockSpec((1, D_MODEL), const2),
            pl.BlockSpec((N_EXPERTS, D_MODEL), const2),
            pl.BlockSpec((N_EXPERTS, 1), const2),
        ],
        out_specs=[
            pl.BlockSpec((tm, D_MODEL), lambda i: (i, 0)),
            pl.BlockSpec((tm, D_MODEL), lambda i: (i, 0)),
            pl.BlockSpec((8, tm), lambda i: (0, i)),
            pl.BlockSpec((8, tm), lambda i: (0, i)),
        ],
        out_shape=[
            jax.ShapeDtypeStruct((t, D_MODEL), F32),
            jax.ShapeDtypeStruct((t, D_MODEL), BF16),
            jax.ShapeDtypeStruct((8, t), I32),
            jax.ShapeDtypeStruct((8, t), F32),
        ],
        compiler_params=_params(("parallel",)),
        name="merge_router",
    )(o_ret, o_ssd, proj, proj, hp, valid, wb0, wb1, wo, norm_w, router_wt, router_b)


ROW_ALIGN = 8
PIECE = 16
LOCAL_ROWS = 1024
META_BEFORE, META_COUNT, META_LSTART = 0, 1, 2


def _rank_body(tiles, e_ref, slot_ref, meta_ref, cnt_ref, run_ref):
    @pl.when(pl.program_id(0) == 0)
    def _():
        run_ref[...] = jnp.zeros_like(run_ref)

    eidx = lax.broadcasted_iota(I32, (N_EXPERTS, CHUNK), 0)
    lane = lax.broadcasted_iota(I32, (N_EXPERTS, CHUNK), 1)
    ri = lax.broadcasted_iota(I32, (CHUNK, CHUNK), 0)
    ci = lax.broadcasted_iota(I32, (CHUNK, CHUNK), 1)
    upper = jnp.where(ri <= ci, 1.0, 0.0).astype(BF16)
    ones = jnp.ones((CHUNK, CHUNK), BF16)
    ei = lax.broadcasted_iota(I32, (N_EXPERTS, N_EXPERTS), 0)
    ej = lax.broadcasted_iota(I32, (N_EXPERTS, N_EXPERTS), 1)
    lower = jnp.where(ej < ei, 1.0, 0.0).astype(BF16)
    round_up = lambda v, m: jnp.floor((v + (m - 1)) * (1.0 / m)) * m
    diag = lambda m: jnp.sum(jnp.where(eidx == lane, m, 0.0), axis=0, keepdims=True)

    run = run_ref[...]
    for s in range(tiles):
        e_tile = e_ref[:, s * CHUNK:(s + 1) * CHUNK]
        seen = jnp.zeros((N_EXPERTS, CHUNK), F32)
        hits, ranks = [], []
        for k in range(TOP_K):
            hit = eidx == e_tile[k:k + 1, :]
            oh = jnp.where(hit, 1.0, 0.0).astype(BF16)
            ranks.append(seen + jnp.dot(oh, upper, preferred_element_type=F32) - 1.0)
            seen = seen + jnp.dot(oh, ones, preferred_element_type=F32)
            hits.append(hit)
        count = seen
        lstart = jnp.dot(lower, round_up(count, PIECE).astype(BF16), preferred_element_type=F32)
        slots = []
        for k in range(TOP_K):
            slot = jnp.sum(jnp.where(hits[k], lstart + ranks[k], 0.0), axis=0, keepdims=True)
            slots.append(jnp.where(e_tile[k:k + 1, :] >= 0, slot, -1.0))
        slot_ref[:, s * CHUNK:(s + 1) * CHUNK] = jnp.concatenate(
            slots + [jnp.full((8 - TOP_K, CHUNK), -1.0, F32)], axis=0).astype(I32)
        meta_ref[s] = jnp.concatenate(
            [diag(run), diag(count), diag(lstart), jnp.zeros((5, CHUNK), F32)], axis=0).astype(I32)
        run = run + round_up(count, ROW_ALIGN)
    run_ref[...] = run
    cnt_ref[...] = run.astype(I32)


def _rank(top_e):
    t = top_e.shape[1]
    n_tiles = t // CHUNK
    tiles = _pick(n_tiles, (4, 2, 1))
    return pl.pallas_call(
        functools.partial(_rank_body, tiles),
        grid=(n_tiles // tiles,),
        in_specs=[pl.BlockSpec((8, tiles * CHUNK), lambda i: (0, i))],
        out_specs=[
            pl.BlockSpec((8, tiles * CHUNK), lambda i: (0, i)),
            pl.BlockSpec((tiles, 8, CHUNK), lambda i: (i, 0, 0)),
            pl.BlockSpec((N_EXPERTS, CHUNK), lambda i: (0, 0)),
        ],
        out_shape=[
            jax.ShapeDtypeStruct((8, t), I32),
            jax.ShapeDtypeStruct((n_tiles, 8, CHUNK), I32),
            jax.ShapeDtypeStruct((N_EXPERTS, CHUNK), I32),
        ],
        scratch_shapes=[pltpu.VMEM((N_EXPERTS, CHUNK), F32)],
        compiler_params=_params(("arbitrary",)),
        name="rank",
    )(top_e)


def _pack_halves(v):
    bits = pltpu.bitcast(v, jnp.uint32)
    return (bits[:, :D_MODEL // 2] >> 16) | bits[:, D_MODEL // 2:]


def _unpack_halves(p):
    lo = pltpu.bitcast(p << 16, F32)
    hi = pltpu.bitcast(p & jnp.uint32(0xFFFF0000), F32)
    return jnp.concatenate([lo, hi], axis=1).astype(BF16)


def _piece_copies(pstart_ref, meta_ref, make_copy, act):
    def per_expert(e, carry):
        n_pieces = lax.shift_right_logical(meta_ref[0, META_COUNT, e] + (PIECE - 1), PIECE.bit_length() - 1)
        local0 = meta_ref[0, META_LSTART, e]
        sorted0 = pstart_ref[e] + meta_ref[0, META_BEFORE, e]

        def per_piece(j, c):
            act(make_copy(pl.multiple_of(local0 + j * PIECE, ROW_ALIGN), pl.multiple_of(sorted0 + j * PIECE, ROW_ALIGN)))
            return c

        return lax.fori_loop(0, n_pieces, per_piece, carry)

    lax.fori_loop(0, N_EXPERTS, per_expert, 0)


def _dispatch_body(pstart_ref, fill_ref, meta_ref, prev_meta_ref, slot_ref, un_ref, xs_ref, loc_ref, zero_ref, sems):
    step = pl.program_id(0)
    last = pl.num_programs(0) - 1
    buf = step % 2

    srow = lax.broadcasted_iota(I32, (LOCAL_ROWS, CHUNK), 0)
    onehot = jnp.zeros((LOCAL_ROWS, CHUNK), F32)
    for k in range(TOP_K):
        onehot = onehot + jnp.where(srow == slot_ref[k:k + 1, :], 1.0, 0.0)
    loc_ref[buf] = _pack_halves(jnp.dot(onehot.astype(BF16), un_ref[...], preferred_element_type=F32))

    def copies_of(b):
        def make_copy(local_row, sorted_row):
            return pltpu.make_async_copy(loc_ref.at[b, pl.ds(local_row, PIECE), :],
                                         xs_ref.at[pl.ds(sorted_row, PIECE), :], sems.at[b])
        return make_copy

    @pl.when(step > 0)
    def _():
        _piece_copies(pstart_ref, prev_meta_ref, copies_of(1 - buf), lambda cp: cp.wait())

    _piece_copies(pstart_ref, meta_ref, copies_of(buf), lambda cp: cp.start())

    @pl.when(step == last)
    def _():
        _piece_copies(pstart_ref, meta_ref, copies_of(buf), lambda cp: cp.wait())
        zero_ref[...] = jnp.zeros_like(zero_ref)

        def fill(act):
            def per_expert(e, carry):
                def per_piece(j, c):
                    row = pl.multiple_of(fill_ref[0, e] + j * ROW_ALIGN, ROW_ALIGN)
                    act(pltpu.make_async_copy(zero_ref, xs_ref.at[pl.ds(row, ROW_ALIGN), :], sems.at[0]))
                    return c
                return lax.fori_loop(0, fill_ref[1, e], per_piece, carry)
            lax.fori_loop(0, N_EXPERTS, per_expert, 0)

        fill(lambda cp: cp.start())
        fill(lambda cp: cp.wait())


def _dispatch(pstart, fill, meta, slot, un, n_rows):
    t = un.shape[0]
    return pl.pallas_call(
        _dispatch_body,
        grid_spec=pltpu.PrefetchScalarGridSpec(
            num_scalar_prefetch=2,
            grid=(t // CHUNK,),
            in_specs=[
                pl.BlockSpec((1, 8, CHUNK), lambda i, ps, fl: (i, 0, 0), memory_space=pltpu.SMEM),
                pl.BlockSpec((1, 8, CHUNK), lambda i, ps, fl: (jnp.maximum(i - 1, 0), 0, 0), memory_space=pltpu.SMEM),
                pl.BlockSpec((8, CHUNK), lambda i, ps, fl: (0, i)),
                pl.BlockSpec((CHUNK, D_MODEL), lambda i, ps, fl: (i, 0)),
            ],
            out_specs=pl.BlockSpec(memory_space=pl.ANY),
            scratch_shapes=[pltpu.VMEM((2, LOCAL_ROWS, D_MODEL // 2), jnp.uint32),
                            pltpu.VMEM((ROW_ALIGN, D_MODEL // 2), jnp.uint32), pltpu.SemaphoreType.DMA((2,))],
        ),
        out_shape=jax.ShapeDtypeStruct((n_rows, D_MODEL // 2), jnp.uint32),
        compiler_params=_params(("arbitrary",)),
        name="dispatch",
    )(pstart, fill, meta, meta, slot, un)


GU_SLAB = 256
GU_HALF = GU_SLAB // 2


def _regroup_body(w_ref, p_ref, o_ref):
    for s in range(2 * EXPERT_FF // GU_SLAB):
        cols = slice(s * GU_SLAB, (s + 1) * GU_SLAB)
        o_ref[0, :, cols] = jnp.dot(w_ref[0, :, cols].astype(BF16), p_ref[...], preferred_element_type=F32).astype(BF16)


def _regroup_gate_up(w_gate_up):
    perm = np.zeros((GU_SLAB, GU_SLAB), np.float32)
    perm[2 * np.arange(GU_HALF), np.arange(GU_HALF)] = 1.0
    perm[2 * np.arange(GU_HALF) + 1, GU_HALF + np.arange(GU_HALF)] = 1.0
    blk = (1, D_MODEL, 2 * EXPERT_FF)
    return pl.pallas_call(
        _regroup_body,
        grid=(N_EXPERTS,),
        in_specs=[pl.BlockSpec(blk, lambda e: (e, 0, 0)), pl.BlockSpec((GU_SLAB, GU_SLAB), lambda e: (0, 0))],
        out_specs=pl.BlockSpec(blk, lambda e: (e, 0, 0)),
        out_shape=jax.ShapeDtypeStruct((N_EXPERTS, D_MODEL, 2 * EXPERT_FF), BF16),
        compiler_params=_params(("parallel",)),
        name="regroup_gate_up",
    )(w_gate_up, jnp.asarray(perm, BF16))


def _regroup_bias(b_gate_up):
    b = b_gate_up.astype(F32).reshape(N_EXPERTS, 2 * EXPERT_FF // GU_SLAB, GU_HALF, 2)
    return jnp.transpose(b, (0, 1, 3, 2)).reshape(N_EXPERTS, 1, 2 * EXPERT_FF)


def _expert_body(be_ref, nu_ref, x_ref, wgu_ref, wd_ref, bgu_ref, bd_ref, y_ref):
    del be_ref

    @pl.when(pl.program_id(0) >= nu_ref[0])
    def _():
        y_ref[...] = jnp.zeros_like(y_ref)

    @pl.when(pl.program_id(0) < nu_ref[0])
    def _():
        x = _unpack_halves(x_ref[...])
        gu = jnp.dot(x, wgu_ref[0], preferred_element_type=F32) + bgu_ref[0]
        acts = []
        for s in range(2 * EXPERT_FF // GU_SLAB):
            gate = jnp.minimum(gu[:, s * GU_SLAB:s * GU_SLAB + GU_HALF], SWIGLU_LIMIT)
            up = jnp.clip(gu[:, s * GU_SLAB + GU_HALF:(s + 1) * GU_SLAB], -SWIGLU_LIMIT, SWIGLU_LIMIT)
            acts.append(((up + 1.0) * (gate * jax.nn.sigmoid(SWIGLU_ALPHA * gate))).astype(BF16))
        act = jnp.concatenate(acts, axis=1)
        y = jnp.dot(act, wd_ref[0], preferred_element_type=F32) + bd_ref[0]
        y_ref[...] = _pack_halves(y.astype(BF16).astype(F32))


def _experts(block_e, n_used, xs, wgu, wd, bgu, bd):
    n_rows = xs.shape[0]
    n_blocks = n_rows // MOE_BLOCK
    wspec = lambda shape: pl.BlockSpec((1,) + shape, lambda i, be, nu: (be[i], 0, 0))
    return pl.pallas_call(
        _expert_body,
        grid_spec=pltpu.PrefetchScalarGridSpec(
            num_scalar_prefetch=2,
            grid=(n_blocks,),
            in_specs=[
                pl.BlockSpec((MOE_BLOCK, D_MODEL // 2), lambda i, be, nu: (jnp.minimum(i, nu[0] - 1), 0)),
                wspec((D_MODEL, 2 * EXPERT_FF)),
                wspec((EXPERT_FF, D_MODEL)),
                wspec((1, 2 * EXPERT_FF)),
                wspec((1, D_MODEL)),
            ],
            out_specs=pl.BlockSpec((MOE_BLOCK, D_MODEL // 2), lambda i, be, nu: (i, 0)),
        ),
        out_shape=jax.ShapeDtypeStruct((n_rows, D_MODEL // 2), jnp.uint32),
        compiler_params=_params(("arbitrary",)),
        name="experts",
    )(block_e, n_used, xs, wgu, wd, bgu, bd)


def _combine_body(pstart_ref, meta_ref, next_meta_ref, slot_ref, w_ref, h1_ref, nw_ref, y_hbm, o_ref, ybuf, sems):
    step = pl.program_id(0)
    buf = step % 2

    def copies_of(b):
        def make_copy(local_row, sorted_row):
            return pltpu.make_async_copy(y_hbm.at[pl.ds(sorted_row, PIECE), :],
                                         ybuf.at[b, pl.ds(local_row, PIECE), :], sems.at[b])
        return make_copy

    @pl.when(step == 0)
    def _():
        ybuf[...] = jnp.zeros_like(ybuf)
        _piece_copies(pstart_ref, meta_ref, copies_of(buf), lambda cp: cp.start())

    @pl.when(step + 1 < pl.num_programs(0))
    def _():
        _piece_copies(pstart_ref, next_meta_ref, copies_of(1 - buf), lambda cp: cp.start())

    _piece_copies(pstart_ref, meta_ref, copies_of(buf), lambda cp: cp.wait())

    srow = lax.broadcasted_iota(I32, (LOCAL_ROWS, CHUNK), 0)
    wmat = jnp.zeros((LOCAL_ROWS, CHUNK), F32)
    for k in range(TOP_K):
        wmat = wmat + jnp.where(srow == slot_ref[k:k + 1, :], w_ref[k:k + 1, :], 0.0)
    wmat = jnp.transpose(wmat)
    w_hi = wmat.astype(BF16)
    w_lo = (wmat - w_hi.astype(F32)).astype(BF16)
    y = _unpack_halves(ybuf[buf])
    f = jnp.dot(w_hi, y, preferred_element_type=F32) + jnp.dot(w_lo, y, preferred_element_type=F32)
    h2 = h1_ref[...] + f
    o_ref[...] = h2 * lax.rsqrt(jnp.mean(h2 * h2, axis=-1, keepdims=True) + NORM_EPS) * nw_ref[...]


def _combine(pstart, meta, slot, top_w, h1, norm_w, y_sorted, bsz, nc):
    seq_chunks = nc - 1
    n_out = bsz * seq_chunks
    src = lambda i: i + i // seq_chunks + 1
    nxt = lambda i: jnp.minimum(i + 1, n_out - 1)
    return pl.pallas_call(
        _combine_body,
        grid_spec=pltpu.PrefetchScalarGridSpec(
            num_scalar_prefetch=1,
            grid=(n_out,),
            in_specs=[
                pl.BlockSpec((1, 8, CHUNK), lambda i, ps: (src(i), 0, 0), memory_space=pltpu.SMEM),
                pl.BlockSpec((1, 8, CHUNK), lambda i, ps: (src(nxt(i)), 0, 0), memory_space=pltpu.SMEM),
                pl.BlockSpec((8, CHUNK), lambda i, ps: (0, src(i))),
                pl.BlockSpec((8, CHUNK), lambda i, ps: (0, src(i))),
                pl.BlockSpec((CHUNK, D_MODEL), lambda i, ps: (src(i), 0)),
                pl.BlockSpec((1, D_MODEL), lambda i, ps: (0, 0)),
                pl.BlockSpec(memory_space=pl.ANY),
            ],
            out_specs=pl.BlockSpec((CHUNK, D_MODEL), lambda i, ps: (i, 0)),
            scratch_shapes=[pltpu.VMEM((2, LOCAL_ROWS, D_MODEL // 2), jnp.uint32), pltpu.SemaphoreType.DMA((2,))],
        ),
        out_shape=jax.ShapeDtypeStruct((n_out * CHUNK, D_MODEL), F32),
        compiler_params=_params(("arbitrary",)),
        name="combine",
    )(pstart, meta, meta, slot, top_w, h1, norm_w, y_sorted)


def kernel(x, meta_tokens, norm_mix, w_in, conv_w, conv_b, dt_bias, a_log, d_skip, ssm_norm, w_branch, w_out, norm_ffn,
           router_w, router_b, w_gate_up, b_gate_up, w_down, b_down, norm_final):
    bsz, seq, _ = x.shape
    assert seq % CHUNK == 0 and norm_mix.shape[0] == 1
    nc = 1 + seq // CHUNK
    lp = nc * CHUNK
    t = bsz * lp

    hp = jnp.concatenate([jnp.zeros((bsz, META_PAD, D_MODEL), x.dtype),
                          jnp.broadcast_to(meta_tokens.astype(x.dtype)[None], (bsz, N_META, D_MODEL)), x], axis=1)
    hp = hp.reshape(t, D_MODEL)
    valid = jnp.asarray(np.tile(np.arange(lp) >= META_PAD, bsz).astype(np.int32).reshape(1, t))

    w_in0 = w_in[0]
    w_main = jnp.concatenate([w_in0[:, :OFF_DT], w_in0[:, OFF_DT + SSM_HEADS:]], axis=1).astype(BF16)
    w_dt = jnp.pad(w_in0[:, OFF_DT:OFF_DT + SSM_HEADS], ((0, 0), (0, DT_PAD - SSM_HEADS))).astype(BF16)
    wgu = _regroup_gate_up(w_gate_up[0])
    bgu = _regroup_bias(b_gate_up[0])
    wd = w_down[0].astype(BF16)
    bd = b_down[0][:, None, :].astype(F32)

    proj, dt_raw = _in_proj(hp, norm_mix[0].reshape(1, D_MODEL).astype(F32), w_main, w_dt)
    o_ret = _retention(proj, bsz, nc)
    o_ssd = _ssd(proj, dt_raw, conv_w[0], conv_b[0], dt_bias[0], a_log[0], d_skip[0], ssm_norm[0], bsz, nc)
    h1, un, top_e, top_w = _merge(
        o_ret, o_ssd, proj, hp, valid, w_branch[0, 0].astype(BF16), w_branch[0, 1].astype(BF16), w_out[0].astype(BF16),
        norm_ffn[0].reshape(1, D_MODEL).astype(F32), jnp.transpose(router_w[0]).astype(F32),
        router_b[0].reshape(N_EXPERTS, 1).astype(F32))

    slot, meta, counts = _rank(top_e)
    counts = counts[:, 0]
    slack = MOE_BLOCK - 1 + PIECE - ROW_ALIGN
    padded = (counts + slack) // MOE_BLOCK * MOE_BLOCK
    pend = jnp.cumsum(padded)
    pstart = (pend - padded).astype(I32)
    n_pairs = bsz * (N_META + seq) * TOP_K
    max_rows = n_pairs + bsz * nc * N_EXPERTS * (ROW_ALIGN - 1)
    n_blocks = (max_rows + N_EXPERTS * slack) // MOE_BLOCK
    blk_row = jnp.arange(n_blocks, dtype=I32) * MOE_BLOCK
    block_e = jnp.minimum(jnp.sum(pend[None, :] <= blk_row[:, None], axis=1), N_EXPERTS - 1).astype(I32)
    n_used = (pend[-1:] // MOE_BLOCK).astype(I32)
    fill = jnp.stack([pstart + counts, (padded - counts) // ROW_ALIGN]).astype(I32)

    xs = _dispatch(pstart, fill, meta, slot, un, n_blocks * MOE_BLOCK)
    ys = _experts(block_e, n_used, xs, wgu, wd, bgu, bd)
    out = _combine(pstart, meta, slot, top_w, h1, norm_final.reshape(1, D_MODEL).astype(F32), ys, bsz, nc)
    return out.reshape(bsz, seq, D_MODEL)
```

```python
import functools

import numpy as np
import jax
import jax.numpy as jnp
from jax import lax
from jax.experimental import pallas as pl
from jax.experimental.pallas import tpu as pltpu

F32 = jnp.float32
BF16 = jnp.bfloat16
I32 = jnp.int32

D_MODEL = 1024
N_META = 16
CHUNK = 128
META_PAD = CHUNK - N_META
NORM_EPS = 1e-6
RET_HEADS = 4
RET_QK_DIM = 256
RET_V_DIM = 512
RET_QK = RET_HEADS * RET_QK_DIM
RET_V = RET_HEADS * RET_V_DIM
ROPE_BASE = 10000.0
SSM_INNER = 2048
SSM_HEAD_DIM = 64
SSM_HEADS = 32
SSM_GROUPS = 4
SSM_STATE = 128
CONV_WIDTH = 4
CONV_CH = SSM_INNER + 2 * SSM_GROUPS * SSM_STATE
HEAD_PAIRS = SSM_HEADS // 2
PAIRS_PER_GROUP = HEAD_PAIRS // SSM_GROUPS
OFF_DT = 2 * RET_QK + 2 * RET_V + SSM_INNER + CONV_CH
N_MAIN = OFF_DT + 2 * D_MODEL
DT_PAD = 128
N_EXPERTS = 32
TOP_K = 4
EXPERT_FF = 1024
SWIGLU_ALPHA = 1.702
SWIGLU_LIMIT = 7.0
MOE_BLOCK = 512

VMEM_LIMIT = 56 * 1024 * 1024
HIGHEST = lax.Precision.HIGHEST


def _params(sem):
    return pltpu.CompilerParams(dimension_semantics=sem, vmem_limit_bytes=VMEM_LIMIT)


def _pick(n, candidates):
    for c in candidates:
        if n % c == 0:
            return c
    raise ValueError(f"no tile for {n} among {candidates}")


def _nt_dot(a, b, **kw):
    return lax.dot_general(a, b, (((1,), (1,)), ((), ())), preferred_element_type=F32, **kw)


def _inproj_body(x_ref, nw_ref, w_ref, wdt_ref, o_ref, dt_ref, xn_ref):
    @pl.when(pl.program_id(1) == 0)
    def _():
        x = x_ref[...]
        ms = jnp.mean(x * x, axis=-1, keepdims=True)
        xn = (x * lax.rsqrt(ms + NORM_EPS) * nw_ref[...]).astype(BF16)
        xn_ref[...] = xn
        dt_ref[...] = jnp.dot(xn, wdt_ref[...], preferred_element_type=F32)

    o_ref[...] = jnp.dot(xn_ref[...], w_ref[...], preferred_element_type=F32).astype(BF16)


def _in_proj(hp, norm_w, w_main, w_dt):
    t = hp.shape[0]
    tm = _pick(t, (1024, 512, 256, 128))
    tn = 1024
    return pl.pallas_call(
        _inproj_body,
        grid=(t // tm, N_MAIN // tn),
        in_specs=[
            pl.BlockSpec((tm, D_MODEL), lambda i, j: (i, 0)),
            pl.BlockSpec((1, D_MODEL), lambda i, j: (0, 0)),
            pl.BlockSpec((D_MODEL, tn), lambda i, j: (0, j)),
            pl.BlockSpec((D_MODEL, DT_PAD), lambda i, j: (0, 0)),
        ],
        out_specs=[
            pl.BlockSpec((tm, tn), lambda i, j: (i, j)),
            pl.BlockSpec((tm, DT_PAD), lambda i, j: (i, 0)),
        ],
        out_shape=[jax.ShapeDtypeStruct((t, N_MAIN), BF16), jax.ShapeDtypeStruct((t, DT_PAD), F32)],
        scratch_shapes=[pltpu.VMEM((tm, D_MODEL), BF16)],
        compiler_params=_params(("parallel", "arbitrary")),
        name="in_proj",
    )(hp, norm_w, w_main, w_dt)


def _ret_tables(lp):
    half = RET_QK_DIM // 2
    inv_freq = ROPE_BASE ** (-np.arange(half, dtype=np.float64) / half)
    pos = np.arange(lp, dtype=np.float64) - META_PAD
    ang = pos[:, None] * inv_freq[None, :]
    log_gamma = np.log(1.0 - 2.0 ** (-5.0 - np.arange(RET_HEADS, dtype=np.float64)))
    idx = np.arange(CHUNK, dtype=np.float64)
    dist = idx[:, None] - idx[None, :]
    intra = np.where(dist >= 0, np.exp(log_gamma[:, None, None] * np.maximum(dist, 0.0)[None]), 0.0)
    q_dec = np.exp(log_gamma[:, None] * (idx[None, :] + 1.0))
    k_dec = np.exp(log_gamma[:, None] * (CHUNK - 1.0 - idx[None, :]))
    q_dec = np.broadcast_to(q_dec[:, :, None], (RET_HEADS, CHUNK, RET_V_DIM))
    k_dec = np.broadcast_to(k_dec[:, :, None], (RET_HEADS, CHUNK, RET_QK_DIM))
    chunk_dec = tuple(float(v) for v in np.exp(log_gamma * CHUNK))
    as32 = lambda a: jnp.asarray(np.ascontiguousarray(a), F32)
    return as32(np.cos(ang)), as32(np.sin(ang)), as32(intra), as32(q_dec), as32(k_dec), chunk_dec


def _ret_body(chunk_dec, q_ref, k_ref, v_ref, g_ref, cos_ref, sin_ref, intra_ref, qd_ref, kd_ref, o_ref, st_ref):
    @pl.when(pl.program_id(1) == 0)
    def _():
        st_ref[...] = jnp.zeros_like(st_ref)

    cos = cos_ref[...]
    sin = sin_ref[...]
    half = RET_QK_DIM // 2

    def rotary(t):
        t1, t2 = t[:, :half], t[:, half:]
        return jnp.concatenate([t1 * cos - t2 * sin, t2 * cos + t1 * sin], axis=1)

    for h in range(RET_HEADS):
        qk = slice(h * RET_QK_DIM, (h + 1) * RET_QK_DIM)
        vv = slice(h * RET_V_DIM, (h + 1) * RET_V_DIM)
        qr = rotary(q_ref[:, qk].astype(F32))
        kr = rotary(k_ref[:, qk].astype(F32)) * (RET_QK_DIM ** -0.5)
        qb = qr.astype(BF16)
        vh = v_ref[:, vv]
        st = st_ref[h]
        s = _nt_dot(qb, kr.astype(BF16)) * intra_ref[h]
        y = jnp.dot(s.astype(BF16), vh, preferred_element_type=F32)
        y = y + jnp.dot(qb, st.astype(BF16), preferred_element_type=F32) * qd_ref[h]
        kdt = jnp.transpose(kr * kd_ref[h]).astype(BF16)
        st_ref[h] = st * chunk_dec[h] + jnp.dot(kdt, vh, preferred_element_type=F32)
        o = y * lax.rsqrt(jnp.mean(y * y, axis=-1, keepdims=True) + NORM_EPS)
        g = g_ref[:, vv].astype(F32)
        o_ref[:, vv] = (g * jax.nn.sigmoid(g) * o).astype(BF16)


def _retention(proj, bsz, nc):
    t = proj.shape[0]
    cos, sin, intra, q_dec, k_dec, chunk_dec = _ret_tables(nc * CHUNK)
    row = lambda b, c: b * nc + c
    const3 = lambda b, c: (0, 0, 0)
    return pl.pallas_call(
        functools.partial(_ret_body, chunk_dec),
        grid=(bsz, nc),
        in_specs=[
            pl.BlockSpec((CHUNK, RET_QK), lambda b, c: (row(b, c), 0)),
            pl.BlockSpec((CHUNK, RET_QK), lambda b, c: (row(b, c), 1)),
            pl.BlockSpec((CHUNK, RET_V), lambda b, c: (row(b, c), 1)),
            pl.BlockSpec((CHUNK, RET_V), lambda b, c: (row(b, c), 2)),
            pl.BlockSpec((CHUNK, RET_QK_DIM // 2), lambda b, c: (c, 0)),
            pl.BlockSpec((CHUNK, RET_QK_DIM // 2), lambda b, c: (c, 0)),
            pl.BlockSpec((RET_HEADS, CHUNK, CHUNK), const3),
            pl.BlockSpec((RET_HEADS, CHUNK, RET_V_DIM), const3),
            pl.BlockSpec((RET_HEADS, CHUNK, RET_QK_DIM), const3),
        ],
        out_specs=pl.BlockSpec((CHUNK, RET_V), lambda b, c: (row(b, c), 0)),
        out_shape=jax.ShapeDtypeStruct((t, RET_V), BF16),
        scratch_shapes=[pltpu.VMEM((RET_HEADS, RET_QK_DIM, RET_V_DIM), F32)],
        compiler_params=_params(("parallel", "arbitrary")),
        name="retention",
    )(proj, proj, proj, proj, cos, sin, intra, q_dec, k_dec)


CONV_COLS = 512


def _ssd_body(z_ref, xs_ref, bc_ref, dt_ref, cw_ref, cb_ref, dtb_ref, alog_ref, dsk_ref, nw_ref,
              o_ref, st_ref, full_ref, xc_ref, y_ref):
    c = pl.program_id(1)

    @pl.when(c == 0)
    def _():
        st_ref[...] = jnp.zeros_like(st_ref)
        full_ref[0:8, :] = jnp.zeros((8, CONV_CH), F32)

    rows = lax.broadcasted_iota(I32, (CHUNK, 1), 0)
    valid = jnp.logical_or(c > 0, rows >= META_PAD)

    full_ref[8:, :SSM_INNER] = xs_ref[...].astype(F32)
    full_ref[8:, SSM_INNER:] = bc_ref[...].astype(F32)
    for s in range(CONV_CH // CONV_COLS):
        cols = slice(s * CONV_COLS, (s + 1) * CONV_COLS)
        acc = cb_ref[:, cols]
        for k in range(CONV_WIDTH):
            off = 8 - (CONV_WIDTH - 1) + k
            acc = acc + cw_ref[k:k + 1, cols] * full_ref[off:off + CHUNK, cols]
        xc_ref[:, cols] = jnp.where(valid, acc * jax.nn.sigmoid(acc), 0.0)
    full_ref[0:8, :] = full_ref[CHUNK:CHUNK + 8, :]

    dtv = dt_ref[...] + dtb_ref[...]
    dt = jnp.maximum(dtv, 0.0) + jnp.log1p(jnp.exp(-jnp.abs(dtv)))
    dt = jnp.where(valid, dt, 0.0)
    a = dt * (-jnp.exp(alog_ref[...]))
    ri = lax.broadcasted_iota(I32, (CHUNK, CHUNK), 0)
    ci = lax.broadcasted_iota(I32, (CHUNK, CHUNK), 1)
    causal = ri >= ci
    acs = jnp.dot(causal.astype(F32), a, preferred_element_type=F32, precision=HIGHEST)
    last = acs[CHUNK - 1:CHUNK, :]
    acs_t = jnp.transpose(acs)
    dt_t = jnp.transpose(dt)
    w_t = jnp.transpose(jnp.exp(last - acs) * dt)
    e_last = jnp.exp(last)
    lanes = lax.broadcasted_iota(I32, (1, CHUNK), 1)
    low = lanes < SSM_HEAD_DIM

    for g in range(SSM_GROUPS):
        bm = xc_ref[:, SSM_INNER + g * SSM_STATE:SSM_INNER + (g + 1) * SSM_STATE]
        cm = xc_ref[:, SSM_INNER + (SSM_GROUPS + g) * SSM_STATE:SSM_INNER + (SSM_GROUPS + g + 1) * SSM_STATE]
        cb = _nt_dot(cm.astype(BF16), bm.astype(BF16))
        bm_t = jnp.transpose(bm)
        for pp in range(PAIRS_PER_GROUP):
            m = g * PAIRS_PER_GROUP + pp
            x_pair = xc_ref[:, m * CHUNK:(m + 1) * CHUNK]
            st_pair = st_ref[m]
            y_pair = jnp.zeros((CHUNK, CHUNK), F32)
            upd = jnp.zeros((SSM_STATE, CHUNK), F32)
            for hh in range(2):
                h = 2 * m + hh
                lane_mask = low if hh == 0 else jnp.logical_not(low)
                col = jnp.broadcast_to(acs[:, h:h + 1], (CHUNK, CHUNK))
                seg = col - acs_t[h:h + 1, :]
                dec = jnp.where(causal, jnp.exp(jnp.minimum(seg, 0.0)), 0.0)
                mat = cb * dec * dt_t[h:h + 1, :]
                lhs = jnp.concatenate([mat, cm * jnp.exp(col)], axis=1).astype(BF16)
                xm = jnp.where(lane_mask, x_pair, 0.0).astype(BF16)
                sm = jnp.where(lane_mask, st_pair, 0.0).astype(BF16)
                y_pair = y_pair + jnp.dot(lhs, jnp.concatenate([xm, sm], axis=0), preferred_element_type=F32)
                upd = upd + jnp.dot((bm_t * w_t[h:h + 1, :]).astype(BF16), xm, preferred_element_type=F32)
            decay = jnp.where(low, jnp.broadcast_to(e_last[:, 2 * m:2 * m + 1], (1, CHUNK)),
                              jnp.broadcast_to(e_last[:, 2 * m + 1:2 * m + 2], (1, CHUNK)))
            st_ref[m] = st_pair * decay + upd
            y_ref[:, m * CHUNK:(m + 1) * CHUNK] = y_pair

    gsz = SSM_INNER // SSM_GROUPS
    for g in range(SSM_GROUPS):
        cols = slice(g * gsz, (g + 1) * gsz)
        z = z_ref[:, cols].astype(F32)
        y = (y_ref[:, cols] + dsk_ref[:, cols] * xc_ref[:, cols]) * (z * jax.nn.sigmoid(z))
        y = y * lax.rsqrt(jnp.mean(y * y, axis=-1, keepdims=True) + NORM_EPS)
        o_ref[:, cols] = (y * nw_ref[:, cols]).astype(BF16)


def _ssd(proj, dt_raw, conv_w, conv_b, dt_bias, a_log, d_skip, norm_w, bsz, nc):
    t = proj.shape[0]
    row = lambda b, c: b * nc + c
    const2 = lambda b, c: (0, 0)
    pad = lambda v: jnp.pad(v.astype(F32), (0, DT_PAD - SSM_HEADS)).reshape(1, DT_PAD)
    return pl.pallas_call(
        _ssd_body,
        grid=(bsz, nc),
        in_specs=[
            pl.BlockSpec((CHUNK, SSM_INNER), lambda b, c: (row(b, c), 3)),
            pl.BlockSpec((CHUNK, SSM_INNER), lambda b, c: (row(b, c), 4)),
            pl.BlockSpec((CHUNK, 1024), lambda b, c: (row(b, c), 10)),
            pl.BlockSpec((CHUNK, DT_PAD), lambda b, c: (row(b, c), 0)),
            pl.BlockSpec((CONV_WIDTH, CONV_CH), const2),
            pl.BlockSpec((1, CONV_CH), const2),
            pl.BlockSpec((1, DT_PAD), const2),
            pl.BlockSpec((1, DT_PAD), const2),
            pl.BlockSpec((1, SSM_INNER), const2),
            pl.BlockSpec((1, SSM_INNER), const2),
        ],
        out_specs=pl.BlockSpec((CHUNK, SSM_INNER), lambda b, c: (row(b, c), 0)),
        out_shape=jax.ShapeDtypeStruct((t, SSM_INNER), BF16),
        scratch_shapes=[
            pltpu.VMEM((HEAD_PAIRS, SSM_STATE, CHUNK), F32),
            pltpu.VMEM((8 + CHUNK, CONV_CH), F32),
            pltpu.VMEM((CHUNK, CONV_CH), F32),
            pltpu.VMEM((CHUNK, SSM_INNER), F32),
        ],
        compiler_params=_params(("parallel", "arbitrary")),
        name="ssd",
    )(proj, proj, proj, dt_raw, conv_w.astype(F32), conv_b.astype(F32).reshape(1, CONV_CH), pad(dt_bias), pad(a_log),
      jnp.repeat(d_skip.astype(F32), SSM_HEAD_DIM).reshape(1, SSM_INNER), norm_w.astype(F32).reshape(1, SSM_INNER))


def _merge_body(oret_ref, ossd_ref, g0_ref, g1_ref, h_ref, valid_ref, wb0_ref, wb1_ref, wo_ref, nw_ref, rwh_ref, rwl_ref,
                rb_ref, h1_ref, un_ref, e_ref, w_ref):
    y_ret = jnp.dot(oret_ref[...], wb0_ref[...], preferred_element_type=F32)
    y_ssd = jnp.dot(ossd_ref[...], wb1_ref[...], preferred_element_type=F32)
    merged = (jax.nn.sigmoid(g0_ref[...].astype(F32)) * y_ret + jax.nn.sigmoid(g1_ref[...].astype(F32)) * y_ssd)
    h1 = h_ref[...] + jnp.dot(merged.astype(BF16), wo_ref[...], preferred_element_type=F32)
    h1_ref[...] = h1
    un = h1 * lax.rsqrt(jnp.mean(h1 * h1, axis=-1, keepdims=True) + NORM_EPS) * nw_ref[...]
    un_hi = un.astype(BF16)
    un_ref[...] = un_hi

    un_lo = (un - un_hi.astype(F32)).astype(BF16)
    logits = (jnp.dot(un_hi, rwh_ref[...], preferred_element_type=F32)
              + jnp.dot(un_lo, rwh_ref[...], preferred_element_type=F32)
              + jnp.dot(un_hi, rwl_ref[...], preferred_element_type=F32)) + rb_ref[...]
    logits = jnp.transpose(logits)[:N_EXPERTS, :]
    tm = logits.shape[1]
    eidx = lax.broadcasted_iota(I32, (N_EXPERTS, tm), 0)
    vals, ids = [], []
    for _ in range(TOP_K):
        best = jnp.max(logits, axis=0, keepdims=True)
        arg = jnp.min(jnp.where(logits == best, eidx, N_EXPERTS), axis=0, keepdims=True)
        vals.append(best)
        ids.append(arg)
        logits = jnp.where(eidx == arg, -jnp.inf, logits)
    ex = [jnp.exp(v - vals[0]) for v in vals]
    denom = ex[0] + ex[1] + ex[2] + ex[3]
    is_tok = valid_ref[...] > 0
    e_ref[...] = jnp.concatenate([jnp.where(is_tok, i, -1) for i in ids] + [jnp.zeros((8 - TOP_K, tm), I32)], axis=0)
    w_ref[...] = jnp.concatenate([x / denom for x in ex] + [jnp.zeros((8 - TOP_K, tm), F32)], axis=0)


def _merge(o_ret, o_ssd, proj, hp, valid, wb0, wb1, wo, norm_w, router_w, router_b):
    t = hp.shape[0]
    tm = _pick(t, (512, 256, 128))
    gate_blk = OFF_DT // D_MODEL
    assert gate_blk * D_MODEL == OFF_DT
    const2 = lambda i: (0, 0)
    rw_hi = router_w.astype(BF16)
    rw_lo = (router_w - rw_hi.astype(F32)).astype(BF16)
    return pl.pallas_call(
        _merge_body,
        grid=(t // tm,),
        in_specs=[
            pl.BlockSpec((tm, RET_V), lambda i: (i, 0)),
            pl.BlockSpec((tm, SSM_INNER), lambda i: (i, 0)),
            pl.BlockSpec((tm, D_MODEL), lambda i: (i, gate_blk)),
            pl.BlockSpec((tm, D_MODEL), lambda i: (i, gate_blk + 1)),
            pl.BlockSpec((tm, D_MODEL), lambda i: (i, 0)),
            pl.BlockSpec((1, tm), lambda i: (0, i)),
            pl.BlockSpec((RET_V, D_MODEL), const2),
            pl.BlockSpec((SSM_INNER, D_MODEL), const2),
            pl.BlockSpec((D_MODEL, D_MODEL), const2),
            pl.BlockSpec((1, D_MODEL), const2),
            pl.BlockSpec((D_MODEL, CHUNK), const2),
            pl.BlockSpec((D_MODEL, CHUNK), const2),
            pl.BlockSpec((1, CHUNK), const2),
        ],
        out_specs=[
            pl.BlockSpec((tm, D_MODEL), lambda i: (i, 0)),
            pl.BlockSpec((tm, D_MODEL), lambda i: (i, 0)),
            pl.BlockSpec((8, tm), lambda i: (0, i)),
            pl.BlockSpec((8, tm), lambda i: (0, i)),
        ],
        out_shape=[
            jax.ShapeDtypeStruct((t, D_MODEL), F32),
            jax.ShapeDtypeStruct((t, D_MODEL), BF16),
            jax.ShapeDtypeStruct((8, t), I32),
            jax.ShapeDtypeStruct((8, t), F32),
        ],
        compiler_params=_params(("parallel",)),
        name="merge_router",
    )(o_ret, o_ssd, proj, proj, hp, valid, wb0, wb1, wo, norm_w, rw_hi, rw_lo, router_b)


ROW_ALIGN = 8
PIECE = 16
LOCAL_ROWS = 1024
META_BEFORE, META_COUNT, META_LSTART, META_PIECES = 0, 1, 2, 3


def _rank_body(tiles, e_ref, slot_ref, meta_ref, cnt_ref, run_ref):
    @pl.when(pl.program_id(0) == 0)
    def _():
        run_ref[...] = jnp.zeros_like(run_ref)

    eidx = lax.broadcasted_iota(I32, (N_EXPERTS, CHUNK), 0)
    lane = lax.broadcasted_iota(I32, (N_EXPERTS, CHUNK), 1)
    ri = lax.broadcasted_iota(I32, (CHUNK, CHUNK), 0)
    ci = lax.broadcasted_iota(I32, (CHUNK, CHUNK), 1)
    upper = jnp.where(ri <= ci, 1.0, 0.0).astype(BF16)
    ones = jnp.ones((CHUNK, CHUNK), BF16)
    ei = lax.broadcasted_iota(I32, (N_EXPERTS, N_EXPERTS), 0)
    ej = lax.broadcasted_iota(I32, (N_EXPERTS, N_EXPERTS), 1)
    lower = jnp.where(ej < ei, 1.0, 0.0).astype(BF16)
    round_up = lambda v, m: jnp.floor((v + (m - 1)) * (1.0 / m)) * m
    diag = lambda m: jnp.sum(jnp.where(eidx == lane, m, 0.0), axis=0, keepdims=True)

    run = run_ref[...]
    for s in range(tiles):
        e_tile = e_ref[:, s * CHUNK:(s + 1) * CHUNK]
        seen = jnp.zeros((N_EXPERTS, CHUNK), F32)
        hits, ranks = [], []
        for k in range(TOP_K):
            hit = eidx == e_tile[k:k + 1, :]
            oh = jnp.where(hit, 1.0, 0.0).astype(BF16)
            ranks.append(seen + jnp.dot(oh, upper, preferred_element_type=F32) - 1.0)
            seen = seen + jnp.dot(oh, ones, preferred_element_type=F32)
            hits.append(hit)
        count = seen
        lstart = jnp.dot(lower, round_up(count, PIECE).astype(BF16), preferred_element_type=F32)
        slots = []
        for k in range(TOP_K):
            slot = jnp.sum(jnp.where(hits[k], lstart + ranks[k], 0.0), axis=0, keepdims=True)
            slots.append(jnp.where(e_tile[k:k + 1, :] >= 0, slot, -1.0))
        slot_ref[:, s * CHUNK:(s + 1) * CHUNK] = jnp.concatenate(
            slots + [jnp.full((8 - TOP_K, CHUNK), -1.0, F32)], axis=0).astype(I32)
        n_pieces = jnp.sum(round_up(count, PIECE) * (1.0 / PIECE), axis=0, keepdims=True)
        meta_ref[s] = jnp.concatenate(
            [diag(run), diag(count), diag(lstart), n_pieces, jnp.zeros((4, CHUNK), F32)], axis=0).astype(I32)
        run = run + round_up(count, ROW_ALIGN)
    run_ref[...] = run
    cnt_ref[...] = run.astype(I32)


def _rank(top_e):
    t = top_e.shape[1]
    n_tiles = t // CHUNK
    tiles = _pick(n_tiles, (4, 2, 1))
    return pl.pallas_call(
        functools.partial(_rank_body, tiles),
        grid=(n_tiles // tiles,),
        in_specs=[pl.BlockSpec((8, tiles * CHUNK), lambda i: (0, i))],
        out_specs=[
            pl.BlockSpec((8, tiles * CHUNK), lambda i: (0, i)),
            pl.BlockSpec((tiles, 8, CHUNK), lambda i: (i, 0, 0)),
            pl.BlockSpec((N_EXPERTS, CHUNK), lambda i: (0, 0)),
        ],
        out_shape=[
            jax.ShapeDtypeStruct((8, t), I32),
            jax.ShapeDtypeStruct((n_tiles, 8, CHUNK), I32),
            jax.ShapeDtypeStruct((N_EXPERTS, CHUNK), I32),
        ],
        scratch_shapes=[pltpu.VMEM((N_EXPERTS, CHUNK), F32)],
        compiler_params=_params(("arbitrary",)),
        name="rank",
    )(top_e)


def _pack_halves(v):
    bits = pltpu.bitcast(v, jnp.uint32)
    return (bits[:, :D_MODEL // 2] >> 16) | bits[:, D_MODEL // 2:]


def _unpack_halves(p):
    lo = pltpu.bitcast(p << 16, F32)
    hi = pltpu.bitcast(p & jnp.uint32(0xFFFF0000), F32)
    return jnp.concatenate([lo, hi], axis=1).astype(BF16)


def _wait_pieces(meta_ref, make_copy):
    def body(j, carry):
        make_copy(0, 0).wait()
        return carry

    lax.fori_loop(0, meta_ref[0, META_PIECES, 0], body, 0)


def _start_pieces(pstart_ref, meta_ref, make_copy):
    def per_expert(e, carry):
        n_pieces = lax.shift_right_logical(meta_ref[0, META_COUNT, e] + (PIECE - 1), PIECE.bit_length() - 1)
        local0 = meta_ref[0, META_LSTART, e]
        sorted0 = pstart_ref[e] + meta_ref[0, META_BEFORE, e]

        def per_piece(j, c):
            make_copy(pl.multiple_of(local0 + j * PIECE, ROW_ALIGN), pl.multiple_of(sorted0 + j * PIECE, ROW_ALIGN)).start()
            return c

        return lax.fori_loop(0, n_pieces, per_piece, carry)

    lax.fori_loop(0, N_EXPERTS, per_expert, 0)


def _dispatch_body(pstart_ref, fill_ref, meta_ref, prev_meta_ref, slot_ref, un_ref, xs_ref, loc_ref, zero_ref, sems):
    step = pl.program_id(0)
    last = pl.num_programs(0) - 1
    buf = step % 2

    srow = lax.broadcasted_iota(I32, (LOCAL_ROWS, CHUNK), 0)
    onehot = jnp.zeros((LOCAL_ROWS, CHUNK), F32)
    for k in range(TOP_K):
        onehot = onehot + jnp.where(srow == slot_ref[k:k + 1, :], 1.0, 0.0)
    loc_ref[buf] = _pack_halves(jnp.dot(onehot.astype(BF16), un_ref[...], preferred_element_type=F32))

    def copies_of(b):
        def make_copy(local_row, sorted_row):
            return pltpu.make_async_copy(loc_ref.at[b, pl.ds(local_row, PIECE), :],
                                         xs_ref.at[pl.ds(sorted_row, PIECE), :], sems.at[b])
        return make_copy

    @pl.when(step > 0)
    def _():
        _wait_pieces(prev_meta_ref, copies_of(1 - buf))

    _start_pieces(pstart_ref, meta_ref, copies_of(buf))

    @pl.when(step == last)
    def _():
        _wait_pieces(meta_ref, copies_of(buf))
        zero_ref[...] = jnp.zeros_like(zero_ref)

        def fill(act):
            def per_expert(e, carry):
                def per_piece(j, c):
                    row = pl.multiple_of(fill_ref[0, e] + j * ROW_ALIGN, ROW_ALIGN)
                    act(pltpu.make_async_copy(zero_ref, xs_ref.at[pl.ds(row, ROW_ALIGN), :], sems.at[0]))
                    return c
                return lax.fori_loop(0, fill_ref[1, e], per_piece, carry)
            lax.fori_loop(0, N_EXPERTS, per_expert, 0)

        fill(lambda cp: cp.start())
        fill(lambda cp: cp.wait())


def _dispatch(pstart, fill, meta, slot, un, n_rows):
    t = un.shape[0]
    return pl.pallas_call(
        _dispatch_body,
        grid_spec=pltpu.PrefetchScalarGridSpec(
            num_scalar_prefetch=2,
            grid=(t // CHUNK,),
            in_specs=[
                pl.BlockSpec((1, 8, CHUNK), lambda i, ps, fl: (i, 0, 0), memory_space=pltpu.SMEM),
                pl.BlockSpec((1, 8, CHUNK), lambda i, ps, fl: (jnp.maximum(i - 1, 0), 0, 0), memory_space=pltpu.SMEM),
                pl.BlockSpec((8, CHUNK), lambda i, ps, fl: (0, i)),
                pl.BlockSpec((CHUNK, D_MODEL), lambda i, ps, fl: (i, 0)),
            ],
            out_specs=pl.BlockSpec(memory_space=pl.ANY),
            scratch_shapes=[pltpu.VMEM((2, LOCAL_ROWS, D_MODEL // 2), jnp.uint32),
                            pltpu.VMEM((ROW_ALIGN, D_MODEL // 2), jnp.uint32), pltpu.SemaphoreType.DMA((2,))],
        ),
        out_shape=jax.ShapeDtypeStruct((n_rows, D_MODEL // 2), jnp.uint32),
        compiler_params=_params(("arbitrary",)),
        name="dispatch",
    )(pstart, fill, meta, meta, slot, un)


GU_SLAB = 256
GU_HALF = GU_SLAB // 2


def _regroup_body(w_ref, p_ref, o_ref):
    for s in range(2 * EXPERT_FF // GU_SLAB):
        cols = slice(s * GU_SLAB, (s + 1) * GU_SLAB)
        o_ref[0, :, cols] = jnp.dot(w_ref[0, :, cols].astype(BF16), p_ref[...], preferred_element_type=F32).astype(BF16)


def _regroup_gate_up(w_gate_up):
    perm = np.zeros((GU_SLAB, GU_SLAB), np.float32)
    perm[2 * np.arange(GU_HALF), np.arange(GU_HALF)] = 1.0
    perm[2 * np.arange(GU_HALF) + 1, GU_HALF + np.arange(GU_HALF)] = 1.0
    blk = (1, D_MODEL, 2 * EXPERT_FF)
    return pl.pallas_call(
        _regroup_body,
        grid=(N_EXPERTS,),
        in_specs=[pl.BlockSpec(blk, lambda e: (e, 0, 0)), pl.BlockSpec((GU_SLAB, GU_SLAB), lambda e: (0, 0))],
        out_specs=pl.BlockSpec(blk, lambda e: (e, 0, 0)),
        out_shape=jax.ShapeDtypeStruct((N_EXPERTS, D_MODEL, 2 * EXPERT_FF), BF16),
        compiler_params=_params(("parallel",)),
        name="regroup_gate_up",
    )(w_gate_up, jnp.asarray(perm, BF16))


def _regroup_bias(b_gate_up):
    b = b_gate_up.astype(F32).reshape(N_EXPERTS, 2 * EXPERT_FF // GU_SLAB, GU_HALF, 2)
    return jnp.transpose(b, (0, 1, 3, 2)).reshape(N_EXPERTS, 1, 2 * EXPERT_FF)


def _expert_body(be_ref, nu_ref, x_ref, wgu_ref, wd_ref, bgu_ref, bd_ref, y_ref):
    del be_ref

    @pl.when(pl.program_id(0) >= nu_ref[0])
    def _():
        y_ref[...] = jnp.zeros_like(y_ref)

    @pl.when(pl.program_id(0) < nu_ref[0])
    def _():
        x = _unpack_halves(x_ref[...])
        gu = jnp.dot(x, wgu_ref[0], preferred_element_type=F32) + bgu_ref[0]
        acts = []
        for s in range(2 * EXPERT_FF // GU_SLAB):
            gate = jnp.minimum(gu[:, s * GU_SLAB:s * GU_SLAB + GU_HALF], SWIGLU_LIMIT)
            up = jnp.clip(gu[:, s * GU_SLAB + GU_HALF:(s + 1) * GU_SLAB], -SWIGLU_LIMIT, SWIGLU_LIMIT)
            acts.append(((up + 1.0) * (gate * jax.nn.sigmoid(SWIGLU_ALPHA * gate))).astype(BF16))
        act = jnp.concatenate(acts, axis=1)
        y = jnp.dot(act, wd_ref[0], preferred_element_type=F32) + bd_ref[0]
        y_ref[...] = _pack_halves(y.astype(BF16).astype(F32))


def _experts(block_e, n_used, xs, wgu, wd, bgu, bd):
    n_rows = xs.shape[0]
    n_blocks = n_rows // MOE_BLOCK
    wspec = lambda shape: pl.BlockSpec((1,) + shape, lambda i, be, nu: (be[i], 0, 0))
    return pl.pallas_call(
        _expert_body,
        grid_spec=pltpu.PrefetchScalarGridSpec(
            num_scalar_prefetch=2,
            grid=(n_blocks,),
            in_specs=[
                pl.BlockSpec((MOE_BLOCK, D_MODEL // 2), lambda i, be, nu: (jnp.minimum(i, nu[0] - 1), 0)),
                wspec((D_MODEL, 2 * EXPERT_FF)),
                wspec((EXPERT_FF, D_MODEL)),
                wspec((1, 2 * EXPERT_FF)),
                wspec((1, D_MODEL)),
            ],
            out_specs=pl.BlockSpec((MOE_BLOCK, D_MODEL // 2), lambda i, be, nu: (i, 0)),
        ),
        out_shape=jax.ShapeDtypeStruct((n_rows, D_MODEL // 2), jnp.uint32),
        compiler_params=_params(("arbitrary",)),
        name="experts",
    )(block_e, n_used, xs, wgu, wd, bgu, bd)


def _combine_body(pstart_ref, meta_ref, next_meta_ref, slot_ref, w_ref, h1_ref, nw_ref, y_hbm, o_ref, ybuf, sems):
    step = pl.program_id(0)
    buf = step % 2

    def copies_of(b):
        def make_copy(local_row, sorted_row):
            return pltpu.make_async_copy(y_hbm.at[pl.ds(sorted_row, PIECE), :],
                                         ybuf.at[b, pl.ds(local_row, PIECE), :], sems.at[b])
        return make_copy

    @pl.when(step == 0)
    def _():
        ybuf[...] = jnp.zeros_like(ybuf)
        _start_pieces(pstart_ref, meta_ref, copies_of(buf))

    @pl.when(step + 1 < pl.num_programs(0))
    def _():
        _start_pieces(pstart_ref, next_meta_ref, copies_of(1 - buf))

    _wait_pieces(meta_ref, copies_of(buf))

    srow = lax.broadcasted_iota(I32, (LOCAL_ROWS, CHUNK), 0)
    wmat = jnp.zeros((LOCAL_ROWS, CHUNK), F32)
    for k in range(TOP_K):
        wmat = wmat + jnp.where(srow == slot_ref[k:k + 1, :], w_ref[k:k + 1, :], 0.0)
    wmat = jnp.transpose(wmat)
    w_hi = wmat.astype(BF16)
    w_lo = (wmat - w_hi.astype(F32)).astype(BF16)
    y = _unpack_halves(ybuf[buf])
    f = jnp.dot(w_hi, y, preferred_element_type=F32) + jnp.dot(w_lo, y, preferred_element_type=F32)
    h2 = h1_ref[...] + f
    o_ref[...] = h2 * lax.rsqrt(jnp.mean(h2 * h2, axis=-1, keepdims=True) + NORM_EPS) * nw_ref[...]


def _combine(pstart, meta, slot, top_w, h1, norm_w, y_sorted, bsz, nc):
    seq_chunks = nc - 1
    n_out = bsz * seq_chunks
    src = lambda i: i + i // seq_chunks + 1
    nxt = lambda i: jnp.minimum(i + 1, n_out - 1)
    return pl.pallas_call(
        _combine_body,
        grid_spec=pltpu.PrefetchScalarGridSpec(
            num_scalar_prefetch=1,
            grid=(n_out,),
            in_specs=[
                pl.BlockSpec((1, 8, CHUNK), lambda i, ps: (src(i), 0, 0), memory_space=pltpu.SMEM),
                pl.BlockSpec((1, 8, CHUNK), lambda i, ps: (src(nxt(i)), 0, 0), memory_space=pltpu.SMEM),
                pl.BlockSpec((8, CHUNK), lambda i, ps: (0, src(i))),
                pl.BlockSpec((8, CHUNK), lambda i, ps: (0, src(i))),
                pl.BlockSpec((CHUNK, D_MODEL), lambda i, ps: (src(i), 0)),
                pl.BlockSpec((1, D_MODEL), lambda i, ps: (0, 0)),
                pl.BlockSpec(memory_space=pl.ANY),
            ],
            out_specs=pl.BlockSpec((CHUNK, D_MODEL), lambda i, ps: (i, 0)),
            scratch_shapes=[pltpu.VMEM((2, LOCAL_ROWS, D_MODEL // 2), jnp.uint32), pltpu.SemaphoreType.DMA((2,))],
        ),
        out_shape=jax.ShapeDtypeStruct((n_out * CHUNK, D_MODEL), F32),
        compiler_params=_params(("arbitrary",)),
        name="combine",
    )(pstart, meta, meta, slot, top_w, h1, norm_w, y_sorted)


def kernel(x, meta_tokens, norm_mix, w_in, conv_w, conv_b, dt_bias, a_log, d_skip, ssm_norm, w_branch, w_out, norm_ffn,
           router_w, router_b, w_gate_up, b_gate_up, w_down, b_down, norm_final):
    bsz, seq, _ = x.shape
    assert seq % CHUNK == 0 and norm_mix.shape[0] == 1
    nc = 1 + seq // CHUNK
    lp = nc * CHUNK
    t = bsz * lp

    hp = jnp.concatenate([jnp.zeros((bsz, META_PAD, D_MODEL), x.dtype),
                          jnp.broadcast_to(meta_tokens.astype(x.dtype)[None], (bsz, N_META, D_MODEL)), x], axis=1)
    hp = hp.reshape(t, D_MODEL)
    valid = jnp.asarray(np.tile(np.arange(lp) >= META_PAD, bsz).astype(np.int32).reshape(1, t))

    w_in0 = w_in[0]
    w_main = jnp.concatenate([w_in0[:, :OFF_DT], w_in0[:, OFF_DT + SSM_HEADS:]], axis=1).astype(BF16)
    w_dt = jnp.pad(w_in0[:, OFF_DT:OFF_DT + SSM_HEADS], ((0, 0), (0, DT_PAD - SSM_HEADS))).astype(BF16)
    wgu = _regroup_gate_up(w_gate_up[0])
    bgu = _regroup_bias(b_gate_up[0])
    wd = w_down[0].astype(BF16)
    bd = b_down[0][:, None, :].astype(F32)

    proj, dt_raw = _in_proj(hp, norm_mix[0].reshape(1, D_MODEL).astype(F32), w_main, w_dt)
    o_ret = _retention(proj, bsz, nc)
    o_ssd = _ssd(proj, dt_raw, conv_w[0], conv_b[0], dt_bias[0], a_log[0], d_skip[0], ssm_norm[0], bsz, nc)
    h1, un, top_e, top_w = _merge(
        o_ret, o_ssd, proj, hp, valid, w_branch[0, 0].astype(BF16), w_branch[0, 1].astype(BF16), w_out[0].astype(BF16),
        norm_ffn[0].reshape(1, D_MODEL).astype(F32),
        jnp.pad(router_w[0].astype(F32), ((0, 0), (0, CHUNK - N_EXPERTS))),
        jnp.pad(router_b[0].astype(F32), (0, CHUNK - N_EXPERTS)).reshape(1, CHUNK))

    slot, meta, counts = _rank(top_e)
    counts = counts[:, 0]
    slack = MOE_BLOCK - 1 + PIECE - ROW_ALIGN
    padded = (counts + slack) // MOE_BLOCK * MOE_BLOCK
    pend = jnp.cumsum(padded)
    pstart = (pend - padded).astype(I32)
    n_pairs = bsz * (N_META + seq) * TOP_K
    max_rows = n_pairs + bsz * nc * N_EXPERTS * (ROW_ALIGN - 1)
    n_blocks = (max_rows + N_EXPERTS * slack) // MOE_BLOCK
    blk_row = jnp.arange(n_blocks, dtype=I32) * MOE_BLOCK
    block_e = jnp.minimum(jnp.sum(pend[None, :] <= blk_row[:, None], axis=1), N_EXPERTS - 1).astype(I32)
    n_used = (pend[-1:] // MOE_BLOCK).astype(I32)
    fill = jnp.stack([pstart + counts, (padded - counts) // ROW_ALIGN]).astype(I32)

    xs = _dispatch(pstart, fill, meta, slot, un, n_blocks * MOE_BLOCK)
    ys = _experts(block_e, n_used, xs, wgu, wd, bgu, bd)
    out = _combine(pstart, meta, slot, top_w, h1, norm_final.reshape(1, D_MODEL).astype(F32), ys, bsz, nc)
    return out.reshape(bsz, seq, D_MODEL)
```

```python
import functools

import numpy as np
import jax
import jax.numpy as jnp
from jax import lax
from jax.experimental import pallas as pl
from jax.experimental.pallas import tpu as pltpu

F32 = jnp.float32
BF16 = jnp.bfloat16
I32 = jnp.int32

D_MODEL = 1024
N_META = 16
CHUNK = 128
META_PAD = CHUNK - N_META
NORM_EPS = 1e-6
RET_HEADS = 4
RET_QK_DIM = 256
RET_V_DIM = 512
RET_QK = RET_HEADS * RET_QK_DIM
RET_V = RET_HEADS * RET_V_DIM
ROPE_BASE = 10000.0
SSM_INNER = 2048
SSM_HEAD_DIM = 64
SSM_HEADS = 32
SSM_GROUPS = 4
SSM_STATE = 128
CONV_WIDTH = 4
CONV_CH = SSM_INNER + 2 * SSM_GROUPS * SSM_STATE
HEAD_PAIRS = SSM_HEADS // 2
PAIRS_PER_GROUP = HEAD_PAIRS // SSM_GROUPS
OFF_DT = 2 * RET_QK + 2 * RET_V + SSM_INNER + CONV_CH
N_MAIN = OFF_DT + 2 * D_MODEL
DT_PAD = 128
N_EXPERTS = 32
TOP_K = 4
EXPERT_FF = 1024
SWIGLU_ALPHA = 1.702
SWIGLU_LIMIT = 7.0
MOE_BLOCK = 512

VMEM_LIMIT = 56 * 1024 * 1024
HIGHEST = lax.Precision.HIGHEST


def _params(sem):
    return pltpu.CompilerParams(dimension_semantics=sem, vmem_limit_bytes=VMEM_LIMIT)


def _pick(n, candidates):
    for c in candidates:
        if n % c == 0:
            return c
    raise ValueError(f"no tile for {n} among {candidates}")


def _nt_dot(a, b, **kw):
    return lax.dot_general(a, b, (((1,), (1,)), ((), ())), preferred_element_type=F32, **kw)


def _inproj_body(x_ref, nw_ref, w_ref, wdt_ref, o_ref, dt_ref, xn_ref):
    @pl.when(pl.program_id(1) == 0)
    def _():
        x = x_ref[...]
        ms = jnp.mean(x * x, axis=-1, keepdims=True)
        xn = (x * lax.rsqrt(ms + NORM_EPS) * nw_ref[...]).astype(BF16)
        xn_ref[...] = xn
        dt_ref[...] = jnp.dot(xn, wdt_ref[...], preferred_element_type=F32)

    o_ref[...] = jnp.dot(xn_ref[...], w_ref[...], preferred_element_type=F32).astype(BF16)


def _in_proj(hp, norm_w, w_main, w_dt):
    t = hp.shape[0]
    tm = _pick(t, (2048, 1024, 512, 256, 128))
    tn = 1024
    return pl.pallas_call(
        _inproj_body,
        grid=(t // tm, N_MAIN // tn),
        in_specs=[
            pl.BlockSpec((tm, D_MODEL), lambda i, j: (i, 0)),
            pl.BlockSpec((1, D_MODEL), lambda i, j: (0, 0)),
            pl.BlockSpec((D_MODEL, tn), lambda i, j: (0, j)),
            pl.BlockSpec((D_MODEL, DT_PAD), lambda i, j: (0, 0)),
        ],
        out_specs=[
            pl.BlockSpec((tm, tn), lambda i, j: (i, j)),
            pl.BlockSpec((tm, DT_PAD), lambda i, j: (i, 0)),
        ],
        out_shape=[jax.ShapeDtypeStruct((t, N_MAIN), BF16), jax.ShapeDtypeStruct((t, DT_PAD), F32)],
        scratch_shapes=[pltpu.VMEM((tm, D_MODEL), BF16)],
        compiler_params=_params(("parallel", "arbitrary")),
        name="in_proj",
    )(hp, norm_w, w_main, w_dt)


def _ret_tables(lp):
    half = RET_QK_DIM // 2
    inv_freq = ROPE_BASE ** (-np.arange(half, dtype=np.float64) / half)
    pos = np.arange(lp, dtype=np.float64) - META_PAD
    ang = pos[:, None] * inv_freq[None, :]
    log_gamma = np.log(1.0 - 2.0 ** (-5.0 - np.arange(RET_HEADS, dtype=np.float64)))
    idx = np.arange(CHUNK, dtype=np.float64)
    dist = idx[:, None] - idx[None, :]
    intra = np.where(dist >= 0, np.exp(log_gamma[:, None, None] * np.maximum(dist, 0.0)[None]), 0.0)
    q_dec = np.exp(log_gamma[:, None] * (idx[None, :] + 1.0))
    k_dec = np.exp(log_gamma[:, None] * (CHUNK - 1.0 - idx[None, :]))
    q_dec = np.broadcast_to(q_dec[:, :, None], (RET_HEADS, CHUNK, RET_V_DIM))
    k_dec = np.broadcast_to(k_dec[:, :, None], (RET_HEADS, CHUNK, RET_QK_DIM))
    chunk_dec = tuple(float(v) for v in np.exp(log_gamma * CHUNK))
    as32 = lambda a: jnp.asarray(np.ascontiguousarray(a), F32)
    return as32(np.cos(ang)), as32(np.sin(ang)), as32(intra), as32(q_dec), as32(k_dec), chunk_dec


def _ret_body(chunk_dec, q_ref, k_ref, v_ref, g_ref, cos_ref, sin_ref, intra_ref, qd_ref, kd_ref, o_ref, st_ref):
    @pl.when(pl.program_id(1) == 0)
    def _():
        st_ref[...] = jnp.zeros_like(st_ref)

    cos = cos_ref[...]
    sin = sin_ref[...]
    half = RET_QK_DIM // 2

    def rotary(t):
        t1, t2 = t[:, :half], t[:, half:]
        return jnp.concatenate([t1 * cos - t2 * sin, t2 * cos + t1 * sin], axis=1)

    for h in range(RET_HEADS):
        qk = slice(h * RET_QK_DIM, (h + 1) * RET_QK_DIM)
        vv = slice(h * RET_V_DIM, (h + 1) * RET_V_DIM)
        qr = rotary(q_ref[:, qk].astype(F32))
        kr = rotary(k_ref[:, qk].astype(F32)) * (RET_QK_DIM ** -0.5)
        qb = qr.astype(BF16)
        vh = v_ref[:, vv]
        st = st_ref[h]
        s = _nt_dot(qb, kr.astype(BF16)) * intra_ref[h]
        y = jnp.dot(s.astype(BF16), vh, preferred_element_type=F32)
        y = y + jnp.dot(qb, st.astype(BF16), preferred_element_type=F32) * qd_ref[h]
        kdt = jnp.transpose(kr * kd_ref[h]).astype(BF16)
        st_ref[h] = st * chunk_dec[h] + jnp.dot(kdt, vh, preferred_element_type=F32)
        o = y * lax.rsqrt(jnp.mean(y * y, axis=-1, keepdims=True) + NORM_EPS)
        g = g_ref[:, vv].astype(F32)
        o_ref[:, vv] = (g * jax.nn.sigmoid(g) * o).astype(BF16)


def _retention(proj, bsz, nc):
    t = proj.shape[0]
    cos, sin, intra, q_dec, k_dec, chunk_dec = _ret_tables(nc * CHUNK)
    row = lambda b, c: b * nc + c
    const3 = lambda b, c: (0, 0, 0)
    return pl.pallas_call(
        functools.partial(_ret_body, chunk_dec),
        grid=(bsz, nc),
        in_specs=[
            pl.BlockSpec((CHUNK, RET_QK), lambda b, c: (row(b, c), 0)),
            pl.BlockSpec((CHUNK, RET_QK), lambda b, c: (row(b, c), 1)),
            pl.BlockSpec((CHUNK, RET_V), lambda b, c: (row(b, c), 1)),
            pl.BlockSpec((CHUNK, RET_V), lambda b, c: (row(b, c), 2)),
            pl.BlockSpec((CHUNK, RET_QK_DIM // 2), lambda b, c: (c, 0)),
            pl.BlockSpec((CHUNK, RET_QK_DIM // 2), lambda b, c: (c, 0)),
            pl.BlockSpec((RET_HEADS, CHUNK, CHUNK), const3),
            pl.BlockSpec((RET_HEADS, CHUNK, RET_V_DIM), const3),
            pl.BlockSpec((RET_HEADS, CHUNK, RET_QK_DIM), const3),
        ],
        out_specs=pl.BlockSpec((CHUNK, RET_V), lambda b, c: (row(b, c), 0)),
        out_shape=jax.ShapeDtypeStruct((t, RET_V), BF16),
        scratch_shapes=[pltpu.VMEM((RET_HEADS, RET_QK_DIM, RET_V_DIM), F32)],
        compiler_params=_params(("parallel", "arbitrary")),
        name="retention",
    )(proj, proj, proj, proj, cos, sin, intra, q_dec, k_dec)


CONV_COLS = 512


def _ssd_body(z_ref, xs_ref, bc_ref, dt_ref, cw_ref, cb_ref, dtb_ref, alog_ref, dsk_ref, nw_ref,
              o_ref, st_ref, full_ref, xc_ref, y_ref):
    c = pl.program_id(1)

    @pl.when(c == 0)
    def _():
        st_ref[...] = jnp.zeros_like(st_ref)
        full_ref[0:8, :] = jnp.zeros((8, CONV_CH), F32)

    rows = lax.broadcasted_iota(I32, (CHUNK, 1), 0)
    valid = jnp.logical_or(c > 0, rows >= META_PAD)

    full_ref[8:, :SSM_INNER] = xs_ref[...].astype(F32)
    full_ref[8:, SSM_INNER:] = bc_ref[...].astype(F32)
    for s in range(CONV_CH // CONV_COLS):
        cols = slice(s * CONV_COLS, (s + 1) * CONV_COLS)
        acc = cb_ref[:, cols]
        for k in range(CONV_WIDTH):
            off = 8 - (CONV_WIDTH - 1) + k
            acc = acc + cw_ref[k:k + 1, cols] * full_ref[off:off + CHUNK, cols]
        xc_ref[:, cols] = jnp.where(valid, acc * jax.nn.sigmoid(acc), 0.0)
    full_ref[0:8, :] = full_ref[CHUNK:CHUNK + 8, :]

    dtv = dt_ref[...] + dtb_ref[...]
    dt = jnp.maximum(dtv, 0.0) + jnp.log1p(jnp.exp(-jnp.abs(dtv)))
    dt = jnp.where(valid, dt, 0.0)
    a = dt * (-jnp.exp(alog_ref[...]))
    ri = lax.broadcasted_iota(I32, (CHUNK, CHUNK), 0)
    ci = lax.broadcasted_iota(I32, (CHUNK, CHUNK), 1)
    causal = ri >= ci
    acs = jnp.dot(causal.astype(F32), a, preferred_element_type=F32, precision=HIGHEST)
    last = acs[CHUNK - 1:CHUNK, :]
    acs_t = jnp.transpose(acs)
    dt_t = jnp.transpose(dt)
    w_t = jnp.transpose(jnp.exp(last - acs) * dt)
    e_last = jnp.exp(last)
    lanes = lax.broadcasted_iota(I32, (1, CHUNK), 1)
    low = lanes < SSM_HEAD_DIM

    for g in range(SSM_GROUPS):
        bm = xc_ref[:, SSM_INNER + g * SSM_STATE:SSM_INNER + (g + 1) * SSM_STATE]
        cm = xc_ref[:, SSM_INNER + (SSM_GROUPS + g) * SSM_STATE:SSM_INNER + (SSM_GROUPS + g + 1) * SSM_STATE]
        cb = _nt_dot(cm.astype(BF16), bm.astype(BF16))
        bm_t = jnp.transpose(bm)
        for pp in range(PAIRS_PER_GROUP):
            m = g * PAIRS_PER_GROUP + pp
            x_pair = xc_ref[:, m * CHUNK:(m + 1) * CHUNK]
            st_pair = st_ref[m]
            y_pair = jnp.zeros((CHUNK, CHUNK), F32)
            upd = jnp.zeros((SSM_STATE, CHUNK), F32)
            for hh in range(2):
                h = 2 * m + hh
                lane_mask = low if hh == 0 else jnp.logical_not(low)
                col = jnp.broadcast_to(acs[:, h:h + 1], (CHUNK, CHUNK))
                seg = col - acs_t[h:h + 1, :]
                dec = jnp.where(causal, jnp.exp(jnp.minimum(seg, 0.0)), 0.0)
                mat = cb * dec * dt_t[h:h + 1, :]
                lhs = jnp.concatenate([mat, cm * jnp.exp(col)], axis=1).astype(BF16)
                xm = jnp.where(lane_mask, x_pair, 0.0).astype(BF16)
                sm = jnp.where(lane_mask, st_pair, 0.0).astype(BF16)
                y_pair = y_pair + jnp.dot(lhs, jnp.concatenate([xm, sm], axis=0), preferred_element_type=F32)
                upd = upd + jnp.dot((bm_t * w_t[h:h + 1, :]).astype(BF16), xm, preferred_element_type=F32)
            decay = jnp.where(low, jnp.broadcast_to(e_last[:, 2 * m:2 * m + 1], (1, CHUNK)),
                              jnp.broadcast_to(e_last[:, 2 * m + 1:2 * m + 2], (1, CHUNK)))
            st_ref[m] = st_pair * decay + upd
            y_ref[:, m * CHUNK:(m + 1) * CHUNK] = y_pair

    gsz = SSM_INNER // SSM_GROUPS
    for g in range(SSM_GROUPS):
        cols = slice(g * gsz, (g + 1) * gsz)
        z = z_ref[:, cols].astype(F32)
        y = (y_ref[:, cols] + dsk_ref[:, cols] * xc_ref[:, cols]) * (z * jax.nn.sigmoid(z))
        y = y * lax.rsqrt(jnp.mean(y * y, axis=-1, keepdims=True) + NORM_EPS)
        o_ref[:, cols] = (y * nw_ref[:, cols]).astype(BF16)


def _ssd(proj, dt_raw, conv_w, conv_b, dt_bias, a_log, d_skip, norm_w, bsz, nc):
    t = proj.shape[0]
    row = lambda b, c: b * nc + c
    const2 = lambda b, c: (0, 0)
    pad = lambda v: jnp.pad(v.astype(F32), (0, DT_PAD - SSM_HEADS)).reshape(1, DT_PAD)
    return pl.pallas_call(
        _ssd_body,
        grid=(bsz, nc),
        in_specs=[
            pl.BlockSpec((CHUNK, SSM_INNER), lambda b, c: (row(b, c), 3)),
            pl.BlockSpec((CHUNK, SSM_INNER), lambda b, c: (row(b, c), 4)),
            pl.BlockSpec((CHUNK, 1024), lambda b, c: (row(b, c), 10)),
            pl.BlockSpec((CHUNK, DT_PAD), lambda b, c: (row(b, c), 0)),
            pl.BlockSpec((CONV_WIDTH, CONV_CH), const2),
            pl.BlockSpec((1, CONV_CH), const2),
            pl.BlockSpec((1, DT_PAD), const2),
            pl.BlockSpec((1, DT_PAD), const2),
            pl.BlockSpec((1, SSM_INNER), const2),
            pl.BlockSpec((1, SSM_INNER), const2),
        ],
        out_specs=pl.BlockSpec((CHUNK, SSM_INNER), lambda b, c: (row(b, c), 0)),
        out_shape=jax.ShapeDtypeStruct((t, SSM_INNER), BF16),
        scratch_shapes=[
            pltpu.VMEM((HEAD_PAIRS, SSM_STATE, CHUNK), F32),
            pltpu.VMEM((8 + CHUNK, CONV_CH), F32),
            pltpu.VMEM((CHUNK, CONV_CH), F32),
            pltpu.VMEM((CHUNK, SSM_INNER), F32),
        ],
        compiler_params=_params(("parallel", "arbitrary")),
        name="ssd",
    )(proj, proj, proj, dt_raw, conv_w.astype(F32), conv_b.astype(F32).reshape(1, CONV_CH), pad(dt_bias), pad(a_log),
      jnp.repeat(d_skip.astype(F32), SSM_HEAD_DIM).reshape(1, SSM_INNER), norm_w.astype(F32).reshape(1, SSM_INNER))


def _merge_body(oret_ref, ossd_ref, g0_ref, g1_ref, h_ref, valid_ref, wb0_ref, wb1_ref, wo_ref, nw_ref, rwh_ref, rwl_ref,
                rb_ref, h1_ref, un_ref, e_ref, w_ref):
    y_ret = jnp.dot(oret_ref[...], wb0_ref[...], preferred_element_type=F32)
    y_ssd = jnp.dot(ossd_ref[...], wb1_ref[...], preferred_element_type=F32)
    merged = (jax.nn.sigmoid(g0_ref[...].astype(F32)) * y_ret + jax.nn.sigmoid(g1_ref[...].astype(F32)) * y_ssd)
    h1 = h_ref[...] + jnp.dot(merged.astype(BF16), wo_ref[...], preferred_element_type=F32)
    h1_ref[...] = h1
    un = h1 * lax.rsqrt(jnp.mean(h1 * h1, axis=-1, keepdims=True) + NORM_EPS) * nw_ref[...]
    un_hi = un.astype(BF16)
    un_ref[...] = un_hi

    un_lo = (un - un_hi.astype(F32)).astype(BF16)
    logits = (jnp.dot(un_hi, rwh_ref[...], preferred_element_type=F32)
              + jnp.dot(un_lo, rwh_ref[...], preferred_element_type=F32)
              + jnp.dot(un_hi, rwl_ref[...], preferred_element_type=F32)) + rb_ref[...]
    logits = jnp.transpose(logits)[:N_EXPERTS, :]
    tm = logits.shape[1]
    eidx = lax.broadcasted_iota(I32, (N_EXPERTS, tm), 0)
    vals, ids = [], []
    for _ in range(TOP_K):
        best = jnp.max(logits, axis=0, keepdims=True)
        arg = jnp.min(jnp.where(logits == best, eidx, N_EXPERTS), axis=0, keepdims=True)
        vals.append(best)
        ids.append(arg)
        logits = jnp.where(eidx == arg, -jnp.inf, logits)
    ex = [jnp.exp(v - vals[0]) for v in vals]
    denom = ex[0] + ex[1] + ex[2] + ex[3]
    is_tok = valid_ref[...] > 0
    e_ref[...] = jnp.concatenate([jnp.where(is_tok, i, -1) for i in ids] + [jnp.zeros((8 - TOP_K, tm), I32)], axis=0)
    w_ref[...] = jnp.concatenate([x / denom for x in ex] + [jnp.zeros((8 - TOP_K, tm), F32)], axis=0)


def _merge(o_ret, o_ssd, proj, hp, valid, wb0, wb1, wo, norm_w, router_w, router_b):
    t = hp.shape[0]
    tm = _pick(t, (512, 256, 128))
    gate_blk = OFF_DT // D_MODEL
    assert gate_blk * D_MODEL == OFF_DT
    const2 = lambda i: (0, 0)
    rw_hi = router_w.astype(BF16)
    rw_lo = (router_w - rw_hi.astype(F32)).astype(BF16)
    return pl.pallas_call(
        _merge_body,
        grid=(t // tm,),
        in_specs=[
            pl.BlockSpec((tm, RET_V), lambda i: (i, 0)),
            pl.BlockSpec((tm, SSM_INNER), lambda i: (i, 0)),
            pl.BlockSpec((tm, D_MODEL), lambda i: (i, gate_blk)),
            pl.BlockSpec((tm, D_MODEL), lambda i: (i, gate_blk + 1)),
            pl.BlockSpec((tm, D_MODEL), lambda i: (i, 0)),
            pl.BlockSpec((1, tm), lambda i: (0, i)),
            pl.BlockSpec((RET_V, D_MODEL), const2),
            pl.BlockSpec((SSM_INNER, D_MODEL), const2),
            pl.BlockSpec((D_MODEL, D_MODEL), const2),
            pl.BlockSpec((1, D_MODEL), const2),
            pl.BlockSpec((D_MODEL, CHUNK), const2),
            pl.BlockSpec((D_MODEL, CHUNK), const2),
            pl.BlockSpec((1, CHUNK), const2),
        ],
        out_specs=[
            pl.BlockSpec((tm, D_MODEL), lambda i: (i, 0)),
            pl.BlockSpec((tm, D_MODEL), lambda i: (i, 0)),
            pl.BlockSpec((8, tm), lambda i: (0, i)),
            pl.BlockSpec((8, tm), lambda i: (0, i)),
        ],
        out_shape=[
            jax.ShapeDtypeStruct((t, D_MODEL), F32),
            jax.ShapeDtypeStruct((t, D_MODEL), BF16),
            jax.ShapeDtypeStruct((8, t), I32),
            jax.ShapeDtypeStruct((8, t), F32),
        ],
        compiler_params=_params(("parallel",)),
        name="merge_router",
    )(o_ret, o_ssd, proj, proj, hp, valid, wb0, wb1, wo, norm_w, rw_hi, rw_lo, router_b)


ROW_ALIGN = 8
PIECE = 16
LOCAL_ROWS = 1024
META_PIECES, META_EXPERT, META_LOCAL, META_SORTED = 0, 1, 2, 3
MAX_PIECES = CHUNK * TOP_K // PIECE + N_EXPERTS
assert MAX_PIECES <= CHUNK


def _rank_body(tiles, e_ref, slot_ref, meta_ref, cnt_ref, run_ref):
    @pl.when(pl.program_id(0) == 0)
    def _():
        run_ref[...] = jnp.zeros_like(run_ref)

    eidx = lax.broadcasted_iota(I32, (N_EXPERTS, CHUNK), 0)
    lane = lax.broadcasted_iota(I32, (N_EXPERTS, CHUNK), 1)
    ri = lax.broadcasted_iota(I32, (CHUNK, CHUNK), 0)
    ci = lax.broadcasted_iota(I32, (CHUNK, CHUNK), 1)
    upper = jnp.where(ri <= ci, 1.0, 0.0).astype(BF16)
    ones = jnp.ones((CHUNK, CHUNK), BF16)
    ei = lax.broadcasted_iota(I32, (N_EXPERTS, N_EXPERTS), 0)
    ej = lax.broadcasted_iota(I32, (N_EXPERTS, N_EXPERTS), 1)
    lower = jnp.where(ej < ei, 1.0, 0.0).astype(BF16)
    round_up = lambda v, m: jnp.floor((v + (m - 1)) * (1.0 / m)) * m

    run = run_ref[...]
    for s in range(tiles):
        e_tile = e_ref[:, s * CHUNK:(s + 1) * CHUNK]
        seen = jnp.zeros((N_EXPERTS, CHUNK), F32)
        hits, ranks = [], []
        for k in range(TOP_K):
            hit = eidx == e_tile[k:k + 1, :]
            oh = jnp.where(hit, 1.0, 0.0).astype(BF16)
            ranks.append(seen + jnp.dot(oh, upper, preferred_element_type=F32) - 1.0)
            seen = seen + jnp.dot(oh, ones, preferred_element_type=F32)
            hits.append(hit)
        count = seen
        lstart = jnp.dot(lower, round_up(count, PIECE).astype(BF16), preferred_element_type=F32)
        slots = []
        for k in range(TOP_K):
            slot = jnp.sum(jnp.where(hits[k], lstart + ranks[k], 0.0), axis=0, keepdims=True)
            slots.append(jnp.where(e_tile[k:k + 1, :] >= 0, slot, -1.0))
        slot_ref[:, s * CHUNK:(s + 1) * CHUNK] = jnp.concatenate(
            slots + [jnp.full((8 - TOP_K, CHUNK), -1.0, F32)], axis=0).astype(I32)
        pieces = round_up(count, PIECE) * (1.0 / PIECE)
        first = jnp.dot(lower, pieces.astype(BF16), preferred_element_type=F32)
        nth = lane.astype(F32) - first
        mine = jnp.logical_and(nth >= 0.0, nth < pieces)
        pick = lambda v: jnp.sum(jnp.where(mine, v, 0.0), axis=0, keepdims=True)
        meta_ref[s] = jnp.concatenate(
            [jnp.sum(pieces, axis=0, keepdims=True), pick(eidx.astype(F32)), pick(lstart + PIECE * nth),
             pick(run + PIECE * nth), jnp.zeros((4, CHUNK), F32)], axis=0).astype(I32)
        run = run + round_up(count, ROW_ALIGN)
    run_ref[...] = run
    cnt_ref[...] = run.astype(I32)


def _rank(top_e):
    t = top_e.shape[1]
    n_tiles = t // CHUNK
    tiles = _pick(n_tiles, (4, 2, 1))
    return pl.pallas_call(
        functools.partial(_rank_body, tiles),
        grid=(n_tiles // tiles,),
        in_specs=[pl.BlockSpec((8, tiles * CHUNK), lambda i: (0, i))],
        out_specs=[
            pl.BlockSpec((8, tiles * CHUNK), lambda i: (0, i)),
            pl.BlockSpec((tiles, 8, CHUNK), lambda i: (i, 0, 0)),
            pl.BlockSpec((N_EXPERTS, CHUNK), lambda i: (0, 0)),
        ],
        out_shape=[
            jax.ShapeDtypeStruct((8, t), I32),
            jax.ShapeDtypeStruct((n_tiles, 8, CHUNK), I32),
            jax.ShapeDtypeStruct((N_EXPERTS, CHUNK), I32),
        ],
        scratch_shapes=[pltpu.VMEM((N_EXPERTS, CHUNK), F32)],
        compiler_params=_params(("arbitrary",)),
        name="rank",
    )(top_e)


def _pack_halves(v):
    bits = pltpu.bitcast(v, jnp.uint32)
    return (bits[:, :D_MODEL // 2] >> 16) | bits[:, D_MODEL // 2:]


def _unpack_halves(p):
    lo = pltpu.bitcast(p << 16, F32)
    hi = pltpu.bitcast(p & jnp.uint32(0xFFFF0000), F32)
    return jnp.concatenate([lo, hi], axis=1).astype(BF16)


def _wait_pieces(meta_ref, make_copy):
    def body(j, carry):
        make_copy(0, 0).wait()
        return carry

    lax.fori_loop(0, meta_ref[0, META_PIECES, 0], body, 0)


def _start_pieces(pstart_ref, meta_ref, make_copy):
    def body(p, carry):
        sorted_row = pstart_ref[meta_ref[0, META_EXPERT, p]] + meta_ref[0, META_SORTED, p]
        make_copy(pl.multiple_of(meta_ref[0, META_LOCAL, p], ROW_ALIGN), pl.multiple_of(sorted_row, ROW_ALIGN)).start()
        return carry

    lax.fori_loop(0, meta_ref[0, META_PIECES, 0], body, 0)


def _dispatch_body(pstart_ref, fill_ref, meta_ref, prev_meta_ref, slot_ref, un_ref, xs_ref, loc_ref, zero_ref, sems):
    step = pl.program_id(0)
    last = pl.num_programs(0) - 1
    buf = step % 2

    srow = lax.broadcasted_iota(I32, (LOCAL_ROWS, CHUNK), 0)
    onehot = jnp.zeros((LOCAL_ROWS, CHUNK), F32)
    for k in range(TOP_K):
        onehot = onehot + jnp.where(srow == slot_ref[k:k + 1, :], 1.0, 0.0)
    loc_ref[buf] = _pack_halves(jnp.dot(onehot.astype(BF16), un_ref[...], preferred_element_type=F32))

    def copies_of(b):
        def make_copy(local_row, sorted_row):
            return pltpu.make_async_copy(loc_ref.at[b, pl.ds(local_row, PIECE), :],
                                         xs_ref.at[pl.ds(sorted_row, PIECE), :], sems.at[b])
        return make_copy

    @pl.when(step > 0)
    def _():
        _wait_pieces(prev_meta_ref, copies_of(1 - buf))

    _start_pieces(pstart_ref, meta_ref, copies_of(buf))

    @pl.when(step == last)
    def _():
        _wait_pieces(meta_ref, copies_of(buf))
        zero_ref[...] = jnp.zeros_like(zero_ref)

        def fill(act):
            def per_expert(e, carry):
                def per_piece(j, c):
                    row = pl.multiple_of(fill_ref[0, e] + j * ROW_ALIGN, ROW_ALIGN)
                    act(pltpu.make_async_copy(zero_ref, xs_ref.at[pl.ds(row, ROW_ALIGN), :], sems.at[0]))
                    return c
                return lax.fori_loop(0, fill_ref[1, e], per_piece, carry)
            lax.fori_loop(0, N_EXPERTS, per_expert, 0)

        fill(lambda cp: cp.start())
        fill(lambda cp: cp.wait())


def _dispatch(pstart, fill, meta, slot, un, n_rows):
    t = un.shape[0]
    return pl.pallas_call(
        _dispatch_body,
        grid_spec=pltpu.PrefetchScalarGridSpec(
            num_scalar_prefetch=2,
            grid=(t // CHUNK,),
            in_specs=[
                pl.BlockSpec((1, 8, CHUNK), lambda i, ps, fl: (i, 0, 0), memory_space=pltpu.SMEM),
                pl.BlockSpec((1, 8, CHUNK), lambda i, ps, fl: (jnp.maximum(i - 1, 0), 0, 0), memory_space=pltpu.SMEM),
                pl.BlockSpec((8, CHUNK), lambda i, ps, fl: (0, i)),
                pl.BlockSpec((CHUNK, D_MODEL), lambda i, ps, fl: (i, 0)),
            ],
            out_specs=pl.BlockSpec(memory_space=pl.ANY),
            scratch_shapes=[pltpu.VMEM((2, LOCAL_ROWS, D_MODEL // 2), jnp.uint32),
                            pltpu.VMEM((ROW_ALIGN, D_MODEL // 2), jnp.uint32), pltpu.SemaphoreType.DMA((2,))],
        ),
        out_shape=jax.ShapeDtypeStruct((n_rows, D_MODEL // 2), jnp.uint32),
        compiler_params=_params(("arbitrary",)),
        name="dispatch",
    )(pstart, fill, meta, meta, slot, un)


GU_SLAB = 256
GU_HALF = GU_SLAB // 2


def _regroup_body(w_ref, p_ref, o_ref):
    for s in range(2 * EXPERT_FF // GU_SLAB):
        cols = slice(s * GU_SLAB, (s + 1) * GU_SLAB)
        o_ref[0, :, cols] = jnp.dot(w_ref[0, :, cols].astype(BF16), p_ref[...], preferred_element_type=F32).astype(BF16)


def _regroup_gate_up(w_gate_up):
    perm = np.zeros((GU_SLAB, GU_SLAB), np.float32)
    perm[2 * np.arange(GU_HALF), np.arange(GU_HALF)] = 1.0
    perm[2 * np.arange(GU_HALF) + 1, GU_HALF + np.arange(GU_HALF)] = 1.0
    blk = (1, D_MODEL, 2 * EXPERT_FF)
    return pl.pallas_call(
        _regroup_body,
        grid=(N_EXPERTS,),
        in_specs=[pl.BlockSpec(blk, lambda e: (e, 0, 0)), pl.BlockSpec((GU_SLAB, GU_SLAB), lambda e: (0, 0))],
        out_specs=pl.BlockSpec(blk, lambda e: (e, 0, 0)),
        out_shape=jax.ShapeDtypeStruct((N_EXPERTS, D_MODEL, 2 * EXPERT_FF), BF16),
        compiler_params=_params(("parallel",)),
        name="regroup_gate_up",
    )(w_gate_up, jnp.asarray(perm, BF16))


def _regroup_bias(b_gate_up):
    b = b_gate_up.astype(F32).reshape(N_EXPERTS, 2 * EXPERT_FF // GU_SLAB, GU_HALF, 2)
    return jnp.transpose(b, (0, 1, 3, 2)).reshape(N_EXPERTS, 1, 2 * EXPERT_FF)


def _expert_body(be_ref, nu_ref, x_ref, wgu_ref, wd_ref, bgu_ref, bd_ref, y_ref):
    del be_ref

    @pl.when(pl.program_id(0) >= nu_ref[0])
    def _():
        y_ref[...] = jnp.zeros_like(y_ref)

    @pl.when(pl.program_id(0) < nu_ref[0])
    def _():
        x = _unpack_halves(x_ref[...])
        gu = jnp.dot(x, wgu_ref[0], preferred_element_type=F32) + bgu_ref[0]
        acts = []
        for s in range(2 * EXPERT_FF // GU_SLAB):
            gate = jnp.minimum(gu[:, s * GU_SLAB:s * GU_SLAB + GU_HALF], SWIGLU_LIMIT)
            up = jnp.clip(gu[:, s * GU_SLAB + GU_HALF:(s + 1) * GU_SLAB], -SWIGLU_LIMIT, SWIGLU_LIMIT)
            acts.append(((up + 1.0) * (gate * jax.nn.sigmoid(SWIGLU_ALPHA * gate))).astype(BF16))
        act = jnp.concatenate(acts, axis=1)
        y = jnp.dot(act, wd_ref[0], preferred_element_type=F32) + bd_ref[0]
        y_ref[...] = _pack_halves(y.astype(BF16).astype(F32))


def _experts(block_e, n_used, xs, wgu, wd, bgu, bd):
    n_rows = xs.shape[0]
    n_blocks = n_rows // MOE_BLOCK
    wspec = lambda shape: pl.BlockSpec((1,) + shape, lambda i, be, nu: (be[i], 0, 0))
    return pl.pallas_call(
        _expert_body,
        grid_spec=pltpu.PrefetchScalarGridSpec(
            num_scalar_prefetch=2,
            grid=(n_blocks,),
            in_specs=[
                pl.BlockSpec((MOE_BLOCK, D_MODEL // 2), lambda i, be, nu: (jnp.minimum(i, nu[0] - 1), 0)),
                wspec((D_MODEL, 2 * EXPERT_FF)),
                wspec((EXPERT_FF, D_MODEL)),
                wspec((1, 2 * EXPERT_FF)),
                wspec((1, D_MODEL)),
            ],
            out_specs=pl.BlockSpec((MOE_BLOCK, D_MODEL // 2), lambda i, be, nu: (i, 0)),
        ),
        out_shape=jax.ShapeDtypeStruct((n_rows, D_MODEL // 2), jnp.uint32),
        compiler_params=_params(("arbitrary",)),
        name="experts",
    )(block_e, n_used, xs, wgu, wd, bgu, bd)


def _combine_body(pstart_ref, meta_ref, next_meta_ref, slot_ref, w_ref, h1_ref, nw_ref, y_hbm, o_ref, ybuf, sems):
    step = pl.program_id(0)
    buf = step % 2

    def copies_of(b):
        def make_copy(local_row, sorted_row):
            return pltpu.make_async_copy(y_hbm.at[pl.ds(sorted_row, PIECE), :],
                                         ybuf.at[b, pl.ds(local_row, PIECE), :], sems.at[b])
        return make_copy

    @pl.when(step == 0)
    def _():
        ybuf[...] = jnp.zeros_like(ybuf)
        _start_pieces(pstart_ref, meta_ref, copies_of(buf))

    @pl.when(step + 1 < pl.num_programs(0))
    def _():
        _start_pieces(pstart_ref, next_meta_ref, copies_of(1 - buf))

    _wait_pieces(meta_ref, copies_of(buf))

    srow = lax.broadcasted_iota(I32, (LOCAL_ROWS, CHUNK), 0)
    wmat = jnp.zeros((LOCAL_ROWS, CHUNK), F32)
    for k in range(TOP_K):
        wmat = wmat + jnp.where(srow == slot_ref[k:k + 1, :], w_ref[k:k + 1, :], 0.0)
    wmat = jnp.transpose(wmat)
    w_hi = wmat.astype(BF16)
    w_lo = (wmat - w_hi.astype(F32)).astype(BF16)
    y = _unpack_halves(ybuf[buf])
    f = jnp.dot(w_hi, y, preferred_element_type=F32) + jnp.dot(w_lo, y, preferred_element_type=F32)
    h2 = h1_ref[...] + f
    o_ref[...] = h2 * lax.rsqrt(jnp.mean(h2 * h2, axis=-1, keepdims=True) + NORM_EPS) * nw_ref[...]


def _combine(pstart, meta, slot, top_w, h1, norm_w, y_sorted, bsz, nc):
    seq_chunks = nc - 1
    n_out = bsz * seq_chunks
    src = lambda i: i + i // seq_chunks + 1
    nxt = lambda i: jnp.minimum(i + 1, n_out - 1)
    return pl.pallas_call(
        _combine_body,
        grid_spec=pltpu.PrefetchScalarGridSpec(
            num_scalar_prefetch=1,
            grid=(n_out,),
            in_specs=[
                pl.BlockSpec((1, 8, CHUNK), lambda i, ps: (src(i), 0, 0), memory_space=pltpu.SMEM),
                pl.BlockSpec((1, 8, CHUNK), lambda i, ps: (src(nxt(i)), 0, 0), memory_space=pltpu.SMEM),
                pl.BlockSpec((8, CHUNK), lambda i, ps: (0, src(i))),
                pl.BlockSpec((8, CHUNK), lambda i, ps: (0, src(i))),
                pl.BlockSpec((CHUNK, D_MODEL), lambda i, ps: (src(i), 0)),
                pl.BlockSpec((1, D_MODEL), lambda i, ps: (0, 0)),
                pl.BlockSpec(memory_space=pl.ANY),
            ],
            out_specs=pl.BlockSpec((CHUNK, D_MODEL), lambda i, ps: (i, 0)),
            scratch_shapes=[pltpu.VMEM((2, LOCAL_ROWS, D_MODEL // 2), jnp.uint32), pltpu.SemaphoreType.DMA((2,))],
        ),
        out_shape=jax.ShapeDtypeStruct((n_out * CHUNK, D_MODEL), F32),
        compiler_params=_params(("arbitrary",)),
        name="combine",
    )(pstart, meta, meta, slot, top_w, h1, norm_w, y_sorted)


def kernel(x, meta_tokens, norm_mix, w_in, conv_w, conv_b, dt_bias, a_log, d_skip, ssm_norm, w_branch, w_out, norm_ffn,
           router_w, router_b, w_gate_up, b_gate_up, w_down, b_down, norm_final):
    bsz, seq, _ = x.shape
    assert seq % CHUNK == 0 and norm_mix.shape[0] == 1
    nc = 1 + seq // CHUNK
    lp = nc * CHUNK
    t = bsz * lp

    hp = jnp.concatenate([jnp.zeros((bsz, META_PAD, D_MODEL), x.dtype),
                          jnp.broadcast_to(meta_tokens.astype(x.dtype)[None], (bsz, N_META, D_MODEL)), x], axis=1)
    hp = hp.reshape(t, D_MODEL)
    valid = jnp.asarray(np.tile(np.arange(lp) >= META_PAD, bsz).astype(np.int32).reshape(1, t))

    w_in0 = w_in[0]
    w_main = jnp.concatenate([w_in0[:, :OFF_DT], w_in0[:, OFF_DT + SSM_HEADS:]], axis=1).astype(BF16)
    w_dt = jnp.pad(w_in0[:, OFF_DT:OFF_DT + SSM_HEADS], ((0, 0), (0, DT_PAD - SSM_HEADS))).astype(BF16)
    wgu = _regroup_gate_up(w_gate_up[0])
    bgu = _regroup_bias(b_gate_up[0])
    wd = w_down[0].astype(BF16)
    bd = b_down[0][:, None, :].astype(F32)

    proj, dt_raw = _in_proj(hp, norm_mix[0].reshape(1, D_MODEL).astype(F32), w_main, w_dt)
    o_ret = _retention(proj, bsz, nc)
    o_ssd = _ssd(proj, dt_raw, conv_w[0], conv_b[0], dt_bias[0], a_log[0], d_skip[0], ssm_norm[0], bsz, nc)
    h1, un, top_e, top_w = _merge(
        o_ret, o_ssd, proj, hp, valid, w_branch[0, 0].astype(BF16), w_branch[0, 1].astype(BF16), w_out[0].astype(BF16),
        norm_ffn[0].reshape(1, D_MODEL).astype(F32),
        jnp.pad(router_w[0].astype(F32), ((0, 0), (0, CHUNK - N_EXPERTS))),
        jnp.pad(router_b[0].astype(F32), (0, CHUNK - N_EXPERTS)).reshape(1, CHUNK))

    slot, meta, counts = _rank(top_e)
    counts = counts[:, 0]
    slack = MOE_BLOCK - 1 + PIECE - ROW_ALIGN
    padded = (counts + slack) // MOE_BLOCK * MOE_BLOCK
    pend = jnp.cumsum(padded)
    pstart = (pend - padded).astype(I32)
    n_pairs = bsz * (N_META + seq) * TOP_K
    max_rows = n_pairs + bsz * nc * N_EXPERTS * (ROW_ALIGN - 1)
    n_blocks = (max_rows + N_EXPERTS * slack) // MOE_BLOCK
    blk_row = jnp.arange(n_blocks, dtype=I32) * MOE_BLOCK
    block_e = jnp.minimum(jnp.sum(pend[None, :] <= blk_row[:, None], axis=1), N_EXPERTS - 1).astype(I32)
    n_used = (pend[-1:] // MOE_BLOCK).astype(I32)
    fill = jnp.stack([pstart + counts, (padded - counts) // ROW_ALIGN]).astype(I32)

    xs = _dispatch(pstart, fill, meta, slot, un, n_blocks * MOE_BLOCK)
    ys = _experts(block_e, n_used, xs, wgu, wd, bgu, bd)
    out = _combine(pstart, meta, slot, top_w, h1, norm_final.reshape(1, D_MODEL).astype(F32), ys, bsz, nc)
    return out.reshape(bsz, seq, D_MODEL)
```

```python
import functools

import numpy as np
import jax
import jax.numpy as jnp
from jax import lax
from jax.experimental import pallas as pl
from jax.experimental.pallas import tpu as pltpu

F32 = jnp.float32
BF16 = jnp.bfloat16
I32 = jnp.int32

D_MODEL = 1024
N_META = 16
CHUNK = 128
META_PAD = CHUNK - N_META
NORM_EPS = 1e-6
RET_HEADS = 4
RET_QK_DIM = 256
RET_V_DIM = 512
RET_QK = RET_HEADS * RET_QK_DIM
RET_V = RET_HEADS * RET_V_DIM
ROPE_BASE = 10000.0
SSM_INNER = 2048
SSM_HEAD_DIM = 64
SSM_HEADS = 32
SSM_GROUPS = 4
SSM_STATE = 128
CONV_WIDTH = 4
CONV_CH = SSM_INNER + 2 * SSM_GROUPS * SSM_STATE
HEAD_PAIRS = SSM_HEADS // 2
PAIRS_PER_GROUP = HEAD_PAIRS // SSM_GROUPS
OFF_DT = 2 * RET_QK + 2 * RET_V + SSM_INNER + CONV_CH
N_MAIN = OFF_DT + 2 * D_MODEL
DT_PAD = 128
N_EXPERTS = 32
TOP_K = 4
EXPERT_FF = 1024
SWIGLU_ALPHA = 1.702
SWIGLU_LIMIT = 7.0
MOE_BLOCK = 512

VMEM_LIMIT = 56 * 1024 * 1024
HIGHEST = lax.Precision.HIGHEST


def _params(sem):
    return pltpu.CompilerParams(dimension_semantics=sem, vmem_limit_bytes=VMEM_LIMIT)


def _pick(n, candidates):
    for c in candidates:
        if n % c == 0:
            return c
    raise ValueError(f"no tile for {n} among {candidates}")


def _sigmoid(x):
    return 0.5 * jnp.tanh(0.5 * x) + 0.5


def _nt_dot(a, b, **kw):
    return lax.dot_general(a, b, (((1,), (1,)), ((), ())), preferred_element_type=F32, **kw)


def _inproj_body(x_ref, nw_ref, w_ref, wdt_ref, o_ref, dt_ref, xn_ref):
    @pl.when(pl.program_id(1) == 0)
    def _():
        x = x_ref[...]
        ms = jnp.mean(x * x, axis=-1, keepdims=True)
        xn = (x * lax.rsqrt(ms + NORM_EPS) * nw_ref[...]).astype(BF16)
        xn_ref[...] = xn
        dt_ref[...] = jnp.dot(xn, wdt_ref[...], preferred_element_type=F32)

    o_ref[...] = jnp.dot(xn_ref[...], w_ref[...], preferred_element_type=F32).astype(BF16)


def _in_proj(hp, norm_w, w_main, w_dt):
    t = hp.shape[0]
    tm = _pick(t, (2048, 1024, 512, 256, 128))
    tn = 1024
    return pl.pallas_call(
        _inproj_body,
        grid=(t // tm, N_MAIN // tn),
        in_specs=[
            pl.BlockSpec((tm, D_MODEL), lambda i, j: (i, 0)),
            pl.BlockSpec((1, D_MODEL), lambda i, j: (0, 0)),
            pl.BlockSpec((D_MODEL, tn), lambda i, j: (0, j)),
            pl.BlockSpec((D_MODEL, DT_PAD), lambda i, j: (0, 0)),
        ],
        out_specs=[
            pl.BlockSpec((tm, tn), lambda i, j: (i, j)),
            pl.BlockSpec((tm, DT_PAD), lambda i, j: (i, 0)),
        ],
        out_shape=[jax.ShapeDtypeStruct((t, N_MAIN), BF16), jax.ShapeDtypeStruct((t, DT_PAD), F32)],
        scratch_shapes=[pltpu.VMEM((tm, D_MODEL), BF16)],
        compiler_params=_params(("parallel", "arbitrary")),
        name="in_proj",
    )(hp, norm_w, w_main, w_dt)


def _ret_tables(lp):
    half = RET_QK_DIM // 2
    inv_freq = ROPE_BASE ** (-np.arange(half, dtype=np.float64) / half)
    pos = np.arange(lp, dtype=np.float64) - META_PAD
    ang = pos[:, None] * inv_freq[None, :]
    log_gamma = np.log(1.0 - 2.0 ** (-5.0 - np.arange(RET_HEADS, dtype=np.float64)))
    idx = np.arange(CHUNK, dtype=np.float64)
    dist = idx[:, None] - idx[None, :]
    intra = np.where(dist >= 0, np.exp(log_gamma[:, None, None] * np.maximum(dist, 0.0)[None]), 0.0)
    q_dec = np.exp(log_gamma[:, None] * (idx[None, :] + 1.0))
    k_dec = np.exp(log_gamma[:, None] * (CHUNK - 1.0 - idx[None, :]))
    q_dec = np.broadcast_to(q_dec[:, :, None], (RET_HEADS, CHUNK, RET_V_DIM))
    k_dec = np.broadcast_to(k_dec[:, :, None], (RET_HEADS, CHUNK, RET_QK_DIM))
    chunk_dec = tuple(float(v) for v in np.exp(log_gamma * CHUNK))
    as32 = lambda a: jnp.asarray(np.ascontiguousarray(a), F32)
    return as32(np.cos(ang)), as32(np.sin(ang)), as32(intra), as32(q_dec), as32(k_dec), chunk_dec


def _ret_body(chunk_dec, q_ref, k_ref, v_ref, g_ref, cos_ref, sin_ref, intra_ref, qd_ref, kd_ref, o_ref, st_ref):
    @pl.when(pl.program_id(1) == 0)
    def _():
        st_ref[...] = jnp.zeros_like(st_ref)

    cos = cos_ref[...]
    sin = sin_ref[...]
    half = RET_QK_DIM // 2

    def rotary(t):
        t1, t2 = t[:, :half], t[:, half:]
        return jnp.concatenate([t1 * cos - t2 * sin, t2 * cos + t1 * sin], axis=1)

    for h in range(RET_HEADS):
        qk = slice(h * RET_QK_DIM, (h + 1) * RET_QK_DIM)
        vv = slice(h * RET_V_DIM, (h + 1) * RET_V_DIM)
        qr = rotary(q_ref[:, qk].astype(F32))
        kr = rotary(k_ref[:, qk].astype(F32)) * (RET_QK_DIM ** -0.5)
        qb = qr.astype(BF16)
        vh = v_ref[:, vv]
        st = st_ref[h]
        s = _nt_dot(qb, kr.astype(BF16)) * intra_ref[h]
        y = jnp.dot(s.astype(BF16), vh, preferred_element_type=F32)
        y = y + jnp.dot(qb, st.astype(BF16), preferred_element_type=F32) * qd_ref[h]
        kdt = jnp.transpose(kr * kd_ref[h]).astype(BF16)
        st_ref[h] = st * chunk_dec[h] + jnp.dot(kdt, vh, preferred_element_type=F32)
        o = y * lax.rsqrt(jnp.mean(y * y, axis=-1, keepdims=True) + NORM_EPS)
        g = g_ref[:, vv].astype(F32)
        o_ref[:, vv] = (g * _sigmoid(g) * o).astype(BF16)


def _retention(proj, bsz, nc):
    t = proj.shape[0]
    cos, sin, intra, q_dec, k_dec, chunk_dec = _ret_tables(nc * CHUNK)
    row = lambda b, c: b * nc + c
    const3 = lambda b, c: (0, 0, 0)
    return pl.pallas_call(
        functools.partial(_ret_body, chunk_dec),
        grid=(bsz, nc),
        in_specs=[
            pl.BlockSpec((CHUNK, RET_QK), lambda b, c: (row(b, c), 0)),
            pl.BlockSpec((CHUNK, RET_QK), lambda b, c: (row(b, c), 1)),
            pl.BlockSpec((CHUNK, RET_V), lambda b, c: (row(b, c), 1)),
            pl.BlockSpec((CHUNK, RET_V), lambda b, c: (row(b, c), 2)),
            pl.BlockSpec((CHUNK, RET_QK_DIM // 2), lambda b, c: (c, 0)),
            pl.BlockSpec((CHUNK, RET_QK_DIM // 2), lambda b, c: (c, 0)),
            pl.BlockSpec((RET_HEADS, CHUNK, CHUNK), const3),
            pl.BlockSpec((RET_HEADS, CHUNK, RET_V_DIM), const3),
            pl.BlockSpec((RET_HEADS, CHUNK, RET_QK_DIM), const3),
        ],
        out_specs=pl.BlockSpec((CHUNK, RET_V), lambda b, c: (row(b, c), 0)),
        out_shape=jax.ShapeDtypeStruct((t, RET_V), BF16),
        scratch_shapes=[pltpu.VMEM((RET_HEADS, RET_QK_DIM, RET_V_DIM), F32)],
        compiler_params=_params(("parallel", "arbitrary")),
        name="retention",
    )(proj, proj, proj, proj, cos, sin, intra, q_dec, k_dec)


CONV_COLS = 256


def _ssd_body(z_ref, xs_ref, bc_ref, dt_ref, shift_ref, cw_ref, cb_ref, dtb_ref, alog_ref, dsk_ref, nw_ref,
              o_ref, st_ref, carry_ref, xc_ref, y_ref):
    c = pl.program_id(1)

    @pl.when(c == 0)
    def _():
        st_ref[...] = jnp.zeros_like(st_ref)
        carry_ref[...] = jnp.zeros_like(carry_ref)

    rows = lax.broadcasted_iota(I32, (CHUNK, 1), 0)
    valid = jnp.logical_or(c > 0, rows >= META_PAD)

    taps = CONV_WIDTH - 1
    for s in range(CONV_CH // CONV_COLS):
        cols = slice(s * CONV_COLS, (s + 1) * CONV_COLS)
        if (s + 1) * CONV_COLS <= SSM_INNER:
            raw = xs_ref[:, cols]
        else:
            raw = bc_ref[:, s * CONV_COLS - SSM_INNER:(s + 1) * CONV_COLS - SSM_INNER]
        shifted = jnp.dot(shift_ref[...], raw, preferred_element_type=F32)
        raw32 = raw.astype(F32)
        acc = cb_ref[:, cols] + cw_ref[taps:taps + 1, cols] * raw32
        window = jnp.concatenate([carry_ref[:, cols], jnp.zeros((8, CONV_COLS), F32)], axis=0)
        head = jnp.zeros((8, CONV_COLS), F32)
        for k in range(taps):
            acc = acc + cw_ref[k:k + 1, cols] * shifted[k * CHUNK:(k + 1) * CHUNK]
            head = head + cw_ref[k:k + 1, cols] * window[8 - taps + k:16 - taps + k]
        acc = jnp.concatenate([acc[:8] + head, acc[8:]], axis=0)
        carry_ref[:, cols] = raw32[CHUNK - 8:]
        xc_ref[:, cols] = jnp.where(valid, acc * _sigmoid(acc), 0.0)

    dtv = dt_ref[...] + dtb_ref[...]
    dt = jnp.maximum(dtv, 0.0) + jnp.log1p(jnp.exp(-jnp.abs(dtv)))
    dt = jnp.where(valid, dt, 0.0)
    a = dt * (-jnp.exp(alog_ref[...]))
    ri = lax.broadcasted_iota(I32, (CHUNK, CHUNK), 0)
    ci = lax.broadcasted_iota(I32, (CHUNK, CHUNK), 1)
    causal = ri >= ci
    acs = jnp.dot(causal.astype(F32), a, preferred_element_type=F32, precision=HIGHEST)
    last = acs[CHUNK - 1:CHUNK, :]
    acs_t = jnp.transpose(acs)
    dt_t = jnp.transpose(dt)
    w_t = jnp.transpose(jnp.exp(last - acs) * dt)
    e_last = jnp.exp(last)
    lanes = lax.broadcasted_iota(I32, (1, CHUNK), 1)
    low = lanes < SSM_HEAD_DIM

    for g in range(SSM_GROUPS):
        bm = xc_ref[:, SSM_INNER + g * SSM_STATE:SSM_INNER + (g + 1) * SSM_STATE]
        cm = xc_ref[:, SSM_INNER + (SSM_GROUPS + g) * SSM_STATE:SSM_INNER + (SSM_GROUPS + g + 1) * SSM_STATE]
        cb = _nt_dot(cm.astype(BF16), bm.astype(BF16))
        bm_t = jnp.transpose(bm)
        for pp in range(PAIRS_PER_GROUP):
            m = g * PAIRS_PER_GROUP + pp
            x_pair = xc_ref[:, m * CHUNK:(m + 1) * CHUNK]
            st_pair = st_ref[m]
            y_pair = jnp.zeros((CHUNK, CHUNK), F32)
            upd = jnp.zeros((SSM_STATE, CHUNK), F32)
            for hh in range(2):
                h = 2 * m + hh
                lane_mask = low if hh == 0 else jnp.logical_not(low)
                col = jnp.broadcast_to(acs[:, h:h + 1], (CHUNK, CHUNK))
                seg = col - acs_t[h:h + 1, :]
                dec = jnp.exp(jnp.where(causal, seg, -1e30))
                mat = cb * dec * dt_t[h:h + 1, :]
                lhs = jnp.concatenate([mat, cm * jnp.exp(col)], axis=1).astype(BF16)
                xm = jnp.where(lane_mask, x_pair, 0.0).astype(BF16)
                sm = jnp.where(lane_mask, st_pair, 0.0).astype(BF16)
                y_pair = y_pair + jnp.dot(lhs, jnp.concatenate([xm, sm], axis=0), preferred_element_type=F32)
                upd = upd + jnp.dot((bm_t * w_t[h:h + 1, :]).astype(BF16), xm, preferred_element_type=F32)
            decay = jnp.where(low, jnp.broadcast_to(e_last[:, 2 * m:2 * m + 1], (1, CHUNK)),
                              jnp.broadcast_to(e_last[:, 2 * m + 1:2 * m + 2], (1, CHUNK)))
            st_ref[m] = st_pair * decay + upd
            y_ref[:, m * CHUNK:(m + 1) * CHUNK] = y_pair

    gsz = SSM_INNER // SSM_GROUPS
    for g in range(SSM_GROUPS):
        cols = slice(g * gsz, (g + 1) * gsz)
        z = z_ref[:, cols].astype(F32)
        y = (y_ref[:, cols] + dsk_ref[:, cols] * xc_ref[:, cols]) * (z * _sigmoid(z))
        y = y * lax.rsqrt(jnp.mean(y * y, axis=-1, keepdims=True) + NORM_EPS)
        o_ref[:, cols] = (y * nw_ref[:, cols]).astype(BF16)


def _ssd(proj, dt_raw, conv_w, conv_b, dt_bias, a_log, d_skip, norm_w, bsz, nc):
    t = proj.shape[0]
    row = lambda b, c: b * nc + c
    const2 = lambda b, c: (0, 0)
    pad = lambda v: jnp.pad(v.astype(F32), (0, DT_PAD - SSM_HEADS)).reshape(1, DT_PAD)
    taps = CONV_WIDTH - 1
    shift = np.zeros((taps * CHUNK, CHUNK), np.float32)
    for k in range(taps):
        for r in range(taps - k, CHUNK):
            shift[k * CHUNK + r, r - (taps - k)] = 1.0
    return pl.pallas_call(
        _ssd_body,
        grid=(bsz, nc),
        in_specs=[
            pl.BlockSpec((CHUNK, SSM_INNER), lambda b, c: (row(b, c), 3)),
            pl.BlockSpec((CHUNK, SSM_INNER), lambda b, c: (row(b, c), 4)),
            pl.BlockSpec((CHUNK, 1024), lambda b, c: (row(b, c), 10)),
            pl.BlockSpec((CHUNK, DT_PAD), lambda b, c: (row(b, c), 0)),
            pl.BlockSpec((taps * CHUNK, CHUNK), const2),
            pl.BlockSpec((CONV_WIDTH, CONV_CH), const2),
            pl.BlockSpec((1, CONV_CH), const2),
            pl.BlockSpec((1, DT_PAD), const2),
            pl.BlockSpec((1, DT_PAD), const2),
            pl.BlockSpec((1, SSM_INNER), const2),
            pl.BlockSpec((1, SSM_INNER), const2),
        ],
        out_specs=pl.BlockSpec((CHUNK, SSM_INNER), lambda b, c: (row(b, c), 0)),
        out_shape=jax.ShapeDtypeStruct((t, SSM_INNER), BF16),
        scratch_shapes=[
            pltpu.VMEM((HEAD_PAIRS, SSM_STATE, CHUNK), F32),
            pltpu.VMEM((8, CONV_CH), F32),
            pltpu.VMEM((CHUNK, CONV_CH), F32),
            pltpu.VMEM((CHUNK, SSM_INNER), F32),
        ],
        compiler_params=_params(("parallel", "arbitrary")),
        name="ssd",
    )(proj, proj, proj, dt_raw, jnp.asarray(shift, BF16), conv_w.astype(F32), conv_b.astype(F32).reshape(1, CONV_CH),
      pad(dt_bias), pad(a_log),
      jnp.repeat(d_skip.astype(F32), SSM_HEAD_DIM).reshape(1, SSM_INNER), norm_w.astype(F32).reshape(1, SSM_INNER))


def _merge_body(oret_ref, ossd_ref, g0_ref, g1_ref, h_ref, valid_ref, wb0_ref, wb1_ref, wo_ref, nw_ref, rwh_ref, rwl_ref,
                rb_ref, h1_ref, un_ref, e_ref, w_ref):
    y_ret = jnp.dot(oret_ref[...], wb0_ref[...], preferred_element_type=F32)
    y_ssd = jnp.dot(ossd_ref[...], wb1_ref[...], preferred_element_type=F32)
    merged = (_sigmoid(g0_ref[...].astype(F32)) * y_ret + _sigmoid(g1_ref[...].astype(F32)) * y_ssd)
    h1 = h_ref[...] + jnp.dot(merged.astype(BF16), wo_ref[...], preferred_element_type=F32)
    h1_ref[...] = h1
    un = h1 * lax.rsqrt(jnp.mean(h1 * h1, axis=-1, keepdims=True) + NORM_EPS) * nw_ref[...]
    un_hi = un.astype(BF16)
    un_ref[...] = un_hi

    un_lo = (un - un_hi.astype(F32)).astype(BF16)
    logits = (jnp.dot(un_hi, rwh_ref[...], preferred_element_type=F32)
              + jnp.dot(un_lo, rwh_ref[...], preferred_element_type=F32)
              + jnp.dot(un_hi, rwl_ref[...], preferred_element_type=F32)) + rb_ref[...]
    logits = jnp.transpose(logits)[:N_EXPERTS, :]
    tm = logits.shape[1]
    eidx = lax.broadcasted_iota(I32, (N_EXPERTS, tm), 0)
    vals, ids = [], []
    for _ in range(TOP_K):
        best = jnp.max(logits, axis=0, keepdims=True)
        arg = jnp.min(jnp.where(logits == best, eidx, N_EXPERTS), axis=0, keepdims=True)
        vals.append(best)
        ids.append(arg)
        logits = jnp.where(eidx == arg, -jnp.inf, logits)
    ex = [jnp.exp(v - vals[0]) for v in vals]
    denom = ex[0] + ex[1] + ex[2] + ex[3]
    is_tok = valid_ref[...] > 0
    e_ref[...] = jnp.concatenate([jnp.where(is_tok, i, -1) for i in ids] + [jnp.zeros((8 - TOP_K, tm), I32)], axis=0)
    w_ref[...] = jnp.concatenate([x / denom for x in ex] + [jnp.zeros((8 - TOP_K, tm), F32)], axis=0)


def _merge(o_ret, o_ssd, proj, hp, valid, wb0, wb1, wo, norm_w, router_w, router_b):
    t = hp.shape[0]
    tm = _pick(t, (512, 256, 128))
    gate_blk = OFF_DT // D_MODEL
    assert gate_blk * D_MODEL == OFF_DT
    const2 = lambda i: (0, 0)
    rw_hi = router_w.astype(BF16)
    rw_lo = (router_w - rw_hi.astype(F32)).astype(BF16)
    return pl.pallas_call(
        _merge_body,
        grid=(t // tm,),
        in_specs=[
            pl.BlockSpec((tm, RET_V), lambda i: (i, 0)),
            pl.BlockSpec((tm, SSM_INNER), lambda i: (i, 0)),
            pl.BlockSpec((tm, D_MODEL), lambda i: (i, gate_blk)),
            pl.BlockSpec((tm, D_MODEL), lambda i: (i, gate_blk + 1)),
            pl.BlockSpec((tm, D_MODEL), lambda i: (i, 0)),
            pl.BlockSpec((1, tm), lambda i: (0, i)),
            pl.BlockSpec((RET_V, D_MODEL), const2),
            pl.BlockSpec((SSM_INNER, D_MODEL), const2),
            pl.BlockSpec((D_MODEL, D_MODEL), const2),
            pl.BlockSpec((1, D_MODEL), const2),
            pl.BlockSpec((D_MODEL, CHUNK), const2),
            pl.BlockSpec((D_MODEL, CHUNK), const2),
            pl.BlockSpec((1, CHUNK), const2),
        ],
        out_specs=[
            pl.BlockSpec((tm, D_MODEL), lambda i: (i, 0)),
            pl.BlockSpec((tm, D_MODEL), lambda i: (i, 0)),
            pl.BlockSpec((8, tm), lambda i: (0, i)),
            pl.BlockSpec((8, tm), lambda i: (0, i)),
        ],
        out_shape=[
            jax.ShapeDtypeStruct((t, D_MODEL), F32),
            jax.ShapeDtypeStruct((t, D_MODEL), BF16),
            jax.ShapeDtypeStruct((8, t), I32),
            jax.ShapeDtypeStruct((8, t), F32),
        ],
        compiler_params=_params(("parallel",)),
        name="merge_router",
    )(o_ret, o_ssd, proj, proj, hp, valid, wb0, wb1, wo, norm_w, rw_hi, rw_lo, router_b)


ROW_ALIGN = 8
PIECE = 16
LOCAL_ROWS = 1024
META_PIECES, META_EXPERT, META_LOCAL, META_SORTED = 0, 1, 2, 3
MAX_PIECES = CHUNK * TOP_K // PIECE + N_EXPERTS
assert MAX_PIECES <= CHUNK


def _rank_body(tiles, e_ref, slot_ref, meta_ref, cnt_ref, run_ref):
    @pl.when(pl.program_id(0) == 0)
    def _():
        run_ref[...] = jnp.zeros_like(run_ref)

    eidx = lax.broadcasted_iota(I32, (N_EXPERTS, CHUNK), 0)
    lane = lax.broadcasted_iota(I32, (N_EXPERTS, CHUNK), 1)
    ri = lax.broadcasted_iota(I32, (CHUNK, CHUNK), 0)
    ci = lax.broadcasted_iota(I32, (CHUNK, CHUNK), 1)
    upper = jnp.where(ri <= ci, 1.0, 0.0).astype(BF16)
    ones = jnp.ones((CHUNK, CHUNK), BF16)
    ei = lax.broadcasted_iota(I32, (N_EXPERTS, N_EXPERTS), 0)
    ej = lax.broadcasted_iota(I32, (N_EXPERTS, N_EXPERTS), 1)
    lower = jnp.where(ej < ei, 1.0, 0.0).astype(BF16)
    round_up = lambda v, m: jnp.floor((v + (m - 1)) * (1.0 / m)) * m

    run = run_ref[...]
    for s in range(tiles):
        e_tile = e_ref[:, s * CHUNK:(s + 1) * CHUNK]
        seen = jnp.zeros((N_EXPERTS, CHUNK), F32)
        hits, ranks = [], []
        for k in range(TOP_K):
            hit = eidx == e_tile[k:k + 1, :]
            oh = jnp.where(hit, 1.0, 0.0).astype(BF16)
            ranks.append(seen + jnp.dot(oh, upper, preferred_element_type=F32) - 1.0)
            seen = seen + jnp.dot(oh, ones, preferred_element_type=F32)
            hits.append(hit)
        count = seen
        lstart = jnp.dot(lower, round_up(count, PIECE).astype(BF16), preferred_element_type=F32)
        slots = []
        for k in range(TOP_K):
            slot = jnp.sum(jnp.where(hits[k], lstart + ranks[k], 0.0), axis=0, keepdims=True)
            slots.append(jnp.where(e_tile[k:k + 1, :] >= 0, slot, -1.0))
        slot_ref[:, s * CHUNK:(s + 1) * CHUNK] = jnp.concatenate(
            slots + [jnp.full((8 - TOP_K, CHUNK), -1.0, F32)], axis=0).astype(I32)
        pieces = round_up(count, PIECE) * (1.0 / PIECE)
        first = jnp.dot(lower, pieces.astype(BF16), preferred_element_type=F32)
        nth = lane.astype(F32) - first
        mine = jnp.logical_and(nth >= 0.0, nth < pieces)
        pick = lambda v: jnp.sum(jnp.where(mine, v, 0.0), axis=0, keepdims=True)
        meta_ref[s] = jnp.concatenate(
            [jnp.sum(pieces, axis=0, keepdims=True), pick(eidx.astype(F32)), pick(lstart + PIECE * nth),
             pick(run + PIECE * nth), jnp.zeros((4, CHUNK), F32)], axis=0).astype(I32)
        run = run + round_up(count, ROW_ALIGN)
    run_ref[...] = run
    cnt_ref[...] = run.astype(I32)


def _rank(top_e):
    t = top_e.shape[1]
    n_tiles = t // CHUNK
    tiles = _pick(n_tiles, (4, 2, 1))
    return pl.pallas_call(
        functools.partial(_rank_body, tiles),
        grid=(n_tiles // tiles,),
        in_specs=[pl.BlockSpec((8, tiles * CHUNK), lambda i: (0, i))],
        out_specs=[
            pl.BlockSpec((8, tiles * CHUNK), lambda i: (0, i)),
            pl.BlockSpec((tiles, 8, CHUNK), lambda i: (i, 0, 0)),
            pl.BlockSpec((N_EXPERTS, CHUNK), lambda i: (0, 0)),
        ],
        out_shape=[
            jax.ShapeDtypeStruct((8, t), I32),
            jax.ShapeDtypeStruct((n_tiles, 8, CHUNK), I32),
            jax.ShapeDtypeStruct((N_EXPERTS, CHUNK), I32),
        ],
        scratch_shapes=[pltpu.VMEM((N_EXPERTS, CHUNK), F32)],
        compiler_params=_params(("arbitrary",)),
        name="rank",
    )(top_e)


def _pack_halves(v):
    bits = pltpu.bitcast(v, jnp.uint32)
    return (bits[:, :D_MODEL // 2] >> 16) | bits[:, D_MODEL // 2:]


def _unpack_halves(p):
    lo = pltpu.bitcast(p << 16, F32)
    hi = pltpu.bitcast(p & jnp.uint32(0xFFFF0000), F32)
    return jnp.concatenate([lo, hi], axis=1).astype(BF16)


def _wait_pieces(meta_ref, make_copy):
    def body(j, carry):
        make_copy(0, 0).wait()
        return carry

    lax.fori_loop(0, meta_ref[0, META_PIECES, 0], body, 0)


def _start_pieces(pstart_ref, meta_ref, make_copy):
    def body(p, carry):
        sorted_row = pstart_ref[meta_ref[0, META_EXPERT, p]] + meta_ref[0, META_SORTED, p]
        make_copy(pl.multiple_of(meta_ref[0, META_LOCAL, p], ROW_ALIGN), pl.multiple_of(sorted_row, ROW_ALIGN)).start()
        return carry

    lax.fori_loop(0, meta_ref[0, META_PIECES, 0], body, 0)


def _dispatch_body(pstart_ref, fill_ref, meta_ref, prev_meta_ref, slot_ref, un_ref, xs_ref, loc_ref, zero_ref, sems):
    step = pl.program_id(0)
    last = pl.num_programs(0) - 1
    buf = step % 2

    srow = lax.broadcasted_iota(I32, (LOCAL_ROWS, CHUNK), 0)
    onehot = jnp.zeros((LOCAL_ROWS, CHUNK), F32)
    for k in range(TOP_K):
        onehot = onehot + jnp.where(srow == slot_ref[k:k + 1, :], 1.0, 0.0)
    loc_ref[buf] = _pack_halves(jnp.dot(onehot.astype(BF16), un_ref[...], preferred_element_type=F32))

    def copies_of(b):
        def make_copy(local_row, sorted_row):
            return pltpu.make_async_copy(loc_ref.at[b, pl.ds(local_row, PIECE), :],
                                         xs_ref.at[pl.ds(sorted_row, PIECE), :], sems.at[b])
        return make_copy

    @pl.when(step > 0)
    def _():
        _wait_pieces(prev_meta_ref, copies_of(1 - buf))

    _start_pieces(pstart_ref, meta_ref, copies_of(buf))

    @pl.when(step == last)
    def _():
        _wait_pieces(meta_ref, copies_of(buf))
        zero_ref[...] = jnp.zeros_like(zero_ref)

        def fill(act):
            def per_expert(e, carry):
                def per_piece(j, c):
                    row = pl.multiple_of(fill_ref[0, e] + j * ROW_ALIGN, ROW_ALIGN)
                    act(pltpu.make_async_copy(zero_ref, xs_ref.at[pl.ds(row, ROW_ALIGN), :], sems.at[0]))
                    return c
                return lax.fori_loop(0, fill_ref[1, e], per_piece, carry)
            lax.fori_loop(0, N_EXPERTS, per_expert, 0)

        fill(lambda cp: cp.start())
        fill(lambda cp: cp.wait())


def _dispatch(pstart, fill, meta, slot, un, n_rows):
    t = un.shape[0]
    return pl.pallas_call(
        _dispatch_body,
        grid_spec=pltpu.PrefetchScalarGridSpec(
            num_scalar_prefetch=2,
            grid=(t // CHUNK,),
            in_specs=[
                pl.BlockSpec((1, 8, CHUNK), lambda i, ps, fl: (i, 0, 0), memory_space=pltpu.SMEM),
                pl.BlockSpec((1, 8, CHUNK), lambda i, ps, fl: (jnp.maximum(i - 1, 0), 0, 0), memory_space=pltpu.SMEM),
                pl.BlockSpec((8, CHUNK), lambda i, ps, fl: (0, i)),
                pl.BlockSpec((CHUNK, D_MODEL), lambda i, ps, fl: (i, 0)),
            ],
            out_specs=pl.BlockSpec(memory_space=pl.ANY),
            scratch_shapes=[pltpu.VMEM((2, LOCAL_ROWS, D_MODEL // 2), jnp.uint32),
                            pltpu.VMEM((ROW_ALIGN, D_MODEL // 2), jnp.uint32), pltpu.SemaphoreType.DMA((2,))],
        ),
        out_shape=jax.ShapeDtypeStruct((n_rows, D_MODEL // 2), jnp.uint32),
        compiler_params=_params(("arbitrary",)),
        name="dispatch",
    )(pstart, fill, meta, meta, slot, un)


GU_SLAB = 256
GU_HALF = GU_SLAB // 2


def _regroup_body(w_ref, p_ref, o_ref):
    for s in range(2 * EXPERT_FF // GU_SLAB):
        cols = slice(s * GU_SLAB, (s + 1) * GU_SLAB)
        o_ref[0, :, cols] = jnp.dot(w_ref[0, :, cols].astype(BF16), p_ref[...], preferred_element_type=F32).astype(BF16)


def _regroup_gate_up(w_gate_up):
    perm = np.zeros((GU_SLAB, GU_SLAB), np.float32)
    perm[2 * np.arange(GU_HALF), np.arange(GU_HALF)] = 1.0
    perm[2 * np.arange(GU_HALF) + 1, GU_HALF + np.arange(GU_HALF)] = 1.0
    blk = (1, D_MODEL, 2 * EXPERT_FF)
    return pl.pallas_call(
        _regroup_body,
        grid=(N_EXPERTS,),
        in_specs=[pl.BlockSpec(blk, lambda e: (e, 0, 0)), pl.BlockSpec((GU_SLAB, GU_SLAB), lambda e: (0, 0))],
        out_specs=pl.BlockSpec(blk, lambda e: (e, 0, 0)),
        out_shape=jax.ShapeDtypeStruct((N_EXPERTS, D_MODEL, 2 * EXPERT_FF), BF16),
        compiler_params=_params(("parallel",)),
        name="regroup_gate_up",
    )(w_gate_up, jnp.asarray(perm, BF16))


def _regroup_bias(b_gate_up):
    b = b_gate_up.astype(F32).reshape(N_EXPERTS, 2 * EXPERT_FF // GU_SLAB, GU_HALF, 2)
    return jnp.transpose(b, (0, 1, 3, 2)).reshape(N_EXPERTS, 1, 2 * EXPERT_FF)


def _expert_body(be_ref, nu_ref, x_ref, wgu_ref, wd_ref, bgu_ref, bd_ref, y_ref):
    del be_ref

    @pl.when(pl.program_id(0) >= nu_ref[0])
    def _():
        y_ref[...] = jnp.zeros_like(y_ref)

    @pl.when(pl.program_id(0) < nu_ref[0])
    def _():
        x = _unpack_halves(x_ref[...])
        gu = jnp.dot(x, wgu_ref[0], preferred_element_type=F32) + bgu_ref[0]
        acts = []
        for s in range(2 * EXPERT_FF // GU_SLAB):
            gate = jnp.minimum(gu[:, s * GU_SLAB:s * GU_SLAB + GU_HALF], SWIGLU_LIMIT)
            up = jnp.clip(gu[:, s * GU_SLAB + GU_HALF:(s + 1) * GU_SLAB], -SWIGLU_LIMIT, SWIGLU_LIMIT)
            acts.append(((up + 1.0) * (gate * _sigmoid(SWIGLU_ALPHA * gate))).astype(BF16))
        act = jnp.concatenate(acts, axis=1)
        y = jnp.dot(act, wd_ref[0], preferred_element_type=F32) + bd_ref[0]
        y_ref[...] = _pack_halves(y.astype(BF16).astype(F32))


def _experts(block_e, n_used, xs, wgu, wd, bgu, bd):
    n_rows = xs.shape[0]
    n_blocks = n_rows // MOE_BLOCK
    wspec = lambda shape: pl.BlockSpec((1,) + shape, lambda i, be, nu: (be[i], 0, 0))
    return pl.pallas_call(
        _expert_body,
        grid_spec=pltpu.PrefetchScalarGridSpec(
            num_scalar_prefetch=2,
            grid=(n_blocks,),
            in_specs=[
                pl.BlockSpec((MOE_BLOCK, D_MODEL // 2), lambda i, be, nu: (jnp.minimum(i, nu[0] - 1), 0)),
                wspec((D_MODEL, 2 * EXPERT_FF)),
                wspec((EXPERT_FF, D_MODEL)),
                wspec((1, 2 * EXPERT_FF)),
                wspec((1, D_MODEL)),
            ],
            out_specs=pl.BlockSpec((MOE_BLOCK, D_MODEL // 2), lambda i, be, nu: (i, 0)),
        ),
        out_shape=jax.ShapeDtypeStruct((n_rows, D_MODEL // 2), jnp.uint32),
        compiler_params=_params(("arbitrary",)),
        name="experts",
    )(block_e, n_used, xs, wgu, wd, bgu, bd)


def _combine_body(pstart_ref, meta_ref, next_meta_ref, slot_ref, w_ref, h1_ref, nw_ref, y_hbm, o_ref, ybuf, sems):
    step = pl.program_id(0)
    buf = step % 2

    def copies_of(b):
        def make_copy(local_row, sorted_row):
            return pltpu.make_async_copy(y_hbm.at[pl.ds(sorted_row, PIECE), :],
                                         ybuf.at[b, pl.ds(local_row, PIECE), :], sems.at[b])
        return make_copy

    @pl.when(step == 0)
    def _():
        ybuf[...] = jnp.zeros_like(ybuf)
        _start_pieces(pstart_ref, meta_ref, copies_of(buf))

    @pl.when(step + 1 < pl.num_programs(0))
    def _():
        _start_pieces(pstart_ref, next_meta_ref, copies_of(1 - buf))

    _wait_pieces(meta_ref, copies_of(buf))

    srow = lax.broadcasted_iota(I32, (LOCAL_ROWS, CHUNK), 0)
    wmat = jnp.zeros((LOCAL_ROWS, CHUNK), F32)
    for k in range(TOP_K):
        wmat = wmat + jnp.where(srow == slot_ref[k:k + 1, :], w_ref[k:k + 1, :], 0.0)
    wmat = jnp.transpose(wmat)
    w_hi = wmat.astype(BF16)
    w_lo = (wmat - w_hi.astype(F32)).astype(BF16)
    y = _unpack_halves(ybuf[buf])
    f = jnp.dot(w_hi, y, preferred_element_type=F32) + jnp.dot(w_lo, y, preferred_element_type=F32)
    h2 = h1_ref[...] + f
    o_ref[...] = h2 * lax.rsqrt(jnp.mean(h2 * h2, axis=-1, keepdims=True) + NORM_EPS) * nw_ref[...]


def _combine(pstart, meta, slot, top_w, h1, norm_w, y_sorted, bsz, nc):
    seq_chunks = nc - 1
    n_out = bsz * seq_chunks
    src = lambda i: i + i // seq_chunks + 1
    nxt = lambda i: jnp.minimum(i + 1, n_out - 1)
    return pl.pallas_call(
        _combine_body,
        grid_spec=pltpu.PrefetchScalarGridSpec(
            num_scalar_prefetch=1,
            grid=(n_out,),
            in_specs=[
                pl.BlockSpec((1, 8, CHUNK), lambda i, ps: (src(i), 0, 0), memory_space=pltpu.SMEM),
                pl.BlockSpec((1, 8, CHUNK), lambda i, ps: (src(nxt(i)), 0, 0), memory_space=pltpu.SMEM),
                pl.BlockSpec((8, CHUNK), lambda i, ps: (0, src(i))),
                pl.BlockSpec((8, CHUNK), lambda i, ps: (0, src(i))),
                pl.BlockSpec((CHUNK, D_MODEL), lambda i, ps: (src(i), 0)),
                pl.BlockSpec((1, D_MODEL), lambda i, ps: (0, 0)),
                pl.BlockSpec(memory_space=pl.ANY),
            ],
            out_specs=pl.BlockSpec((CHUNK, D_MODEL), lambda i, ps: (i, 0)),
            scratch_shapes=[pltpu.VMEM((2, LOCAL_ROWS, D_MODEL // 2), jnp.uint32), pltpu.SemaphoreType.DMA((2,))],
        ),
        out_shape=jax.ShapeDtypeStruct((n_out * CHUNK, D_MODEL), F32),
        compiler_params=_params(("arbitrary",)),
        name="combine",
    )(pstart, meta, meta, slot, top_w, h1, norm_w, y_sorted)


def kernel(x, meta_tokens, norm_mix, w_in, conv_w, conv_b, dt_bias, a_log, d_skip, ssm_norm, w_branch, w_out, norm_ffn,
           router_w, router_b, w_gate_up, b_gate_up, w_down, b_down, norm_final):
    bsz, seq, _ = x.shape
    assert seq % CHUNK == 0 and norm_mix.shape[0] == 1
    nc = 1 + seq // CHUNK
    lp = nc * CHUNK
    t = bsz * lp

    hp = jnp.concatenate([jnp.zeros((bsz, META_PAD, D_MODEL), x.dtype),
                          jnp.broadcast_to(meta_tokens.astype(x.dtype)[None], (bsz, N_META, D_MODEL)), x], axis=1)
    hp = hp.reshape(t, D_MODEL)
    valid = jnp.asarray(np.tile(np.arange(lp) >= META_PAD, bsz).astype(np.int32).reshape(1, t))

    w_in0 = w_in[0]
    w_main = jnp.concatenate([w_in0[:, :OFF_DT], w_in0[:, OFF_DT + SSM_HEADS:]], axis=1).astype(BF16)
    w_dt = jnp.pad(w_in0[:, OFF_DT:OFF_DT + SSM_HEADS], ((0, 0), (0, DT_PAD - SSM_HEADS))).astype(BF16)
    wgu = _regroup_gate_up(w_gate_up[0])
    bgu = _regroup_bias(b_gate_up[0])
    wd = w_down[0].astype(BF16)
    bd = b_down[0][:, None, :].astype(F32)

    proj, dt_raw = _in_proj(hp, norm_mix[0].reshape(1, D_MODEL).astype(F32), w_main, w_dt)
    o_ret = _retention(proj, bsz, nc)
    o_ssd = _ssd(proj, dt_raw, conv_w[0], conv_b[0], dt_bias[0], a_log[0], d_skip[0], ssm_norm[0], bsz, nc)
    h1, un, top_e, top_w = _merge(
        o_ret, o_ssd, proj, hp, valid, w_branch[0, 0].astype(BF16), w_branch[0, 1].astype(BF16), w_out[0].astype(BF16),
        norm_ffn[0].reshape(1, D_MODEL).astype(F32),
        jnp.pad(router_w[0].astype(F32), ((0, 0), (0, CHUNK - N_EXPERTS))),
        jnp.pad(router_b[0].astype(F32), (0, CHUNK - N_EXPERTS)).reshape(1, CHUNK))

    slot, meta, counts = _rank(top_e)
    counts = counts[:, 0]
    slack = MOE_BLOCK - 1 + PIECE - ROW_ALIGN
    padded = (counts + slack) // MOE_BLOCK * MOE_BLOCK
    pend = jnp.cumsum(padded)
    pstart = (pend - padded).astype(I32)
    n_pairs = bsz * (N_META + seq) * TOP_K
    max_rows = n_pairs + bsz * nc * N_EXPERTS * (ROW_ALIGN - 1)
    n_blocks = (max_rows + N_EXPERTS * slack) // MOE_BLOCK
    blk_row = jnp.arange(n_blocks, dtype=I32) * MOE_BLOCK
    block_e = jnp.minimum(jnp.sum(pend[None, :] <= blk_row[:, None], axis=1), N_EXPERTS - 1).astype(I32)
    n_used = (pend[-1:] // MOE_BLOCK).astype(I32)
    fill = jnp.stack([pstart + counts, (padded - counts) // ROW_ALIGN]).astype(I32)

    xs = _dispatch(pstart, fill, meta, slot, un, n_blocks * MOE_BLOCK)
    ys = _experts(block_e, n_used, xs, wgu, wd, bgu, bd)
    out = _combine(pstart, meta, slot, top_w, h1, norm_final.reshape(1, D_MODEL).astype(F32), ys, bsz, nc)
    return out.reshape(bsz, seq, D_MODEL)
```

```python
import functools

import numpy as np
import jax
import jax.numpy as jnp
from jax import lax
from jax.experimental import pallas as pl
from jax.experimental.pallas import tpu as pltpu

F32 = jnp.float32
BF16 = jnp.bfloat16
I32 = jnp.int32

D_MODEL = 1024
N_META = 16
CHUNK = 128
META_PAD = CHUNK - N_META
NORM_EPS = 1e-6
RET_HEADS = 4
RET_QK_DIM = 256
RET_V_DIM = 512
RET_QK = RET_HEADS * RET_QK_DIM
RET_V = RET_HEADS * RET_V_DIM
ROPE_BASE = 10000.0
SSM_INNER = 2048
SSM_HEAD_DIM = 64
SSM_HEADS = 32
SSM_GROUPS = 4
SSM_STATE = 128
CONV_WIDTH = 4
CONV_CH = SSM_INNER + 2 * SSM_GROUPS * SSM_STATE
HEAD_PAIRS = SSM_HEADS // 2
PAIRS_PER_GROUP = HEAD_PAIRS // SSM_GROUPS
OFF_DT = 2 * RET_QK + 2 * RET_V + SSM_INNER + CONV_CH
N_MAIN = OFF_DT + 2 * D_MODEL
DT_PAD = 128
N_EXPERTS = 32
TOP_K = 4
EXPERT_FF = 1024
SWIGLU_ALPHA = 1.702
SWIGLU_LIMIT = 7.0
MOE_BLOCK = 512

VMEM_LIMIT = 56 * 1024 * 1024
HIGHEST = lax.Precision.HIGHEST


def _params(sem):
    return pltpu.CompilerParams(dimension_semantics=sem, vmem_limit_bytes=VMEM_LIMIT)


def _pick(n, candidates):
    for c in candidates:
        if n % c == 0:
            return c
    raise ValueError(f"no tile for {n} among {candidates}")


def _sigmoid(x):
    return 0.5 * jnp.tanh(0.5 * x) + 0.5


def _nt_dot(a, b, **kw):
    return lax.dot_general(a, b, (((1,), (1,)), ((), ())), preferred_element_type=F32, **kw)


def _inproj_body(x_ref, nw_ref, w_ref, wdt_ref, o_ref, dt_ref, xn_ref):
    @pl.when(pl.program_id(1) == 0)
    def _():
        x = x_ref[...]
        ms = jnp.mean(x * x, axis=-1, keepdims=True)
        xn = (x * lax.rsqrt(ms + NORM_EPS) * nw_ref[...]).astype(BF16)
        xn_ref[...] = xn
        dt_ref[...] = jnp.dot(xn, wdt_ref[...], preferred_element_type=F32)

    o_ref[...] = jnp.dot(xn_ref[...], w_ref[...], preferred_element_type=F32).astype(BF16)


def _in_proj(hp, norm_w, w_main, w_dt):
    t = hp.shape[0]
    tm = _pick(t, (2048, 1024, 512, 256, 128))
    tn = 1024
    return pl.pallas_call(
        _inproj_body,
        grid=(t // tm, N_MAIN // tn),
        in_specs=[
            pl.BlockSpec((tm, D_MODEL), lambda i, j: (i, 0)),
            pl.BlockSpec((1, D_MODEL), lambda i, j: (0, 0)),
            pl.BlockSpec((D_MODEL, tn), lambda i, j: (0, j)),
            pl.BlockSpec((D_MODEL, DT_PAD), lambda i, j: (0, 0)),
        ],
        out_specs=[
            pl.BlockSpec((tm, tn), lambda i, j: (i, j)),
            pl.BlockSpec((tm, DT_PAD), lambda i, j: (i, 0)),
        ],
        out_shape=[jax.ShapeDtypeStruct((t, N_MAIN), BF16), jax.ShapeDtypeStruct((t, DT_PAD), F32)],
        scratch_shapes=[pltpu.VMEM((tm, D_MODEL), BF16)],
        compiler_params=_params(("parallel", "arbitrary")),
        name="in_proj",
    )(hp, norm_w, w_main, w_dt)


def _ret_tables(n_rows, first_pos):
    half = RET_QK_DIM // 2
    inv_freq = ROPE_BASE ** (-np.arange(half, dtype=np.float64) / half)
    pos = np.arange(n_rows, dtype=np.float64) + first_pos
    ang = pos[:, None] * inv_freq[None, :]
    log_gamma = np.log(1.0 - 2.0 ** (-5.0 - np.arange(RET_HEADS, dtype=np.float64)))
    idx = np.arange(CHUNK, dtype=np.float64)
    dist = idx[:, None] - idx[None, :]
    intra = np.where(dist >= 0, np.exp(log_gamma[:, None, None] * np.maximum(dist, 0.0)[None]), 0.0)
    q_dec = np.exp(log_gamma[:, None] * (idx[None, :] + 1.0))
    k_dec = np.exp(log_gamma[:, None] * (CHUNK - 1.0 - idx[None, :]))
    q_dec = np.broadcast_to(q_dec[:, :, None], (RET_HEADS, CHUNK, RET_V_DIM))
    k_dec = np.broadcast_to(k_dec[:, :, None], (RET_HEADS, CHUNK, RET_QK_DIM))
    chunk_dec = tuple(float(v) for v in np.exp(log_gamma * CHUNK))
    as32 = lambda a: jnp.asarray(np.ascontiguousarray(a), F32)
    return as32(np.cos(ang)), as32(np.sin(ang)), as32(intra), as32(q_dec), as32(k_dec), chunk_dec


def _ret_body(chunk_dec, emit_state, q_ref, k_ref, v_ref, g_ref, cos_ref, sin_ref, intra_ref, qd_ref, kd_ref, init_ref,
              o_ref, *rest):
    st_ref = rest[-1]

    @pl.when(pl.program_id(1) == 0)
    def _():
        st_ref[...] = init_ref[...]

    cos = cos_ref[...]
    sin = sin_ref[...]
    half = RET_QK_DIM // 2

    def rotary(t):
        t1, t2 = t[:, :half], t[:, half:]
        return jnp.concatenate([t1 * cos - t2 * sin, t2 * cos + t1 * sin], axis=1)

    for h in range(RET_HEADS):
        qk = slice(h * RET_QK_DIM, (h + 1) * RET_QK_DIM)
        vv = slice(h * RET_V_DIM, (h + 1) * RET_V_DIM)
        qr = rotary(q_ref[:, qk].astype(F32))
        kr = rotary(k_ref[:, qk].astype(F32)) * (RET_QK_DIM ** -0.5)
        qb = qr.astype(BF16)
        vh = v_ref[:, vv]
        st = st_ref[h]
        s = _nt_dot(qb, kr.astype(BF16)) * intra_ref[h]
        y = jnp.dot(s.astype(BF16), vh, preferred_element_type=F32)
        y = y + jnp.dot(qb, st.astype(BF16), preferred_element_type=F32) * qd_ref[h]
        kdt = jnp.transpose(kr * kd_ref[h]).astype(BF16)
        st_ref[h] = st * chunk_dec[h] + jnp.dot(kdt, vh, preferred_element_type=F32)
        o = y * lax.rsqrt(jnp.mean(y * y, axis=-1, keepdims=True) + NORM_EPS)
        g = g_ref[:, vv].astype(F32)
        o_ref[:, vv] = (g * _sigmoid(g) * o).astype(BF16)

    if emit_state:
        @pl.when(pl.program_id(1) == pl.num_programs(1) - 1)
        def _():
            rest[0][...] = st_ref[...]


def _retention(proj, bsz, nc, first_pos, init_state, emit_state):
    t = proj.shape[0]
    cos, sin, intra, q_dec, k_dec, chunk_dec = _ret_tables(nc * CHUNK, first_pos)
    row = lambda b, c: b * nc + c
    const3 = lambda b, c: (0, 0, 0)
    state_shape = (RET_HEADS, RET_QK_DIM, RET_V_DIM)
    out_specs = [pl.BlockSpec((CHUNK, RET_V), lambda b, c: (row(b, c), 0))]
    out_shape = [jax.ShapeDtypeStruct((t, RET_V), BF16)]
    if emit_state:
        out_specs.append(pl.BlockSpec(state_shape, const3))
        out_shape.append(jax.ShapeDtypeStruct(state_shape, F32))
    return pl.pallas_call(
        functools.partial(_ret_body, chunk_dec, emit_state),
        grid=(bsz, nc),
        in_specs=[
            pl.BlockSpec((CHUNK, RET_QK), lambda b, c: (row(b, c), 0)),
            pl.BlockSpec((CHUNK, RET_QK), lambda b, c: (row(b, c), 1)),
            pl.BlockSpec((CHUNK, RET_V), lambda b, c: (row(b, c), 1)),
            pl.BlockSpec((CHUNK, RET_V), lambda b, c: (row(b, c), 2)),
            pl.BlockSpec((CHUNK, RET_QK_DIM // 2), lambda b, c: (c, 0)),
            pl.BlockSpec((CHUNK, RET_QK_DIM // 2), lambda b, c: (c, 0)),
            pl.BlockSpec((RET_HEADS, CHUNK, CHUNK), const3),
            pl.BlockSpec((RET_HEADS, CHUNK, RET_V_DIM), const3),
            pl.BlockSpec((RET_HEADS, CHUNK, RET_QK_DIM), const3),
            pl.BlockSpec(state_shape, const3),
        ],
        out_specs=out_specs,
        out_shape=out_shape,
        scratch_shapes=[pltpu.VMEM(state_shape, F32)],
        compiler_params=_params(("parallel", "arbitrary")),
        name="retention_meta" if emit_state else "retention",
    )(proj, proj, proj, proj, cos, sin, intra, q_dec, k_dec, init_state)


CONV_COLS = 256


def _ssd_body(pad_rows, emit_state, z_ref, xs_ref, bc_ref, dt_ref, shift_ref, cw_ref, cb_ref, dtb_ref, alog_ref, dsk_ref,
              nw_ref, st0_ref, carry0_ref, o_ref, *rest):
    st_ref, carry_ref, xc_ref, y_ref = rest[-4:]
    c = pl.program_id(1)

    @pl.when(c == 0)
    def _():
        st_ref[...] = st0_ref[...]
        carry_ref[...] = carry0_ref[...]

    if pad_rows:
        rows = lax.broadcasted_iota(I32, (CHUNK, 1), 0)
        valid = jnp.logical_or(c > 0, rows >= pad_rows)
        keep = lambda v: jnp.where(valid, v, 0.0)
    else:
        keep = lambda v: v

    taps = CONV_WIDTH - 1
    for s in range(CONV_CH // CONV_COLS):
        cols = slice(s * CONV_COLS, (s + 1) * CONV_COLS)
        if (s + 1) * CONV_COLS <= SSM_INNER:
            raw = xs_ref[:, cols]
        else:
            raw = bc_ref[:, s * CONV_COLS - SSM_INNER:(s + 1) * CONV_COLS - SSM_INNER]
        shifted = jnp.dot(shift_ref[...], raw, preferred_element_type=F32)
        raw32 = raw.astype(F32)
        acc = cb_ref[:, cols] + cw_ref[taps:taps + 1, cols] * raw32
        window = jnp.concatenate([carry_ref[:, cols], jnp.zeros((8, CONV_COLS), F32)], axis=0)
        head = jnp.zeros((8, CONV_COLS), F32)
        for k in range(taps):
            acc = acc + cw_ref[k:k + 1, cols] * shifted[k * CHUNK:(k + 1) * CHUNK]
            head = head + cw_ref[k:k + 1, cols] * window[8 - taps + k:16 - taps + k]
        acc = jnp.concatenate([acc[:8] + head, acc[8:]], axis=0)
        carry_ref[:, cols] = raw32[CHUNK - 8:]
        xc_ref[:, cols] = keep(acc * _sigmoid(acc))

    dtv = dt_ref[...] + dtb_ref[...]
    dt = jnp.maximum(dtv, 0.0) + jnp.log1p(jnp.exp(-jnp.abs(dtv)))
    dt = keep(dt)
    a = dt * (-jnp.exp(alog_ref[...]))
    ri = lax.broadcasted_iota(I32, (CHUNK, CHUNK), 0)
    ci = lax.broadcasted_iota(I32, (CHUNK, CHUNK), 1)
    causal = ri >= ci
    acs = jnp.dot(causal.astype(F32), a, preferred_element_type=F32, precision=HIGHEST)
    last = acs[CHUNK - 1:CHUNK, :]
    acs_t = jnp.transpose(acs)
    dt_t = jnp.transpose(dt)
    w_t = jnp.transpose(jnp.exp(last - acs) * dt)
    e_last = jnp.exp(last)
    lanes = lax.broadcasted_iota(I32, (1, CHUNK), 1)
    low = lanes < SSM_HEAD_DIM

    for g in range(SSM_GROUPS):
        bm = xc_ref[:, SSM_INNER + g * SSM_STATE:SSM_INNER + (g + 1) * SSM_STATE]
        cm = xc_ref[:, SSM_INNER + (SSM_GROUPS + g) * SSM_STATE:SSM_INNER + (SSM_GROUPS + g + 1) * SSM_STATE]
        cb = _nt_dot(cm.astype(BF16), bm.astype(BF16))
        bm_t = jnp.transpose(bm)
        for pp in range(PAIRS_PER_GROUP):
            m = g * PAIRS_PER_GROUP + pp
            x_pair = xc_ref[:, m * CHUNK:(m + 1) * CHUNK]
            st_pair = st_ref[m]
            y_pair = jnp.zeros((CHUNK, CHUNK), F32)
            upd = jnp.zeros((SSM_STATE, CHUNK), F32)
            for hh in range(2):
                h = 2 * m + hh
                lane_mask = low if hh == 0 else jnp.logical_not(low)
                col = jnp.broadcast_to(acs[:, h:h + 1], (CHUNK, CHUNK))
                seg = col - acs_t[h:h + 1, :]
                dec = jnp.exp(jnp.where(causal, seg, -1e30))
                mat = cb * dec * dt_t[h:h + 1, :]
                lhs = jnp.concatenate([mat, cm * jnp.exp(col)], axis=1).astype(BF16)
                xm = jnp.where(lane_mask, x_pair, 0.0).astype(BF16)
                sm = jnp.where(lane_mask, st_pair, 0.0).astype(BF16)
                y_pair = y_pair + jnp.dot(lhs, jnp.concatenate([xm, sm], axis=0), preferred_element_type=F32)
                upd = upd + jnp.dot((bm_t * w_t[h:h + 1, :]).astype(BF16), xm, preferred_element_type=F32)
            decay = jnp.where(low, jnp.broadcast_to(e_last[:, 2 * m:2 * m + 1], (1, CHUNK)),
                              jnp.broadcast_to(e_last[:, 2 * m + 1:2 * m + 2], (1, CHUNK)))
            st_ref[m] = st_pair * decay + upd
            y_ref[:, m * CHUNK:(m + 1) * CHUNK] = y_pair

    gsz = SSM_INNER // SSM_GROUPS
    for g in range(SSM_GROUPS):
        cols = slice(g * gsz, (g + 1) * gsz)
        z = z_ref[:, cols].astype(F32)
        y = (y_ref[:, cols] + dsk_ref[:, cols] * xc_ref[:, cols]) * (z * _sigmoid(z))
        y = y * lax.rsqrt(jnp.mean(y * y, axis=-1, keepdims=True) + NORM_EPS)
        o_ref[:, cols] = (y * nw_ref[:, cols]).astype(BF16)

    if emit_state:
        @pl.when(c == pl.num_programs(1) - 1)
        def _():
            rest[0][...] = st_ref[...]
            rest[1][...] = carry_ref[...]


def _ssd(proj, dt_raw, conv_w, conv_b, dt_bias, a_log, d_skip, norm_w, bsz, nc, pad_rows, init, emit_state):
    t = proj.shape[0]
    row = lambda b, c: b * nc + c
    const2 = lambda b, c: (0, 0)
    const3 = lambda b, c: (0, 0, 0)
    state_shape = (HEAD_PAIRS, SSM_STATE, CHUNK)
    carry_shape = (8, CONV_CH)
    out_specs = [pl.BlockSpec((CHUNK, SSM_INNER), lambda b, c: (row(b, c), 0))]
    out_shape = [jax.ShapeDtypeStruct((t, SSM_INNER), BF16)]
    if emit_state:
        out_specs += [pl.BlockSpec(state_shape, const3), pl.BlockSpec(carry_shape, const2)]
        out_shape += [jax.ShapeDtypeStruct(state_shape, F32), jax.ShapeDtypeStruct(carry_shape, F32)]
    pad = lambda v: jnp.pad(v.astype(F32), (0, DT_PAD - SSM_HEADS)).reshape(1, DT_PAD)
    taps = CONV_WIDTH - 1
    shift = np.zeros((taps * CHUNK, CHUNK), np.float32)
    for k in range(taps):
        for r in range(taps - k, CHUNK):
            shift[k * CHUNK + r, r - (taps - k)] = 1.0
    return pl.pallas_call(
        functools.partial(_ssd_body, pad_rows, emit_state),
        grid=(bsz, nc),
        in_specs=[
            pl.BlockSpec((CHUNK, SSM_INNER), lambda b, c: (row(b, c), 3)),
            pl.BlockSpec((CHUNK, SSM_INNER), lambda b, c: (row(b, c), 4)),
            pl.BlockSpec((CHUNK, 1024), lambda b, c: (row(b, c), 10)),
            pl.BlockSpec((CHUNK, DT_PAD), lambda b, c: (row(b, c), 0)),
            pl.BlockSpec((taps * CHUNK, CHUNK), const2),
            pl.BlockSpec((CONV_WIDTH, CONV_CH), const2),
            pl.BlockSpec((1, CONV_CH), const2),
            pl.BlockSpec((1, DT_PAD), const2),
            pl.BlockSpec((1, DT_PAD), const2),
            pl.BlockSpec((1, SSM_INNER), const2),
            pl.BlockSpec((1, SSM_INNER), const2),
            pl.BlockSpec(state_shape, const3),
            pl.BlockSpec(carry_shape, const2),
        ],
        out_specs=out_specs,
        out_shape=out_shape,
        scratch_shapes=[
            pltpu.VMEM(state_shape, F32),
            pltpu.VMEM(carry_shape, F32),
            pltpu.VMEM((CHUNK, CONV_CH), F32),
            pltpu.VMEM((CHUNK, SSM_INNER), F32),
        ],
        compiler_params=_params(("parallel", "arbitrary")),
        name="ssd_meta" if emit_state else "ssd",
    )(proj, proj, proj, dt_raw, jnp.asarray(shift, BF16), conv_w.astype(F32), conv_b.astype(F32).reshape(1, CONV_CH),
      pad(dt_bias), pad(a_log),
      jnp.repeat(d_skip.astype(F32), SSM_HEAD_DIM).reshape(1, SSM_INNER), norm_w.astype(F32).reshape(1, SSM_INNER),
      *init)


def _merge_body(oret_ref, ossd_ref, g0_ref, g1_ref, h_ref, wb0_ref, wb1_ref, wo_ref, nw_ref, rwh_ref, rwl_ref,
                rb_ref, h1_ref, un_ref, e_ref, w_ref):
    y_ret = jnp.dot(oret_ref[...], wb0_ref[...], preferred_element_type=F32)
    y_ssd = jnp.dot(ossd_ref[...], wb1_ref[...], preferred_element_type=F32)
    merged = (_sigmoid(g0_ref[...].astype(F32)) * y_ret + _sigmoid(g1_ref[...].astype(F32)) * y_ssd)
    h1 = h_ref[...] + jnp.dot(merged.astype(BF16), wo_ref[...], preferred_element_type=F32)
    h1_ref[...] = h1
    un = h1 * lax.rsqrt(jnp.mean(h1 * h1, axis=-1, keepdims=True) + NORM_EPS) * nw_ref[...]
    un_hi = un.astype(BF16)
    un_ref[...] = un_hi

    un_lo = (un - un_hi.astype(F32)).astype(BF16)
    logits = (jnp.dot(un_hi, rwh_ref[...], preferred_element_type=F32)
              + jnp.dot(un_lo, rwh_ref[...], preferred_element_type=F32)
              + jnp.dot(un_hi, rwl_ref[...], preferred_element_type=F32)) + rb_ref[...]
    logits = jnp.transpose(logits)[:N_EXPERTS, :]
    tm = logits.shape[1]
    eidx = lax.broadcasted_iota(I32, (N_EXPERTS, tm), 0)
    vals, ids = [], []
    for _ in range(TOP_K):
        best = jnp.max(logits, axis=0, keepdims=True)
        arg = jnp.min(jnp.where(logits == best, eidx, N_EXPERTS), axis=0, keepdims=True)
        vals.append(best)
        ids.append(arg)
        logits = jnp.where(eidx == arg, -jnp.inf, logits)
    ex = [jnp.exp(v - vals[0]) for v in vals]
    denom = ex[0] + ex[1] + ex[2] + ex[3]
    e_ref[...] = jnp.concatenate(ids + [jnp.zeros((8 - TOP_K, tm), I32)], axis=0)
    w_ref[...] = jnp.concatenate([x / denom for x in ex] + [jnp.zeros((8 - TOP_K, tm), F32)], axis=0)


def _merge(o_ret, o_ssd, proj, h, wb0, wb1, wo, norm_w, router_w, router_b):
    t = h.shape[0]
    tm = _pick(t, (512, 256, 128))
    gate_blk = OFF_DT // D_MODEL
    assert gate_blk * D_MODEL == OFF_DT
    const2 = lambda i: (0, 0)
    rw_hi = router_w.astype(BF16)
    rw_lo = (router_w - rw_hi.astype(F32)).astype(BF16)
    return pl.pallas_call(
        _merge_body,
        grid=(t // tm,),
        in_specs=[
            pl.BlockSpec((tm, RET_V), lambda i: (i, 0)),
            pl.BlockSpec((tm, SSM_INNER), lambda i: (i, 0)),
            pl.BlockSpec((tm, D_MODEL), lambda i: (i, gate_blk)),
            pl.BlockSpec((tm, D_MODEL), lambda i: (i, gate_blk + 1)),
            pl.BlockSpec((tm, D_MODEL), lambda i: (i, 0)),
            pl.BlockSpec((RET_V, D_MODEL), const2),
            pl.BlockSpec((SSM_INNER, D_MODEL), const2),
            pl.BlockSpec((D_MODEL, D_MODEL), const2),
            pl.BlockSpec((1, D_MODEL), const2),
            pl.BlockSpec((D_MODEL, CHUNK), const2),
            pl.BlockSpec((D_MODEL, CHUNK), const2),
            pl.BlockSpec((1, CHUNK), const2),
        ],
        out_specs=[
            pl.BlockSpec((tm, D_MODEL), lambda i: (i, 0)),
            pl.BlockSpec((tm, D_MODEL), lambda i: (i, 0)),
            pl.BlockSpec((8, tm), lambda i: (0, i)),
            pl.BlockSpec((8, tm), lambda i: (0, i)),
        ],
        out_shape=[
            jax.ShapeDtypeStruct((t, D_MODEL), F32),
            jax.ShapeDtypeStruct((t, D_MODEL), BF16),
            jax.ShapeDtypeStruct((8, t), I32),
            jax.ShapeDtypeStruct((8, t), F32),
        ],
        compiler_params=_params(("parallel",)),
        name="merge_router",
    )(o_ret, o_ssd, proj, proj, h, wb0, wb1, wo, norm_w, rw_hi, rw_lo, router_b)


ROW_ALIGN = 8
PIECE = 16
LOCAL_ROWS = 1024
META_PIECES, META_EXPERT, META_LOCAL, META_SORTED = 0, 1, 2, 3
MAX_PIECES = CHUNK * TOP_K // PIECE + N_EXPERTS
assert MAX_PIECES <= CHUNK


def _rank_body(tiles, e_ref, slot_ref, meta_ref, cnt_ref, run_ref):
    @pl.when(pl.program_id(0) == 0)
    def _():
        run_ref[...] = jnp.zeros_like(run_ref)

    eidx = lax.broadcasted_iota(I32, (N_EXPERTS, CHUNK), 0)
    lane = lax.broadcasted_iota(I32, (N_EXPERTS, CHUNK), 1)
    ri = lax.broadcasted_iota(I32, (CHUNK, CHUNK), 0)
    ci = lax.broadcasted_iota(I32, (CHUNK, CHUNK), 1)
    upper = jnp.where(ri <= ci, 1.0, 0.0).astype(BF16)
    ones = jnp.ones((CHUNK, CHUNK), BF16)
    ei = lax.broadcasted_iota(I32, (N_EXPERTS, N_EXPERTS), 0)
    ej = lax.broadcasted_iota(I32, (N_EXPERTS, N_EXPERTS), 1)
    lower = jnp.where(ej < ei, 1.0, 0.0).astype(BF16)
    round_up = lambda v, m: jnp.floor((v + (m - 1)) * (1.0 / m)) * m

    run = run_ref[...]
    for s in range(tiles):
        e_tile = e_ref[:, s * CHUNK:(s + 1) * CHUNK]
        seen = jnp.zeros((N_EXPERTS, CHUNK), F32)
        hits, ranks = [], []
        for k in range(TOP_K):
            hit = eidx == e_tile[k:k + 1, :]
            oh = jnp.where(hit, 1.0, 0.0).astype(BF16)
            ranks.append(seen + jnp.dot(oh, upper, preferred_element_type=F32) - 1.0)
            seen = seen + jnp.dot(oh, ones, preferred_element_type=F32)
            hits.append(hit)
        count = seen
        lstart = jnp.dot(lower, round_up(count, PIECE).astype(BF16), preferred_element_type=F32)
        slots = []
        for k in range(TOP_K):
            slots.append(jnp.sum(jnp.where(hits[k], lstart + ranks[k], 0.0), axis=0, keepdims=True))
        slot_ref[:, s * CHUNK:(s + 1) * CHUNK] = jnp.concatenate(
            slots + [jnp.full((8 - TOP_K, CHUNK), -1.0, F32)], axis=0).astype(I32)
        pieces = round_up(count, PIECE) * (1.0 / PIECE)
        first = jnp.dot(lower, pieces.astype(BF16), preferred_element_type=F32)
        nth = lane.astype(F32) - first
        mine = jnp.logical_and(nth >= 0.0, nth < pieces)
        pick = lambda v: jnp.sum(jnp.where(mine, v, 0.0), axis=0, keepdims=True)
        meta_ref[s] = jnp.concatenate(
            [jnp.sum(pieces, axis=0, keepdims=True), pick(eidx.astype(F32)), pick(lstart + PIECE * nth),
             pick(run + PIECE * nth), jnp.zeros((4, CHUNK), F32)], axis=0).astype(I32)
        run = run + round_up(count, ROW_ALIGN)
    run_ref[...] = run
    cnt_ref[...] = run.astype(I32)


def _rank(top_e):
    t = top_e.shape[1]
    n_tiles = t // CHUNK
    tiles = _pick(n_tiles, (4, 2, 1))
    return pl.pallas_call(
        functools.partial(_rank_body, tiles),
        grid=(n_tiles // tiles,),
        in_specs=[pl.BlockSpec((8, tiles * CHUNK), lambda i: (0, i))],
        out_specs=[
            pl.BlockSpec((8, tiles * CHUNK), lambda i: (0, i)),
            pl.BlockSpec((tiles, 8, CHUNK), lambda i: (i, 0, 0)),
            pl.BlockSpec((N_EXPERTS, CHUNK), lambda i: (0, 0)),
        ],
        out_shape=[
            jax.ShapeDtypeStruct((8, t), I32),
            jax.ShapeDtypeStruct((n_tiles, 8, CHUNK), I32),
            jax.ShapeDtypeStruct((N_EXPERTS, CHUNK), I32),
        ],
        scratch_shapes=[pltpu.VMEM((N_EXPERTS, CHUNK), F32)],
        compiler_params=_params(("arbitrary",)),
        name="rank",
    )(top_e)


def _pack_halves(v):
    bits = pltpu.bitcast(v, jnp.uint32)
    return (bits[:, :D_MODEL // 2] >> 16) | bits[:, D_MODEL // 2:]


def _unpack_halves(p):
    lo = pltpu.bitcast(p << 16, F32)
    hi = pltpu.bitcast(p & jnp.uint32(0xFFFF0000), F32)
    return jnp.concatenate([lo, hi], axis=1).astype(BF16)


def _wait_pieces(meta_ref, make_copy):
    def body(j, carry):
        make_copy(0, 0).wait()
        return carry

    lax.fori_loop(0, meta_ref[0, META_PIECES, 0], body, 0)


def _start_pieces(pstart_ref, meta_ref, make_copy):
    def body(p, carry):
        sorted_row = pstart_ref[meta_ref[0, META_EXPERT, p]] + meta_ref[0, META_SORTED, p]
        make_copy(pl.multiple_of(meta_ref[0, META_LOCAL, p], ROW_ALIGN), pl.multiple_of(sorted_row, ROW_ALIGN)).start()
        return carry

    lax.fori_loop(0, meta_ref[0, META_PIECES, 0], body, 0)


def _dispatch_body(pstart_ref, fill_ref, meta_ref, prev_meta_ref, slot_ref, un_ref, xs_ref, loc_ref, zero_ref, sems):
    step = pl.program_id(0)
    last = pl.num_programs(0) - 1
    buf = step % 2

    srow = lax.broadcasted_iota(I32, (LOCAL_ROWS, CHUNK), 0)
    onehot = jnp.zeros((LOCAL_ROWS, CHUNK), F32)
    for k in range(TOP_K):
        onehot = onehot + jnp.where(srow == slot_ref[k:k + 1, :], 1.0, 0.0)
    loc_ref[buf] = _pack_halves(jnp.dot(onehot.astype(BF16), un_ref[...], preferred_element_type=F32))

    def copies_of(b):
        def make_copy(local_row, sorted_row):
            return pltpu.make_async_copy(loc_ref.at[b, pl.ds(local_row, PIECE), :],
                                         xs_ref.at[pl.ds(sorted_row, PIECE), :], sems.at[b])
        return make_copy

    @pl.when(step > 0)
    def _():
        _wait_pieces(prev_meta_ref, copies_of(1 - buf))

    _start_pieces(pstart_ref, meta_ref, copies_of(buf))

    @pl.when(step == last)
    def _():
        _wait_pieces(meta_ref, copies_of(buf))
        zero_ref[...] = jnp.zeros_like(zero_ref)

        def fill(act):
            def per_expert(e, carry):
                def per_piece(j, c):
                    row = pl.multiple_of(fill_ref[0, e] + j * ROW_ALIGN, ROW_ALIGN)
                    act(pltpu.make_async_copy(zero_ref, xs_ref.at[pl.ds(row, ROW_ALIGN), :], sems.at[0]))
                    return c
                return lax.fori_loop(0, fill_ref[1, e], per_piece, carry)
            lax.fori_loop(0, N_EXPERTS, per_expert, 0)

        fill(lambda cp: cp.start())
        fill(lambda cp: cp.wait())


def _dispatch(pstart, fill, meta, slot, un, n_rows):
    t = un.shape[0]
    return pl.pallas_call(
        _dispatch_body,
        grid_spec=pltpu.PrefetchScalarGridSpec(
            num_scalar_prefetch=2,
            grid=(t // CHUNK,),
            in_specs=[
                pl.BlockSpec((1, 8, CHUNK), lambda i, ps, fl: (i, 0, 0), memory_space=pltpu.SMEM),
                pl.BlockSpec((1, 8, CHUNK), lambda i, ps, fl: (jnp.maximum(i - 1, 0), 0, 0), memory_space=pltpu.SMEM),
                pl.BlockSpec((8, CHUNK), lambda i, ps, fl: (0, i)),
                pl.BlockSpec((CHUNK, D_MODEL), lambda i, ps, fl: (i, 0)),
            ],
            out_specs=pl.BlockSpec(memory_space=pl.ANY),
            scratch_shapes=[pltpu.VMEM((2, LOCAL_ROWS, D_MODEL // 2), jnp.uint32),
                            pltpu.VMEM((ROW_ALIGN, D_MODEL // 2), jnp.uint32), pltpu.SemaphoreType.DMA((2,))],
        ),
        out_shape=jax.ShapeDtypeStruct((n_rows, D_MODEL // 2), jnp.uint32),
        compiler_params=_params(("arbitrary",)),
        name="dispatch",
    )(pstart, fill, meta, meta, slot, un)


GU_SLAB = 256
GU_HALF = GU_SLAB // 2


def _regroup_body(w_ref, p_ref, o_ref):
    for s in range(2 * EXPERT_FF // GU_SLAB):
        cols = slice(s * GU_SLAB, (s + 1) * GU_SLAB)
        o_ref[0, :, cols] = jnp.dot(w_ref[0, :, cols].astype(BF16), p_ref[...], preferred_element_type=F32).astype(BF16)


def _regroup_gate_up(w_gate_up):
    perm = np.zeros((GU_SLAB, GU_SLAB), np.float32)
    perm[2 * np.arange(GU_HALF), np.arange(GU_HALF)] = 1.0
    perm[2 * np.arange(GU_HALF) + 1, GU_HALF + np.arange(GU_HALF)] = 1.0
    blk = (1, D_MODEL, 2 * EXPERT_FF)
    return pl.pallas_call(
        _regroup_body,
        grid=(N_EXPERTS,),
        in_specs=[pl.BlockSpec(blk, lambda e: (e, 0, 0)), pl.BlockSpec((GU_SLAB, GU_SLAB), lambda e: (0, 0))],
        out_specs=pl.BlockSpec(blk, lambda e: (e, 0, 0)),
        out_shape=jax.ShapeDtypeStruct((N_EXPERTS, D_MODEL, 2 * EXPERT_FF), BF16),
        compiler_params=_params(("parallel",)),
        name="regroup_gate_up",
    )(w_gate_up, jnp.asarray(perm, BF16))


def _regroup_bias(b_gate_up):
    b = b_gate_up.astype(F32).reshape(N_EXPERTS, 2 * EXPERT_FF // GU_SLAB, GU_HALF, 2)
    return jnp.transpose(b, (0, 1, 3, 2)).reshape(N_EXPERTS, 1, 2 * EXPERT_FF)


def _expert_body(be_ref, nu_ref, x_ref, wgu_ref, wd_ref, bgu_ref, bd_ref, y_ref):
    del be_ref

    @pl.when(pl.program_id(0) >= nu_ref[0])
    def _():
        y_ref[...] = jnp.zeros_like(y_ref)

    @pl.when(pl.program_id(0) < nu_ref[0])
    def _():
        x = _unpack_halves(x_ref[...])
        gu = jnp.dot(x, wgu_ref[0], preferred_element_type=F32) + bgu_ref[0]
        acts = []
        for s in range(2 * EXPERT_FF // GU_SLAB):
            gate = jnp.minimum(gu[:, s * GU_SLAB:s * GU_SLAB + GU_HALF], SWIGLU_LIMIT)
            up = jnp.clip(gu[:, s * GU_SLAB + GU_HALF:(s + 1) * GU_SLAB], -SWIGLU_LIMIT, SWIGLU_LIMIT)
            acts.append(((up + 1.0) * (gate * _sigmoid(SWIGLU_ALPHA * gate))).astype(BF16))
        act = jnp.concatenate(acts, axis=1)
        y = jnp.dot(act, wd_ref[0], preferred_element_type=F32) + bd_ref[0]
        y_ref[...] = _pack_halves(y.astype(BF16).astype(F32))


def _experts(block_e, n_used, xs, wgu, wd, bgu, bd):
    n_rows = xs.shape[0]
    n_blocks = n_rows // MOE_BLOCK
    wspec = lambda shape: pl.BlockSpec((1,) + shape, lambda i, be, nu: (be[i], 0, 0))
    return pl.pallas_call(
        _expert_body,
        grid_spec=pltpu.PrefetchScalarGridSpec(
            num_scalar_prefetch=2,
            grid=(n_blocks,),
            in_specs=[
                pl.BlockSpec((MOE_BLOCK, D_MODEL // 2), lambda i, be, nu: (jnp.minimum(i, nu[0] - 1), 0)),
                wspec((D_MODEL, 2 * EXPERT_FF)),
                wspec((EXPERT_FF, D_MODEL)),
                wspec((1, 2 * EXPERT_FF)),
                wspec((1, D_MODEL)),
            ],
            out_specs=pl.BlockSpec((MOE_BLOCK, D_MODEL // 2), lambda i, be, nu: (i, 0)),
        ),
        out_shape=jax.ShapeDtypeStruct((n_rows, D_MODEL // 2), jnp.uint32),
        compiler_params=_params(("arbitrary",)),
        name="experts",
    )(block_e, n_used, xs, wgu, wd, bgu, bd)


def _combine_body(pstart_ref, meta_ref, next_meta_ref, slot_ref, w_ref, h1_ref, nw_ref, y_hbm, o_ref, ybuf, sems):
    step = pl.program_id(0)
    buf = step % 2

    def copies_of(b):
        def make_copy(local_row, sorted_row):
            return pltpu.make_async_copy(y_hbm.at[pl.ds(sorted_row, PIECE), :],
                                         ybuf.at[b, pl.ds(local_row, PIECE), :], sems.at[b])
        return make_copy

    @pl.when(step == 0)
    def _():
        ybuf[...] = jnp.zeros_like(ybuf)
        _start_pieces(pstart_ref, meta_ref, copies_of(buf))

    @pl.when(step + 1 < pl.num_programs(0))
    def _():
        _start_pieces(pstart_ref, next_meta_ref, copies_of(1 - buf))

    _wait_pieces(meta_ref, copies_of(buf))

    srow = lax.broadcasted_iota(I32, (LOCAL_ROWS, CHUNK), 0)
    wmat = jnp.zeros((LOCAL_ROWS, CHUNK), F32)
    for k in range(TOP_K):
        wmat = wmat + jnp.where(srow == slot_ref[k:k + 1, :], w_ref[k:k + 1, :], 0.0)
    wmat = jnp.transpose(wmat)
    w_hi = wmat.astype(BF16)
    w_lo = (wmat - w_hi.astype(F32)).astype(BF16)
    y = _unpack_halves(ybuf[buf])
    f = jnp.dot(w_hi, y, preferred_element_type=F32) + jnp.dot(w_lo, y, preferred_element_type=F32)
    h2 = h1_ref[...] + f
    o_ref[...] = h2 * lax.rsqrt(jnp.mean(h2 * h2, axis=-1, keepdims=True) + NORM_EPS) * nw_ref[...]


def _combine(pstart, meta, slot, top_w, h1, norm_w, y_sorted):
    t = h1.shape[0]
    n_tiles = t // CHUNK
    nxt = lambda i: jnp.minimum(i + 1, n_tiles - 1)
    return pl.pallas_call(
        _combine_body,
        grid_spec=pltpu.PrefetchScalarGridSpec(
            num_scalar_prefetch=1,
            grid=(n_tiles,),
            in_specs=[
                pl.BlockSpec((1, 8, CHUNK), lambda i, ps: (i, 0, 0), memory_space=pltpu.SMEM),
                pl.BlockSpec((1, 8, CHUNK), lambda i, ps: (nxt(i), 0, 0), memory_space=pltpu.SMEM),
                pl.BlockSpec((8, CHUNK), lambda i, ps: (0, i)),
                pl.BlockSpec((8, CHUNK), lambda i, ps: (0, i)),
                pl.BlockSpec((CHUNK, D_MODEL), lambda i, ps: (i, 0)),
                pl.BlockSpec((1, D_MODEL), lambda i, ps: (0, 0)),
                pl.BlockSpec(memory_space=pl.ANY),
            ],
            out_specs=pl.BlockSpec((CHUNK, D_MODEL), lambda i, ps: (i, 0)),
            scratch_shapes=[pltpu.VMEM((2, LOCAL_ROWS, D_MODEL // 2), jnp.uint32), pltpu.SemaphoreType.DMA((2,))],
        ),
        out_shape=jax.ShapeDtypeStruct((t, D_MODEL), F32),
        compiler_params=_params(("arbitrary",)),
        name="combine",
    )(pstart, meta, meta, slot, top_w, h1, norm_w, y_sorted)


def kernel(x, meta_tokens, norm_mix, w_in, conv_w, conv_b, dt_bias, a_log, d_skip, ssm_norm, w_branch, w_out, norm_ffn,
           router_w, router_b, w_gate_up, b_gate_up, w_down, b_down, norm_final):
    bsz, seq, _ = x.shape
    assert seq % CHUNK == 0 and norm_mix.shape[0] == 1
    nc = seq // CHUNK
    t = bsz * seq
    x2d = x.reshape(t, D_MODEL)
    meta_chunk = jnp.concatenate([jnp.zeros((META_PAD, D_MODEL), x.dtype), meta_tokens.astype(x.dtype)], axis=0)

    w_in0 = w_in[0]
    w_main = jnp.concatenate([w_in0[:, :OFF_DT], w_in0[:, OFF_DT + SSM_HEADS:]], axis=1).astype(BF16)
    w_dt = jnp.pad(w_in0[:, OFF_DT:OFF_DT + SSM_HEADS], ((0, 0), (0, DT_PAD - SSM_HEADS))).astype(BF16)
    wgu = _regroup_gate_up(w_gate_up[0])
    bgu = _regroup_bias(b_gate_up[0])
    wd = w_down[0].astype(BF16)
    bd = b_down[0][:, None, :].astype(F32)

    norm_mix_w = norm_mix[0].reshape(1, D_MODEL).astype(F32)
    ssd_params = (conv_w[0], conv_b[0], dt_bias[0], a_log[0], d_skip[0], ssm_norm[0])
    proj_m, dt_m = _in_proj(meta_chunk, norm_mix_w, w_main, w_dt)
    _, ret_state = _retention(proj_m, 1, 1, -META_PAD, jnp.zeros((RET_HEADS, RET_QK_DIM, RET_V_DIM), F32), True)
    _, ssd_state, conv_tail = _ssd(proj_m, dt_m, *ssd_params, 1, 1, META_PAD,
                                   (jnp.zeros((HEAD_PAIRS, SSM_STATE, CHUNK), F32), jnp.zeros((8, CONV_CH), F32)), True)

    proj, dt_raw = _in_proj(x2d, norm_mix_w, w_main, w_dt)
    o_ret, = _retention(proj, bsz, nc, N_META, ret_state, False)
    o_ssd, = _ssd(proj, dt_raw, *ssd_params, bsz, nc, 0, (ssd_state, conv_tail), False)
    h1, un, top_e, top_w = _merge(
        o_ret, o_ssd, proj, x2d, w_branch[0, 0].astype(BF16), w_branch[0, 1].astype(BF16), w_out[0].astype(BF16),
        norm_ffn[0].reshape(1, D_MODEL).astype(F32),
        jnp.pad(router_w[0].astype(F32), ((0, 0), (0, CHUNK - N_EXPERTS))),
        jnp.pad(router_b[0].astype(F32), (0, CHUNK - N_EXPERTS)).reshape(1, CHUNK))

    slot, meta, counts = _rank(top_e)
    counts = counts[:, 0]
    slack = MOE_BLOCK - 1 + PIECE - ROW_ALIGN
    padded = (counts + slack) // MOE_BLOCK * MOE_BLOCK
    pend = jnp.cumsum(padded)
    pstart = (pend - padded).astype(I32)
    n_pairs = t * TOP_K
    max_rows = n_pairs + bsz * nc * N_EXPERTS * (ROW_ALIGN - 1)
    n_blocks = (max_rows + N_EXPERTS * slack) // MOE_BLOCK
    blk_row = jnp.arange(n_blocks, dtype=I32) * MOE_BLOCK
    block_e = jnp.minimum(jnp.sum(pend[None, :] <= blk_row[:, None], axis=1), N_EXPERTS - 1).astype(I32)
    n_used = (pend[-1:] // MOE_BLOCK).astype(I32)
    fill = jnp.stack([pstart + counts, (padded - counts) // ROW_ALIGN]).astype(I32)

    xs = _dispatch(pstart, fill, meta, slot, un, n_blocks * MOE_BLOCK)
    ys = _experts(block_e, n_used, xs, wgu, wd, bgu, bd)
    out = _combine(pstart, meta, slot, top_w, h1, norm_final.reshape(1, D_MODEL).astype(F32), ys)
    return out.reshape(bsz, seq, D_MODEL)
```

```python
import functools

import numpy as np
import jax
import jax.numpy as jnp
from jax import lax
from jax.experimental import pallas as pl
from jax.experimental.pallas import tpu as pltpu

F32 = jnp.float32
BF16 = jnp.bfloat16
I32 = jnp.int32

D_MODEL = 1024
N_META = 16
CHUNK = 128
META_PAD = CHUNK - N_META
NORM_EPS = 1e-6
RET_HEADS = 4
RET_QK_DIM = 256
RET_V_DIM = 512
RET_QK = RET_HEADS * RET_QK_DIM
RET_V = RET_HEADS * RET_V_DIM
ROPE_BASE = 10000.0
SSM_INNER = 2048
SSM_HEAD_DIM = 64
SSM_HEADS = 32
SSM_GROUPS = 4
SSM_STATE = 128
CONV_WIDTH = 4
CONV_CH = SSM_INNER + 2 * SSM_GROUPS * SSM_STATE
HEAD_PAIRS = SSM_HEADS // 2
PAIRS_PER_GROUP = HEAD_PAIRS // SSM_GROUPS
OFF_DT = 2 * RET_QK + 2 * RET_V + SSM_INNER + CONV_CH
N_MAIN = OFF_DT + 2 * D_MODEL
DT_PAD = 128
N_EXPERTS = 32
TOP_K = 4
EXPERT_FF = 1024
SWIGLU_ALPHA = 1.702
SWIGLU_LIMIT = 7.0
MOE_BLOCK = 1024

VMEM_LIMIT = 56 * 1024 * 1024
HIGHEST = lax.Precision.HIGHEST


def _params(sem):
    return pltpu.CompilerParams(dimension_semantics=sem, vmem_limit_bytes=VMEM_LIMIT)


def _pick(n, candidates):
    for c in candidates:
        if n % c == 0:
            return c
    raise ValueError(f"no tile for {n} among {candidates}")


def _sigmoid(x):
    return 0.5 * jnp.tanh(0.5 * x) + 0.5


def _nt_dot(a, b, **kw):
    return lax.dot_general(a, b, (((1,), (1,)), ((), ())), preferred_element_type=F32, **kw)


def _inproj_body(x_ref, nw_ref, w_ref, wdt_ref, o_ref, dt_ref, xn_ref):
    @pl.when(pl.program_id(1) == 0)
    def _():
        x = x_ref[...]
        ms = jnp.mean(x * x, axis=-1, keepdims=True)
        xn = (x * lax.rsqrt(ms + NORM_EPS) * nw_ref[...]).astype(BF16)
        xn_ref[...] = xn
        dt_ref[...] = jnp.dot(xn, wdt_ref[...], preferred_element_type=F32)

    o_ref[...] = jnp.dot(xn_ref[...], w_ref[...], preferred_element_type=F32).astype(BF16)


def _in_proj(hp, norm_w, w_main, w_dt):
    t = hp.shape[0]
    tm = _pick(t, (2048, 1024, 512, 256, 128))
    tn = 1024
    return pl.pallas_call(
        _inproj_body,
        grid=(t // tm, N_MAIN // tn),
        in_specs=[
            pl.BlockSpec((tm, D_MODEL), lambda i, j: (i, 0)),
            pl.BlockSpec((1, D_MODEL), lambda i, j: (0, 0)),
            pl.BlockSpec((D_MODEL, tn), lambda i, j: (0, j)),
            pl.BlockSpec((D_MODEL, DT_PAD), lambda i, j: (0, 0)),
        ],
        out_specs=[
            pl.BlockSpec((tm, tn), lambda i, j: (i, j)),
            pl.BlockSpec((tm, DT_PAD), lambda i, j: (i, 0)),
        ],
        out_shape=[jax.ShapeDtypeStruct((t, N_MAIN), BF16), jax.ShapeDtypeStruct((t, DT_PAD), F32)],
        scratch_shapes=[pltpu.VMEM((tm, D_MODEL), BF16)],
        compiler_params=_params(("parallel", "arbitrary")),
        name="in_proj",
    )(hp, norm_w, w_main, w_dt)


def _ret_tables(n_rows, first_pos):
    half = RET_QK_DIM // 2
    inv_freq = ROPE_BASE ** (-np.arange(half, dtype=np.float64) / half)
    pos = np.arange(n_rows, dtype=np.float64) + first_pos
    ang = pos[:, None] * inv_freq[None, :]
    log_gamma = np.log(1.0 - 2.0 ** (-5.0 - np.arange(RET_HEADS, dtype=np.float64)))
    idx = np.arange(CHUNK, dtype=np.float64)
    dist = idx[:, None] - idx[None, :]
    intra = np.where(dist >= 0, np.exp(log_gamma[:, None, None] * np.maximum(dist, 0.0)[None]), 0.0)
    q_dec = np.exp(log_gamma[:, None] * (idx[None, :] + 1.0))
    k_dec = np.exp(log_gamma[:, None] * (CHUNK - 1.0 - idx[None, :]))
    q_dec = np.broadcast_to(q_dec[:, :, None], (RET_HEADS, CHUNK, RET_V_DIM))
    k_dec = np.broadcast_to(k_dec[:, :, None], (RET_HEADS, CHUNK, RET_QK_DIM))
    chunk_dec = tuple(float(v) for v in np.exp(log_gamma * CHUNK))
    as32 = lambda a: jnp.asarray(np.ascontiguousarray(a), F32)
    return as32(np.cos(ang)), as32(np.sin(ang)), as32(intra), as32(q_dec), as32(k_dec), chunk_dec


def _ret_body(chunk_dec, emit_state, q_ref, k_ref, v_ref, g_ref, cos_ref, sin_ref, intra_ref, qd_ref, kd_ref, init_ref,
              o_ref, *rest):
    st_ref = rest[-1]

    @pl.when(pl.program_id(1) == 0)
    def _():
        st_ref[...] = init_ref[...]

    cos = cos_ref[...]
    sin = sin_ref[...]
    half = RET_QK_DIM // 2

    def rotary(t):
        t1, t2 = t[:, :half], t[:, half:]
        return jnp.concatenate([t1 * cos - t2 * sin, t2 * cos + t1 * sin], axis=1)

    for h in range(RET_HEADS):
        qk = slice(h * RET_QK_DIM, (h + 1) * RET_QK_DIM)
        vv = slice(h * RET_V_DIM, (h + 1) * RET_V_DIM)
        qr = rotary(q_ref[:, qk].astype(F32))
        kr = rotary(k_ref[:, qk].astype(F32)) * (RET_QK_DIM ** -0.5)
        qb = qr.astype(BF16)
        vh = v_ref[:, vv]
        st = st_ref[h]
        s = _nt_dot(qb, kr.astype(BF16)) * intra_ref[h]
        y = jnp.dot(s.astype(BF16), vh, preferred_element_type=F32)
        y = y + jnp.dot(qb, st.astype(BF16), preferred_element_type=F32) * qd_ref[h]
        kdt = jnp.transpose(kr * kd_ref[h]).astype(BF16)
        st_ref[h] = st * chunk_dec[h] + jnp.dot(kdt, vh, preferred_element_type=F32)
        o = y * lax.rsqrt(jnp.mean(y * y, axis=-1, keepdims=True) + NORM_EPS)
        g = g_ref[:, vv].astype(F32)
        o_ref[:, vv] = (g * _sigmoid(g) * o).astype(BF16)

    if emit_state:
        @pl.when(pl.program_id(1) == pl.num_programs(1) - 1)
        def _():
            rest[0][...] = st_ref[...]


def _retention(proj, bsz, nc, first_pos, init_state, emit_state):
    t = proj.shape[0]
    cos, sin, intra, q_dec, k_dec, chunk_dec = _ret_tables(nc * CHUNK, first_pos)
    row = lambda b, c: b * nc + c
    const3 = lambda b, c: (0, 0, 0)
    state_shape = (RET_HEADS, RET_QK_DIM, RET_V_DIM)
    out_specs = [pl.BlockSpec((CHUNK, RET_V), lambda b, c: (row(b, c), 0))]
    out_shape = [jax.ShapeDtypeStruct((t, RET_V), BF16)]
    if emit_state:
        out_specs.append(pl.BlockSpec(state_shape, const3))
        out_shape.append(jax.ShapeDtypeStruct(state_shape, F32))
    return pl.pallas_call(
        functools.partial(_ret_body, chunk_dec, emit_state),
        grid=(bsz, nc),
        in_specs=[
            pl.BlockSpec((CHUNK, RET_QK), lambda b, c: (row(b, c), 0)),
            pl.BlockSpec((CHUNK, RET_QK), lambda b, c: (row(b, c), 1)),
            pl.BlockSpec((CHUNK, RET_V), lambda b, c: (row(b, c), 1)),
            pl.BlockSpec((CHUNK, RET_V), lambda b, c: (row(b, c), 2)),
            pl.BlockSpec((CHUNK, RET_QK_DIM // 2), lambda b, c: (c, 0)),
            pl.BlockSpec((CHUNK, RET_QK_DIM // 2), lambda b, c: (c, 0)),
            pl.BlockSpec((RET_HEADS, CHUNK, CHUNK), const3),
            pl.BlockSpec((RET_HEADS, CHUNK, RET_V_DIM), const3),
            pl.BlockSpec((RET_HEADS, CHUNK, RET_QK_DIM), const3),
            pl.BlockSpec(state_shape, const3),
        ],
        out_specs=out_specs,
        out_shape=out_shape,
        scratch_shapes=[pltpu.VMEM(state_shape, F32)],
        compiler_params=_params(("parallel", "arbitrary")),
        name="retention_meta" if emit_state else "retention",
    )(proj, proj, proj, proj, cos, sin, intra, q_dec, k_dec, init_state)


CONV_COLS = 256


def _ssd_body(pad_rows, emit_state, z_ref, xs_ref, bc_ref, dt_ref, shift_ref, cw_ref, cb_ref, dtb_ref, alog_ref, dsk_ref,
              nw_ref, st0_ref, carry0_ref, o_ref, *rest):
    st_ref, carry_ref, xc_ref, y_ref = rest[-4:]
    c = pl.program_id(1)

    @pl.when(c == 0)
    def _():
        st_ref[...] = st0_ref[...]
        carry_ref[...] = carry0_ref[...]

    if pad_rows:
        rows = lax.broadcasted_iota(I32, (CHUNK, 1), 0)
        valid = jnp.logical_or(c > 0, rows >= pad_rows)
        keep = lambda v: jnp.where(valid, v, 0.0)
    else:
        keep = lambda v: v

    taps = CONV_WIDTH - 1
    for s in range(CONV_CH // CONV_COLS):
        cols = slice(s * CONV_COLS, (s + 1) * CONV_COLS)
        if (s + 1) * CONV_COLS <= SSM_INNER:
            raw = xs_ref[:, cols]
        else:
            raw = bc_ref[:, s * CONV_COLS - SSM_INNER:(s + 1) * CONV_COLS - SSM_INNER]
        shifted = jnp.dot(shift_ref[...], raw, preferred_element_type=F32)
        raw32 = raw.astype(F32)
        acc = cb_ref[:, cols] + cw_ref[taps:taps + 1, cols] * raw32
        window = jnp.concatenate([carry_ref[:, cols], jnp.zeros((8, CONV_COLS), F32)], axis=0)
        head = jnp.zeros((8, CONV_COLS), F32)
        for k in range(taps):
            acc = acc + cw_ref[k:k + 1, cols] * shifted[k * CHUNK:(k + 1) * CHUNK]
            head = head + cw_ref[k:k + 1, cols] * window[8 - taps + k:16 - taps + k]
        acc = jnp.concatenate([acc[:8] + head, acc[8:]], axis=0)
        carry_ref[:, cols] = raw32[CHUNK - 8:]
        xc_ref[:, cols] = keep(acc * _sigmoid(acc))

    dtv = dt_ref[...] + dtb_ref[...]
    dt = jnp.maximum(dtv, 0.0) + jnp.log1p(jnp.exp(-jnp.abs(dtv)))
    dt = keep(dt)
    a = dt * (-jnp.exp(alog_ref[...]))
    ri = lax.broadcasted_iota(I32, (CHUNK, CHUNK), 0)
    ci = lax.broadcasted_iota(I32, (CHUNK, CHUNK), 1)
    causal = ri >= ci
    acs = jnp.dot(causal.astype(F32), a, preferred_element_type=F32, precision=HIGHEST)
    last = acs[CHUNK - 1:CHUNK, :]
    acs_t = jnp.transpose(acs)
    dt_t = jnp.transpose(dt)
    w_t = jnp.transpose(jnp.exp(last - acs) * dt)
    e_last = jnp.exp(last)
    e_acs = jnp.exp(acs)
    lanes = lax.broadcasted_iota(I32, (1, CHUNK), 1)
    low = lanes < SSM_HEAD_DIM

    for g in range(SSM_GROUPS):
        bm = xc_ref[:, SSM_INNER + g * SSM_STATE:SSM_INNER + (g + 1) * SSM_STATE]
        cm = xc_ref[:, SSM_INNER + (SSM_GROUPS + g) * SSM_STATE:SSM_INNER + (SSM_GROUPS + g + 1) * SSM_STATE]
        cb = _nt_dot(cm.astype(BF16), bm.astype(BF16))
        bm_t = jnp.transpose(bm)
        for pp in range(PAIRS_PER_GROUP):
            m = g * PAIRS_PER_GROUP + pp
            x_pair = xc_ref[:, m * CHUNK:(m + 1) * CHUNK]
            st_pair = st_ref[m]
            y_pair = jnp.zeros((CHUNK, CHUNK), F32)
            upd = jnp.zeros((SSM_STATE, CHUNK), F32)
            for hh in range(2):
                h = 2 * m + hh
                lane_mask = low if hh == 0 else jnp.logical_not(low)
                col = jnp.broadcast_to(acs[:, h:h + 1], (CHUNK, CHUNK))
                seg = col - acs_t[h:h + 1, :]
                dec = jnp.exp(jnp.where(causal, seg, -1e30))
                mat = cb * dec * dt_t[h:h + 1, :]
                from_start = jnp.broadcast_to(e_acs[:, h:h + 1], (CHUNK, CHUNK))
                lhs = jnp.concatenate([mat, cm * from_start], axis=1).astype(BF16)
                xm = jnp.where(lane_mask, x_pair, 0.0).astype(BF16)
                sm = jnp.where(lane_mask, st_pair, 0.0).astype(BF16)
                y_pair = y_pair + jnp.dot(lhs, jnp.concatenate([xm, sm], axis=0), preferred_element_type=F32)
                upd = upd + jnp.dot((bm_t * w_t[h:h + 1, :]).astype(BF16), xm, preferred_element_type=F32)
            decay = jnp.where(low, jnp.broadcast_to(e_last[:, 2 * m:2 * m + 1], (1, CHUNK)),
                              jnp.broadcast_to(e_last[:, 2 * m + 1:2 * m + 2], (1, CHUNK)))
            st_ref[m] = st_pair * decay + upd
            y_ref[:, m * CHUNK:(m + 1) * CHUNK] = y_pair

    gsz = SSM_INNER // SSM_GROUPS
    for g in range(SSM_GROUPS):
        cols = slice(g * gsz, (g + 1) * gsz)
        z = z_ref[:, cols].astype(F32)
        y = (y_ref[:, cols] + dsk_ref[:, cols] * xc_ref[:, cols]) * (z * _sigmoid(z))
        y = y * lax.rsqrt(jnp.mean(y * y, axis=-1, keepdims=True) + NORM_EPS)
        o_ref[:, cols] = (y * nw_ref[:, cols]).astype(BF16)

    if emit_state:
        @pl.when(c == pl.num_programs(1) - 1)
        def _():
            rest[0][...] = st_ref[...]
            rest[1][...] = carry_ref[...]


def _ssd(proj, dt_raw, conv_w, conv_b, dt_bias, a_log, d_skip, norm_w, bsz, nc, pad_rows, init, emit_state):
    t = proj.shape[0]
    row = lambda b, c: b * nc + c
    const2 = lambda b, c: (0, 0)
    const3 = lambda b, c: (0, 0, 0)
    state_shape = (HEAD_PAIRS, SSM_STATE, CHUNK)
    carry_shape = (8, CONV_CH)
    out_specs = [pl.BlockSpec((CHUNK, SSM_INNER), lambda b, c: (row(b, c), 0))]
    out_shape = [jax.ShapeDtypeStruct((t, SSM_INNER), BF16)]
    if emit_state:
        out_specs += [pl.BlockSpec(state_shape, const3), pl.BlockSpec(carry_shape, const2)]
        out_shape += [jax.ShapeDtypeStruct(state_shape, F32), jax.ShapeDtypeStruct(carry_shape, F32)]
    pad = lambda v: jnp.pad(v.astype(F32), (0, DT_PAD - SSM_HEADS)).reshape(1, DT_PAD)
    taps = CONV_WIDTH - 1
    shift = np.zeros((taps * CHUNK, CHUNK), np.float32)
    for k in range(taps):
        for r in range(taps - k, CHUNK):
            shift[k * CHUNK + r, r - (taps - k)] = 1.0
    return pl.pallas_call(
        functools.partial(_ssd_body, pad_rows, emit_state),
        grid=(bsz, nc),
        in_specs=[
            pl.BlockSpec((CHUNK, SSM_INNER), lambda b, c: (row(b, c), 3)),
            pl.BlockSpec((CHUNK, SSM_INNER), lambda b, c: (row(b, c), 4)),
            pl.BlockSpec((CHUNK, 1024), lambda b, c: (row(b, c), 10)),
            pl.BlockSpec((CHUNK, DT_PAD), lambda b, c: (row(b, c), 0)),
            pl.BlockSpec((taps * CHUNK, CHUNK), const2),
            pl.BlockSpec((CONV_WIDTH, CONV_CH), const2),
            pl.BlockSpec((1, CONV_CH), const2),
            pl.BlockSpec((1, DT_PAD), const2),
            pl.BlockSpec((1, DT_PAD), const2),
            pl.BlockSpec((1, SSM_INNER), const2),
            pl.BlockSpec((1, SSM_INNER), const2),
            pl.BlockSpec(state_shape, const3),
            pl.BlockSpec(carry_shape, const2),
        ],
        out_specs=out_specs,
        out_shape=out_shape,
        scratch_shapes=[
            pltpu.VMEM(state_shape, F32),
            pltpu.VMEM(carry_shape, F32),
            pltpu.VMEM((CHUNK, CONV_CH), F32),
            pltpu.VMEM((CHUNK, SSM_INNER), F32),
        ],
        compiler_params=_params(("parallel", "arbitrary")),
        name="ssd_meta" if emit_state else "ssd",
    )(proj, proj, proj, dt_raw, jnp.asarray(shift, BF16), conv_w.astype(F32), conv_b.astype(F32).reshape(1, CONV_CH),
      pad(dt_bias), pad(a_log),
      jnp.repeat(d_skip.astype(F32), SSM_HEAD_DIM).reshape(1, SSM_INNER), norm_w.astype(F32).reshape(1, SSM_INNER),
      *init)


def _merge_body(oret_ref, ossd_ref, g0_ref, g1_ref, h_ref, wb0_ref, wb1_ref, wo_ref, nw_ref, rwh_ref, rwl_ref,
                rb_ref, h1_ref, un_ref, e_ref, w_ref):
    y_ret = jnp.dot(oret_ref[...], wb0_ref[...], preferred_element_type=F32)
    y_ssd = jnp.dot(ossd_ref[...], wb1_ref[...], preferred_element_type=F32)
    merged = (_sigmoid(g0_ref[...].astype(F32)) * y_ret + _sigmoid(g1_ref[...].astype(F32)) * y_ssd)
    h1 = h_ref[...] + jnp.dot(merged.astype(BF16), wo_ref[...], preferred_element_type=F32)
    h1_ref[...] = h1
    un = h1 * lax.rsqrt(jnp.mean(h1 * h1, axis=-1, keepdims=True) + NORM_EPS) * nw_ref[...]
    un_hi = un.astype(BF16)
    un_ref[...] = un_hi

    un_lo = (un - un_hi.astype(F32)).astype(BF16)
    logits = (jnp.dot(un_hi, rwh_ref[...], preferred_element_type=F32)
              + jnp.dot(un_lo, rwh_ref[...], preferred_element_type=F32)
              + jnp.dot(un_hi, rwl_ref[...], preferred_element_type=F32)) + rb_ref[...]
    logits = jnp.transpose(logits)[:N_EXPERTS, :]
    tm = logits.shape[1]
    eidx = lax.broadcasted_iota(I32, (N_EXPERTS, tm), 0)
    vals, ids = [], []
    for _ in range(TOP_K):
        best = jnp.max(logits, axis=0, keepdims=True)
        arg = jnp.min(jnp.where(logits == best, eidx, N_EXPERTS), axis=0, keepdims=True)
        vals.append(best)
        ids.append(arg)
        logits = jnp.where(eidx == arg, -jnp.inf, logits)
    ex = [jnp.exp(v - vals[0]) for v in vals]
    denom = ex[0] + ex[1] + ex[2] + ex[3]
    e_ref[...] = jnp.concatenate(ids + [jnp.zeros((8 - TOP_K, tm), I32)], axis=0)
    w_ref[...] = jnp.concatenate([x / denom for x in ex] + [jnp.zeros((8 - TOP_K, tm), F32)], axis=0)


def _merge(o_ret, o_ssd, proj, h, wb0, wb1, wo, norm_w, router_w, router_b):
    t = h.shape[0]
    tm = _pick(t, (512, 256, 128))
    gate_blk = OFF_DT // D_MODEL
    assert gate_blk * D_MODEL == OFF_DT
    const2 = lambda i: (0, 0)
    rw_hi = router_w.astype(BF16)
    rw_lo = (router_w - rw_hi.astype(F32)).astype(BF16)
    return pl.pallas_call(
        _merge_body,
        grid=(t // tm,),
        in_specs=[
            pl.BlockSpec((tm, RET_V), lambda i: (i, 0)),
            pl.BlockSpec((tm, SSM_INNER), lambda i: (i, 0)),
            pl.BlockSpec((tm, D_MODEL), lambda i: (i, gate_blk)),
            pl.BlockSpec((tm, D_MODEL), lambda i: (i, gate_blk + 1)),
            pl.BlockSpec((tm, D_MODEL), lambda i: (i, 0)),
            pl.BlockSpec((RET_V, D_MODEL), const2),
            pl.BlockSpec((SSM_INNER, D_MODEL), const2),
            pl.BlockSpec((D_MODEL, D_MODEL), const2),
            pl.BlockSpec((1, D_MODEL), const2),
            pl.BlockSpec((D_MODEL, CHUNK), const2),
            pl.BlockSpec((D_MODEL, CHUNK), const2),
            pl.BlockSpec((1, CHUNK), const2),
        ],
        out_specs=[
            pl.BlockSpec((tm, D_MODEL), lambda i: (i, 0)),
            pl.BlockSpec((tm, D_MODEL), lambda i: (i, 0)),
            pl.BlockSpec((8, tm), lambda i: (0, i)),
            pl.BlockSpec((8, tm), lambda i: (0, i)),
        ],
        out_shape=[
            jax.ShapeDtypeStruct((t, D_MODEL), F32),
            jax.ShapeDtypeStruct((t, D_MODEL), BF16),
            jax.ShapeDtypeStruct((8, t), I32),
            jax.ShapeDtypeStruct((8, t), F32),
        ],
        compiler_params=_params(("parallel",)),
        name="merge_router",
    )(o_ret, o_ssd, proj, proj, h, wb0, wb1, wo, norm_w, rw_hi, rw_lo, router_b)


ROW_ALIGN = 8
PIECE = 16
LOCAL_ROWS = 1024
META_PIECES, META_EXPERT, META_LOCAL, META_SORTED = 0, 1, 2, 3
MAX_PIECES = CHUNK * TOP_K // PIECE + N_EXPERTS
assert MAX_PIECES <= CHUNK


def _rank_body(tiles, e_ref, slot_ref, meta_ref, cnt_ref, run_ref):
    @pl.when(pl.program_id(0) == 0)
    def _():
        run_ref[...] = jnp.zeros_like(run_ref)

    eidx = lax.broadcasted_iota(I32, (N_EXPERTS, CHUNK), 0)
    lane = lax.broadcasted_iota(I32, (N_EXPERTS, CHUNK), 1)
    ri = lax.broadcasted_iota(I32, (CHUNK, CHUNK), 0)
    ci = lax.broadcasted_iota(I32, (CHUNK, CHUNK), 1)
    upper = jnp.where(ri <= ci, 1.0, 0.0).astype(BF16)
    ones = jnp.ones((CHUNK, CHUNK), BF16)
    ei = lax.broadcasted_iota(I32, (N_EXPERTS, N_EXPERTS), 0)
    ej = lax.broadcasted_iota(I32, (N_EXPERTS, N_EXPERTS), 1)
    lower = jnp.where(ej < ei, 1.0, 0.0).astype(BF16)
    round_up = lambda v, m: jnp.floor((v + (m - 1)) * (1.0 / m)) * m

    run = run_ref[...]
    for s in range(tiles):
        e_tile = e_ref[:, s * CHUNK:(s + 1) * CHUNK]
        seen = jnp.zeros((N_EXPERTS, CHUNK), F32)
        hits, ranks = [], []
        for k in range(TOP_K):
            hit = eidx == e_tile[k:k + 1, :]
            oh = jnp.where(hit, 1.0, 0.0).astype(BF16)
            ranks.append(seen + jnp.dot(oh, upper, preferred_element_type=F32) - 1.0)
            seen = seen + jnp.dot(oh, ones, preferred_element_type=F32)
            hits.append(hit)
        count = seen
        lstart = jnp.dot(lower, round_up(count, PIECE).astype(BF16), preferred_element_type=F32)
        slots = []
        for k in range(TOP_K):
            slots.append(jnp.sum(jnp.where(hits[k], lstart + ranks[k], 0.0), axis=0, keepdims=True))
        slot_ref[:, s * CHUNK:(s + 1) * CHUNK] = jnp.concatenate(
            slots + [jnp.full((8 - TOP_K, CHUNK), -1.0, F32)], axis=0).astype(I32)
        pieces = round_up(count, PIECE) * (1.0 / PIECE)
        first = jnp.dot(lower, pieces.astype(BF16), preferred_element_type=F32)
        nth = lane.astype(F32) - first
        mine = jnp.logical_and(nth >= 0.0, nth < pieces)
        pick = lambda v: jnp.sum(jnp.where(mine, v, 0.0), axis=0, keepdims=True)
        meta_ref[s] = jnp.concatenate(
            [jnp.sum(pieces, axis=0, keepdims=True), pick(eidx.astype(F32)), pick(lstart + PIECE * nth),
             pick(run + PIECE * nth), jnp.zeros((4, CHUNK), F32)], axis=0).astype(I32)
        run = run + round_up(count, ROW_ALIGN)
    run_ref[...] = run
    cnt_ref[...] = run.astype(I32)


def _rank(top_e):
    t = top_e.shape[1]
    n_tiles = t // CHUNK
    tiles = _pick(n_tiles, (8, 4, 2, 1))
    return pl.pallas_call(
        functools.partial(_rank_body, tiles),
        grid=(n_tiles // tiles,),
        in_specs=[pl.BlockSpec((8, tiles * CHUNK), lambda i: (0, i))],
        out_specs=[
            pl.BlockSpec((8, tiles * CHUNK), lambda i: (0, i)),
            pl.BlockSpec((tiles, 8, CHUNK), lambda i: (i, 0, 0)),
            pl.BlockSpec((N_EXPERTS, CHUNK), lambda i: (0, 0)),
        ],
        out_shape=[
            jax.ShapeDtypeStruct((8, t), I32),
            jax.ShapeDtypeStruct((n_tiles, 8, CHUNK), I32),
            jax.ShapeDtypeStruct((N_EXPERTS, CHUNK), I32),
        ],
        scratch_shapes=[pltpu.VMEM((N_EXPERTS, CHUNK), F32)],
        compiler_params=_params(("arbitrary",)),
        name="rank",
    )(top_e)


def _pack_halves(v):
    bits = pltpu.bitcast(v, jnp.uint32)
    return (bits[:, :D_MODEL // 2] >> 16) | bits[:, D_MODEL // 2:]


def _unpack_halves(p):
    lo = pltpu.bitcast(p << 16, F32)
    hi = pltpu.bitcast(p & jnp.uint32(0xFFFF0000), F32)
    return jnp.concatenate([lo, hi], axis=1).astype(BF16)


def _wait_pieces(meta_ref, make_copy):
    def body(j, carry):
        make_copy(0, 0).wait()
        return carry

    lax.fori_loop(0, meta_ref[0, META_PIECES, 0], body, 0)


def _start_pieces(pstart_ref, meta_ref, make_copy):
    def body(p, carry):
        sorted_row = pstart_ref[meta_ref[0, META_EXPERT, p]] + meta_ref[0, META_SORTED, p]
        make_copy(pl.multiple_of(meta_ref[0, META_LOCAL, p], ROW_ALIGN), pl.multiple_of(sorted_row, ROW_ALIGN)).start()
        return carry

    lax.fori_loop(0, meta_ref[0, META_PIECES, 0], body, 0)


def _dispatch_body(pstart_ref, fill_ref, meta_ref, prev_meta_ref, slot_ref, un_ref, xs_ref, loc_ref, zero_ref, sems):
    step = pl.program_id(0)
    last = pl.num_programs(0) - 1
    buf = step % 2

    srow = lax.broadcasted_iota(I32, (LOCAL_ROWS, CHUNK), 0)
    onehot = jnp.zeros((LOCAL_ROWS, CHUNK), F32)
    for k in range(TOP_K):
        onehot = jnp.where(srow == slot_ref[k:k + 1, :], 1.0, onehot)
    loc_ref[buf] = _pack_halves(jnp.dot(onehot.astype(BF16), un_ref[...], preferred_element_type=F32))

    def copies_of(b):
        def make_copy(local_row, sorted_row):
            return pltpu.make_async_copy(loc_ref.at[b, pl.ds(local_row, PIECE), :],
                                         xs_ref.at[pl.ds(sorted_row, PIECE), :], sems.at[b])
        return make_copy

    @pl.when(step > 0)
    def _():
        _wait_pieces(prev_meta_ref, copies_of(1 - buf))

    _start_pieces(pstart_ref, meta_ref, copies_of(buf))

    @pl.when(step == last)
    def _():
        _wait_pieces(meta_ref, copies_of(buf))
        zero_ref[...] = jnp.zeros_like(zero_ref)

        def fill(act):
            def per_expert(e, carry):
                def per_piece(j, c):
                    row = pl.multiple_of(fill_ref[0, e] + j * ROW_ALIGN, ROW_ALIGN)
                    act(pltpu.make_async_copy(zero_ref, xs_ref.at[pl.ds(row, ROW_ALIGN), :], sems.at[0]))
                    return c
                return lax.fori_loop(0, fill_ref[1, e], per_piece, carry)
            lax.fori_loop(0, N_EXPERTS, per_expert, 0)

        fill(lambda cp: cp.start())
        fill(lambda cp: cp.wait())


def _dispatch(pstart, fill, meta, slot, un, n_rows):
    t = un.shape[0]
    return pl.pallas_call(
        _dispatch_body,
        grid_spec=pltpu.PrefetchScalarGridSpec(
            num_scalar_prefetch=2,
            grid=(t // CHUNK,),
            in_specs=[
                pl.BlockSpec((1, 8, CHUNK), lambda i, ps, fl: (i, 0, 0), memory_space=pltpu.SMEM),
                pl.BlockSpec((1, 8, CHUNK), lambda i, ps, fl: (jnp.maximum(i - 1, 0), 0, 0), memory_space=pltpu.SMEM),
                pl.BlockSpec((8, CHUNK), lambda i, ps, fl: (0, i)),
                pl.BlockSpec((CHUNK, D_MODEL), lambda i, ps, fl: (i, 0)),
            ],
            out_specs=pl.BlockSpec(memory_space=pl.ANY),
            scratch_shapes=[pltpu.VMEM((2, LOCAL_ROWS, D_MODEL // 2), jnp.uint32),
                            pltpu.VMEM((ROW_ALIGN, D_MODEL // 2), jnp.uint32), pltpu.SemaphoreType.DMA((2,))],
        ),
        out_shape=jax.ShapeDtypeStruct((n_rows, D_MODEL // 2), jnp.uint32),
        compiler_params=_params(("arbitrary",)),
        name="dispatch",
    )(pstart, fill, meta, meta, slot, un)


GU_SLAB = 256
GU_HALF = GU_SLAB // 2


def _regroup_body(w_ref, p_ref, o_ref):
    for s in range(2 * EXPERT_FF // GU_SLAB):
        cols = slice(s * GU_SLAB, (s + 1) * GU_SLAB)
        o_ref[0, :, cols] = jnp.dot(w_ref[0, :, cols].astype(BF16), p_ref[...], preferred_element_type=F32).astype(BF16)


def _regroup_gate_up(w_gate_up):
    perm = np.zeros((GU_SLAB, GU_SLAB), np.float32)
    perm[2 * np.arange(GU_HALF), np.arange(GU_HALF)] = 1.0
    perm[2 * np.arange(GU_HALF) + 1, GU_HALF + np.arange(GU_HALF)] = 1.0
    blk = (1, D_MODEL, 2 * EXPERT_FF)
    return pl.pallas_call(
        _regroup_body,
        grid=(N_EXPERTS,),
        in_specs=[pl.BlockSpec(blk, lambda e: (e, 0, 0)), pl.BlockSpec((GU_SLAB, GU_SLAB), lambda e: (0, 0))],
        out_specs=pl.BlockSpec(blk, lambda e: (e, 0, 0)),
        out_shape=jax.ShapeDtypeStruct((N_EXPERTS, D_MODEL, 2 * EXPERT_FF), BF16),
        compiler_params=_params(("parallel",)),
        name="regroup_gate_up",
    )(w_gate_up, jnp.asarray(perm, BF16))


def _regroup_bias(b_gate_up):
    b = b_gate_up.astype(F32).reshape(N_EXPERTS, 2 * EXPERT_FF // GU_SLAB, GU_HALF, 2)
    return jnp.transpose(b, (0, 1, 3, 2)).reshape(N_EXPERTS, 1, 2 * EXPERT_FF)


def _expert_body(be_ref, nu_ref, x_ref, wgu_ref, wd_ref, bgu_ref, bd_ref, y_ref):
    del be_ref

    @pl.when(pl.program_id(0) >= nu_ref[0])
    def _():
        y_ref[...] = jnp.zeros_like(y_ref)

    @pl.when(pl.program_id(0) < nu_ref[0])
    def _():
        x = _unpack_halves(x_ref[...])
        gu = jnp.dot(x, wgu_ref[0], preferred_element_type=F32) + bgu_ref[0]
        acts = []
        for s in range(2 * EXPERT_FF // GU_SLAB):
            gate = jnp.minimum(gu[:, s * GU_SLAB:s * GU_SLAB + GU_HALF], SWIGLU_LIMIT)
            up = jnp.clip(gu[:, s * GU_SLAB + GU_HALF:(s + 1) * GU_SLAB], -SWIGLU_LIMIT, SWIGLU_LIMIT)
            acts.append(((up + 1.0) * (gate * _sigmoid(SWIGLU_ALPHA * gate))).astype(BF16))
        act = jnp.concatenate(acts, axis=1)
        y = jnp.dot(act, wd_ref[0], preferred_element_type=F32) + bd_ref[0]
        y_ref[...] = _pack_halves(y.astype(BF16).astype(F32))


def _experts(block_e, n_used, xs, wgu, wd, bgu, bd):
    n_rows = xs.shape[0]
    n_blocks = n_rows // MOE_BLOCK
    wspec = lambda shape: pl.BlockSpec((1,) + shape, lambda i, be, nu: (be[i], 0, 0))
    return pl.pallas_call(
        _expert_body,
        grid_spec=pltpu.PrefetchScalarGridSpec(
            num_scalar_prefetch=2,
            grid=(n_blocks,),
            in_specs=[
                pl.BlockSpec((MOE_BLOCK, D_MODEL // 2), lambda i, be, nu: (jnp.minimum(i, nu[0] - 1), 0)),
                wspec((D_MODEL, 2 * EXPERT_FF)),
                wspec((EXPERT_FF, D_MODEL)),
                wspec((1, 2 * EXPERT_FF)),
                wspec((1, D_MODEL)),
            ],
            out_specs=pl.BlockSpec((MOE_BLOCK, D_MODEL // 2), lambda i, be, nu: (i, 0)),
        ),
        out_shape=jax.ShapeDtypeStruct((n_rows, D_MODEL // 2), jnp.uint32),
        compiler_params=_params(("arbitrary",)),
        name="experts",
    )(block_e, n_used, xs, wgu, wd, bgu, bd)


def _combine_body(pstart_ref, meta_ref, next_meta_ref, slot_ref, w_ref, h1_ref, nw_ref, y_hbm, o_ref, ybuf, sems):
    step = pl.program_id(0)
    buf = step % 2

    def copies_of(b):
        def make_copy(local_row, sorted_row):
            return pltpu.make_async_copy(y_hbm.at[pl.ds(sorted_row, PIECE), :],
                                         ybuf.at[b, pl.ds(local_row, PIECE), :], sems.at[b])
        return make_copy

    @pl.when(step == 0)
    def _():
        ybuf[...] = jnp.zeros_like(ybuf)
        _start_pieces(pstart_ref, meta_ref, copies_of(buf))

    @pl.when(step + 1 < pl.num_programs(0))
    def _():
        _start_pieces(pstart_ref, next_meta_ref, copies_of(1 - buf))

    _wait_pieces(meta_ref, copies_of(buf))

    srow = lax.broadcasted_iota(I32, (LOCAL_ROWS, CHUNK), 0)
    wmat = jnp.zeros((LOCAL_ROWS, CHUNK), F32)
    for k in range(TOP_K):
        wmat = jnp.where(srow == slot_ref[k:k + 1, :], w_ref[k:k + 1, :], wmat)
    wmat = jnp.transpose(wmat)
    w_hi = wmat.astype(BF16)
    w_lo = (wmat - w_hi.astype(F32)).astype(BF16)
    y = _unpack_halves(ybuf[buf])
    f = jnp.dot(w_hi, y, preferred_element_type=F32) + jnp.dot(w_lo, y, preferred_element_type=F32)
    h2 = h1_ref[...] + f
    o_ref[...] = h2 * lax.rsqrt(jnp.mean(h2 * h2, axis=-1, keepdims=True) + NORM_EPS) * nw_ref[...]


def _combine(pstart, meta, slot, top_w, h1, norm_w, y_sorted):
    t = h1.shape[0]
    n_tiles = t // CHUNK
    nxt = lambda i: jnp.minimum(i + 1, n_tiles - 1)
    return pl.pallas_call(
        _combine_body,
        grid_spec=pltpu.PrefetchScalarGridSpec(
            num_scalar_prefetch=1,
            grid=(n_tiles,),
            in_specs=[
                pl.BlockSpec((1, 8, CHUNK), lambda i, ps: (i, 0, 0), memory_space=pltpu.SMEM),
                pl.BlockSpec((1, 8, CHUNK), lambda i, ps: (nxt(i), 0, 0), memory_space=pltpu.SMEM),
                pl.BlockSpec((8, CHUNK), lambda i, ps: (0, i)),
                pl.BlockSpec((8, CHUNK), lambda i, ps: (0, i)),
                pl.BlockSpec((CHUNK, D_MODEL), lambda i, ps: (i, 0)),
                pl.BlockSpec((1, D_MODEL), lambda i, ps: (0, 0)),
                pl.BlockSpec(memory_space=pl.ANY),
            ],
            out_specs=pl.BlockSpec((CHUNK, D_MODEL), lambda i, ps: (i, 0)),
            scratch_shapes=[pltpu.VMEM((2, LOCAL_ROWS, D_MODEL // 2), jnp.uint32), pltpu.SemaphoreType.DMA((2,))],
        ),
        out_shape=jax.ShapeDtypeStruct((t, D_MODEL), F32),
        compiler_params=_params(("arbitrary",)),
        name="combine",
    )(pstart, meta, meta, slot, top_w, h1, norm_w, y_sorted)


def kernel(x, meta_tokens, norm_mix, w_in, conv_w, conv_b, dt_bias, a_log, d_skip, ssm_norm, w_branch, w_out, norm_ffn,
           router_w, router_b, w_gate_up, b_gate_up, w_down, b_down, norm_final):
    bsz, seq, _ = x.shape
    assert seq % CHUNK == 0 and norm_mix.shape[0] == 1
    nc = seq // CHUNK
    t = bsz * seq
    x2d = x.reshape(t, D_MODEL)
    meta_chunk = jnp.concatenate([jnp.zeros((META_PAD, D_MODEL), x.dtype), meta_tokens.astype(x.dtype)], axis=0)

    w_in0 = w_in[0]
    w_main = jnp.concatenate([w_in0[:, :OFF_DT], w_in0[:, OFF_DT + SSM_HEADS:]], axis=1).astype(BF16)
    w_dt = jnp.pad(w_in0[:, OFF_DT:OFF_DT + SSM_HEADS], ((0, 0), (0, DT_PAD - SSM_HEADS))).astype(BF16)
    wgu = _regroup_gate_up(w_gate_up[0])
    bgu = _regroup_bias(b_gate_up[0])
    wd = w_down[0].astype(BF16)
    bd = b_down[0][:, None, :].astype(F32)

    norm_mix_w = norm_mix[0].reshape(1, D_MODEL).astype(F32)
    ssd_params = (conv_w[0], conv_b[0], dt_bias[0], a_log[0], d_skip[0], ssm_norm[0])
    proj_m, dt_m = _in_proj(meta_chunk, norm_mix_w, w_main, w_dt)
    _, ret_state = _retention(proj_m, 1, 1, -META_PAD, jnp.zeros((RET_HEADS, RET_QK_DIM, RET_V_DIM), F32), True)
    _, ssd_state, conv_tail = _ssd(proj_m, dt_m, *ssd_params, 1, 1, META_PAD,
                                   (jnp.zeros((HEAD_PAIRS, SSM_STATE, CHUNK), F32), jnp.zeros((8, CONV_CH), F32)), True)

    proj, dt_raw = _in_proj(x2d, norm_mix_w, w_main, w_dt)
    o_ret, = _retention(proj, bsz, nc, N_META, ret_state, False)
    o_ssd, = _ssd(proj, dt_raw, *ssd_params, bsz, nc, 0, (ssd_state, conv_tail), False)
    h1, un, top_e, top_w = _merge(
        o_ret, o_ssd, proj, x2d, w_branch[0, 0].astype(BF16), w_branch[0, 1].astype(BF16), w_out[0].astype(BF16),
        norm_ffn[0].reshape(1, D_MODEL).astype(F32),
        jnp.pad(router_w[0].astype(F32), ((0, 0), (0, CHUNK - N_EXPERTS))),
        jnp.pad(router_b[0].astype(F32), (0, CHUNK - N_EXPERTS)).reshape(1, CHUNK))

    slot, meta, counts = _rank(top_e)
    counts = counts[:, 0]
    slack = MOE_BLOCK - 1 + PIECE - ROW_ALIGN
    padded = (counts + slack) // MOE_BLOCK * MOE_BLOCK
    pend = jnp.cumsum(padded)
    pstart = (pend - padded).astype(I32)
    n_pairs = t * TOP_K
    max_rows = n_pairs + bsz * nc * N_EXPERTS * (ROW_ALIGN - 1)
    n_blocks = (max_rows + N_EXPERTS * slack) // MOE_BLOCK
    blk_row = jnp.arange(n_blocks, dtype=I32) * MOE_BLOCK
    block_e = jnp.minimum(jnp.sum(pend[None, :] <= blk_row[:, None], axis=1), N_EXPERTS - 1).astype(I32)
    n_used = (pend[-1:] // MOE_BLOCK).astype(I32)
    fill = jnp.stack([pstart + counts, (padded - counts) // ROW_ALIGN]).astype(I32)

    xs = _dispatch(pstart, fill, meta, slot, un, n_blocks * MOE_BLOCK)
    ys = _experts(block_e, n_used, xs, wgu, wd, bgu, bd)
    out = _combine(pstart, meta, slot, top_w, h1, norm_final.reshape(1, D_MODEL).astype(F32), ys)
    return out.reshape(bsz, seq, D_MODEL)
```

```python
import functools

import numpy as np
import jax
import jax.numpy as jnp
from jax import lax
from jax.experimental import pallas as pl
from jax.experimental.pallas import tpu as pltpu

F32 = jnp.float32
BF16 = jnp.bfloat16
I32 = jnp.int32

D_MODEL = 1024
N_META = 16
CHUNK = 128
META_PAD = CHUNK - N_META
NORM_EPS = 1e-6
RET_HEADS = 4
RET_QK_DIM = 256
RET_V_DIM = 512
RET_QK = RET_HEADS * RET_QK_DIM
RET_V = RET_HEADS * RET_V_DIM
ROPE_BASE = 10000.0
SSM_INNER = 2048
SSM_HEAD_DIM = 64
SSM_HEADS = 32
SSM_GROUPS = 4
SSM_STATE = 128
CONV_WIDTH = 4
CONV_CH = SSM_INNER + 2 * SSM_GROUPS * SSM_STATE
HEAD_PAIRS = SSM_HEADS // 2
PAIRS_PER_GROUP = HEAD_PAIRS // SSM_GROUPS
OFF_DT = 2 * RET_QK + 2 * RET_V + SSM_INNER + CONV_CH
N_MAIN = OFF_DT + 2 * D_MODEL
DT_PAD = 128
N_EXPERTS = 32
TOP_K = 4
EXPERT_FF = 1024
SWIGLU_ALPHA = 1.702
SWIGLU_LIMIT = 7.0
MOE_BLOCK = 1024

VMEM_LIMIT = 56 * 1024 * 1024
HIGHEST = lax.Precision.HIGHEST


def _params(sem):
    return pltpu.CompilerParams(dimension_semantics=sem, vmem_limit_bytes=VMEM_LIMIT)


def _pick(n, candidates):
    for c in candidates:
        if n % c == 0:
            return c
    raise ValueError(f"no tile for {n} among {candidates}")


def _sigmoid(x):
    return 0.5 * jnp.tanh(0.5 * x) + 0.5


def _nt_dot(a, b, **kw):
    return lax.dot_general(a, b, (((1,), (1,)), ((), ())), preferred_element_type=F32, **kw)


def _inproj_body(x_ref, nw_ref, w_ref, wdt_ref, o_ref, dt_ref, xn_ref):
    @pl.when(pl.program_id(1) == 0)
    def _():
        x = x_ref[...]
        ms = jnp.mean(x * x, axis=-1, keepdims=True)
        xn = (x * lax.rsqrt(ms + NORM_EPS) * nw_ref[...]).astype(BF16)
        xn_ref[...] = xn
        dt_ref[...] = jnp.dot(xn, wdt_ref[...], preferred_element_type=F32)

    o_ref[...] = jnp.dot(xn_ref[...], w_ref[...], preferred_element_type=F32).astype(BF16)


def _in_proj(hp, norm_w, w_main, w_dt):
    t = hp.shape[0]
    tm = _pick(t, (2048, 1024, 512, 256, 128))
    tn = 1024
    return pl.pallas_call(
        _inproj_body,
        grid=(t // tm, N_MAIN // tn),
        in_specs=[
            pl.BlockSpec((tm, D_MODEL), lambda i, j: (i, 0)),
            pl.BlockSpec((1, D_MODEL), lambda i, j: (0, 0)),
            pl.BlockSpec((D_MODEL, tn), lambda i, j: (0, j)),
            pl.BlockSpec((D_MODEL, DT_PAD), lambda i, j: (0, 0)),
        ],
        out_specs=[
            pl.BlockSpec((tm, tn), lambda i, j: (i, j)),
            pl.BlockSpec((tm, DT_PAD), lambda i, j: (i, 0)),
        ],
        out_shape=[jax.ShapeDtypeStruct((t, N_MAIN), BF16), jax.ShapeDtypeStruct((t, DT_PAD), F32)],
        scratch_shapes=[pltpu.VMEM((tm, D_MODEL), BF16)],
        compiler_params=_params(("parallel", "arbitrary")),
        name="in_proj",
    )(hp, norm_w, w_main, w_dt)


def _ret_tables(n_rows, first_pos):
    half = RET_QK_DIM // 2
    inv_freq = ROPE_BASE ** (-np.arange(half, dtype=np.float64) / half)
    pos = np.arange(n_rows, dtype=np.float64) + first_pos
    ang = pos[:, None] * inv_freq[None, :]
    log_gamma = np.log(1.0 - 2.0 ** (-5.0 - np.arange(RET_HEADS, dtype=np.float64)))
    idx = np.arange(CHUNK, dtype=np.float64)
    dist = idx[:, None] - idx[None, :]
    intra = np.where(dist >= 0, np.exp(log_gamma[:, None, None] * np.maximum(dist, 0.0)[None]), 0.0)
    q_dec = np.exp(log_gamma[:, None] * (idx[None, :] + 1.0))
    k_dec = np.exp(log_gamma[:, None] * (CHUNK - 1.0 - idx[None, :]))
    q_dec = np.broadcast_to(q_dec[:, :, None], (RET_HEADS, CHUNK, RET_V_DIM))
    k_dec = np.broadcast_to(k_dec[:, :, None], (RET_HEADS, CHUNK, RET_QK_DIM))
    chunk_dec = tuple(float(v) for v in np.exp(log_gamma * CHUNK))
    as32 = lambda a: jnp.asarray(np.ascontiguousarray(a), F32)
    return as32(np.cos(ang)), as32(np.sin(ang)), as32(intra), as32(q_dec), as32(k_dec), chunk_dec


def _ret_body(chunk_dec, q_ref, k_ref, v_ref, g_ref, cos_ref, sin_ref, intra_ref, qd_ref, kd_ref, o_ref, st_ref):
    cos = cos_ref[...]
    sin = sin_ref[...]
    half = RET_QK_DIM // 2

    def rotary(t):
        t1, t2 = t[:, :half], t[:, half:]
        return jnp.concatenate([t1 * cos - t2 * sin, t2 * cos + t1 * sin], axis=1)

    for h in range(RET_HEADS):
        qk = slice(h * RET_QK_DIM, (h + 1) * RET_QK_DIM)
        vv = slice(h * RET_V_DIM, (h + 1) * RET_V_DIM)
        qr = rotary(q_ref[:, qk].astype(F32))
        kr = rotary(k_ref[:, qk].astype(F32)) * (RET_QK_DIM ** -0.5)
        qb = qr.astype(BF16)
        vh = v_ref[:, vv]
        st = st_ref[h]
        s = _nt_dot(qb, kr.astype(BF16)) * intra_ref[h]
        y = jnp.dot(s.astype(BF16), vh, preferred_element_type=F32)
        y = y + jnp.dot(qb, st.astype(BF16), preferred_element_type=F32) * qd_ref[h]
        kdt = jnp.transpose(kr * kd_ref[h]).astype(BF16)
        st_ref[h] = st * chunk_dec[h] + jnp.dot(kdt, vh, preferred_element_type=F32)
        o = y * lax.rsqrt(jnp.mean(y * y, axis=-1, keepdims=True) + NORM_EPS)
        g = g_ref[:, vv].astype(F32)
        o_ref[:, vv] = (g * _sigmoid(g) * o).astype(BF16)


RET_STATE = (RET_HEADS, RET_QK_DIM, RET_V_DIM)
N_RET_IN = 10


def _retention_io(proj, nc, first_pos, init_state):
    cos, sin, intra, q_dec, k_dec, chunk_dec = _ret_tables(nc * CHUNK, first_pos)
    row = lambda b, c: b * nc + c
    const3 = lambda b, c: (0, 0, 0)
    in_specs = [
        pl.BlockSpec((CHUNK, RET_QK), lambda b, c: (row(b, c), 0)),
        pl.BlockSpec((CHUNK, RET_QK), lambda b, c: (row(b, c), 1)),
        pl.BlockSpec((CHUNK, RET_V), lambda b, c: (row(b, c), 1)),
        pl.BlockSpec((CHUNK, RET_V), lambda b, c: (row(b, c), 2)),
        pl.BlockSpec((CHUNK, RET_QK_DIM // 2), lambda b, c: (c, 0)),
        pl.BlockSpec((CHUNK, RET_QK_DIM // 2), lambda b, c: (c, 0)),
        pl.BlockSpec((RET_HEADS, CHUNK, CHUNK), const3),
        pl.BlockSpec((RET_HEADS, CHUNK, RET_V_DIM), const3),
        pl.BlockSpec((RET_HEADS, CHUNK, RET_QK_DIM), const3),
        pl.BlockSpec(RET_STATE, const3),
    ]
    assert len(in_specs) == N_RET_IN
    return in_specs, (proj, proj, proj, proj, cos, sin, intra, q_dec, k_dec, init_state), chunk_dec


CONV_COLS = 256


def _ssd_body(pad_rows, z_ref, xs_ref, bc_ref, dt_ref, shift_ref, cw_ref, cb_ref, dtb_ref, alog_ref, dsk_ref, nw_ref,
              o_ref, st_ref, carry_ref, xc_ref, y_ref):
    c = pl.program_id(1)

    if pad_rows:
        rows = lax.broadcasted_iota(I32, (CHUNK, 1), 0)
        valid = jnp.logical_or(c > 0, rows >= pad_rows)
        keep = lambda v: jnp.where(valid, v, 0.0)
    else:
        keep = lambda v: v

    taps = CONV_WIDTH - 1
    for s in range(CONV_CH // CONV_COLS):
        cols = slice(s * CONV_COLS, (s + 1) * CONV_COLS)
        if (s + 1) * CONV_COLS <= SSM_INNER:
            raw = xs_ref[:, cols]
        else:
            raw = bc_ref[:, s * CONV_COLS - SSM_INNER:(s + 1) * CONV_COLS - SSM_INNER]
        shifted = jnp.dot(shift_ref[...], raw, preferred_element_type=F32)
        raw32 = raw.astype(F32)
        acc = cb_ref[:, cols] + cw_ref[taps:taps + 1, cols] * raw32
        window = jnp.concatenate([carry_ref[:, cols], jnp.zeros((8, CONV_COLS), F32)], axis=0)
        head = jnp.zeros((8, CONV_COLS), F32)
        for k in range(taps):
            acc = acc + cw_ref[k:k + 1, cols] * shifted[k * CHUNK:(k + 1) * CHUNK]
            head = head + cw_ref[k:k + 1, cols] * window[8 - taps + k:16 - taps + k]
        acc = jnp.concatenate([acc[:8] + head, acc[8:]], axis=0)
        carry_ref[:, cols] = raw32[CHUNK - 8:]
        xc_ref[:, cols] = keep(acc * _sigmoid(acc))

    dtv = dt_ref[...] + dtb_ref[...]
    dt = jnp.maximum(dtv, 0.0) + jnp.log1p(jnp.exp(-jnp.abs(dtv)))
    dt = keep(dt)
    a = dt * (-jnp.exp(alog_ref[...]))
    ri = lax.broadcasted_iota(I32, (CHUNK, CHUNK), 0)
    ci = lax.broadcasted_iota(I32, (CHUNK, CHUNK), 1)
    causal = ri >= ci
    acs = jnp.dot(causal.astype(F32), a, preferred_element_type=F32, precision=HIGHEST)
    last = acs[CHUNK - 1:CHUNK, :]
    acs_t = jnp.transpose(acs)
    dt_t = jnp.transpose(dt)
    w_t = jnp.transpose(jnp.exp(last - acs) * dt)
    e_last = jnp.exp(last)
    e_acs = jnp.exp(acs)
    lanes = lax.broadcasted_iota(I32, (1, CHUNK), 1)
    low = lanes < SSM_HEAD_DIM

    for g in range(SSM_GROUPS):
        bm = xc_ref[:, SSM_INNER + g * SSM_STATE:SSM_INNER + (g + 1) * SSM_STATE]
        cm = xc_ref[:, SSM_INNER + (SSM_GROUPS + g) * SSM_STATE:SSM_INNER + (SSM_GROUPS + g + 1) * SSM_STATE]
        cb = _nt_dot(cm.astype(BF16), bm.astype(BF16))
        bm_t = jnp.transpose(bm)
        for pp in range(PAIRS_PER_GROUP):
            m = g * PAIRS_PER_GROUP + pp
            x_pair = xc_ref[:, m * CHUNK:(m + 1) * CHUNK]
            st_pair = st_ref[m]
            y_pair = jnp.zeros((CHUNK, CHUNK), F32)
            upd = jnp.zeros((SSM_STATE, CHUNK), F32)
            for hh in range(2):
                h = 2 * m + hh
                lane_mask = low if hh == 0 else jnp.logical_not(low)
                col = jnp.broadcast_to(acs[:, h:h + 1], (CHUNK, CHUNK))
                seg = col - acs_t[h:h + 1, :]
                dec = jnp.exp(jnp.where(causal, seg, -1e30))
                mat = cb * dec * dt_t[h:h + 1, :]
                from_start = jnp.broadcast_to(e_acs[:, h:h + 1], (CHUNK, CHUNK))
                lhs = jnp.concatenate([mat, cm * from_start], axis=1).astype(BF16)
                xm = jnp.where(lane_mask, x_pair, 0.0).astype(BF16)
                sm = jnp.where(lane_mask, st_pair, 0.0).astype(BF16)
                y_pair = y_pair + jnp.dot(lhs, jnp.concatenate([xm, sm], axis=0), preferred_element_type=F32)
                upd = upd + jnp.dot((bm_t * w_t[h:h + 1, :]).astype(BF16), xm, preferred_element_type=F32)
            decay = jnp.where(low, jnp.broadcast_to(e_last[:, 2 * m:2 * m + 1], (1, CHUNK)),
                              jnp.broadcast_to(e_last[:, 2 * m + 1:2 * m + 2], (1, CHUNK)))
            st_ref[m] = st_pair * decay + upd
            y_ref[:, m * CHUNK:(m + 1) * CHUNK] = y_pair

    gsz = SSM_INNER // SSM_GROUPS
    for g in range(SSM_GROUPS):
        cols = slice(g * gsz, (g + 1) * gsz)
        z = z_ref[:, cols].astype(F32)
        y = (y_ref[:, cols] + dsk_ref[:, cols] * xc_ref[:, cols]) * (z * _sigmoid(z))
        y = y * lax.rsqrt(jnp.mean(y * y, axis=-1, keepdims=True) + NORM_EPS)
        o_ref[:, cols] = (y * nw_ref[:, cols]).astype(BF16)


SSD_STATE = (HEAD_PAIRS, SSM_STATE, CHUNK)
CONV_TAIL = (8, CONV_CH)
N_SSD_IN = 13


def _ssd_io(proj, dt_raw, conv_w, conv_b, dt_bias, a_log, d_skip, norm_w, nc, init_state, init_tail):
    row = lambda b, c: b * nc + c
    const2 = lambda b, c: (0, 0)
    const3 = lambda b, c: (0, 0, 0)
    pad = lambda v: jnp.pad(v.astype(F32), (0, DT_PAD - SSM_HEADS)).reshape(1, DT_PAD)
    taps = CONV_WIDTH - 1
    shift = np.zeros((taps * CHUNK, CHUNK), np.float32)
    for k in range(taps):
        for r in range(taps - k, CHUNK):
            shift[k * CHUNK + r, r - (taps - k)] = 1.0
    in_specs = [
        pl.BlockSpec((CHUNK, SSM_INNER), lambda b, c: (row(b, c), 3)),
        pl.BlockSpec((CHUNK, SSM_INNER), lambda b, c: (row(b, c), 4)),
        pl.BlockSpec((CHUNK, 1024), lambda b, c: (row(b, c), 10)),
        pl.BlockSpec((CHUNK, DT_PAD), lambda b, c: (row(b, c), 0)),
        pl.BlockSpec((taps * CHUNK, CHUNK), const2),
        pl.BlockSpec((CONV_WIDTH, CONV_CH), const2),
        pl.BlockSpec((1, CONV_CH), const2),
        pl.BlockSpec((1, DT_PAD), const2),
        pl.BlockSpec((1, DT_PAD), const2),
        pl.BlockSpec((1, SSM_INNER), const2),
        pl.BlockSpec((1, SSM_INNER), const2),
        pl.BlockSpec(SSD_STATE, const3),
        pl.BlockSpec(CONV_TAIL, const2),
    ]
    assert len(in_specs) == N_SSD_IN
    args = (proj, proj, proj, dt_raw, jnp.asarray(shift, BF16), conv_w.astype(F32), conv_b.astype(F32).reshape(1, CONV_CH),
            pad(dt_bias), pad(a_log), jnp.repeat(d_skip.astype(F32), SSM_HEAD_DIM).reshape(1, SSM_INNER),
            norm_w.astype(F32).reshape(1, SSM_INNER), init_state, init_tail)
    return in_specs, args


def _mixers_body(chunk_dec, pad_rows, emit_state, *refs):
    ret_in, refs = refs[:N_RET_IN], refs[N_RET_IN:]
    ssd_in, refs = refs[:N_SSD_IN], refs[N_SSD_IN:]
    n_out = 5 if emit_state else 2
    outs, scratch = refs[:n_out], refs[n_out:]
    states = scratch[:3]
    inits = (ret_in[-1], ssd_in[-2], ssd_in[-1])
    c = pl.program_id(1)

    @pl.when(c == 0)
    def _():
        for state, init in zip(states, inits):
            state[...] = init[...]

    _ret_body(chunk_dec, *ret_in[:-1], outs[0], scratch[0])
    _ssd_body(pad_rows, *ssd_in[:-2], outs[1], *scratch[1:])

    if emit_state:
        @pl.when(c == pl.num_programs(1) - 1)
        def _():
            for final, state in zip(outs[2:], states):
                final[...] = state[...]


def _mixers(proj, dt_raw, ssd_params, bsz, nc, first_pos, pad_rows, init, emit_state):
    t = proj.shape[0]
    ret_specs, ret_args, chunk_dec = _retention_io(proj, nc, first_pos, init[0])
    ssd_specs, ssd_args = _ssd_io(proj, dt_raw, *ssd_params, nc, init[1], init[2])
    row = lambda b, c: b * nc + c
    out_specs = [pl.BlockSpec((CHUNK, RET_V), lambda b, c: (row(b, c), 0)),
                 pl.BlockSpec((CHUNK, SSM_INNER), lambda b, c: (row(b, c), 0))]
    out_shape = [jax.ShapeDtypeStruct((t, RET_V), BF16), jax.ShapeDtypeStruct((t, SSM_INNER), BF16)]
    if emit_state:
        out_specs += [pl.BlockSpec(RET_STATE, lambda b, c: (0, 0, 0)), pl.BlockSpec(SSD_STATE, lambda b, c: (0, 0, 0)),
                      pl.BlockSpec(CONV_TAIL, lambda b, c: (0, 0))]
        out_shape += [jax.ShapeDtypeStruct(s, F32) for s in (RET_STATE, SSD_STATE, CONV_TAIL)]
    return pl.pallas_call(
        functools.partial(_mixers_body, chunk_dec, pad_rows, emit_state),
        grid=(bsz, nc),
        in_specs=ret_specs + ssd_specs,
        out_specs=out_specs,
        out_shape=out_shape,
        scratch_shapes=[
            pltpu.VMEM(RET_STATE, F32),
            pltpu.VMEM(SSD_STATE, F32),
            pltpu.VMEM(CONV_TAIL, F32),
            pltpu.VMEM((CHUNK, CONV_CH), F32),
            pltpu.VMEM((CHUNK, SSM_INNER), F32),
        ],
        compiler_params=_params(("parallel", "arbitrary")),
        name="mixers_meta" if emit_state else "mixers",
    )(*ret_args, *ssd_args)


def _merge_body(oret_ref, ossd_ref, g0_ref, g1_ref, h_ref, wb0_ref, wb1_ref, wo_ref, nw_ref, rwh_ref, rwl_ref,
                rb_ref, h1_ref, un_ref, e_ref, w_ref):
    y_ret = jnp.dot(oret_ref[...], wb0_ref[...], preferred_element_type=F32)
    y_ssd = jnp.dot(ossd_ref[...], wb1_ref[...], preferred_element_type=F32)
    merged = (_sigmoid(g0_ref[...].astype(F32)) * y_ret + _sigmoid(g1_ref[...].astype(F32)) * y_ssd)
    h1 = h_ref[...] + jnp.dot(merged.astype(BF16), wo_ref[...], preferred_element_type=F32)
    h1_ref[...] = h1
    un = h1 * lax.rsqrt(jnp.mean(h1 * h1, axis=-1, keepdims=True) + NORM_EPS) * nw_ref[...]
    un_hi = un.astype(BF16)
    un_ref[...] = un_hi

    un_lo = (un - un_hi.astype(F32)).astype(BF16)
    logits = (jnp.dot(un_hi, rwh_ref[...], preferred_element_type=F32)
              + jnp.dot(un_lo, rwh_ref[...], preferred_element_type=F32)
              + jnp.dot(un_hi, rwl_ref[...], preferred_element_type=F32)) + rb_ref[...]
    logits = jnp.transpose(logits)[:N_EXPERTS, :]
    tm = logits.shape[1]
    eidx = lax.broadcasted_iota(I32, (N_EXPERTS, tm), 0)
    vals, ids = [], []
    for _ in range(TOP_K):
        best = jnp.max(logits, axis=0, keepdims=True)
        arg = jnp.min(jnp.where(logits == best, eidx, N_EXPERTS), axis=0, keepdims=True)
        vals.append(best)
        ids.append(arg)
        logits = jnp.where(eidx == arg, -jnp.inf, logits)
    ex = [jnp.exp(v - vals[0]) for v in vals]
    denom = ex[0] + ex[1] + ex[2] + ex[3]
    e_ref[...] = jnp.concatenate(ids + [jnp.zeros((8 - TOP_K, tm), I32)], axis=0)
    w_ref[...] = jnp.concatenate([x / denom for x in ex] + [jnp.zeros((8 - TOP_K, tm), F32)], axis=0)


def _merge(o_ret, o_ssd, proj, h, wb0, wb1, wo, norm_w, router_w, router_b):
    t = h.shape[0]
    tm = _pick(t, (512, 256, 128))
    gate_blk = OFF_DT // D_MODEL
    assert gate_blk * D_MODEL == OFF_DT
    const2 = lambda i: (0, 0)
    rw_hi = router_w.astype(BF16)
    rw_lo = (router_w - rw_hi.astype(F32)).astype(BF16)
    return pl.pallas_call(
        _merge_body,
        grid=(t // tm,),
        in_specs=[
            pl.BlockSpec((tm, RET_V), lambda i: (i, 0)),
            pl.BlockSpec((tm, SSM_INNER), lambda i: (i, 0)),
            pl.BlockSpec((tm, D_MODEL), lambda i: (i, gate_blk)),
            pl.BlockSpec((tm, D_MODEL), lambda i: (i, gate_blk + 1)),
            pl.BlockSpec((tm, D_MODEL), lambda i: (i, 0)),
            pl.BlockSpec((RET_V, D_MODEL), const2),
            pl.BlockSpec((SSM_INNER, D_MODEL), const2),
            pl.BlockSpec((D_MODEL, D_MODEL), const2),
            pl.BlockSpec((1, D_MODEL), const2),
            pl.BlockSpec((D_MODEL, CHUNK), const2),
            pl.BlockSpec((D_MODEL, CHUNK), const2),
            pl.BlockSpec((1, CHUNK), const2),
        ],
        out_specs=[
            pl.BlockSpec((tm, D_MODEL), lambda i: (i, 0)),
            pl.BlockSpec((tm, D_MODEL), lambda i: (i, 0)),
            pl.BlockSpec((8, tm), lambda i: (0, i)),
            pl.BlockSpec((8, tm), lambda i: (0, i)),
        ],
        out_shape=[
            jax.ShapeDtypeStruct((t, D_MODEL), F32),
            jax.ShapeDtypeStruct((t, D_MODEL), BF16),
            jax.ShapeDtypeStruct((8, t), I32),
            jax.ShapeDtypeStruct((8, t), F32),
        ],
        compiler_params=_params(("parallel",)),
        name="merge_router",
    )(o_ret, o_ssd, proj, proj, h, wb0, wb1, wo, norm_w, rw_hi, rw_lo, router_b)


ROW_ALIGN = 8
PIECE = 16
LOCAL_ROWS = 1024
META_PIECES, META_EXPERT, META_LOCAL, META_SORTED = 0, 1, 2, 3
MAX_PIECES = CHUNK * TOP_K // PIECE + N_EXPERTS
assert MAX_PIECES <= CHUNK


def _rank_body(tiles, e_ref, slot_ref, meta_ref, cnt_ref, run_ref):
    @pl.when(pl.program_id(0) == 0)
    def _():
        run_ref[...] = jnp.zeros_like(run_ref)

    eidx = lax.broadcasted_iota(I32, (N_EXPERTS, CHUNK), 0)
    lane = lax.broadcasted_iota(I32, (N_EXPERTS, CHUNK), 1)
    ri = lax.broadcasted_iota(I32, (CHUNK, CHUNK), 0)
    ci = lax.broadcasted_iota(I32, (CHUNK, CHUNK), 1)
    upper = jnp.where(ri <= ci, 1.0, 0.0).astype(BF16)
    ones = jnp.ones((CHUNK, CHUNK), BF16)
    ei = lax.broadcasted_iota(I32, (N_EXPERTS, N_EXPERTS), 0)
    ej = lax.broadcasted_iota(I32, (N_EXPERTS, N_EXPERTS), 1)
    lower = jnp.where(ej < ei, 1.0, 0.0).astype(BF16)
    round_up = lambda v, m: jnp.floor((v + (m - 1)) * (1.0 / m)) * m

    run = run_ref[...]
    for s in range(tiles):
        e_tile = e_ref[:, s * CHUNK:(s + 1) * CHUNK]
        seen = jnp.zeros((N_EXPERTS, CHUNK), F32)
        hits, ranks = [], []
        for k in range(TOP_K):
            hit = eidx == e_tile[k:k + 1, :]
            oh = jnp.where(hit, 1.0, 0.0).astype(BF16)
            ranks.append(seen + jnp.dot(oh, upper, preferred_element_type=F32) - 1.0)
            seen = seen + jnp.dot(oh, ones, preferred_element_type=F32)
            hits.append(hit)
        count = seen
        lstart = jnp.dot(lower, round_up(count, PIECE).astype(BF16), preferred_element_type=F32)
        slots = []
        for k in range(TOP_K):
            slots.append(jnp.sum(jnp.where(hits[k], lstart + ranks[k], 0.0), axis=0, keepdims=True))
        slot_ref[:, s * CHUNK:(s + 1) * CHUNK] = jnp.concatenate(
            slots + [jnp.full((8 - TOP_K, CHUNK), -1.0, F32)], axis=0).astype(I32)
        pieces = round_up(count, PIECE) * (1.0 / PIECE)
        first = jnp.dot(lower, pieces.astype(BF16), preferred_element_type=F32)
        nth = lane.astype(F32) - first
        mine = jnp.logical_and(nth >= 0.0, nth < pieces)
        pick = lambda v: jnp.sum(jnp.where(mine, v, 0.0), axis=0, keepdims=True)
        meta_ref[s] = jnp.concatenate(
            [jnp.sum(pieces, axis=0, keepdims=True), pick(eidx.astype(F32)), pick(lstart + PIECE * nth),
             pick(run + PIECE * nth), jnp.zeros((4, CHUNK), F32)], axis=0).astype(I32)
        run = run + round_up(count, ROW_ALIGN)
    run_ref[...] = run
    cnt_ref[...] = run.astype(I32)


def _rank(top_e):
    t = top_e.shape[1]
    n_tiles = t // CHUNK
    tiles = _pick(n_tiles, (8, 4, 2, 1))
    return pl.pallas_call(
        functools.partial(_rank_body, tiles),
        grid=(n_tiles // tiles,),
        in_specs=[pl.BlockSpec((8, tiles * CHUNK), lambda i: (0, i))],
        out_specs=[
            pl.BlockSpec((8, tiles * CHUNK), lambda i: (0, i)),
            pl.BlockSpec((tiles, 8, CHUNK), lambda i: (i, 0, 0)),
            pl.BlockSpec((N_EXPERTS, CHUNK), lambda i: (0, 0)),
        ],
        out_shape=[
            jax.ShapeDtypeStruct((8, t), I32),
            jax.ShapeDtypeStruct((n_tiles, 8, CHUNK), I32),
            jax.ShapeDtypeStruct((N_EXPERTS, CHUNK), I32),
        ],
        scratch_shapes=[pltpu.VMEM((N_EXPERTS, CHUNK), F32)],
        compiler_params=_params(("arbitrary",)),
        name="rank",
    )(top_e)


def _pack_halves(v):
    bits = pltpu.bitcast(v, jnp.uint32)
    return (bits[:, :D_MODEL // 2] >> 16) | bits[:, D_MODEL // 2:]


def _unpack_halves(p):
    lo = pltpu.bitcast(p << 16, F32)
    hi = pltpu.bitcast(p & jnp.uint32(0xFFFF0000), F32)
    return jnp.concatenate([lo, hi], axis=1).astype(BF16)


def _wait_pieces(meta_ref, make_copy):
    def body(j, carry):
        make_copy(0, 0).wait()
        return carry

    lax.fori_loop(0, meta_ref[0, META_PIECES, 0], body, 0)


def _start_pieces(pstart_ref, meta_ref, make_copy):
    def body(p, carry):
        sorted_row = pstart_ref[meta_ref[0, META_EXPERT, p]] + meta_ref[0, META_SORTED, p]
        make_copy(pl.multiple_of(meta_ref[0, META_LOCAL, p], ROW_ALIGN), pl.multiple_of(sorted_row, ROW_ALIGN)).start()
        return carry

    lax.fori_loop(0, meta_ref[0, META_PIECES, 0], body, 0)


def _dispatch_body(pstart_ref, fill_ref, meta_ref, prev_meta_ref, slot_ref, un_ref, xs_ref, loc_ref, zero_ref, sems):
    step = pl.program_id(0)
    last = pl.num_programs(0) - 1
    buf = step % 2

    srow = lax.broadcasted_iota(I32, (LOCAL_ROWS, CHUNK), 0)
    onehot = jnp.zeros((LOCAL_ROWS, CHUNK), F32)
    for k in range(TOP_K):
        onehot = jnp.where(srow == slot_ref[k:k + 1, :], 1.0, onehot)
    loc_ref[buf] = _pack_halves(jnp.dot(onehot.astype(BF16), un_ref[...], preferred_element_type=F32))

    def copies_of(b):
        def make_copy(local_row, sorted_row):
            return pltpu.make_async_copy(loc_ref.at[b, pl.ds(local_row, PIECE), :],
                                         xs_ref.at[pl.ds(sorted_row, PIECE), :], sems.at[b])
        return make_copy

    @pl.when(step > 0)
    def _():
        _wait_pieces(prev_meta_ref, copies_of(1 - buf))

    _start_pieces(pstart_ref, meta_ref, copies_of(buf))

    @pl.when(step == last)
    def _():
        _wait_pieces(meta_ref, copies_of(buf))
        zero_ref[...] = jnp.zeros_like(zero_ref)

        def fill(act):
            def per_expert(e, carry):
                def per_piece(j, c):
                    row = pl.multiple_of(fill_ref[0, e] + j * ROW_ALIGN, ROW_ALIGN)
                    act(pltpu.make_async_copy(zero_ref, xs_ref.at[pl.ds(row, ROW_ALIGN), :], sems.at[0]))
                    return c
                return lax.fori_loop(0, fill_ref[1, e], per_piece, carry)
            lax.fori_loop(0, N_EXPERTS, per_expert, 0)

        fill(lambda cp: cp.start())
        fill(lambda cp: cp.wait())


def _dispatch(pstart, fill, meta, slot, un, n_rows):
    t = un.shape[0]
    return pl.pallas_call(
        _dispatch_body,
        grid_spec=pltpu.PrefetchScalarGridSpec(
            num_scalar_prefetch=2,
            grid=(t // CHUNK,),
            in_specs=[
                pl.BlockSpec((1, 8, CHUNK), lambda i, ps, fl: (i, 0, 0), memory_space=pltpu.SMEM),
                pl.BlockSpec((1, 8, CHUNK), lambda i, ps, fl: (jnp.maximum(i - 1, 0), 0, 0), memory_space=pltpu.SMEM),
                pl.BlockSpec((8, CHUNK), lambda i, ps, fl: (0, i)),
                pl.BlockSpec((CHUNK, D_MODEL), lambda i, ps, fl: (i, 0)),
            ],
            out_specs=pl.BlockSpec(memory_space=pl.ANY),
            scratch_shapes=[pltpu.VMEM((2, LOCAL_ROWS, D_MODEL // 2), jnp.uint32),
                            pltpu.VMEM((ROW_ALIGN, D_MODEL // 2), jnp.uint32), pltpu.SemaphoreType.DMA((2,))],
        ),
        out_shape=jax.ShapeDtypeStruct((n_rows, D_MODEL // 2), jnp.uint32),
        compiler_params=_params(("arbitrary",)),
        name="dispatch",
    )(pstart, fill, meta, meta, slot, un)


GU_SLAB = 256
GU_HALF = GU_SLAB // 2


def _regroup_body(w_ref, p_ref, o_ref):
    for s in range(2 * EXPERT_FF // GU_SLAB):
        cols = slice(s * GU_SLAB, (s + 1) * GU_SLAB)
        o_ref[0, :, cols] = jnp.dot(w_ref[0, :, cols].astype(BF16), p_ref[...], preferred_element_type=F32).astype(BF16)


def _regroup_gate_up(w_gate_up):
    perm = np.zeros((GU_SLAB, GU_SLAB), np.float32)
    perm[2 * np.arange(GU_HALF), np.arange(GU_HALF)] = 1.0
    perm[2 * np.arange(GU_HALF) + 1, GU_HALF + np.arange(GU_HALF)] = 1.0
    blk = (1, D_MODEL, 2 * EXPERT_FF)
    return pl.pallas_call(
        _regroup_body,
        grid=(N_EXPERTS,),
        in_specs=[pl.BlockSpec(blk, lambda e: (e, 0, 0)), pl.BlockSpec((GU_SLAB, GU_SLAB), lambda e: (0, 0))],
        out_specs=pl.BlockSpec(blk, lambda e: (e, 0, 0)),
        out_shape=jax.ShapeDtypeStruct((N_EXPERTS, D_MODEL, 2 * EXPERT_FF), BF16),
        compiler_params=_params(("parallel",)),
        name="regroup_gate_up",
    )(w_gate_up, jnp.asarray(perm, BF16))


def _regroup_bias(b_gate_up):
    b = b_gate_up.astype(F32).reshape(N_EXPERTS, 2 * EXPERT_FF // GU_SLAB, GU_HALF, 2)
    return jnp.transpose(b, (0, 1, 3, 2)).reshape(N_EXPERTS, 1, 2 * EXPERT_FF)


def _expert_body(be_ref, nu_ref, x_ref, wgu_ref, wd_ref, bgu_ref, bd_ref, y_ref):
    del be_ref

    @pl.when(pl.program_id(0) >= nu_ref[0])
    def _():
        y_ref[...] = jnp.zeros_like(y_ref)

    @pl.when(pl.program_id(0) < nu_ref[0])
    def _():
        x = _unpack_halves(x_ref[...])
        gu = jnp.dot(x, wgu_ref[0], preferred_element_type=F32) + bgu_ref[0]
        acts = []
        for s in range(2 * EXPERT_FF // GU_SLAB):
            gate = jnp.minimum(gu[:, s * GU_SLAB:s * GU_SLAB + GU_HALF], SWIGLU_LIMIT)
            up = jnp.clip(gu[:, s * GU_SLAB + GU_HALF:(s + 1) * GU_SLAB], -SWIGLU_LIMIT, SWIGLU_LIMIT)
            acts.append(((up + 1.0) * (gate * _sigmoid(SWIGLU_ALPHA * gate))).astype(BF16))
        act = jnp.concatenate(acts, axis=1)
        y = jnp.dot(act, wd_ref[0], preferred_element_type=F32) + bd_ref[0]
        y_ref[...] = _pack_halves(y.astype(BF16).astype(F32))


def _experts(block_e, n_used, xs, wgu, wd, bgu, bd):
    n_rows = xs.shape[0]
    n_blocks = n_rows // MOE_BLOCK
    wspec = lambda shape: pl.BlockSpec((1,) + shape, lambda i, be, nu: (be[i], 0, 0))
    return pl.pallas_call(
        _expert_body,
        grid_spec=pltpu.PrefetchScalarGridSpec(
            num_scalar_prefetch=2,
            grid=(n_blocks,),
            in_specs=[
                pl.BlockSpec((MOE_BLOCK, D_MODEL // 2), lambda i, be, nu: (jnp.minimum(i, nu[0] - 1), 0)),
                wspec((D_MODEL, 2 * EXPERT_FF)),
                wspec((EXPERT_FF, D_MODEL)),
                wspec((1, 2 * EXPERT_FF)),
                wspec((1, D_MODEL)),
            ],
            out_specs=pl.BlockSpec((MOE_BLOCK, D_MODEL // 2), lambda i, be, nu: (i, 0)),
        ),
        out_shape=jax.ShapeDtypeStruct((n_rows, D_MODEL // 2), jnp.uint32),
        compiler_params=_params(("arbitrary",)),
        name="experts",
    )(block_e, n_used, xs, wgu, wd, bgu, bd)


def _combine_body(pstart_ref, meta_ref, next_meta_ref, slot_ref, w_ref, h1_ref, nw_ref, y_hbm, o_ref, ybuf, sems):
    step = pl.program_id(0)
    buf = step % 2

    def copies_of(b):
        def make_copy(local_row, sorted_row):
            return pltpu.make_async_copy(y_hbm.at[pl.ds(sorted_row, PIECE), :],
                                         ybuf.at[b, pl.ds(local_row, PIECE), :], sems.at[b])
        return make_copy

    @pl.when(step == 0)
    def _():
        ybuf[...] = jnp.zeros_like(ybuf)
        _start_pieces(pstart_ref, meta_ref, copies_of(buf))

    @pl.when(step + 1 < pl.num_programs(0))
    def _():
        _start_pieces(pstart_ref, next_meta_ref, copies_of(1 - buf))

    _wait_pieces(meta_ref, copies_of(buf))

    srow = lax.broadcasted_iota(I32, (LOCAL_ROWS, CHUNK), 0)
    wmat = jnp.zeros((LOCAL_ROWS, CHUNK), F32)
    for k in range(TOP_K):
        wmat = jnp.where(srow == slot_ref[k:k + 1, :], w_ref[k:k + 1, :], wmat)
    wmat = jnp.transpose(wmat)
    w_hi = wmat.astype(BF16)
    w_lo = (wmat - w_hi.astype(F32)).astype(BF16)
    y = _unpack_halves(ybuf[buf])
    f = jnp.dot(w_hi, y, preferred_element_type=F32) + jnp.dot(w_lo, y, preferred_element_type=F32)
    h2 = h1_ref[...] + f
    o_ref[...] = h2 * lax.rsqrt(jnp.mean(h2 * h2, axis=-1, keepdims=True) + NORM_EPS) * nw_ref[...]


def _combine(pstart, meta, slot, top_w, h1, norm_w, y_sorted):
    t = h1.shape[0]
    n_tiles = t // CHUNK
    nxt = lambda i: jnp.minimum(i + 1, n_tiles - 1)
    return pl.pallas_call(
        _combine_body,
        grid_spec=pltpu.PrefetchScalarGridSpec(
            num_scalar_prefetch=1,
            grid=(n_tiles,),
            in_specs=[
                pl.BlockSpec((1, 8, CHUNK), lambda i, ps: (i, 0, 0), memory_space=pltpu.SMEM),
                pl.BlockSpec((1, 8, CHUNK), lambda i, ps: (nxt(i), 0, 0), memory_space=pltpu.SMEM),
                pl.BlockSpec((8, CHUNK), lambda i, ps: (0, i)),
                pl.BlockSpec((8, CHUNK), lambda i, ps: (0, i)),
                pl.BlockSpec((CHUNK, D_MODEL), lambda i, ps: (i, 0)),
                pl.BlockSpec((1, D_MODEL), lambda i, ps: (0, 0)),
                pl.BlockSpec(memory_space=pl.ANY),
            ],
            out_specs=pl.BlockSpec((CHUNK, D_MODEL), lambda i, ps: (i, 0)),
            scratch_shapes=[pltpu.VMEM((2, LOCAL_ROWS, D_MODEL // 2), jnp.uint32), pltpu.SemaphoreType.DMA((2,))],
        ),
        out_shape=jax.ShapeDtypeStruct((t, D_MODEL), F32),
        compiler_params=_params(("arbitrary",)),
        name="combine",
    )(pstart, meta, meta, slot, top_w, h1, norm_w, y_sorted)


def kernel(x, meta_tokens, norm_mix, w_in, conv_w, conv_b, dt_bias, a_log, d_skip, ssm_norm, w_branch, w_out, norm_ffn,
           router_w, router_b, w_gate_up, b_gate_up, w_down, b_down, norm_final):
    bsz, seq, _ = x.shape
    assert seq % CHUNK == 0 and norm_mix.shape[0] == 1
    nc = seq // CHUNK
    t = bsz * seq
    x2d = x.reshape(t, D_MODEL)
    meta_chunk = jnp.concatenate([jnp.zeros((META_PAD, D_MODEL), x.dtype), meta_tokens.astype(x.dtype)], axis=0)

    w_in0 = w_in[0]
    w_main = jnp.concatenate([w_in0[:, :OFF_DT], w_in0[:, OFF_DT + SSM_HEADS:]], axis=1).astype(BF16)
    w_dt = jnp.pad(w_in0[:, OFF_DT:OFF_DT + SSM_HEADS], ((0, 0), (0, DT_PAD - SSM_HEADS))).astype(BF16)
    wgu = _regroup_gate_up(w_gate_up[0])
    bgu = _regroup_bias(b_gate_up[0])
    wd = w_down[0].astype(BF16)
    bd = b_down[0][:, None, :].astype(F32)

    norm_mix_w = norm_mix[0].reshape(1, D_MODEL).astype(F32)
    ssd_params = (conv_w[0], conv_b[0], dt_bias[0], a_log[0], d_skip[0], ssm_norm[0])
    proj_m, dt_m = _in_proj(meta_chunk, norm_mix_w, w_main, w_dt)
    zero_states = tuple(jnp.zeros(s, F32) for s in (RET_STATE, SSD_STATE, CONV_TAIL))
    meta_states = _mixers(proj_m, dt_m, ssd_params, 1, 1, -META_PAD, META_PAD, zero_states, True)[2:]

    proj, dt_raw = _in_proj(x2d, norm_mix_w, w_main, w_dt)
    o_ret, o_ssd = _mixers(proj, dt_raw, ssd_params, bsz, nc, N_META, 0, meta_states, False)
    h1, un, top_e, top_w = _merge(
        o_ret, o_ssd, proj, x2d, w_branch[0, 0].astype(BF16), w_branch[0, 1].astype(BF16), w_out[0].astype(BF16),
        norm_ffn[0].reshape(1, D_MODEL).astype(F32),
        jnp.pad(router_w[0].astype(F32), ((0, 0), (0, CHUNK - N_EXPERTS))),
        jnp.pad(router_b[0].astype(F32), (0, CHUNK - N_EXPERTS)).reshape(1, CHUNK))

    slot, meta, counts = _rank(top_e)
    counts = counts[:, 0]
    slack = MOE_BLOCK - 1 + PIECE - ROW_ALIGN
    padded = (counts + slack) // MOE_BLOCK * MOE_BLOCK
    pend = jnp.cumsum(padded)
    pstart = (pend - padded).astype(I32)
    n_pairs = t * TOP_K
    max_rows = n_pairs + bsz * nc * N_EXPERTS * (ROW_ALIGN - 1)
    n_blocks = (max_rows + N_EXPERTS * slack) // MOE_BLOCK
    blk_row = jnp.arange(n_blocks, dtype=I32) * MOE_BLOCK
    block_e = jnp.minimum(jnp.sum(pend[None, :] <= blk_row[:, None], axis=1), N_EXPERTS - 1).astype(I32)
    n_used = (pend[-1:] // MOE_BLOCK).astype(I32)
    fill = jnp.stack([pstart + counts, (padded - counts) // ROW_ALIGN]).astype(I32)

    xs = _dispatch(pstart, fill, meta, slot, un, n_blocks * MOE_BLOCK)
    ys = _experts(block_e, n_used, xs, wgu, wd, bgu, bd)
    out = _combine(pstart, meta, slot, top_w, h1, norm_final.reshape(1, D_MODEL).astype(F32), ys)
    return out.reshape(bsz, seq, D_MODEL)
```

```python
import functools

import numpy as np
import jax
import jax.numpy as jnp
from jax import lax
from jax.experimental import pallas as pl
from jax.experimental.pallas import tpu as pltpu

F32 = jnp.float32
BF16 = jnp.bfloat16
I32 = jnp.int32

D_MODEL = 1024
N_META = 16
CHUNK = 128
META_PAD = CHUNK - N_META
NORM_EPS = 1e-6
RET_HEADS = 4
RET_QK_DIM = 256
RET_V_DIM = 512
RET_QK = RET_HEADS * RET_QK_DIM
RET_V = RET_HEADS * RET_V_DIM
ROPE_BASE = 10000.0
SSM_INNER = 2048
SSM_HEAD_DIM = 64
SSM_HEADS = 32
SSM_GROUPS = 4
SSM_STATE = 128
CONV_WIDTH = 4
CONV_CH = SSM_INNER + 2 * SSM_GROUPS * SSM_STATE
HEAD_PAIRS = SSM_HEADS // 2
PAIRS_PER_GROUP = HEAD_PAIRS // SSM_GROUPS
COL_Q = 0
COL_K = COL_Q + RET_QK
COL_V = COL_K + RET_QK
COL_G = COL_V + RET_V
COL_Z = COL_G + RET_V
COL_X = COL_Z + SSM_INNER
COL_BC = COL_X + SSM_INNER
BC_WIDTH = 2 * SSM_GROUPS * SSM_STATE
OFF_DT = COL_BC + BC_WIDTH
N_MAIN = OFF_DT + 2 * D_MODEL
DT_PAD = 128


def _col_block(offset, width):
    assert offset % width == 0
    return offset // width
N_EXPERTS = 32
TOP_K = 4
EXPERT_FF = 1024
SWIGLU_ALPHA = 1.702
SWIGLU_LIMIT = 7.0
MOE_BLOCK = 1024

VMEM_LIMIT = 56 * 1024 * 1024
HIGHEST = lax.Precision.HIGHEST


def _params(sem):
    return pltpu.CompilerParams(dimension_semantics=sem, vmem_limit_bytes=VMEM_LIMIT)


def _pick(n, candidates):
    for c in candidates:
        if n % c == 0:
            return c
    raise ValueError(f"no tile for {n} among {candidates}")


def _sigmoid(x):
    return 0.5 * jnp.tanh(0.5 * x) + 0.5


def _nt_dot(a, b, **kw):
    return lax.dot_general(a, b, (((1,), (1,)), ((), ())), preferred_element_type=F32, **kw)


def _inproj_body(x_ref, nw_ref, w_ref, wdt_ref, o_ref, dt_ref, xn_ref):
    @pl.when(pl.program_id(1) == 0)
    def _():
        x = x_ref[...]
        ms = jnp.mean(x * x, axis=-1, keepdims=True)
        xn = (x * lax.rsqrt(ms + NORM_EPS) * nw_ref[...]).astype(BF16)
        xn_ref[...] = xn
        dt_ref[...] = jnp.dot(xn, wdt_ref[...], preferred_element_type=F32)

    o_ref[...] = jnp.dot(xn_ref[...], w_ref[...], preferred_element_type=F32).astype(BF16)


def _in_proj(hp, norm_w, w_main, w_dt):
    t = hp.shape[0]
    tm = _pick(t, (2048, 1024, 512, 256, 128))
    tn = 1024
    return pl.pallas_call(
        _inproj_body,
        grid=(t // tm, N_MAIN // tn),
        in_specs=[
            pl.BlockSpec((tm, D_MODEL), lambda i, j: (i, 0)),
            pl.BlockSpec((1, D_MODEL), lambda i, j: (0, 0)),
            pl.BlockSpec((D_MODEL, tn), lambda i, j: (0, j)),
            pl.BlockSpec((D_MODEL, DT_PAD), lambda i, j: (0, 0)),
        ],
        out_specs=[
            pl.BlockSpec((tm, tn), lambda i, j: (i, j)),
            pl.BlockSpec((tm, DT_PAD), lambda i, j: (i, 0)),
        ],
        out_shape=[jax.ShapeDtypeStruct((t, N_MAIN), BF16), jax.ShapeDtypeStruct((t, DT_PAD), F32)],
        scratch_shapes=[pltpu.VMEM((tm, D_MODEL), BF16)],
        compiler_params=_params(("parallel", "arbitrary")),
        name="in_proj",
    )(hp, norm_w, w_main, w_dt)


def _ret_tables(n_rows, first_pos):
    half = RET_QK_DIM // 2
    inv_freq = ROPE_BASE ** (-np.arange(half, dtype=np.float64) / half)
    pos = np.arange(n_rows, dtype=np.float64) + first_pos
    ang = pos[:, None] * inv_freq[None, :]
    log_gamma = np.log(1.0 - 2.0 ** (-5.0 - np.arange(RET_HEADS, dtype=np.float64)))
    idx = np.arange(CHUNK, dtype=np.float64)
    dist = idx[:, None] - idx[None, :]
    intra = np.where(dist >= 0, np.exp(log_gamma[:, None, None] * np.maximum(dist, 0.0)[None]), 0.0)
    q_dec = np.exp(log_gamma[:, None] * (idx[None, :] + 1.0))
    k_dec = np.exp(log_gamma[:, None] * (CHUNK - 1.0 - idx[None, :]))
    q_dec = np.broadcast_to(q_dec[:, :, None], (RET_HEADS, CHUNK, RET_V_DIM))
    k_dec = np.broadcast_to(k_dec[:, :, None], (RET_HEADS, CHUNK, RET_QK_DIM))
    chunk_dec = tuple(float(v) for v in np.exp(log_gamma * CHUNK))
    as32 = lambda a: jnp.asarray(np.ascontiguousarray(a), F32)
    return as32(np.cos(ang)), as32(np.sin(ang)), as32(intra), as32(q_dec), as32(k_dec), chunk_dec


def _ret_body(chunk_dec, q_ref, k_ref, v_ref, g_ref, cos_ref, sin_ref, intra_ref, qd_ref, kd_ref, o_ref, st_ref):
    cos = cos_ref[...]
    sin = sin_ref[...]
    half = RET_QK_DIM // 2

    def rotary(t):
        t1, t2 = t[:, :half], t[:, half:]
        return jnp.concatenate([t1 * cos - t2 * sin, t2 * cos + t1 * sin], axis=1)

    for h in range(RET_HEADS):
        qk = slice(h * RET_QK_DIM, (h + 1) * RET_QK_DIM)
        vv = slice(h * RET_V_DIM, (h + 1) * RET_V_DIM)
        qr = rotary(q_ref[:, qk].astype(F32))
        kr = rotary(k_ref[:, qk].astype(F32)) * (RET_QK_DIM ** -0.5)
        qb = qr.astype(BF16)
        vh = v_ref[:, vv]
        st = st_ref[h]
        s = _nt_dot(qb, kr.astype(BF16)) * intra_ref[h]
        y = jnp.dot(s.astype(BF16), vh, preferred_element_type=F32)
        y = y + jnp.dot(qb, st.astype(BF16), preferred_element_type=F32) * qd_ref[h]
        kdt = jnp.transpose(kr * kd_ref[h]).astype(BF16)
        st_ref[h] = st * chunk_dec[h] + jnp.dot(kdt, vh, preferred_element_type=F32)
        o = y * lax.rsqrt(jnp.mean(y * y, axis=-1, keepdims=True) + NORM_EPS)
        g = g_ref[:, vv].astype(F32)
        o_ref[:, vv] = (g * _sigmoid(g) * o).astype(BF16)


RET_STATE = (RET_HEADS, RET_QK_DIM, RET_V_DIM)
N_RET_IN = 10


def _retention_io(proj, nc, first_pos, init_state):
    cos, sin, intra, q_dec, k_dec, chunk_dec = _ret_tables(nc * CHUNK, first_pos)
    row = lambda b, c: b * nc + c
    const3 = lambda b, c: (0, 0, 0)
    in_specs = [
        pl.BlockSpec((CHUNK, RET_QK), lambda b, c: (row(b, c), _col_block(COL_Q, RET_QK))),
        pl.BlockSpec((CHUNK, RET_QK), lambda b, c: (row(b, c), _col_block(COL_K, RET_QK))),
        pl.BlockSpec((CHUNK, RET_V), lambda b, c: (row(b, c), _col_block(COL_V, RET_V))),
        pl.BlockSpec((CHUNK, RET_V), lambda b, c: (row(b, c), _col_block(COL_G, RET_V))),
        pl.BlockSpec((CHUNK, RET_QK_DIM // 2), lambda b, c: (c, 0)),
        pl.BlockSpec((CHUNK, RET_QK_DIM // 2), lambda b, c: (c, 0)),
        pl.BlockSpec((RET_HEADS, CHUNK, CHUNK), const3),
        pl.BlockSpec((RET_HEADS, CHUNK, RET_V_DIM), const3),
        pl.BlockSpec((RET_HEADS, CHUNK, RET_QK_DIM), const3),
        pl.BlockSpec(RET_STATE, const3),
    ]
    assert len(in_specs) == N_RET_IN
    return in_specs, (proj, proj, proj, proj, cos, sin, intra, q_dec, k_dec, init_state), chunk_dec


CONV_COLS = 256


def _ssd_body(pad_rows, z_ref, xs_ref, bc_ref, dt_ref, shift_ref, cw_ref, cb_ref, dtb_ref, alog_ref, dsk_ref, nw_ref,
              o_ref, st_ref, carry_ref, xc_ref, y_ref):
    c = pl.program_id(1)

    if pad_rows:
        rows = lax.broadcasted_iota(I32, (CHUNK, 1), 0)
        valid = jnp.logical_or(c > 0, rows >= pad_rows)
        keep = lambda v: jnp.where(valid, v, 0.0)
    else:
        keep = lambda v: v

    taps = CONV_WIDTH - 1
    for s in range(CONV_CH // CONV_COLS):
        cols = slice(s * CONV_COLS, (s + 1) * CONV_COLS)
        if (s + 1) * CONV_COLS <= SSM_INNER:
            raw = xs_ref[:, cols]
        else:
            raw = bc_ref[:, s * CONV_COLS - SSM_INNER:(s + 1) * CONV_COLS - SSM_INNER]
        shifted = jnp.dot(shift_ref[...], raw, preferred_element_type=F32)
        raw32 = raw.astype(F32)
        acc = cb_ref[:, cols] + cw_ref[taps:taps + 1, cols] * raw32
        window = jnp.concatenate([carry_ref[:, cols], jnp.zeros((8, CONV_COLS), F32)], axis=0)
        head = jnp.zeros((8, CONV_COLS), F32)
        for k in range(taps):
            acc = acc + cw_ref[k:k + 1, cols] * shifted[k * CHUNK:(k + 1) * CHUNK]
            head = head + cw_ref[k:k + 1, cols] * window[8 - taps + k:16 - taps + k]
        acc = jnp.concatenate([acc[:8] + head, acc[8:]], axis=0)
        carry_ref[:, cols] = raw32[CHUNK - 8:]
        xc_ref[:, cols] = keep(acc * _sigmoid(acc))

    dtv = dt_ref[...] + dtb_ref[...]
    dt = jnp.maximum(dtv, 0.0) + jnp.log1p(jnp.exp(-jnp.abs(dtv)))
    dt = keep(dt)
    a = dt * (-jnp.exp(alog_ref[...]))
    ri = lax.broadcasted_iota(I32, (CHUNK, CHUNK), 0)
    ci = lax.broadcasted_iota(I32, (CHUNK, CHUNK), 1)
    causal = ri >= ci
    acs = jnp.dot(causal.astype(F32), a, preferred_element_type=F32, precision=HIGHEST)
    last = acs[CHUNK - 1:CHUNK, :]
    acs_t = jnp.transpose(acs)
    dt_t = jnp.transpose(dt)
    w_t = jnp.transpose(jnp.exp(last - acs) * dt)
    e_last = jnp.exp(last)
    e_acs = jnp.exp(acs)
    lanes = lax.broadcasted_iota(I32, (1, CHUNK), 1)
    low = lanes < SSM_HEAD_DIM

    for g in range(SSM_GROUPS):
        bm = xc_ref[:, SSM_INNER + g * SSM_STATE:SSM_INNER + (g + 1) * SSM_STATE]
        cm = xc_ref[:, SSM_INNER + (SSM_GROUPS + g) * SSM_STATE:SSM_INNER + (SSM_GROUPS + g + 1) * SSM_STATE]
        cb = _nt_dot(cm.astype(BF16), bm.astype(BF16))
        bm_t = jnp.transpose(bm)
        for pp in range(PAIRS_PER_GROUP):
            m = g * PAIRS_PER_GROUP + pp
            x_pair = xc_ref[:, m * CHUNK:(m + 1) * CHUNK]
            st_pair = st_ref[m]
            y_pair = jnp.zeros((CHUNK, CHUNK), F32)
            upd = jnp.zeros((SSM_STATE, CHUNK), F32)
            for hh in range(2):
                h = 2 * m + hh
                lane_mask = low if hh == 0 else jnp.logical_not(low)
                col = jnp.broadcast_to(acs[:, h:h + 1], (CHUNK, CHUNK))
                seg = col - acs_t[h:h + 1, :]
                dec = jnp.exp(jnp.where(causal, seg, -1e30))
                mat = cb * dec * dt_t[h:h + 1, :]
                from_start = jnp.broadcast_to(e_acs[:, h:h + 1], (CHUNK, CHUNK))
                lhs = jnp.concatenate([mat, cm * from_start], axis=1).astype(BF16)
                xm = jnp.where(lane_mask, x_pair, 0.0).astype(BF16)
                sm = jnp.where(lane_mask, st_pair, 0.0).astype(BF16)
                y_pair = y_pair + jnp.dot(lhs, jnp.concatenate([xm, sm], axis=0), preferred_element_type=F32)
                upd = upd + jnp.dot((bm_t * w_t[h:h + 1, :]).astype(BF16), xm, preferred_element_type=F32)
            decay = jnp.where(low, jnp.broadcast_to(e_last[:, 2 * m:2 * m + 1], (1, CHUNK)),
                              jnp.broadcast_to(e_last[:, 2 * m + 1:2 * m + 2], (1, CHUNK)))
            st_ref[m] = st_pair * decay + upd
            y_ref[:, m * CHUNK:(m + 1) * CHUNK] = y_pair

    gsz = SSM_INNER // SSM_GROUPS
    for g in range(SSM_GROUPS):
        cols = slice(g * gsz, (g + 1) * gsz)
        z = z_ref[:, cols].astype(F32)
        y = (y_ref[:, cols] + dsk_ref[:, cols] * xc_ref[:, cols]) * (z * _sigmoid(z))
        y = y * lax.rsqrt(jnp.mean(y * y, axis=-1, keepdims=True) + NORM_EPS)
        o_ref[:, cols] = (y * nw_ref[:, cols]).astype(BF16)


SSD_STATE = (HEAD_PAIRS, SSM_STATE, CHUNK)
CONV_TAIL = (8, CONV_CH)
N_SSD_IN = 13


def _ssd_io(proj, dt_raw, conv_w, conv_b, dt_bias, a_log, d_skip, norm_w, nc, init_state, init_tail):
    row = lambda b, c: b * nc + c
    const2 = lambda b, c: (0, 0)
    const3 = lambda b, c: (0, 0, 0)
    pad = lambda v: jnp.pad(v.astype(F32), (0, DT_PAD - SSM_HEADS)).reshape(1, DT_PAD)
    taps = CONV_WIDTH - 1
    shift = np.zeros((taps * CHUNK, CHUNK), np.float32)
    for k in range(taps):
        for r in range(taps - k, CHUNK):
            shift[k * CHUNK + r, r - (taps - k)] = 1.0
    in_specs = [
        pl.BlockSpec((CHUNK, SSM_INNER), lambda b, c: (row(b, c), _col_block(COL_Z, SSM_INNER))),
        pl.BlockSpec((CHUNK, SSM_INNER), lambda b, c: (row(b, c), _col_block(COL_X, SSM_INNER))),
        pl.BlockSpec((CHUNK, BC_WIDTH), lambda b, c: (row(b, c), _col_block(COL_BC, BC_WIDTH))),
        pl.BlockSpec((CHUNK, DT_PAD), lambda b, c: (row(b, c), 0)),
        pl.BlockSpec((taps * CHUNK, CHUNK), const2),
        pl.BlockSpec((CONV_WIDTH, CONV_CH), const2),
        pl.BlockSpec((1, CONV_CH), const2),
        pl.BlockSpec((1, DT_PAD), const2),
        pl.BlockSpec((1, DT_PAD), const2),
        pl.BlockSpec((1, SSM_INNER), const2),
        pl.BlockSpec((1, SSM_INNER), const2),
        pl.BlockSpec(SSD_STATE, const3),
        pl.BlockSpec(CONV_TAIL, const2),
    ]
    assert len(in_specs) == N_SSD_IN
    args = (proj, proj, proj, dt_raw, jnp.asarray(shift, BF16), conv_w.astype(F32), conv_b.astype(F32).reshape(1, CONV_CH),
            pad(dt_bias), pad(a_log), jnp.repeat(d_skip.astype(F32), SSM_HEAD_DIM).reshape(1, SSM_INNER),
            norm_w.astype(F32).reshape(1, SSM_INNER), init_state, init_tail)
    return in_specs, args


def _mixers_body(chunk_dec, pad_rows, emit_state, *refs):
    ret_in, refs = refs[:N_RET_IN], refs[N_RET_IN:]
    ssd_in, refs = refs[:N_SSD_IN], refs[N_SSD_IN:]
    n_out = 5 if emit_state else 2
    outs, scratch = refs[:n_out], refs[n_out:]
    states = scratch[:3]
    inits = (ret_in[-1], ssd_in[-2], ssd_in[-1])
    c = pl.program_id(1)

    @pl.when(c == 0)
    def _():
        for state, init in zip(states, inits):
            state[...] = init[...]

    _ret_body(chunk_dec, *ret_in[:-1], outs[0], scratch[0])
    _ssd_body(pad_rows, *ssd_in[:-2], outs[1], *scratch[1:])

    if emit_state:
        @pl.when(c == pl.num_programs(1) - 1)
        def _():
            for final, state in zip(outs[2:], states):
                final[...] = state[...]


def _mixers(proj, dt_raw, ssd_params, bsz, nc, first_pos, pad_rows, init, emit_state):
    t = proj.shape[0]
    ret_specs, ret_args, chunk_dec = _retention_io(proj, nc, first_pos, init[0])
    ssd_specs, ssd_args = _ssd_io(proj, dt_raw, *ssd_params, nc, init[1], init[2])
    row = lambda b, c: b * nc + c
    out_specs = [pl.BlockSpec((CHUNK, RET_V), lambda b, c: (row(b, c), 0)),
                 pl.BlockSpec((CHUNK, SSM_INNER), lambda b, c: (row(b, c), 0))]
    out_shape = [jax.ShapeDtypeStruct((t, RET_V), BF16), jax.ShapeDtypeStruct((t, SSM_INNER), BF16)]
    if emit_state:
        out_specs += [pl.BlockSpec(RET_STATE, lambda b, c: (0, 0, 0)), pl.BlockSpec(SSD_STATE, lambda b, c: (0, 0, 0)),
                      pl.BlockSpec(CONV_TAIL, lambda b, c: (0, 0))]
        out_shape += [jax.ShapeDtypeStruct(s, F32) for s in (RET_STATE, SSD_STATE, CONV_TAIL)]
    return pl.pallas_call(
        functools.partial(_mixers_body, chunk_dec, pad_rows, emit_state),
        grid=(bsz, nc),
        in_specs=ret_specs + ssd_specs,
        out_specs=out_specs,
        out_shape=out_shape,
        scratch_shapes=[
            pltpu.VMEM(RET_STATE, F32),
            pltpu.VMEM(SSD_STATE, F32),
            pltpu.VMEM(CONV_TAIL, F32),
            pltpu.VMEM((CHUNK, CONV_CH), F32),
            pltpu.VMEM((CHUNK, SSM_INNER), F32),
        ],
        compiler_params=_params(("parallel", "arbitrary")),
        name="mixers_meta" if emit_state else "mixers",
    )(*ret_args, *ssd_args)


def _merge_body(oret_ref, ossd_ref, g0_ref, g1_ref, h_ref, wb0_ref, wb1_ref, wo_ref, nw_ref, rwh_ref, rwl_ref,
                rb_ref, h1_ref, un_ref, e_ref, w_ref):
    y_ret = jnp.dot(oret_ref[...], wb0_ref[...], preferred_element_type=F32)
    y_ssd = jnp.dot(ossd_ref[...], wb1_ref[...], preferred_element_type=F32)
    merged = (_sigmoid(g0_ref[...].astype(F32)) * y_ret + _sigmoid(g1_ref[...].astype(F32)) * y_ssd)
    h1 = h_ref[...] + jnp.dot(merged.astype(BF16), wo_ref[...], preferred_element_type=F32)
    h1_ref[...] = h1
    un = h1 * lax.rsqrt(jnp.mean(h1 * h1, axis=-1, keepdims=True) + NORM_EPS) * nw_ref[...]
    un_hi = un.astype(BF16)
    un_ref[...] = un_hi

    un_lo = (un - un_hi.astype(F32)).astype(BF16)
    logits = (jnp.dot(un_hi, rwh_ref[...], preferred_element_type=F32)
              + jnp.dot(un_lo, rwh_ref[...], preferred_element_type=F32)
              + jnp.dot(un_hi, rwl_ref[...], preferred_element_type=F32)) + rb_ref[...]
    logits = jnp.transpose(logits)[:N_EXPERTS, :]
    tm = logits.shape[1]
    eidx = lax.broadcasted_iota(I32, (N_EXPERTS, tm), 0)
    vals, ids = [], []
    for _ in range(TOP_K):
        best = jnp.max(logits, axis=0, keepdims=True)
        arg = jnp.min(jnp.where(logits == best, eidx, N_EXPERTS), axis=0, keepdims=True)
        vals.append(best)
        ids.append(arg)
        logits = jnp.where(eidx == arg, -jnp.inf, logits)
    ex = [jnp.exp(v - vals[0]) for v in vals]
    denom = ex[0] + ex[1] + ex[2] + ex[3]
    e_ref[...] = jnp.concatenate(ids + [jnp.zeros((8 - TOP_K, tm), I32)], axis=0)
    w_ref[...] = jnp.concatenate([x / denom for x in ex] + [jnp.zeros((8 - TOP_K, tm), F32)], axis=0)


def _merge(o_ret, o_ssd, proj, h, wb0, wb1, wo, norm_w, router_w, router_b):
    t = h.shape[0]
    tm = _pick(t, (512, 256, 128))
    gate_blk = _col_block(OFF_DT, D_MODEL)
    const2 = lambda i: (0, 0)
    rw_hi = router_w.astype(BF16)
    rw_lo = (router_w - rw_hi.astype(F32)).astype(BF16)
    return pl.pallas_call(
        _merge_body,
        grid=(t // tm,),
        in_specs=[
            pl.BlockSpec((tm, RET_V), lambda i: (i, 0)),
            pl.BlockSpec((tm, SSM_INNER), lambda i: (i, 0)),
            pl.BlockSpec((tm, D_MODEL), lambda i: (i, gate_blk)),
            pl.BlockSpec((tm, D_MODEL), lambda i: (i, gate_blk + 1)),
            pl.BlockSpec((tm, D_MODEL), lambda i: (i, 0)),
            pl.BlockSpec((RET_V, D_MODEL), const2),
            pl.BlockSpec((SSM_INNER, D_MODEL), const2),
            pl.BlockSpec((D_MODEL, D_MODEL), const2),
            pl.BlockSpec((1, D_MODEL), const2),
            pl.BlockSpec((D_MODEL, CHUNK), const2),
            pl.BlockSpec((D_MODEL, CHUNK), const2),
            pl.BlockSpec((1, CHUNK), const2),
        ],
        out_specs=[
            pl.BlockSpec((tm, D_MODEL), lambda i: (i, 0)),
            pl.BlockSpec((tm, D_MODEL), lambda i: (i, 0)),
            pl.BlockSpec((8, tm), lambda i: (0, i)),
            pl.BlockSpec((8, tm), lambda i: (0, i)),
        ],
        out_shape=[
            jax.ShapeDtypeStruct((t, D_MODEL), F32),
            jax.ShapeDtypeStruct((t, D_MODEL), BF16),
            jax.ShapeDtypeStruct((8, t), I32),
            jax.ShapeDtypeStruct((8, t), F32),
        ],
        compiler_params=_params(("parallel",)),
        name="merge_router",
    )(o_ret, o_ssd, proj, proj, h, wb0, wb1, wo, norm_w, rw_hi, rw_lo, router_b)


ROW_ALIGN = 8
PIECE = 16
LOCAL_ROWS = 1024
META_PIECES, META_EXPERT, META_LOCAL, META_SORTED = 0, 1, 2, 3
MAX_PIECES = CHUNK * TOP_K // PIECE + N_EXPERTS
assert MAX_PIECES <= CHUNK


def _rank_body(tiles, e_ref, slot_ref, meta_ref, cnt_ref, run_ref):
    @pl.when(pl.program_id(0) == 0)
    def _():
        run_ref[...] = jnp.zeros_like(run_ref)

    eidx = lax.broadcasted_iota(I32, (N_EXPERTS, CHUNK), 0)
    lane = lax.broadcasted_iota(I32, (N_EXPERTS, CHUNK), 1)
    ri = lax.broadcasted_iota(I32, (CHUNK, CHUNK), 0)
    ci = lax.broadcasted_iota(I32, (CHUNK, CHUNK), 1)
    upper = jnp.where(ri <= ci, 1.0, 0.0).astype(BF16)
    ones = jnp.ones((CHUNK, CHUNK), BF16)
    ei = lax.broadcasted_iota(I32, (N_EXPERTS, N_EXPERTS), 0)
    ej = lax.broadcasted_iota(I32, (N_EXPERTS, N_EXPERTS), 1)
    lower = jnp.where(ej < ei, 1.0, 0.0).astype(BF16)
    round_up = lambda v, m: jnp.floor((v + (m - 1)) * (1.0 / m)) * m

    run = run_ref[...]
    for s in range(tiles):
        e_tile = e_ref[:, s * CHUNK:(s + 1) * CHUNK]
        seen = jnp.zeros((N_EXPERTS, CHUNK), F32)
        hits, ranks = [], []
        for k in range(TOP_K):
            hit = eidx == e_tile[k:k + 1, :]
            oh = jnp.where(hit, 1.0, 0.0).astype(BF16)
            ranks.append(seen + jnp.dot(oh, upper, preferred_element_type=F32) - 1.0)
            seen = seen + jnp.dot(oh, ones, preferred_element_type=F32)
            hits.append(hit)
        count = seen
        lstart = jnp.dot(lower, round_up(count, PIECE).astype(BF16), preferred_element_type=F32)
        slots = []
        for k in range(TOP_K):
            slots.append(jnp.sum(jnp.where(hits[k], lstart + ranks[k], 0.0), axis=0, keepdims=True))
        slot_ref[:, s * CHUNK:(s + 1) * CHUNK] = jnp.concatenate(
            slots + [jnp.full((8 - TOP_K, CHUNK), -1.0, F32)], axis=0).astype(I32)
        pieces = round_up(count, PIECE) * (1.0 / PIECE)
        first = jnp.dot(lower, pieces.astype(BF16), preferred_element_type=F32)
        nth = lane.astype(F32) - first
        mine = jnp.logical_and(nth >= 0.0, nth < pieces)
        pick = lambda v: jnp.sum(jnp.where(mine, v, 0.0), axis=0, keepdims=True)
        meta_ref[s] = jnp.concatenate(
            [jnp.sum(pieces, axis=0, keepdims=True), pick(eidx.astype(F32)), pick(lstart + PIECE * nth),
             pick(run + PIECE * nth), jnp.zeros((4, CHUNK), F32)], axis=0).astype(I32)
        run = run + round_up(count, ROW_ALIGN)
    run_ref[...] = run
    cnt_ref[...] = run.astype(I32)


def _rank(top_e):
    t = top_e.shape[1]
    n_tiles = t // CHUNK
    tiles = _pick(n_tiles, (8, 4, 2, 1))
    return pl.pallas_call(
        functools.partial(_rank_body, tiles),
        grid=(n_tiles // tiles,),
        in_specs=[pl.BlockSpec((8, tiles * CHUNK), lambda i: (0, i))],
        out_specs=[
            pl.BlockSpec((8, tiles * CHUNK), lambda i: (0, i)),
            pl.BlockSpec((tiles, 8, CHUNK), lambda i: (i, 0, 0)),
            pl.BlockSpec((N_EXPERTS, CHUNK), lambda i: (0, 0)),
        ],
        out_shape=[
            jax.ShapeDtypeStruct((8, t), I32),
            jax.ShapeDtypeStruct((n_tiles, 8, CHUNK), I32),
            jax.ShapeDtypeStruct((N_EXPERTS, CHUNK), I32),
        ],
        scratch_shapes=[pltpu.VMEM((N_EXPERTS, CHUNK), F32)],
        compiler_params=_params(("arbitrary",)),
        name="rank",
    )(top_e)


def _pack_halves(v):
    bits = pltpu.bitcast(v, jnp.uint32)
    return (bits[:, :D_MODEL // 2] >> 16) | bits[:, D_MODEL // 2:]


def _unpack_halves(p):
    lo = pltpu.bitcast(p << 16, F32)
    hi = pltpu.bitcast(p & jnp.uint32(0xFFFF0000), F32)
    return jnp.concatenate([lo, hi], axis=1).astype(BF16)


def _wait_pieces(meta_ref, make_copy):
    def body(j, carry):
        make_copy(0, 0).wait()
        return carry

    lax.fori_loop(0, meta_ref[0, META_PIECES, 0], body, 0)


def _start_pieces(pstart_ref, meta_ref, make_copy):
    def body(p, carry):
        sorted_row = pstart_ref[meta_ref[0, META_EXPERT, p]] + meta_ref[0, META_SORTED, p]
        make_copy(pl.multiple_of(meta_ref[0, META_LOCAL, p], ROW_ALIGN), pl.multiple_of(sorted_row, ROW_ALIGN)).start()
        return carry

    lax.fori_loop(0, meta_ref[0, META_PIECES, 0], body, 0)


def _dispatch_body(pstart_ref, fill_ref, meta_ref, prev_meta_ref, slot_ref, un_ref, xs_ref, loc_ref, sems):
    step = pl.program_id(0)
    last = pl.num_programs(0) - 1
    buf = step % 2

    srow = lax.broadcasted_iota(I32, (LOCAL_ROWS, CHUNK), 0)
    onehot = jnp.zeros((LOCAL_ROWS, CHUNK), F32)
    for k in range(TOP_K):
        onehot = jnp.where(srow == slot_ref[k:k + 1, :], 1.0, onehot)
    loc_ref[buf] = _pack_halves(jnp.dot(onehot.astype(BF16), un_ref[...], preferred_element_type=F32))

    def copies_of(b):
        def make_copy(local_row, sorted_row):
            return pltpu.make_async_copy(loc_ref.at[b, pl.ds(local_row, PIECE), :],
                                         xs_ref.at[pl.ds(sorted_row, PIECE), :], sems.at[b])
        return make_copy

    @pl.when(step > 0)
    def _():
        _wait_pieces(prev_meta_ref, copies_of(1 - buf))

    _start_pieces(pstart_ref, meta_ref, copies_of(buf))

    @pl.when(step == last)
    def _():
        _wait_pieces(meta_ref, copies_of(buf))
        loc_ref[0] = jnp.zeros((LOCAL_ROWS, D_MODEL // 2), jnp.uint32)

        def zero_rows(first, n_rows):
            return pltpu.make_async_copy(loc_ref.at[0, pl.ds(0, n_rows), :],
                                         xs_ref.at[pl.ds(pl.multiple_of(first, ROW_ALIGN), n_rows), :], sems.at[0])

        def fill(act):
            def per_expert(e, carry):
                def per_piece(j, c):
                    act(zero_rows(fill_ref[0, e] + j * ROW_ALIGN, ROW_ALIGN))
                    return c
                return lax.fori_loop(0, fill_ref[1, e], per_piece, carry)

            def per_block(j, c):
                act(zero_rows(fill_ref[2, 0] + j * LOCAL_ROWS, LOCAL_ROWS))
                return c

            lax.fori_loop(0, N_EXPERTS, per_expert, 0)
            lax.fori_loop(0, fill_ref[2, 1], per_block, 0)

        fill(lambda cp: cp.start())
        fill(lambda cp: cp.wait())


def _dispatch(pstart, fill, meta, slot, un, n_rows):
    t = un.shape[0]
    return pl.pallas_call(
        _dispatch_body,
        grid_spec=pltpu.PrefetchScalarGridSpec(
            num_scalar_prefetch=2,
            grid=(t // CHUNK,),
            in_specs=[
                pl.BlockSpec((1, 8, CHUNK), lambda i, ps, fl: (i, 0, 0), memory_space=pltpu.SMEM),
                pl.BlockSpec((1, 8, CHUNK), lambda i, ps, fl: (jnp.maximum(i - 1, 0), 0, 0), memory_space=pltpu.SMEM),
                pl.BlockSpec((8, CHUNK), lambda i, ps, fl: (0, i)),
                pl.BlockSpec((CHUNK, D_MODEL), lambda i, ps, fl: (i, 0)),
            ],
            out_specs=pl.BlockSpec(memory_space=pl.ANY),
            scratch_shapes=[pltpu.VMEM((2, LOCAL_ROWS, D_MODEL // 2), jnp.uint32), pltpu.SemaphoreType.DMA((2,))],
        ),
        out_shape=jax.ShapeDtypeStruct((n_rows, D_MODEL // 2), jnp.uint32),
        compiler_params=_params(("arbitrary",)),
        name="dispatch",
    )(pstart, fill, meta, meta, slot, un)


GU_SLAB = 256
GU_HALF = GU_SLAB // 2


def _regroup_body(w_ref, p_ref, o_ref):
    for s in range(2 * EXPERT_FF // GU_SLAB):
        cols = slice(s * GU_SLAB, (s + 1) * GU_SLAB)
        o_ref[0, :, cols] = jnp.dot(w_ref[0, :, cols].astype(BF16), p_ref[...], preferred_element_type=F32).astype(BF16)


def _regroup_gate_up(w_gate_up):
    perm = np.zeros((GU_SLAB, GU_SLAB), np.float32)
    perm[2 * np.arange(GU_HALF), np.arange(GU_HALF)] = 1.0
    perm[2 * np.arange(GU_HALF) + 1, GU_HALF + np.arange(GU_HALF)] = 1.0
    blk = (1, D_MODEL, 2 * EXPERT_FF)
    return pl.pallas_call(
        _regroup_body,
        grid=(N_EXPERTS,),
        in_specs=[pl.BlockSpec(blk, lambda e: (e, 0, 0)), pl.BlockSpec((GU_SLAB, GU_SLAB), lambda e: (0, 0))],
        out_specs=pl.BlockSpec(blk, lambda e: (e, 0, 0)),
        out_shape=jax.ShapeDtypeStruct((N_EXPERTS, D_MODEL, 2 * EXPERT_FF), BF16),
        compiler_params=_params(("parallel",)),
        name="regroup_gate_up",
    )(w_gate_up, jnp.asarray(perm, BF16))


def _regroup_bias(b_gate_up):
    b = b_gate_up.astype(F32).reshape(N_EXPERTS, 2 * EXPERT_FF // GU_SLAB, GU_HALF, 2)
    return jnp.transpose(b, (0, 1, 3, 2)).reshape(N_EXPERTS, 1, 2 * EXPERT_FF)


def _expert_body(be_ref, nu_ref, x_ref, wgu_ref, wd_ref, bgu_ref, bd_ref, y_ref):
    del be_ref

    @pl.when(pl.program_id(0) >= nu_ref[0])
    def _():
        y_ref[...] = jnp.zeros_like(y_ref)

    @pl.when(pl.program_id(0) < nu_ref[0])
    def _():
        x = _unpack_halves(x_ref[...])
        gu = jnp.dot(x, wgu_ref[0], preferred_element_type=F32) + bgu_ref[0]
        acts = []
        for s in range(2 * EXPERT_FF // GU_SLAB):
            gate = jnp.minimum(gu[:, s * GU_SLAB:s * GU_SLAB + GU_HALF], SWIGLU_LIMIT)
            up = jnp.clip(gu[:, s * GU_SLAB + GU_HALF:(s + 1) * GU_SLAB], -SWIGLU_LIMIT, SWIGLU_LIMIT)
            acts.append(((up + 1.0) * (gate * _sigmoid(SWIGLU_ALPHA * gate))).astype(BF16))
        act = jnp.concatenate(acts, axis=1)
        y = jnp.dot(act, wd_ref[0], preferred_element_type=F32) + bd_ref[0]
        y_ref[...] = _pack_halves(y.astype(BF16).astype(F32))


def _experts(block_e, n_used, xs, wgu, wd, bgu, bd):
    n_rows = xs.shape[0]
    n_blocks = n_rows // MOE_BLOCK
    wspec = lambda shape: pl.BlockSpec((1,) + shape, lambda i, be, nu: (be[i], 0, 0))
    return pl.pallas_call(
        _expert_body,
        grid_spec=pltpu.PrefetchScalarGridSpec(
            num_scalar_prefetch=2,
            grid=(n_blocks,),
            in_specs=[
                pl.BlockSpec((MOE_BLOCK, D_MODEL // 2), lambda i, be, nu: (jnp.minimum(i, nu[0] - 1), 0)),
                wspec((D_MODEL, 2 * EXPERT_FF)),
                wspec((EXPERT_FF, D_MODEL)),
                wspec((1, 2 * EXPERT_FF)),
                wspec((1, D_MODEL)),
            ],
            out_specs=pl.BlockSpec((MOE_BLOCK, D_MODEL // 2), lambda i, be, nu: (i, 0)),
        ),
        out_shape=jax.ShapeDtypeStruct((n_rows, D_MODEL // 2), jnp.uint32),
        compiler_params=_params(("arbitrary",)),
        name="experts",
    )(block_e, n_used, xs, wgu, wd, bgu, bd)


def _combine_body(pstart_ref, meta_ref, next_meta_ref, slot_ref, w_ref, h1_ref, nw_ref, y_hbm, o_ref, ybuf, sems):
    step = pl.program_id(0)
    buf = step % 2

    def copies_of(b):
        def make_copy(local_row, sorted_row):
            return pltpu.make_async_copy(y_hbm.at[pl.ds(sorted_row, PIECE), :],
                                         ybuf.at[b, pl.ds(local_row, PIECE), :], sems.at[b])
        return make_copy

    @pl.when(step == 0)
    def _():
        ybuf[...] = jnp.zeros_like(ybuf)
        _start_pieces(pstart_ref, meta_ref, copies_of(buf))

    @pl.when(step + 1 < pl.num_programs(0))
    def _():
        _start_pieces(pstart_ref, next_meta_ref, copies_of(1 - buf))

    _wait_pieces(meta_ref, copies_of(buf))

    srow = lax.broadcasted_iota(I32, (LOCAL_ROWS, CHUNK), 0)
    wmat = jnp.zeros((LOCAL_ROWS, CHUNK), F32)
    for k in range(TOP_K):
        wmat = jnp.where(srow == slot_ref[k:k + 1, :], w_ref[k:k + 1, :], wmat)
    wmat = jnp.transpose(wmat)
    w_hi = wmat.astype(BF16)
    w_lo = (wmat - w_hi.astype(F32)).astype(BF16)
    y = _unpack_halves(ybuf[buf])
    f = jnp.dot(w_hi, y, preferred_element_type=F32) + jnp.dot(w_lo, y, preferred_element_type=F32)
    h2 = h1_ref[...] + f
    o_ref[...] = h2 * lax.rsqrt(jnp.mean(h2 * h2, axis=-1, keepdims=True) + NORM_EPS) * nw_ref[...]


def _combine(pstart, meta, slot, top_w, h1, norm_w, y_sorted):
    t = h1.shape[0]
    n_tiles = t // CHUNK
    nxt = lambda i: jnp.minimum(i + 1, n_tiles - 1)
    return pl.pallas_call(
        _combine_body,
        grid_spec=pltpu.PrefetchScalarGridSpec(
            num_scalar_prefetch=1,
            grid=(n_tiles,),
            in_specs=[
                pl.BlockSpec((1, 8, CHUNK), lambda i, ps: (i, 0, 0), memory_space=pltpu.SMEM),
                pl.BlockSpec((1, 8, CHUNK), lambda i, ps: (nxt(i), 0, 0), memory_space=pltpu.SMEM),
                pl.BlockSpec((8, CHUNK), lambda i, ps: (0, i)),
                pl.BlockSpec((8, CHUNK), lambda i, ps: (0, i)),
                pl.BlockSpec((CHUNK, D_MODEL), lambda i, ps: (i, 0)),
                pl.BlockSpec((1, D_MODEL), lambda i, ps: (0, 0)),
                pl.BlockSpec(memory_space=pl.ANY),
            ],
            out_specs=pl.BlockSpec((CHUNK, D_MODEL), lambda i, ps: (i, 0)),
            scratch_shapes=[pltpu.VMEM((2, LOCAL_ROWS, D_MODEL // 2), jnp.uint32), pltpu.SemaphoreType.DMA((2,))],
        ),
        out_shape=jax.ShapeDtypeStruct((t, D_MODEL), F32),
        compiler_params=_params(("arbitrary",)),
        name="combine",
    )(pstart, meta, meta, slot, top_w, h1, norm_w, y_sorted)


def kernel(x, meta_tokens, norm_mix, w_in, conv_w, conv_b, dt_bias, a_log, d_skip, ssm_norm, w_branch, w_out, norm_ffn,
           router_w, router_b, w_gate_up, b_gate_up, w_down, b_down, norm_final):
    bsz, seq, _ = x.shape
    assert seq % CHUNK == 0 and norm_mix.shape[0] == 1
    nc = seq // CHUNK
    t = bsz * seq
    x2d = x.reshape(t, D_MODEL)
    meta_chunk = jnp.concatenate([jnp.zeros((META_PAD, D_MODEL), x.dtype), meta_tokens.astype(x.dtype)], axis=0)

    w_in0 = w_in[0]
    w_main = jnp.concatenate([w_in0[:, :OFF_DT], w_in0[:, OFF_DT + SSM_HEADS:]], axis=1).astype(BF16)
    w_dt = jnp.pad(w_in0[:, OFF_DT:OFF_DT + SSM_HEADS], ((0, 0), (0, DT_PAD - SSM_HEADS))).astype(BF16)
    wgu = _regroup_gate_up(w_gate_up[0])
    bgu = _regroup_bias(b_gate_up[0])
    wd = w_down[0].astype(BF16)
    bd = b_down[0][:, None, :].astype(F32)

    norm_mix_w = norm_mix[0].reshape(1, D_MODEL).astype(F32)
    ssd_params = (conv_w[0], conv_b[0], dt_bias[0], a_log[0], d_skip[0], ssm_norm[0])
    proj_m, dt_m = _in_proj(meta_chunk, norm_mix_w, w_main, w_dt)
    zero_states = tuple(jnp.zeros(s, F32) for s in (RET_STATE, SSD_STATE, CONV_TAIL))
    meta_states = _mixers(proj_m, dt_m, ssd_params, 1, 1, -META_PAD, META_PAD, zero_states, True)[2:]

    proj, dt_raw = _in_proj(x2d, norm_mix_w, w_main, w_dt)
    o_ret, o_ssd = _mixers(proj, dt_raw, ssd_params, bsz, nc, N_META, 0, meta_states, False)
    h1, un, top_e, top_w = _merge(
        o_ret, o_ssd, proj, x2d, w_branch[0, 0].astype(BF16), w_branch[0, 1].astype(BF16), w_out[0].astype(BF16),
        norm_ffn[0].reshape(1, D_MODEL).astype(F32),
        jnp.pad(router_w[0].astype(F32), ((0, 0), (0, CHUNK - N_EXPERTS))),
        jnp.pad(router_b[0].astype(F32), (0, CHUNK - N_EXPERTS)).reshape(1, CHUNK))

    slot, meta, counts = _rank(top_e)
    counts = counts[:, 0]
    slack = MOE_BLOCK - 1 + PIECE - ROW_ALIGN
    padded = (counts + slack) // MOE_BLOCK * MOE_BLOCK
    pend = jnp.cumsum(padded)
    pstart = (pend - padded).astype(I32)
    n_pairs = t * TOP_K
    max_rows = n_pairs + bsz * nc * N_EXPERTS * (ROW_ALIGN - 1)
    n_blocks = (max_rows + N_EXPERTS * slack) // MOE_BLOCK
    blk_row = jnp.arange(n_blocks, dtype=I32) * MOE_BLOCK
    block_e = jnp.minimum(jnp.sum(pend[None, :] <= blk_row[:, None], axis=1), N_EXPERTS - 1).astype(I32)
    n_used = (pend[-1:] // MOE_BLOCK).astype(I32)
    assert MOE_BLOCK % LOCAL_ROWS == 0
    tail = jnp.zeros((N_EXPERTS,), I32).at[0].set(pend[-1]).at[1].set((n_blocks * MOE_BLOCK - pend[-1]) // LOCAL_ROWS)
    fill = jnp.stack([pstart + counts, (padded - counts) // ROW_ALIGN, tail]).astype(I32)

    xs = _dispatch(pstart, fill, meta, slot, un, n_blocks * MOE_BLOCK)
    ys = _experts(block_e, n_used, xs, wgu, wd, bgu, bd)
    out = _combine(pstart, meta, slot, top_w, h1, norm_final.reshape(1, D_MODEL).astype(F32), ys)
    return out.reshape(bsz, seq, D_MODEL)
```

```python
import functools

import numpy as np
import jax
import jax.numpy as jnp
from jax import lax
from jax.experimental import pallas as pl
from jax.experimental.pallas import tpu as pltpu

F32 = jnp.float32
BF16 = jnp.bfloat16
I32 = jnp.int32

D_MODEL = 1024
N_META = 16
CHUNK = 128
META_PAD = CHUNK - N_META
NORM_EPS = 1e-6
RET_HEADS = 4
RET_QK_DIM = 256
RET_V_DIM = 512
RET_QK = RET_HEADS * RET_QK_DIM
RET_V = RET_HEADS * RET_V_DIM
ROPE_BASE = 10000.0
SSM_INNER = 2048
SSM_HEAD_DIM = 64
SSM_HEADS = 32
SSM_GROUPS = 4
SSM_STATE = 128
CONV_WIDTH = 4
CONV_CH = SSM_INNER + 2 * SSM_GROUPS * SSM_STATE
HEAD_PAIRS = SSM_HEADS // 2
PAIRS_PER_GROUP = HEAD_PAIRS // SSM_GROUPS
COL_Q = 0
COL_K = COL_Q + RET_QK
COL_V = COL_K + RET_QK
COL_G = COL_V + RET_V
COL_Z = COL_G + RET_V
COL_X = COL_Z + SSM_INNER
COL_BC = COL_X + SSM_INNER
BC_WIDTH = 2 * SSM_GROUPS * SSM_STATE
OFF_DT = COL_BC + BC_WIDTH
N_MAIN = OFF_DT + 2 * D_MODEL
DT_PAD = 128
N_EXPERTS = 32
TOP_K = 4
EXPERT_FF = 1024
SWIGLU_ALPHA = 1.702
SWIGLU_LIMIT = 7.0
MOE_BLOCK = 1024

VMEM_LIMIT = 56 * 1024 * 1024
HIGHEST = lax.Precision.HIGHEST


def _params(sem):
    return pltpu.CompilerParams(dimension_semantics=sem, vmem_limit_bytes=VMEM_LIMIT)


def _pick(n, candidates):
    for c in candidates:
        if n % c == 0:
            return c
    raise ValueError(f"no tile for {n} among {candidates}")


def _sigmoid(x):
    return 0.5 * jnp.tanh(0.5 * x) + 0.5


def _nt_dot(a, b):
    return lax.dot_general(a, b, (((1,), (1,)), ((), ())), preferred_element_type=F32)


def _col_block(offset, width):
    assert offset % width == 0
    return offset // width


def _inproj_body(x_ref, nw_ref, w_ref, wdt_ref, o_ref, dt_ref, xn_ref):
    @pl.when(pl.program_id(1) == 0)
    def _():
        x = x_ref[...]
        ms = jnp.mean(x * x, axis=-1, keepdims=True)
        xn = (x * lax.rsqrt(ms + NORM_EPS) * nw_ref[...]).astype(BF16)
        xn_ref[...] = xn
        dt_ref[...] = jnp.dot(xn, wdt_ref[...], preferred_element_type=F32)

    o_ref[...] = jnp.dot(xn_ref[...], w_ref[...], preferred_element_type=F32).astype(BF16)


def _in_proj(h, norm_w, w_main, w_dt):
    t = h.shape[0]
    tm = _pick(t, (2048, 1024, 512, 256, 128))
    tn = 1024
    return pl.pallas_call(
        _inproj_body,
        grid=(t // tm, N_MAIN // tn),
        in_specs=[
            pl.BlockSpec((tm, D_MODEL), lambda i, j: (i, 0)),
            pl.BlockSpec((1, D_MODEL), lambda i, j: (0, 0)),
            pl.BlockSpec((D_MODEL, tn), lambda i, j: (0, j)),
            pl.BlockSpec((D_MODEL, DT_PAD), lambda i, j: (0, 0)),
        ],
        out_specs=[
            pl.BlockSpec((tm, tn), lambda i, j: (i, j)),
            pl.BlockSpec((tm, DT_PAD), lambda i, j: (i, 0)),
        ],
        out_shape=[jax.ShapeDtypeStruct((t, N_MAIN), BF16), jax.ShapeDtypeStruct((t, DT_PAD), F32)],
        scratch_shapes=[pltpu.VMEM((tm, D_MODEL), BF16)],
        compiler_params=_params(("parallel", "arbitrary")),
        name="in_proj",
    )(h, norm_w, w_main, w_dt)


def _ret_tables(n_rows, first_pos):
    half = RET_QK_DIM // 2
    inv_freq = ROPE_BASE ** (-np.arange(half, dtype=np.float64) / half)
    pos = np.arange(n_rows, dtype=np.float64) + first_pos
    ang = pos[:, None] * inv_freq[None, :]
    log_gamma = np.log(1.0 - 2.0 ** (-5.0 - np.arange(RET_HEADS, dtype=np.float64)))
    idx = np.arange(CHUNK, dtype=np.float64)
    dist = idx[:, None] - idx[None, :]
    intra = np.where(dist >= 0, np.exp(log_gamma[:, None, None] * np.maximum(dist, 0.0)[None]), 0.0)
    q_dec = np.exp(log_gamma[:, None] * (idx[None, :] + 1.0))
    k_dec = np.exp(log_gamma[:, None] * (CHUNK - 1.0 - idx[None, :]))
    q_dec = np.broadcast_to(q_dec[:, :, None], (RET_HEADS, CHUNK, RET_V_DIM))
    k_dec = np.broadcast_to(k_dec[:, :, None], (RET_HEADS, CHUNK, RET_QK_DIM))
    chunk_dec = tuple(float(v) for v in np.exp(log_gamma * CHUNK))
    as32 = lambda a: jnp.asarray(np.ascontiguousarray(a), F32)
    return as32(np.cos(ang)), as32(np.sin(ang)), as32(intra), as32(q_dec), as32(k_dec), chunk_dec


def _ret_body(chunk_dec, q_ref, k_ref, v_ref, g_ref, cos_ref, sin_ref, intra_ref, qd_ref, kd_ref, o_ref, st_ref):
    cos = cos_ref[...]
    sin = sin_ref[...]
    half = RET_QK_DIM // 2

    def rotary(t):
        t1, t2 = t[:, :half], t[:, half:]
        return jnp.concatenate([t1 * cos - t2 * sin, t2 * cos + t1 * sin], axis=1)

    for h in range(RET_HEADS):
        qk = slice(h * RET_QK_DIM, (h + 1) * RET_QK_DIM)
        vv = slice(h * RET_V_DIM, (h + 1) * RET_V_DIM)
        qr = rotary(q_ref[:, qk].astype(F32))
        kr = rotary(k_ref[:, qk].astype(F32)) * (RET_QK_DIM ** -0.5)
        qb = qr.astype(BF16)
        vh = v_ref[:, vv]
        st = st_ref[h]
        s = _nt_dot(qb, kr.astype(BF16)) * intra_ref[h]
        y = jnp.dot(s.astype(BF16), vh, preferred_element_type=F32)
        y = y + jnp.dot(qb, st.astype(BF16), preferred_element_type=F32) * qd_ref[h]
        kdt = jnp.transpose(kr * kd_ref[h]).astype(BF16)
        st_ref[h] = st * chunk_dec[h] + jnp.dot(kdt, vh, preferred_element_type=F32)
        o = y * lax.rsqrt(jnp.mean(y * y, axis=-1, keepdims=True) + NORM_EPS)
        g = g_ref[:, vv].astype(F32)
        o_ref[:, vv] = (g * _sigmoid(g) * o).astype(BF16)


RET_STATE = (RET_HEADS, RET_QK_DIM, RET_V_DIM)
N_RET_IN = 10


def _retention_io(proj, nc, first_pos, init_state):
    cos, sin, intra, q_dec, k_dec, chunk_dec = _ret_tables(nc * CHUNK, first_pos)
    row = lambda b, c: b * nc + c
    const3 = lambda b, c: (0, 0, 0)
    in_specs = [
        pl.BlockSpec((CHUNK, RET_QK), lambda b, c: (row(b, c), _col_block(COL_Q, RET_QK))),
        pl.BlockSpec((CHUNK, RET_QK), lambda b, c: (row(b, c), _col_block(COL_K, RET_QK))),
        pl.BlockSpec((CHUNK, RET_V), lambda b, c: (row(b, c), _col_block(COL_V, RET_V))),
        pl.BlockSpec((CHUNK, RET_V), lambda b, c: (row(b, c), _col_block(COL_G, RET_V))),
        pl.BlockSpec((CHUNK, RET_QK_DIM // 2), lambda b, c: (c, 0)),
        pl.BlockSpec((CHUNK, RET_QK_DIM // 2), lambda b, c: (c, 0)),
        pl.BlockSpec((RET_HEADS, CHUNK, CHUNK), const3),
        pl.BlockSpec((RET_HEADS, CHUNK, RET_V_DIM), const3),
        pl.BlockSpec((RET_HEADS, CHUNK, RET_QK_DIM), const3),
        pl.BlockSpec(RET_STATE, const3),
    ]
    assert len(in_specs) == N_RET_IN
    return in_specs, (proj, proj, proj, proj, cos, sin, intra, q_dec, k_dec, init_state), chunk_dec


CONV_COLS = 256


def _ssd_body(pad_rows, z_ref, xs_ref, bc_ref, dt_ref, shift_ref, cw_ref, cb_ref, dtb_ref, alog_ref, dsk_ref, nw_ref,
              o_ref, st_ref, carry_ref, xc_ref, y_ref):
    c = pl.program_id(1)

    if pad_rows:
        rows = lax.broadcasted_iota(I32, (CHUNK, 1), 0)
        valid = jnp.logical_or(c > 0, rows >= pad_rows)
        keep = lambda v: jnp.where(valid, v, 0.0)
    else:
        keep = lambda v: v

    taps = CONV_WIDTH - 1
    for s in range(CONV_CH // CONV_COLS):
        cols = slice(s * CONV_COLS, (s + 1) * CONV_COLS)
        if (s + 1) * CONV_COLS <= SSM_INNER:
            raw = xs_ref[:, cols]
        else:
            raw = bc_ref[:, s * CONV_COLS - SSM_INNER:(s + 1) * CONV_COLS - SSM_INNER]
        shifted = jnp.dot(shift_ref[...], raw, preferred_element_type=F32)
        raw32 = raw.astype(F32)
        acc = cb_ref[:, cols] + cw_ref[taps:taps + 1, cols] * raw32
        window = jnp.concatenate([carry_ref[:, cols], jnp.zeros((8, CONV_COLS), F32)], axis=0)
        head = jnp.zeros((8, CONV_COLS), F32)
        for k in range(taps):
            acc = acc + cw_ref[k:k + 1, cols] * shifted[k * CHUNK:(k + 1) * CHUNK]
            head = head + cw_ref[k:k + 1, cols] * window[8 - taps + k:16 - taps + k]
        acc = jnp.concatenate([acc[:8] + head, acc[8:]], axis=0)
        carry_ref[:, cols] = raw32[CHUNK - 8:]
        xc_ref[:, cols] = keep(acc * _sigmoid(acc))

    dtv = dt_ref[...] + dtb_ref[...]
    dt = jnp.maximum(dtv, 0.0) + jnp.log1p(jnp.exp(-jnp.abs(dtv)))
    dt = keep(dt)
    a = dt * (-jnp.exp(alog_ref[...]))
    ri = lax.broadcasted_iota(I32, (CHUNK, CHUNK), 0)
    ci = lax.broadcasted_iota(I32, (CHUNK, CHUNK), 1)
    causal = ri >= ci
    acs = jnp.dot(causal.astype(F32), a, preferred_element_type=F32, precision=HIGHEST)
    last = acs[CHUNK - 1:CHUNK, :]
    acs_t = jnp.transpose(acs)
    dt_t = jnp.transpose(dt)
    w_t = jnp.transpose(jnp.exp(last - acs) * dt)
    e_last = jnp.exp(last)
    e_acs = jnp.exp(acs)
    lanes = lax.broadcasted_iota(I32, (1, CHUNK), 1)
    low = lanes < SSM_HEAD_DIM

    for g in range(SSM_GROUPS):
        bm = xc_ref[:, SSM_INNER + g * SSM_STATE:SSM_INNER + (g + 1) * SSM_STATE]
        cm = xc_ref[:, SSM_INNER + (SSM_GROUPS + g) * SSM_STATE:SSM_INNER + (SSM_GROUPS + g + 1) * SSM_STATE]
        cb = _nt_dot(cm.astype(BF16), bm.astype(BF16))
        bm_t = jnp.transpose(bm)
        for pp in range(PAIRS_PER_GROUP):
            m = g * PAIRS_PER_GROUP + pp
            x_pair = xc_ref[:, m * CHUNK:(m + 1) * CHUNK]
            st_pair = st_ref[m]
            y_pair = jnp.zeros((CHUNK, CHUNK), F32)
            upd = jnp.zeros((SSM_STATE, CHUNK), F32)
            for hh in range(2):
                h = 2 * m + hh
                lane_mask = low if hh == 0 else jnp.logical_not(low)
                col = jnp.broadcast_to(acs[:, h:h + 1], (CHUNK, CHUNK))
                seg = col - acs_t[h:h + 1, :]
                dec = jnp.exp(jnp.where(causal, seg, -1e30))
                mat = cb * dec * dt_t[h:h + 1, :]
                from_start = jnp.broadcast_to(e_acs[:, h:h + 1], (CHUNK, CHUNK))
                lhs = jnp.concatenate([mat, cm * from_start], axis=1).astype(BF16)
                xm = jnp.where(lane_mask, x_pair, 0.0).astype(BF16)
                sm = jnp.where(lane_mask, st_pair, 0.0).astype(BF16)
                y_pair = y_pair + jnp.dot(lhs, jnp.concatenate([xm, sm], axis=0), preferred_element_type=F32)
                upd = upd + jnp.dot((bm_t * w_t[h:h + 1, :]).astype(BF16), xm, preferred_element_type=F32)
            decay = jnp.where(low, jnp.broadcast_to(e_last[:, 2 * m:2 * m + 1], (1, CHUNK)),
                              jnp.broadcast_to(e_last[:, 2 * m + 1:2 * m + 2], (1, CHUNK)))
            st_ref[m] = st_pair * decay + upd
            y_ref[:, m * CHUNK:(m + 1) * CHUNK] = y_pair

    gsz = SSM_INNER // SSM_GROUPS
    for g in range(SSM_GROUPS):
        cols = slice(g * gsz, (g + 1) * gsz)
        z = z_ref[:, cols].astype(F32)
        y = (y_ref[:, cols] + dsk_ref[:, cols] * xc_ref[:, cols]) * (z * _sigmoid(z))
        y = y * lax.rsqrt(jnp.mean(y * y, axis=-1, keepdims=True) + NORM_EPS)
        o_ref[:, cols] = (y * nw_ref[:, cols]).astype(BF16)


SSD_STATE = (HEAD_PAIRS, SSM_STATE, CHUNK)
CONV_TAIL = (8, CONV_CH)
N_SSD_IN = 13


def _ssd_io(proj, dt_raw, conv_w, conv_b, dt_bias, a_log, d_skip, norm_w, nc, init_state, init_tail):
    row = lambda b, c: b * nc + c
    const2 = lambda b, c: (0, 0)
    const3 = lambda b, c: (0, 0, 0)
    pad = lambda v: jnp.pad(v.astype(F32), (0, DT_PAD - SSM_HEADS)).reshape(1, DT_PAD)
    taps = CONV_WIDTH - 1
    shift = np.zeros((taps * CHUNK, CHUNK), np.float32)
    for k in range(taps):
        for r in range(taps - k, CHUNK):
            shift[k * CHUNK + r, r - (taps - k)] = 1.0
    in_specs = [
        pl.BlockSpec((CHUNK, SSM_INNER), lambda b, c: (row(b, c), _col_block(COL_Z, SSM_INNER))),
        pl.BlockSpec((CHUNK, SSM_INNER), lambda b, c: (row(b, c), _col_block(COL_X, SSM_INNER))),
        pl.BlockSpec((CHUNK, BC_WIDTH), lambda b, c: (row(b, c), _col_block(COL_BC, BC_WIDTH))),
        pl.BlockSpec((CHUNK, DT_PAD), lambda b, c: (row(b, c), 0)),
        pl.BlockSpec((taps * CHUNK, CHUNK), const2),
        pl.BlockSpec((CONV_WIDTH, CONV_CH), const2),
        pl.BlockSpec((1, CONV_CH), const2),
        pl.BlockSpec((1, DT_PAD), const2),
        pl.BlockSpec((1, DT_PAD), const2),
        pl.BlockSpec((1, SSM_INNER), const2),
        pl.BlockSpec((1, SSM_INNER), const2),
        pl.BlockSpec(SSD_STATE, const3),
        pl.BlockSpec(CONV_TAIL, const2),
    ]
    assert len(in_specs) == N_SSD_IN
    args = (proj, proj, proj, dt_raw, jnp.asarray(shift, BF16), conv_w.astype(F32), conv_b.astype(F32).reshape(1, CONV_CH),
            pad(dt_bias), pad(a_log), jnp.repeat(d_skip.astype(F32), SSM_HEAD_DIM).reshape(1, SSM_INNER),
            norm_w.astype(F32).reshape(1, SSM_INNER), init_state, init_tail)
    return in_specs, args


def _mixers_body(chunk_dec, pad_rows, emit_state, *refs):
    ret_in, refs = refs[:N_RET_IN], refs[N_RET_IN:]
    ssd_in, refs = refs[:N_SSD_IN], refs[N_SSD_IN:]
    n_out = 5 if emit_state else 2
    outs, scratch = refs[:n_out], refs[n_out:]
    states = scratch[:3]
    inits = (ret_in[-1], ssd_in[-2], ssd_in[-1])
    c = pl.program_id(1)

    @pl.when(c == 0)
    def _():
        for state, init in zip(states, inits):
            state[...] = init[...]

    _ret_body(chunk_dec, *ret_in[:-1], outs[0], scratch[0])
    _ssd_body(pad_rows, *ssd_in[:-2], outs[1], *scratch[1:])

    if emit_state:
        @pl.when(c == pl.num_programs(1) - 1)
        def _():
            for final, state in zip(outs[2:], states):
                final[...] = state[...]


def _mixers(proj, dt_raw, ssd_params, bsz, nc, first_pos, pad_rows, init, emit_state):
    t = proj.shape[0]
    ret_specs, ret_args, chunk_dec = _retention_io(proj, nc, first_pos, init[0])
    ssd_specs, ssd_args = _ssd_io(proj, dt_raw, *ssd_params, nc, init[1], init[2])
    row = lambda b, c: b * nc + c
    out_specs = [pl.BlockSpec((CHUNK, RET_V), lambda b, c: (row(b, c), 0)),
                 pl.BlockSpec((CHUNK, SSM_INNER), lambda b, c: (row(b, c), 0))]
    out_shape = [jax.ShapeDtypeStruct((t, RET_V), BF16), jax.ShapeDtypeStruct((t, SSM_INNER), BF16)]
    if emit_state:
        out_specs += [pl.BlockSpec(RET_STATE, lambda b, c: (0, 0, 0)), pl.BlockSpec(SSD_STATE, lambda b, c: (0, 0, 0)),
                      pl.BlockSpec(CONV_TAIL, lambda b, c: (0, 0))]
        out_shape += [jax.ShapeDtypeStruct(s, F32) for s in (RET_STATE, SSD_STATE, CONV_TAIL)]
    return pl.pallas_call(
        functools.partial(_mixers_body, chunk_dec, pad_rows, emit_state),
        grid=(bsz, nc),
        in_specs=ret_specs + ssd_specs,
        out_specs=out_specs,
        out_shape=out_shape,
        scratch_shapes=[
            pltpu.VMEM(RET_STATE, F32),
            pltpu.VMEM(SSD_STATE, F32),
            pltpu.VMEM(CONV_TAIL, F32),
            pltpu.VMEM((CHUNK, CONV_CH), F32),
            pltpu.VMEM((CHUNK, SSM_INNER), F32),
        ],
        compiler_params=_params(("parallel", "arbitrary")),
        name="mixers_meta" if emit_state else "mixers",
    )(*ret_args, *ssd_args)


def _merge_body(oret_ref, ossd_ref, g0_ref, g1_ref, h_ref, wb0_ref, wb1_ref, wo_ref, nw_ref, rwh_ref, rwl_ref,
                rb_ref, h1_ref, un_ref, e_ref, w_ref):
    y_ret = jnp.dot(oret_ref[...], wb0_ref[...], preferred_element_type=F32)
    y_ssd = jnp.dot(ossd_ref[...], wb1_ref[...], preferred_element_type=F32)
    merged = (_sigmoid(g0_ref[...].astype(F32)) * y_ret + _sigmoid(g1_ref[...].astype(F32)) * y_ssd)
    h1 = h_ref[...] + jnp.dot(merged.astype(BF16), wo_ref[...], preferred_element_type=F32)
    h1_ref[...] = h1
    un = h1 * lax.rsqrt(jnp.mean(h1 * h1, axis=-1, keepdims=True) + NORM_EPS) * nw_ref[...]
    un_hi = un.astype(BF16)
    un_ref[...] = un_hi

    un_lo = (un - un_hi.astype(F32)).astype(BF16)
    logits = (jnp.dot(un_hi, rwh_ref[...], preferred_element_type=F32)
              + jnp.dot(un_lo, rwh_ref[...], preferred_element_type=F32)
              + jnp.dot(un_hi, rwl_ref[...], preferred_element_type=F32)) + rb_ref[...]
    logits = jnp.transpose(logits)[:N_EXPERTS, :]
    tm = logits.shape[1]
    eidx = lax.broadcasted_iota(I32, (N_EXPERTS, tm), 0)
    vals, ids = [], []
    for _ in range(TOP_K):
        best = jnp.max(logits, axis=0, keepdims=True)
        arg = jnp.min(jnp.where(logits == best, eidx, N_EXPERTS), axis=0, keepdims=True)
        vals.append(best)
        ids.append(arg)
        logits = jnp.where(eidx == arg, -jnp.inf, logits)
    ex = [jnp.exp(v - vals[0]) for v in vals]
    denom = ex[0] + ex[1] + ex[2] + ex[3]
    e_ref[...] = jnp.concatenate(ids + [jnp.zeros((8 - TOP_K, tm), I32)], axis=0)
    w_ref[...] = jnp.concatenate([x / denom for x in ex] + [jnp.zeros((8 - TOP_K, tm), F32)], axis=0)


def _merge(o_ret, o_ssd, proj, h, wb0, wb1, wo, norm_w, router_w, router_b):
    t = h.shape[0]
    tm = _pick(t, (512, 256, 128))
    gate_blk = _col_block(OFF_DT, D_MODEL)
    const2 = lambda i: (0, 0)
    rw_hi = router_w.astype(BF16)
    rw_lo = (router_w - rw_hi.astype(F32)).astype(BF16)
    return pl.pallas_call(
        _merge_body,
        grid=(t // tm,),
        in_specs=[
            pl.BlockSpec((tm, RET_V), lambda i: (i, 0)),
            pl.BlockSpec((tm, SSM_INNER), lambda i: (i, 0)),
            pl.BlockSpec((tm, D_MODEL), lambda i: (i, gate_blk)),
            pl.BlockSpec((tm, D_MODEL), lambda i: (i, gate_blk + 1)),
            pl.BlockSpec((tm, D_MODEL), lambda i: (i, 0)),
            pl.BlockSpec((RET_V, D_MODEL), const2),
            pl.BlockSpec((SSM_INNER, D_MODEL), const2),
            pl.BlockSpec((D_MODEL, D_MODEL), const2),
            pl.BlockSpec((1, D_MODEL), const2),
            pl.BlockSpec((D_MODEL, CHUNK), const2),
            pl.BlockSpec((D_MODEL, CHUNK), const2),
            pl.BlockSpec((1, CHUNK), const2),
        ],
        out_specs=[
            pl.BlockSpec((tm, D_MODEL), lambda i: (i, 0)),
            pl.BlockSpec((tm, D_MODEL), lambda i: (i, 0)),
            pl.BlockSpec((8, tm), lambda i: (0, i)),
            pl.BlockSpec((8, tm), lambda i: (0, i)),
        ],
        out_shape=[
            jax.ShapeDtypeStruct((t, D_MODEL), F32),
            jax.ShapeDtypeStruct((t, D_MODEL), BF16),
            jax.ShapeDtypeStruct((8, t), I32),
            jax.ShapeDtypeStruct((8, t), F32),
        ],
        compiler_params=_params(("parallel",)),
        name="merge_router",
    )(o_ret, o_ssd, proj, proj, h, wb0, wb1, wo, norm_w, rw_hi, rw_lo, router_b)


ROW_ALIGN = 8
PIECE = 16
LOCAL_ROWS = 1024
META_PIECES, META_EXPERT, META_LOCAL, META_SORTED = 0, 1, 2, 3
MAX_PIECES = CHUNK * TOP_K // PIECE + N_EXPERTS
assert MAX_PIECES <= CHUNK


def _rank_body(tiles, e_ref, slot_ref, meta_ref, cnt_ref, run_ref):
    @pl.when(pl.program_id(0) == 0)
    def _():
        run_ref[...] = jnp.zeros_like(run_ref)

    eidx = lax.broadcasted_iota(I32, (N_EXPERTS, CHUNK), 0)
    lane = lax.broadcasted_iota(I32, (N_EXPERTS, CHUNK), 1)
    ri = lax.broadcasted_iota(I32, (CHUNK, CHUNK), 0)
    ci = lax.broadcasted_iota(I32, (CHUNK, CHUNK), 1)
    upper = jnp.where(ri <= ci, 1.0, 0.0).astype(BF16)
    ones = jnp.ones((CHUNK, CHUNK), BF16)
    ei = lax.broadcasted_iota(I32, (N_EXPERTS, N_EXPERTS), 0)
    ej = lax.broadcasted_iota(I32, (N_EXPERTS, N_EXPERTS), 1)
    lower = jnp.where(ej < ei, 1.0, 0.0).astype(BF16)
    round_up = lambda v, m: jnp.floor((v + (m - 1)) * (1.0 / m)) * m

    run = run_ref[...]
    for s in range(tiles):
        e_tile = e_ref[:, s * CHUNK:(s + 1) * CHUNK]
        seen = jnp.zeros((N_EXPERTS, CHUNK), F32)
        hits, ranks = [], []
        for k in range(TOP_K):
            hit = eidx == e_tile[k:k + 1, :]
            oh = jnp.where(hit, 1.0, 0.0).astype(BF16)
            ranks.append(seen + jnp.dot(oh, upper, preferred_element_type=F32) - 1.0)
            seen = seen + jnp.dot(oh, ones, preferred_element_type=F32)
            hits.append(hit)
        count = seen
        lstart = jnp.dot(lower, round_up(count, PIECE).astype(BF16), preferred_element_type=F32)
        slots = []
        for k in range(TOP_K):
            slots.append(jnp.sum(jnp.where(hits[k], lstart + ranks[k], 0.0), axis=0, keepdims=True))
        slot_ref[:, s * CHUNK:(s + 1) * CHUNK] = jnp.concatenate(
            slots + [jnp.full((8 - TOP_K, CHUNK), -1.0, F32)], axis=0).astype(I32)
        pieces = round_up(count, PIECE) * (1.0 / PIECE)
        first = jnp.dot(lower, pieces.astype(BF16), preferred_element_type=F32)
        nth = lane.astype(F32) - first
        mine = jnp.logical_and(nth >= 0.0, nth < pieces)
        pick = lambda v: jnp.sum(jnp.where(mine, v, 0.0), axis=0, keepdims=True)
        meta_ref[s] = jnp.concatenate(
            [jnp.sum(pieces, axis=0, keepdims=True), pick(eidx.astype(F32)), pick(lstart + PIECE * nth),
             pick(run + PIECE * nth), jnp.zeros((4, CHUNK), F32)], axis=0).astype(I32)
        run = run + round_up(count, ROW_ALIGN)
    run_ref[...] = run
    cnt_ref[...] = run.astype(I32)


def _rank(top_e):
    t = top_e.shape[1]
    n_tiles = t // CHUNK
    tiles = _pick(n_tiles, (8, 4, 2, 1))
    return pl.pallas_call(
        functools.partial(_rank_body, tiles),
        grid=(n_tiles // tiles,),
        in_specs=[pl.BlockSpec((8, tiles * CHUNK), lambda i: (0, i))],
        out_specs=[
            pl.BlockSpec((8, tiles * CHUNK), lambda i: (0, i)),
            pl.BlockSpec((tiles, 8, CHUNK), lambda i: (i, 0, 0)),
            pl.BlockSpec((N_EXPERTS, CHUNK), lambda i: (0, 0)),
        ],
        out_shape=[
            jax.ShapeDtypeStruct((8, t), I32),
            jax.ShapeDtypeStruct((n_tiles, 8, CHUNK), I32),
            jax.ShapeDtypeStruct((N_EXPERTS, CHUNK), I32),
        ],
        scratch_shapes=[pltpu.VMEM((N_EXPERTS, CHUNK), F32)],
        compiler_params=_params(("arbitrary",)),
        name="rank",
    )(top_e)


def _pack_halves(v):
    bits = pltpu.bitcast(v, jnp.uint32)
    return (bits[:, :D_MODEL // 2] >> 16) | bits[:, D_MODEL // 2:]


def _unpack_halves(p):
    lo = pltpu.bitcast(p << 16, F32)
    hi = pltpu.bitcast(p & jnp.uint32(0xFFFF0000), F32)
    return jnp.concatenate([lo, hi], axis=1).astype(BF16)


def _wait_pieces(meta_ref, make_copy):
    def body(j, carry):
        make_copy(0, 0).wait()
        return carry

    lax.fori_loop(0, meta_ref[0, META_PIECES, 0], body, 0)


def _start_pieces(pstart_ref, meta_ref, make_copy):
    def body(p, carry):
        sorted_row = pstart_ref[meta_ref[0, META_EXPERT, p]] + meta_ref[0, META_SORTED, p]
        make_copy(pl.multiple_of(meta_ref[0, META_LOCAL, p], ROW_ALIGN), pl.multiple_of(sorted_row, ROW_ALIGN)).start()
        return carry

    lax.fori_loop(0, meta_ref[0, META_PIECES, 0], body, 0)


def _dispatch_body(pstart_ref, fill_ref, meta_ref, prev_meta_ref, slot_ref, un_ref, xs_ref, loc_ref, sems):
    step = pl.program_id(0)
    last = pl.num_programs(0) - 1
    buf = step % 2

    srow = lax.broadcasted_iota(I32, (LOCAL_ROWS, CHUNK), 0)
    onehot = jnp.zeros((LOCAL_ROWS, CHUNK), F32)
    for k in range(TOP_K):
        onehot = jnp.where(srow == slot_ref[k:k + 1, :], 1.0, onehot)
    loc_ref[buf] = _pack_halves(jnp.dot(onehot.astype(BF16), un_ref[...], preferred_element_type=F32))

    def copies_of(b):
        def make_copy(local_row, sorted_row):
            return pltpu.make_async_copy(loc_ref.at[b, pl.ds(local_row, PIECE), :],
                                         xs_ref.at[pl.ds(sorted_row, PIECE), :], sems.at[b])
        return make_copy

    @pl.when(step > 0)
    def _():
        _wait_pieces(prev_meta_ref, copies_of(1 - buf))

    _start_pieces(pstart_ref, meta_ref, copies_of(buf))

    @pl.when(step == last)
    def _():
        _wait_pieces(meta_ref, copies_of(buf))
        loc_ref[0] = jnp.zeros((LOCAL_ROWS, D_MODEL // 2), jnp.uint32)

        def zero_rows(first, n_rows):
            return pltpu.make_async_copy(loc_ref.at[0, pl.ds(0, n_rows), :],
                                         xs_ref.at[pl.ds(pl.multiple_of(first, ROW_ALIGN), n_rows), :], sems.at[0])

        def fill(act):
            def per_expert(e, carry):
                def per_piece(j, c):
                    act(zero_rows(fill_ref[0, e] + j * ROW_ALIGN, ROW_ALIGN))
                    return c
                return lax.fori_loop(0, fill_ref[1, e], per_piece, carry)

            def per_block(j, c):
                act(zero_rows(fill_ref[2, 0] + j * LOCAL_ROWS, LOCAL_ROWS))
                return c

            lax.fori_loop(0, N_EXPERTS, per_expert, 0)
            lax.fori_loop(0, fill_ref[2, 1], per_block, 0)

        fill(lambda cp: cp.start())
        fill(lambda cp: cp.wait())


def _dispatch(pstart, fill, meta, slot, un, n_rows):
    t = un.shape[0]
    return pl.pallas_call(
        _dispatch_body,
        grid_spec=pltpu.PrefetchScalarGridSpec(
            num_scalar_prefetch=2,
            grid=(t // CHUNK,),
            in_specs=[
                pl.BlockSpec((1, 8, CHUNK), lambda i, ps, fl: (i, 0, 0), memory_space=pltpu.SMEM),
                pl.BlockSpec((1, 8, CHUNK), lambda i, ps, fl: (jnp.maximum(i - 1, 0), 0, 0), memory_space=pltpu.SMEM),
                pl.BlockSpec((8, CHUNK), lambda i, ps, fl: (0, i)),
                pl.BlockSpec((CHUNK, D_MODEL), lambda i, ps, fl: (i, 0)),
            ],
            out_specs=pl.BlockSpec(memory_space=pl.ANY),
            scratch_shapes=[pltpu.VMEM((2, LOCAL_ROWS, D_MODEL // 2), jnp.uint32), pltpu.SemaphoreType.DMA((2,))],
        ),
        out_shape=jax.ShapeDtypeStruct((n_rows, D_MODEL // 2), jnp.uint32),
        compiler_params=_params(("arbitrary",)),
        name="dispatch",
    )(pstart, fill, meta, meta, slot, un)


GU_SLAB = 256
GU_HALF = GU_SLAB // 2


def _regroup_body(w_ref, p_ref, o_ref):
    for s in range(2 * EXPERT_FF // GU_SLAB):
        cols = slice(s * GU_SLAB, (s + 1) * GU_SLAB)
        o_ref[0, :, cols] = jnp.dot(w_ref[0, :, cols].astype(BF16), p_ref[...], preferred_element_type=F32).astype(BF16)


def _regroup_gate_up(w_gate_up):
    perm = np.zeros((GU_SLAB, GU_SLAB), np.float32)
    perm[2 * np.arange(GU_HALF), np.arange(GU_HALF)] = 1.0
    perm[2 * np.arange(GU_HALF) + 1, GU_HALF + np.arange(GU_HALF)] = 1.0
    blk = (1, D_MODEL, 2 * EXPERT_FF)
    return pl.pallas_call(
        _regroup_body,
        grid=(N_EXPERTS,),
        in_specs=[pl.BlockSpec(blk, lambda e: (e, 0, 0)), pl.BlockSpec((GU_SLAB, GU_SLAB), lambda e: (0, 0))],
        out_specs=pl.BlockSpec(blk, lambda e: (e, 0, 0)),
        out_shape=jax.ShapeDtypeStruct((N_EXPERTS, D_MODEL, 2 * EXPERT_FF), BF16),
        compiler_params=_params(("parallel",)),
        name="regroup_gate_up",
    )(w_gate_up, jnp.asarray(perm, BF16))


def _regroup_bias(b_gate_up):
    b = b_gate_up.astype(F32).reshape(N_EXPERTS, 2 * EXPERT_FF // GU_SLAB, GU_HALF, 2)
    return jnp.transpose(b, (0, 1, 3, 2)).reshape(N_EXPERTS, 1, 2 * EXPERT_FF)


def _expert_body(be_ref, nu_ref, x_ref, wgu_ref, wd_ref, bgu_ref, bd_ref, y_ref):
    del be_ref

    @pl.when(pl.program_id(0) >= nu_ref[0])
    def _():
        y_ref[...] = jnp.zeros_like(y_ref)

    @pl.when(pl.program_id(0) < nu_ref[0])
    def _():
        x = _unpack_halves(x_ref[...])
        gu = jnp.dot(x, wgu_ref[0], preferred_element_type=F32) + bgu_ref[0]
        acts = []
        for s in range(2 * EXPERT_FF // GU_SLAB):
            gate = jnp.minimum(gu[:, s * GU_SLAB:s * GU_SLAB + GU_HALF], SWIGLU_LIMIT)
            up = jnp.clip(gu[:, s * GU_SLAB + GU_HALF:(s + 1) * GU_SLAB], -SWIGLU_LIMIT, SWIGLU_LIMIT)
            acts.append(((up + 1.0) * (gate * _sigmoid(SWIGLU_ALPHA * gate))).astype(BF16))
        act = jnp.concatenate(acts, axis=1)
        y = jnp.dot(act, wd_ref[0], preferred_element_type=F32) + bd_ref[0]
        y_ref[...] = _pack_halves(y.astype(BF16).astype(F32))


def _experts(block_e, n_used, xs, wgu, wd, bgu, bd):
    n_rows = xs.shape[0]
    n_blocks = n_rows // MOE_BLOCK
    wspec = lambda shape: pl.BlockSpec((1,) + shape, lambda i, be, nu: (be[i], 0, 0))
    return pl.pallas_call(
        _expert_body,
        grid_spec=pltpu.PrefetchScalarGridSpec(
            num_scalar_prefetch=2,
            grid=(n_blocks,),
            in_specs=[
                pl.BlockSpec((MOE_BLOCK, D_MODEL // 2), lambda i, be, nu: (jnp.minimum(i, nu[0] - 1), 0)),
                wspec((D_MODEL, 2 * EXPERT_FF)),
                wspec((EXPERT_FF, D_MODEL)),
                wspec((1, 2 * EXPERT_FF)),
                wspec((1, D_MODEL)),
            ],
            out_specs=pl.BlockSpec((MOE_BLOCK, D_MODEL // 2), lambda i, be, nu: (i, 0)),
        ),
        out_shape=jax.ShapeDtypeStruct((n_rows, D_MODEL // 2), jnp.uint32),
        compiler_params=_params(("arbitrary",)),
        name="experts",
    )(block_e, n_used, xs, wgu, wd, bgu, bd)


def _combine_body(pstart_ref, meta_ref, next_meta_ref, slot_ref, w_ref, h1_ref, nw_ref, y_hbm, o_ref, ybuf, sems):
    step = pl.program_id(0)
    buf = step % 2

    def copies_of(b):
        def make_copy(local_row, sorted_row):
            return pltpu.make_async_copy(y_hbm.at[pl.ds(sorted_row, PIECE), :],
                                         ybuf.at[b, pl.ds(local_row, PIECE), :], sems.at[b])
        return make_copy

    @pl.when(step == 0)
    def _():
        ybuf[...] = jnp.zeros_like(ybuf)
        _start_pieces(pstart_ref, meta_ref, copies_of(buf))

    @pl.when(step + 1 < pl.num_programs(0))
    def _():
        _start_pieces(pstart_ref, next_meta_ref, copies_of(1 - buf))

    _wait_pieces(meta_ref, copies_of(buf))

    srow = lax.broadcasted_iota(I32, (LOCAL_ROWS, CHUNK), 0)
    wmat = jnp.zeros((LOCAL_ROWS, CHUNK), F32)
    for k in range(TOP_K):
        wmat = jnp.where(srow == slot_ref[k:k + 1, :], w_ref[k:k + 1, :], wmat)
    wmat = jnp.transpose(wmat)
    w_hi = wmat.astype(BF16)
    w_lo = (wmat - w_hi.astype(F32)).astype(BF16)
    y = _unpack_halves(ybuf[buf])
    f = jnp.dot(w_hi, y, preferred_element_type=F32) + jnp.dot(w_lo, y, preferred_element_type=F32)
    h2 = h1_ref[...] + f
    o_ref[...] = h2 * lax.rsqrt(jnp.mean(h2 * h2, axis=-1, keepdims=True) + NORM_EPS) * nw_ref[...]


def _combine(pstart, meta, slot, top_w, h1, norm_w, y_sorted):
    t = h1.shape[0]
    n_tiles = t // CHUNK
    nxt = lambda i: jnp.minimum(i + 1, n_tiles - 1)
    return pl.pallas_call(
        _combine_body,
        grid_spec=pltpu.PrefetchScalarGridSpec(
            num_scalar_prefetch=1,
            grid=(n_tiles,),
            in_specs=[
                pl.BlockSpec((1, 8, CHUNK), lambda i, ps: (i, 0, 0), memory_space=pltpu.SMEM),
                pl.BlockSpec((1, 8, CHUNK), lambda i, ps: (nxt(i), 0, 0), memory_space=pltpu.SMEM),
                pl.BlockSpec((8, CHUNK), lambda i, ps: (0, i)),
                pl.BlockSpec((8, CHUNK), lambda i, ps: (0, i)),
                pl.BlockSpec((CHUNK, D_MODEL), lambda i, ps: (i, 0)),
                pl.BlockSpec((1, D_MODEL), lambda i, ps: (0, 0)),
                pl.BlockSpec(memory_space=pl.ANY),
            ],
            out_specs=pl.BlockSpec((CHUNK, D_MODEL), lambda i, ps: (i, 0)),
            scratch_shapes=[pltpu.VMEM((2, LOCAL_ROWS, D_MODEL // 2), jnp.uint32), pltpu.SemaphoreType.DMA((2,))],
        ),
        out_shape=jax.ShapeDtypeStruct((t, D_MODEL), F32),
        compiler_params=_params(("arbitrary",)),
        name="combine",
    )(pstart, meta, meta, slot, top_w, h1, norm_w, y_sorted)


def kernel(x, meta_tokens, norm_mix, w_in, conv_w, conv_b, dt_bias, a_log, d_skip, ssm_norm, w_branch, w_out, norm_ffn,
           router_w, router_b, w_gate_up, b_gate_up, w_down, b_down, norm_final):
    bsz, seq, _ = x.shape
    assert seq % CHUNK == 0 and norm_mix.shape[0] == 1
    nc = seq // CHUNK
    t = bsz * seq
    x2d = x.reshape(t, D_MODEL)
    meta_chunk = jnp.concatenate([jnp.zeros((META_PAD, D_MODEL), x.dtype), meta_tokens.astype(x.dtype)], axis=0)

    w_in0 = w_in[0]
    w_main = jnp.concatenate([w_in0[:, :OFF_DT], w_in0[:, OFF_DT + SSM_HEADS:]], axis=1).astype(BF16)
    w_dt = jnp.pad(w_in0[:, OFF_DT:OFF_DT + SSM_HEADS], ((0, 0), (0, DT_PAD - SSM_HEADS))).astype(BF16)
    wgu = _regroup_gate_up(w_gate_up[0])
    bgu = _regroup_bias(b_gate_up[0])
    wd = w_down[0].astype(BF16)
    bd = b_down[0][:, None, :].astype(F32)

    norm_mix_w = norm_mix[0].reshape(1, D_MODEL).astype(F32)
    ssd_params = (conv_w[0], conv_b[0], dt_bias[0], a_log[0], d_skip[0], ssm_norm[0])
    proj_m, dt_m = _in_proj(meta_chunk, norm_mix_w, w_main, w_dt)
    zero_states = tuple(jnp.zeros(s, F32) for s in (RET_STATE, SSD_STATE, CONV_TAIL))
    meta_states = _mixers(proj_m, dt_m, ssd_params, 1, 1, -META_PAD, META_PAD, zero_states, True)[2:]

    proj, dt_raw = _in_proj(x2d, norm_mix_w, w_main, w_dt)
    o_ret, o_ssd = _mixers(proj, dt_raw, ssd_params, bsz, nc, N_META, 0, meta_states, False)
    h1, un, top_e, top_w = _merge(
        o_ret, o_ssd, proj, x2d, w_branch[0, 0].astype(BF16), w_branch[0, 1].astype(BF16), w_out[0].astype(BF16),
        norm_ffn[0].reshape(1, D_MODEL).astype(F32),
        jnp.pad(router_w[0].astype(F32), ((0, 0), (0, CHUNK - N_EXPERTS))),
        jnp.pad(router_b[0].astype(F32), (0, CHUNK - N_EXPERTS)).reshape(1, CHUNK))

    slot, meta, counts = _rank(top_e)
    counts = counts[:, 0]
    slack = MOE_BLOCK - 1 + PIECE - ROW_ALIGN
    padded = (counts + slack) // MOE_BLOCK * MOE_BLOCK
    pend = jnp.cumsum(padded)
    pstart = (pend - padded).astype(I32)
    n_pairs = t * TOP_K
    max_rows = n_pairs + bsz * nc * N_EXPERTS * (ROW_ALIGN - 1)
    n_blocks = (max_rows + N_EXPERTS * slack) // MOE_BLOCK
    blk_row = jnp.arange(n_blocks, dtype=I32) * MOE_BLOCK
    block_e = jnp.minimum(jnp.sum(pend[None, :] <= blk_row[:, None], axis=1), N_EXPERTS - 1).astype(I32)
    n_used = (pend[-1:] // MOE_BLOCK).astype(I32)
    assert MOE_BLOCK % LOCAL_ROWS == 0
    tail = jnp.zeros((N_EXPERTS,), I32).at[0].set(pend[-1]).at[1].set((n_blocks * MOE_BLOCK - pend[-1]) // LOCAL_ROWS)
    fill = jnp.stack([pstart + counts, (padded - counts) // ROW_ALIGN, tail]).astype(I32)

    xs = _dispatch(pstart, fill, meta, slot, un, n_blocks * MOE_BLOCK)
    ys = _experts(block_e, n_used, xs, wgu, wd, bgu, bd)
    out = _combine(pstart, meta, slot, top_w, h1, norm_final.reshape(1, D_MODEL).astype(F32), ys)
    return out.reshape(bsz, seq, D_MODEL)
```

```python
import functools

import numpy as np
import jax
import jax.numpy as jnp
from jax import lax
from jax.experimental import pallas as pl
from jax.experimental.pallas import tpu as pltpu

F32 = jnp.float32
BF16 = jnp.bfloat16
I32 = jnp.int32

D_MODEL = 1024
N_META = 16
CHUNK = 128
META_PAD = CHUNK - N_META
NORM_EPS = 1e-6
RET_HEADS = 4
RET_QK_DIM = 256
RET_V_DIM = 512
RET_QK = RET_HEADS * RET_QK_DIM
RET_V = RET_HEADS * RET_V_DIM
ROPE_BASE = 10000.0
SSM_INNER = 2048
SSM_HEAD_DIM = 64
SSM_HEADS = 32
SSM_GROUPS = 4
SSM_STATE = 128
CONV_WIDTH = 4
CONV_CH = SSM_INNER + 2 * SSM_GROUPS * SSM_STATE
HEAD_PAIRS = SSM_HEADS // 2
PAIRS_PER_GROUP = HEAD_PAIRS // SSM_GROUPS
COL_Q = 0
COL_K = COL_Q + RET_QK
COL_V = COL_K + RET_QK
COL_G = COL_V + RET_V
COL_Z = COL_G + RET_V
COL_X = COL_Z + SSM_INNER
COL_BC = COL_X + SSM_INNER
BC_WIDTH = 2 * SSM_GROUPS * SSM_STATE
OFF_DT = COL_BC + BC_WIDTH
N_MAIN = OFF_DT + 2 * D_MODEL
DT_PAD = 128
N_EXPERTS = 32
TOP_K = 4
EXPERT_FF = 1024
SWIGLU_ALPHA = 1.702
SWIGLU_LIMIT = 7.0
MOE_BLOCK = 1024

VMEM_LIMIT = 56 * 1024 * 1024
HIGHEST = lax.Precision.HIGHEST


def _params(sem):
    return pltpu.CompilerParams(dimension_semantics=sem, vmem_limit_bytes=VMEM_LIMIT)


def _pick(n, candidates):
    for c in candidates:
        if n % c == 0:
            return c
    raise ValueError(f"no tile for {n} among {candidates}")


def _sigmoid(x):
    return 0.5 * jnp.tanh(0.5 * x) + 0.5


def _nt_dot(a, b):
    return lax.dot_general(a, b, (((1,), (1,)), ((), ())), preferred_element_type=F32)


def _col_block(offset, width):
    assert offset % width == 0
    return offset // width


def _inproj_body(x_ref, nw_ref, w_ref, wdt_ref, o_ref, dt_ref, xn_ref):
    @pl.when(pl.program_id(1) == 0)
    def _():
        x = x_ref[...]
        ms = jnp.mean(x * x, axis=-1, keepdims=True)
        xn = (x * lax.rsqrt(ms + NORM_EPS) * nw_ref[...]).astype(BF16)
        xn_ref[...] = xn
        dt_ref[...] = jnp.dot(xn, wdt_ref[...], preferred_element_type=F32)

    o_ref[...] = jnp.dot(xn_ref[...], w_ref[...], preferred_element_type=F32).astype(BF16)


def _in_proj(h, norm_w, w_main, w_dt):
    t = h.shape[0]
    tm = _pick(t, (2048, 1024, 512, 256, 128))
    tn = 1024
    return pl.pallas_call(
        _inproj_body,
        grid=(t // tm, N_MAIN // tn),
        in_specs=[
            pl.BlockSpec((tm, D_MODEL), lambda i, j: (i, 0)),
            pl.BlockSpec((1, D_MODEL), lambda i, j: (0, 0)),
            pl.BlockSpec((D_MODEL, tn), lambda i, j: (0, j)),
            pl.BlockSpec((D_MODEL, DT_PAD), lambda i, j: (0, 0)),
        ],
        out_specs=[
            pl.BlockSpec((tm, tn), lambda i, j: (i, j)),
            pl.BlockSpec((tm, DT_PAD), lambda i, j: (i, 0)),
        ],
        out_shape=[jax.ShapeDtypeStruct((t, N_MAIN), BF16), jax.ShapeDtypeStruct((t, DT_PAD), F32)],
        scratch_shapes=[pltpu.VMEM((tm, D_MODEL), BF16)],
        compiler_params=_params(("parallel", "arbitrary")),
        name="in_proj",
    )(h, norm_w, w_main, w_dt)


def _ret_tables(n_rows, first_pos):
    half = RET_QK_DIM // 2
    inv_freq = ROPE_BASE ** (-np.arange(half, dtype=np.float64) / half)
    pos = np.arange(n_rows, dtype=np.float64) + first_pos
    ang = pos[:, None] * inv_freq[None, :]
    log_gamma = np.log(1.0 - 2.0 ** (-5.0 - np.arange(RET_HEADS, dtype=np.float64)))
    idx = np.arange(CHUNK, dtype=np.float64)
    dist = idx[:, None] - idx[None, :]
    intra = np.where(dist >= 0, np.exp(log_gamma[:, None, None] * np.maximum(dist, 0.0)[None]), 0.0)
    q_dec = np.exp(log_gamma[:, None] * (idx[None, :] + 1.0))
    k_dec = np.exp(log_gamma[:, None] * (CHUNK - 1.0 - idx[None, :]))
    q_dec = np.broadcast_to(q_dec[:, :, None], (RET_HEADS, CHUNK, RET_V_DIM))
    k_dec = np.broadcast_to(k_dec[:, :, None], (RET_HEADS, CHUNK, RET_QK_DIM))
    chunk_dec = tuple(float(v) for v in np.exp(log_gamma * CHUNK))
    as32 = lambda a: jnp.asarray(np.ascontiguousarray(a), F32)
    return as32(np.cos(ang)), as32(np.sin(ang)), as32(intra), as32(q_dec), as32(k_dec), chunk_dec


def _ret_body(chunk_dec, q_ref, k_ref, v_ref, g_ref, cos_ref, sin_ref, intra_ref, qd_ref, kd_ref, o_ref, st_ref):
    cos = cos_ref[...]
    sin = sin_ref[...]
    half = RET_QK_DIM // 2

    def rotary(t):
        t1, t2 = t[:, :half], t[:, half:]
        return jnp.concatenate([t1 * cos - t2 * sin, t2 * cos + t1 * sin], axis=1)

    for h in range(RET_HEADS):
        qk = slice(h * RET_QK_DIM, (h + 1) * RET_QK_DIM)
        vv = slice(h * RET_V_DIM, (h + 1) * RET_V_DIM)
        qr = rotary(q_ref[:, qk].astype(F32))
        kr = rotary(k_ref[:, qk].astype(F32)) * (RET_QK_DIM ** -0.5)
        qb = qr.astype(BF16)
        vh = v_ref[:, vv]
        st = st_ref[h]
        s = _nt_dot(qb, kr.astype(BF16)) * intra_ref[h]
        y = jnp.dot(s.astype(BF16), vh, preferred_element_type=F32)
        y = y + jnp.dot(qb, st.astype(BF16), preferred_element_type=F32) * qd_ref[h]
        kdt = jnp.transpose(kr * kd_ref[h]).astype(BF16)
        st_ref[h] = st * chunk_dec[h] + jnp.dot(kdt, vh, preferred_element_type=F32)
        o = y * lax.rsqrt(jnp.mean(y * y, axis=-1, keepdims=True) + NORM_EPS)
        g = g_ref[:, vv].astype(F32)
        o_ref[:, vv] = (g * _sigmoid(g) * o).astype(BF16)


RET_STATE = (RET_HEADS, RET_QK_DIM, RET_V_DIM)
N_RET_IN = 10


def _retention_io(proj, nc, first_pos, init_state):
    cos, sin, intra, q_dec, k_dec, chunk_dec = _ret_tables(nc * CHUNK, first_pos)
    row = lambda b, c: b * nc + c
    const3 = lambda b, c: (0, 0, 0)
    in_specs = [
        pl.BlockSpec((CHUNK, RET_QK), lambda b, c: (row(b, c), _col_block(COL_Q, RET_QK))),
        pl.BlockSpec((CHUNK, RET_QK), lambda b, c: (row(b, c), _col_block(COL_K, RET_QK))),
        pl.BlockSpec((CHUNK, RET_V), lambda b, c: (row(b, c), _col_block(COL_V, RET_V))),
        pl.BlockSpec((CHUNK, RET_V), lambda b, c: (row(b, c), _col_block(COL_G, RET_V))),
        pl.BlockSpec((CHUNK, RET_QK_DIM // 2), lambda b, c: (c, 0)),
        pl.BlockSpec((CHUNK, RET_QK_DIM // 2), lambda b, c: (c, 0)),
        pl.BlockSpec((RET_HEADS, CHUNK, CHUNK), const3),
        pl.BlockSpec((RET_HEADS, CHUNK, RET_V_DIM), const3),
        pl.BlockSpec((RET_HEADS, CHUNK, RET_QK_DIM), const3),
        pl.BlockSpec(RET_STATE, const3),
    ]
    assert len(in_specs) == N_RET_IN
    return in_specs, (proj, proj, proj, proj, cos, sin, intra, q_dec, k_dec, init_state), chunk_dec


CONV_COLS = 256


def _ssd_body(pad_rows, z_ref, xs_ref, bc_ref, dt_ref, shift_ref, cw_ref, cb_ref, dtb_ref, alog_ref, dsk_ref, nw_ref,
              o_ref, st_ref, carry_ref, xc_ref, y_ref):
    c = pl.program_id(1)

    if pad_rows:
        rows = lax.broadcasted_iota(I32, (CHUNK, 1), 0)
        valid = jnp.logical_or(c > 0, rows >= pad_rows)
        keep = lambda v: jnp.where(valid, v, 0.0)
    else:
        keep = lambda v: v

    taps = CONV_WIDTH - 1
    for s in range(CONV_CH // CONV_COLS):
        cols = slice(s * CONV_COLS, (s + 1) * CONV_COLS)
        if (s + 1) * CONV_COLS <= SSM_INNER:
            raw = xs_ref[:, cols]
        else:
            raw = bc_ref[:, s * CONV_COLS - SSM_INNER:(s + 1) * CONV_COLS - SSM_INNER]
        shifted = jnp.dot(shift_ref[...], raw, preferred_element_type=F32)
        raw32 = raw.astype(F32)
        acc = cb_ref[:, cols] + cw_ref[taps:taps + 1, cols] * raw32
        window = jnp.concatenate([carry_ref[:, cols], jnp.zeros((8, CONV_COLS), F32)], axis=0)
        head = jnp.zeros((8, CONV_COLS), F32)
        for k in range(taps):
            acc = acc + cw_ref[k:k + 1, cols] * shifted[k * CHUNK:(k + 1) * CHUNK]
            head = head + cw_ref[k:k + 1, cols] * window[8 - taps + k:16 - taps + k]
        acc = jnp.concatenate([acc[:8] + head, acc[8:]], axis=0)
        carry_ref[:, cols] = raw32[CHUNK - 8:]
        xc_ref[:, cols] = keep(acc * _sigmoid(acc))

    dtv = dt_ref[...] + dtb_ref[...]
    dt = jnp.maximum(dtv, 0.0) + jnp.log1p(jnp.exp(-jnp.abs(dtv)))
    dt = keep(dt)
    a = dt * (-jnp.exp(alog_ref[...]))
    ri = lax.broadcasted_iota(I32, (CHUNK, CHUNK), 0)
    ci = lax.broadcasted_iota(I32, (CHUNK, CHUNK), 1)
    causal = ri >= ci
    acs = jnp.dot(causal.astype(F32), a, preferred_element_type=F32, precision=HIGHEST)
    last = acs[CHUNK - 1:CHUNK, :]
    acs_t = jnp.transpose(acs)
    dt_t = jnp.transpose(dt)
    w_t = jnp.transpose(jnp.exp(last - acs) * dt)
    e_last = jnp.exp(last)
    e_acs = jnp.exp(acs)
    lanes = lax.broadcasted_iota(I32, (1, CHUNK), 1)
    low = lanes < SSM_HEAD_DIM

    for g in range(SSM_GROUPS):
        bm = xc_ref[:, SSM_INNER + g * SSM_STATE:SSM_INNER + (g + 1) * SSM_STATE]
        cm = xc_ref[:, SSM_INNER + (SSM_GROUPS + g) * SSM_STATE:SSM_INNER + (SSM_GROUPS + g + 1) * SSM_STATE]
        cb = _nt_dot(cm.astype(BF16), bm.astype(BF16))
        bm_t = jnp.transpose(bm)
        for pp in range(PAIRS_PER_GROUP):
            m = g * PAIRS_PER_GROUP + pp
            x_pair = xc_ref[:, m * CHUNK:(m + 1) * CHUNK]
            st_pair = st_ref[m]
            y_pair = jnp.zeros((CHUNK, CHUNK), F32)
            upd = jnp.zeros((SSM_STATE, CHUNK), F32)
            for hh in range(2):
                h = 2 * m + hh
                lane_mask = low if hh == 0 else jnp.logical_not(low)
                col = jnp.broadcast_to(acs[:, h:h + 1], (CHUNK, CHUNK))
                seg = col - acs_t[h:h + 1, :]
                dec = jnp.exp(jnp.where(causal, seg, -1e30))
                mat = cb * dec * dt_t[h:h + 1, :]
                from_start = jnp.broadcast_to(e_acs[:, h:h + 1], (CHUNK, CHUNK))
                lhs = jnp.concatenate([mat, cm * from_start], axis=1).astype(BF16)
                xm = jnp.where(lane_mask, x_pair, 0.0).astype(BF16)
                sm = jnp.where(lane_mask, st_pair, 0.0).astype(BF16)
                y_pair = y_pair + jnp.dot(lhs, jnp.concatenate([xm, sm], axis=0), preferred_element_type=F32)
                upd = upd + jnp.dot((bm_t * w_t[h:h + 1, :]).astype(BF16), xm, preferred_element_type=F32)
            decay = jnp.where(low, jnp.broadcast_to(e_last[:, 2 * m:2 * m + 1], (1, CHUNK)),
                              jnp.broadcast_to(e_last[:, 2 * m + 1:2 * m + 2], (1, CHUNK)))
            st_ref[m] = st_pair * decay + upd
            y_ref[:, m * CHUNK:(m + 1) * CHUNK] = y_pair

    gsz = SSM_INNER // SSM_GROUPS
    for g in range(SSM_GROUPS):
        cols = slice(g * gsz, (g + 1) * gsz)
        z = z_ref[:, cols].astype(F32)
        y = (y_ref[:, cols] + dsk_ref[:, cols] * xc_ref[:, cols]) * (z * _sigmoid(z))
        y = y * lax.rsqrt(jnp.mean(y * y, axis=-1, keepdims=True) + NORM_EPS)
        o_ref[:, cols] = (y * nw_ref[:, cols]).astype(BF16)


SSD_STATE = (HEAD_PAIRS, SSM_STATE, CHUNK)
CONV_TAIL = (8, CONV_CH)
N_SSD_IN = 13


def _ssd_io(proj, dt_raw, conv_w, conv_b, dt_bias, a_log, d_skip, norm_w, nc, init_state, init_tail):
    row = lambda b, c: b * nc + c
    const2 = lambda b, c: (0, 0)
    const3 = lambda b, c: (0, 0, 0)
    pad = lambda v: jnp.pad(v.astype(F32), (0, DT_PAD - SSM_HEADS)).reshape(1, DT_PAD)
    taps = CONV_WIDTH - 1
    shift = np.zeros((taps * CHUNK, CHUNK), np.float32)
    for k in range(taps):
        for r in range(taps - k, CHUNK):
            shift[k * CHUNK + r, r - (taps - k)] = 1.0
    in_specs = [
        pl.BlockSpec((CHUNK, SSM_INNER), lambda b, c: (row(b, c), _col_block(COL_Z, SSM_INNER))),
        pl.BlockSpec((CHUNK, SSM_INNER), lambda b, c: (row(b, c), _col_block(COL_X, SSM_INNER))),
        pl.BlockSpec((CHUNK, BC_WIDTH), lambda b, c: (row(b, c), _col_block(COL_BC, BC_WIDTH))),
        pl.BlockSpec((CHUNK, DT_PAD), lambda b, c: (row(b, c), 0)),
        pl.BlockSpec((taps * CHUNK, CHUNK), const2),
        pl.BlockSpec((CONV_WIDTH, CONV_CH), const2),
        pl.BlockSpec((1, CONV_CH), const2),
        pl.BlockSpec((1, DT_PAD), const2),
        pl.BlockSpec((1, DT_PAD), const2),
        pl.BlockSpec((1, SSM_INNER), const2),
        pl.BlockSpec((1, SSM_INNER), const2),
        pl.BlockSpec(SSD_STATE, const3),
        pl.BlockSpec(CONV_TAIL, const2),
    ]
    assert len(in_specs) == N_SSD_IN
    args = (proj, proj, proj, dt_raw, jnp.asarray(shift, BF16), conv_w.astype(F32), conv_b.astype(F32).reshape(1, CONV_CH),
            pad(dt_bias), pad(a_log), jnp.repeat(d_skip.astype(F32), SSM_HEAD_DIM).reshape(1, SSM_INNER),
            norm_w.astype(F32).reshape(1, SSM_INNER), init_state, init_tail)
    return in_specs, args


def _mixers_body(chunk_dec, pad_rows, emit_state, *refs):
    ret_in, refs = refs[:N_RET_IN], refs[N_RET_IN:]
    ssd_in, refs = refs[:N_SSD_IN], refs[N_SSD_IN:]
    n_out = 5 if emit_state else 2
    outs, scratch = refs[:n_out], refs[n_out:]
    states = scratch[:3]
    inits = (ret_in[-1], ssd_in[-2], ssd_in[-1])
    c = pl.program_id(1)

    @pl.when(c == 0)
    def _():
        for state, init in zip(states, inits):
            state[...] = init[...]

    _ret_body(chunk_dec, *ret_in[:-1], outs[0], scratch[0])
    _ssd_body(pad_rows, *ssd_in[:-2], outs[1], *scratch[1:])

    if emit_state:
        @pl.when(c == pl.num_programs(1) - 1)
        def _():
            for final, state in zip(outs[2:], states):
                final[...] = state[...]


def _mixers(proj, dt_raw, ssd_params, bsz, nc, first_pos, pad_rows, init, emit_state):
    t = proj.shape[0]
    ret_specs, ret_args, chunk_dec = _retention_io(proj, nc, first_pos, init[0])
    ssd_specs, ssd_args = _ssd_io(proj, dt_raw, *ssd_params, nc, init[1], init[2])
    row = lambda b, c: b * nc + c
    out_specs = [pl.BlockSpec((CHUNK, RET_V), lambda b, c: (row(b, c), 0)),
                 pl.BlockSpec((CHUNK, SSM_INNER), lambda b, c: (row(b, c), 0))]
    out_shape = [jax.ShapeDtypeStruct((t, RET_V), BF16), jax.ShapeDtypeStruct((t, SSM_INNER), BF16)]
    if emit_state:
        out_specs += [pl.BlockSpec(RET_STATE, lambda b, c: (0, 0, 0)), pl.BlockSpec(SSD_STATE, lambda b, c: (0, 0, 0)),
                      pl.BlockSpec(CONV_TAIL, lambda b, c: (0, 0))]
        out_shape += [jax.ShapeDtypeStruct(s, F32) for s in (RET_STATE, SSD_STATE, CONV_TAIL)]
    return pl.pallas_call(
        functools.partial(_mixers_body, chunk_dec, pad_rows, emit_state),
        grid=(bsz, nc),
        in_specs=ret_specs + ssd_specs,
        out_specs=out_specs,
        out_shape=out_shape,
        scratch_shapes=[
            pltpu.VMEM(RET_STATE, F32),
            pltpu.VMEM(SSD_STATE, F32),
            pltpu.VMEM(CONV_TAIL, F32),
            pltpu.VMEM((CHUNK, CONV_CH), F32),
            pltpu.VMEM((CHUNK, SSM_INNER), F32),
        ],
        compiler_params=_params(("parallel", "arbitrary")),
        name="mixers_meta" if emit_state else "mixers",
    )(*ret_args, *ssd_args)


def _merge_body(oret_ref, ossd_ref, g0_ref, g1_ref, h_ref, wb0_ref, wb1_ref, wo_ref, nw_ref, rwh_ref, rwl_ref,
                rb_ref, h1_ref, un_ref, e_ref, w_ref):
    y_ret = jnp.dot(oret_ref[...], wb0_ref[...], preferred_element_type=F32)
    y_ssd = jnp.dot(ossd_ref[...], wb1_ref[...], preferred_element_type=F32)
    merged = (_sigmoid(g0_ref[...].astype(F32)) * y_ret + _sigmoid(g1_ref[...].astype(F32)) * y_ssd)
    h1 = h_ref[...] + jnp.dot(merged.astype(BF16), wo_ref[...], preferred_element_type=F32)
    h1_ref[...] = h1
    un = h1 * lax.rsqrt(jnp.mean(h1 * h1, axis=-1, keepdims=True) + NORM_EPS) * nw_ref[...]
    un_hi = un.astype(BF16)
    un_ref[...] = un_hi

    un_lo = (un - un_hi.astype(F32)).astype(BF16)
    logits = (jnp.dot(un_hi, rwh_ref[...], preferred_element_type=F32)
              + jnp.dot(un_lo, rwh_ref[...], preferred_element_type=F32)
              + jnp.dot(un_hi, rwl_ref[...], preferred_element_type=F32)) + rb_ref[...]
    logits = jnp.transpose(logits)[:N_EXPERTS, :]
    tm = logits.shape[1]
    eidx = lax.broadcasted_iota(I32, (N_EXPERTS, tm), 0)
    vals, ids = [], []
    for _ in range(TOP_K):
        best = jnp.max(logits, axis=0, keepdims=True)
        arg = jnp.min(jnp.where(logits == best, eidx, N_EXPERTS), axis=0, keepdims=True)
        vals.append(best)
        ids.append(arg)
        logits = jnp.where(eidx == arg, -jnp.inf, logits)
    ex = [jnp.exp(v - vals[0]) for v in vals]
    denom = ex[0] + ex[1] + ex[2] + ex[3]
    e_ref[...] = jnp.concatenate(ids + [jnp.zeros((8 - TOP_K, tm), I32)], axis=0)
    w_ref[...] = jnp.concatenate([x / denom for x in ex] + [jnp.zeros((8 - TOP_K, tm), F32)], axis=0)


def _merge(o_ret, o_ssd, proj, h, wb0, wb1, wo, norm_w, router_w, router_b):
    t = h.shape[0]
    tm = _pick(t, (512, 256, 128))
    gate_blk = _col_block(OFF_DT, D_MODEL)
    const2 = lambda i: (0, 0)
    rw_hi = router_w.astype(BF16)
    rw_lo = (router_w - rw_hi.astype(F32)).astype(BF16)
    return pl.pallas_call(
        _merge_body,
        grid=(t // tm,),
        in_specs=[
            pl.BlockSpec((tm, RET_V), lambda i: (i, 0)),
            pl.BlockSpec((tm, SSM_INNER), lambda i: (i, 0)),
            pl.BlockSpec((tm, D_MODEL), lambda i: (i, gate_blk)),
            pl.BlockSpec((tm, D_MODEL), lambda i: (i, gate_blk + 1)),
            pl.BlockSpec((tm, D_MODEL), lambda i: (i, 0)),
            pl.BlockSpec((RET_V, D_MODEL), const2),
            pl.BlockSpec((SSM_INNER, D_MODEL), const2),
            pl.BlockSpec((D_MODEL, D_MODEL), const2),
            pl.BlockSpec((1, D_MODEL), const2),
            pl.BlockSpec((D_MODEL, CHUNK), const2),
            pl.BlockSpec((D_MODEL, CHUNK), const2),
            pl.BlockSpec((1, CHUNK), const2),
        ],
        out_specs=[
            pl.BlockSpec((tm, D_MODEL), lambda i: (i, 0)),
            pl.BlockSpec((tm, D_MODEL), lambda i: (i, 0)),
            pl.BlockSpec((8, tm), lambda i: (0, i)),
            pl.BlockSpec((8, tm), lambda i: (0, i)),
        ],
        out_shape=[
            jax.ShapeDtypeStruct((t, D_MODEL), F32),
            jax.ShapeDtypeStruct((t, D_MODEL), BF16),
            jax.ShapeDtypeStruct((8, t), I32),
            jax.ShapeDtypeStruct((8, t), F32),
        ],
        compiler_params=_params(("parallel",)),
        name="merge_router",
    )(o_ret, o_ssd, proj, proj, h, wb0, wb1, wo, norm_w, rw_hi, rw_lo, router_b)


ROW_ALIGN = 8
PIECE = 16
LOCAL_ROWS = 1024
META_PIECES, META_EXPERT, META_LOCAL, META_SORTED = 0, 1, 2, 3
MAX_PIECES = CHUNK * TOP_K // PIECE + N_EXPERTS
assert MAX_PIECES <= CHUNK


def _rank_body(tiles, e_ref, slot_ref, meta_ref, cnt_ref, run_ref):
    @pl.when(pl.program_id(0) == 0)
    def _():
        run_ref[...] = jnp.zeros_like(run_ref)

    eidx = lax.broadcasted_iota(I32, (N_EXPERTS, CHUNK), 0)
    lane = lax.broadcasted_iota(I32, (N_EXPERTS, CHUNK), 1)
    ri = lax.broadcasted_iota(I32, (CHUNK, CHUNK), 0)
    ci = lax.broadcasted_iota(I32, (CHUNK, CHUNK), 1)
    upper = jnp.where(ri <= ci, 1.0, 0.0).astype(BF16)
    ones = jnp.ones((CHUNK, CHUNK), BF16)
    ei = lax.broadcasted_iota(I32, (N_EXPERTS, N_EXPERTS), 0)
    ej = lax.broadcasted_iota(I32, (N_EXPERTS, N_EXPERTS), 1)
    lower = jnp.where(ej < ei, 1.0, 0.0).astype(BF16)
    round_up = lambda v, m: jnp.floor((v + (m - 1)) * (1.0 / m)) * m

    run = run_ref[...]
    for s in range(tiles):
        e_tile = e_ref[:, s * CHUNK:(s + 1) * CHUNK]
        seen = jnp.zeros((N_EXPERTS, CHUNK), F32)
        hits, ranks = [], []
        for k in range(TOP_K):
            hit = eidx == e_tile[k:k + 1, :]
            oh = jnp.where(hit, 1.0, 0.0).astype(BF16)
            ranks.append(seen + jnp.dot(oh, upper, preferred_element_type=F32) - 1.0)
            seen = seen + jnp.dot(oh, ones, preferred_element_type=F32)
            hits.append(hit)
        count = seen
        lstart = jnp.dot(lower, round_up(count, PIECE).astype(BF16), preferred_element_type=F32)
        slots = []
        for k in range(TOP_K):
            slots.append(jnp.sum(jnp.where(hits[k], lstart + ranks[k], 0.0), axis=0, keepdims=True))
        slot_ref[:, s * CHUNK:(s + 1) * CHUNK] = jnp.concatenate(
            slots + [jnp.full((8 - TOP_K, CHUNK), -1.0, F32)], axis=0).astype(I32)
        pieces = round_up(count, PIECE) * (1.0 / PIECE)
        first = jnp.dot(lower, pieces.astype(BF16), preferred_element_type=F32)
        nth = lane.astype(F32) - first
        mine = jnp.logical_and(nth >= 0.0, nth < pieces)
        pick = lambda v: jnp.sum(jnp.where(mine, v, 0.0), axis=0, keepdims=True)
        meta_ref[s] = jnp.concatenate(
            [jnp.sum(pieces, axis=0, keepdims=True), pick(eidx.astype(F32)), pick(lstart + PIECE * nth),
             pick(run + PIECE * nth), jnp.zeros((4, CHUNK), F32)], axis=0).astype(I32)
        run = run + round_up(count, ROW_ALIGN)
    run_ref[...] = run
    cnt_ref[...] = run.astype(I32)


def _rank(top_e):
    t = top_e.shape[1]
    n_tiles = t // CHUNK
    tiles = _pick(n_tiles, (8, 4, 2, 1))
    return pl.pallas_call(
        functools.partial(_rank_body, tiles),
        grid=(n_tiles // tiles,),
        in_specs=[pl.BlockSpec((8, tiles * CHUNK), lambda i: (0, i))],
        out_specs=[
            pl.BlockSpec((8, tiles * CHUNK), lambda i: (0, i)),
            pl.BlockSpec((tiles, 8, CHUNK), lambda i: (i, 0, 0)),
            pl.BlockSpec((N_EXPERTS, CHUNK), lambda i: (0, 0)),
        ],
        out_shape=[
            jax.ShapeDtypeStruct((8, t), I32),
            jax.ShapeDtypeStruct((n_tiles, 8, CHUNK), I32),
            jax.ShapeDtypeStruct((N_EXPERTS, CHUNK), I32),
        ],
        scratch_shapes=[pltpu.VMEM((N_EXPERTS, CHUNK), F32)],
        compiler_params=_params(("arbitrary",)),
        name="rank",
    )(top_e)


def _pack_halves(v):
    bits = pltpu.bitcast(v, jnp.uint32)
    return (bits[:, :D_MODEL // 2] >> 16) | bits[:, D_MODEL // 2:]


def _unpack_halves(p):
    lo = pltpu.bitcast(p << 16, F32)
    hi = pltpu.bitcast(p & jnp.uint32(0xFFFF0000), F32)
    return jnp.concatenate([lo, hi], axis=1).astype(BF16)


def _wait_pieces(meta_ref, make_copy):
    def body(j, carry):
        make_copy(0, 0).wait()
        return carry

    lax.fori_loop(0, meta_ref[0, META_PIECES, 0], body, 0)


def _start_pieces(pstart_ref, meta_ref, make_copy):
    n_pieces = meta_ref[0, META_PIECES, 0]

    def start(p, priority):
        sorted_row = pstart_ref[meta_ref[0, META_EXPERT, p]] + meta_ref[0, META_SORTED, p]
        make_copy(pl.multiple_of(meta_ref[0, META_LOCAL, p], ROW_ALIGN),
                  pl.multiple_of(sorted_row, ROW_ALIGN)).start(priority=priority)

    def body(j, carry):
        start(2 * j, 0)

        @pl.when(2 * j + 1 < n_pieces)
        def _():
            start(2 * j + 1, 1)

        return carry

    lax.fori_loop(0, lax.shift_right_logical(n_pieces + 1, 1), body, 0)


def _dispatch_body(pstart_ref, fill_ref, meta_ref, prev_meta_ref, slot_ref, un_ref, xs_ref, loc_ref, sems):
    step = pl.program_id(0)
    last = pl.num_programs(0) - 1
    buf = step % 2

    srow = lax.broadcasted_iota(I32, (LOCAL_ROWS, CHUNK), 0)
    onehot = jnp.zeros((LOCAL_ROWS, CHUNK), F32)
    for k in range(TOP_K):
        onehot = jnp.where(srow == slot_ref[k:k + 1, :], 1.0, onehot)
    loc_ref[buf] = _pack_halves(jnp.dot(onehot.astype(BF16), un_ref[...], preferred_element_type=F32))

    def copies_of(b):
        def make_copy(local_row, sorted_row):
            return pltpu.make_async_copy(loc_ref.at[b, pl.ds(local_row, PIECE), :],
                                         xs_ref.at[pl.ds(sorted_row, PIECE), :], sems.at[b])
        return make_copy

    @pl.when(step > 0)
    def _():
        _wait_pieces(prev_meta_ref, copies_of(1 - buf))

    _start_pieces(pstart_ref, meta_ref, copies_of(buf))

    @pl.when(step == last)
    def _():
        _wait_pieces(meta_ref, copies_of(buf))
        loc_ref[0] = jnp.zeros((LOCAL_ROWS, D_MODEL // 2), jnp.uint32)

        def zero_rows(first, n_rows):
            return pltpu.make_async_copy(loc_ref.at[0, pl.ds(0, n_rows), :],
                                         xs_ref.at[pl.ds(pl.multiple_of(first, ROW_ALIGN), n_rows), :], sems.at[0])

        def fill(act):
            def per_expert(e, carry):
                def per_piece(j, c):
                    act(zero_rows(fill_ref[0, e] + j * ROW_ALIGN, ROW_ALIGN))
                    return c
                return lax.fori_loop(0, fill_ref[1, e], per_piece, carry)

            def per_block(j, c):
                act(zero_rows(fill_ref[2, 0] + j * LOCAL_ROWS, LOCAL_ROWS))
                return c

            lax.fori_loop(0, N_EXPERTS, per_expert, 0)
            lax.fori_loop(0, fill_ref[2, 1], per_block, 0)

        fill(lambda cp: cp.start())
        fill(lambda cp: cp.wait())


def _dispatch(pstart, fill, meta, slot, un, n_rows):
    t = un.shape[0]
    return pl.pallas_call(
        _dispatch_body,
        grid_spec=pltpu.PrefetchScalarGridSpec(
            num_scalar_prefetch=2,
            grid=(t // CHUNK,),
            in_specs=[
                pl.BlockSpec((1, 8, CHUNK), lambda i, ps, fl: (i, 0, 0), memory_space=pltpu.SMEM),
                pl.BlockSpec((1, 8, CHUNK), lambda i, ps, fl: (jnp.maximum(i - 1, 0), 0, 0), memory_space=pltpu.SMEM),
                pl.BlockSpec((8, CHUNK), lambda i, ps, fl: (0, i)),
                pl.BlockSpec((CHUNK, D_MODEL), lambda i, ps, fl: (i, 0)),
            ],
            out_specs=pl.BlockSpec(memory_space=pl.ANY),
            scratch_shapes=[pltpu.VMEM((2, LOCAL_ROWS, D_MODEL // 2), jnp.uint32), pltpu.SemaphoreType.DMA((2,))],
        ),
        out_shape=jax.ShapeDtypeStruct((n_rows, D_MODEL // 2), jnp.uint32),
        compiler_params=_params(("arbitrary",)),
        name="dispatch",
    )(pstart, fill, meta, meta, slot, un)


GU_SLAB = 256
GU_HALF = GU_SLAB // 2


def _regroup_body(w_ref, p_ref, o_ref):
    for s in range(2 * EXPERT_FF // GU_SLAB):
        cols = slice(s * GU_SLAB, (s + 1) * GU_SLAB)
        o_ref[0, :, cols] = jnp.dot(w_ref[0, :, cols].astype(BF16), p_ref[...], preferred_element_type=F32).astype(BF16)


def _regroup_gate_up(w_gate_up):
    perm = np.zeros((GU_SLAB, GU_SLAB), np.float32)
    perm[2 * np.arange(GU_HALF), np.arange(GU_HALF)] = 1.0
    perm[2 * np.arange(GU_HALF) + 1, GU_HALF + np.arange(GU_HALF)] = 1.0
    blk = (1, D_MODEL, 2 * EXPERT_FF)
    return pl.pallas_call(
        _regroup_body,
        grid=(N_EXPERTS,),
        in_specs=[pl.BlockSpec(blk, lambda e: (e, 0, 0)), pl.BlockSpec((GU_SLAB, GU_SLAB), lambda e: (0, 0))],
        out_specs=pl.BlockSpec(blk, lambda e: (e, 0, 0)),
        out_shape=jax.ShapeDtypeStruct((N_EXPERTS, D_MODEL, 2 * EXPERT_FF), BF16),
        compiler_params=_params(("parallel",)),
        name="regroup_gate_up",
    )(w_gate_up, jnp.asarray(perm, BF16))


def _regroup_bias(b_gate_up):
    b = b_gate_up.astype(F32).reshape(N_EXPERTS, 2 * EXPERT_FF // GU_SLAB, GU_HALF, 2)
    return jnp.transpose(b, (0, 1, 3, 2)).reshape(N_EXPERTS, 1, 2 * EXPERT_FF)


def _expert_body(be_ref, nu_ref, x_ref, wgu_ref, wd_ref, bgu_ref, bd_ref, y_ref):
    del be_ref

    @pl.when(pl.program_id(0) >= nu_ref[0])
    def _():
        y_ref[...] = jnp.zeros_like(y_ref)

    @pl.when(pl.program_id(0) < nu_ref[0])
    def _():
        x = _unpack_halves(x_ref[...])
        gu = jnp.dot(x, wgu_ref[0], preferred_element_type=F32) + bgu_ref[0]
        acts = []
        for s in range(2 * EXPERT_FF // GU_SLAB):
            gate = jnp.minimum(gu[:, s * GU_SLAB:s * GU_SLAB + GU_HALF], SWIGLU_LIMIT)
            up = jnp.clip(gu[:, s * GU_SLAB + GU_HALF:(s + 1) * GU_SLAB], -SWIGLU_LIMIT, SWIGLU_LIMIT)
            acts.append(((up + 1.0) * (gate * _sigmoid(SWIGLU_ALPHA * gate))).astype(BF16))
        act = jnp.concatenate(acts, axis=1)
        y = jnp.dot(act, wd_ref[0], preferred_element_type=F32) + bd_ref[0]
        y_ref[...] = _pack_halves(y.astype(BF16).astype(F32))


def _experts(block_e, n_used, xs, wgu, wd, bgu, bd):
    n_rows = xs.shape[0]
    n_blocks = n_rows // MOE_BLOCK
    wspec = lambda shape: pl.BlockSpec((1,) + shape, lambda i, be, nu: (be[i], 0, 0))
    return pl.pallas_call(
        _expert_body,
        grid_spec=pltpu.PrefetchScalarGridSpec(
            num_scalar_prefetch=2,
            grid=(n_blocks,),
            in_specs=[
                pl.BlockSpec((MOE_BLOCK, D_MODEL // 2), lambda i, be, nu: (jnp.minimum(i, nu[0] - 1), 0)),
                wspec((D_MODEL, 2 * EXPERT_FF)),
                wspec((EXPERT_FF, D_MODEL)),
                wspec((1, 2 * EXPERT_FF)),
                wspec((1, D_MODEL)),
            ],
            out_specs=pl.BlockSpec((MOE_BLOCK, D_MODEL // 2), lambda i, be, nu: (i, 0)),
        ),
        out_shape=jax.ShapeDtypeStruct((n_rows, D_MODEL // 2), jnp.uint32),
        compiler_params=_params(("arbitrary",)),
        name="experts",
    )(block_e, n_used, xs, wgu, wd, bgu, bd)


def _combine_body(pstart_ref, meta_ref, next_meta_ref, slot_ref, w_ref, h1_ref, nw_ref, y_hbm, o_ref, ybuf, sems):
    step = pl.program_id(0)
    buf = step % 2

    def copies_of(b):
        def make_copy(local_row, sorted_row):
            return pltpu.make_async_copy(y_hbm.at[pl.ds(sorted_row, PIECE), :],
                                         ybuf.at[b, pl.ds(local_row, PIECE), :], sems.at[b])
        return make_copy

    @pl.when(step == 0)
    def _():
        ybuf[...] = jnp.zeros_like(ybuf)
        _start_pieces(pstart_ref, meta_ref, copies_of(buf))

    @pl.when(step + 1 < pl.num_programs(0))
    def _():
        _start_pieces(pstart_ref, next_meta_ref, copies_of(1 - buf))

    _wait_pieces(meta_ref, copies_of(buf))

    srow = lax.broadcasted_iota(I32, (LOCAL_ROWS, CHUNK), 0)
    wmat = jnp.zeros((LOCAL_ROWS, CHUNK), F32)
    for k in range(TOP_K):
        wmat = jnp.where(srow == slot_ref[k:k + 1, :], w_ref[k:k + 1, :], wmat)
    wmat = jnp.transpose(wmat)
    w_hi = wmat.astype(BF16)
    w_lo = (wmat - w_hi.astype(F32)).astype(BF16)
    y = _unpack_halves(ybuf[buf])
    f = jnp.dot(w_hi, y, preferred_element_type=F32) + jnp.dot(w_lo, y, preferred_element_type=F32)
    h2 = h1_ref[...] + f
    o_ref[...] = h2 * lax.rsqrt(jnp.mean(h2 * h2, axis=-1, keepdims=True) + NORM_EPS) * nw_ref[...]


def _combine(pstart, meta, slot, top_w, h1, norm_w, y_sorted):
    t = h1.shape[0]
    n_tiles = t // CHUNK
    nxt = lambda i: jnp.minimum(i + 1, n_tiles - 1)
    return pl.pallas_call(
        _combine_body,
        grid_spec=pltpu.PrefetchScalarGridSpec(
            num_scalar_prefetch=1,
            grid=(n_tiles,),
            in_specs=[
                pl.BlockSpec((1, 8, CHUNK), lambda i, ps: (i, 0, 0), memory_space=pltpu.SMEM),
                pl.BlockSpec((1, 8, CHUNK), lambda i, ps: (nxt(i), 0, 0), memory_space=pltpu.SMEM),
                pl.BlockSpec((8, CHUNK), lambda i, ps: (0, i)),
                pl.BlockSpec((8, CHUNK), lambda i, ps: (0, i)),
                pl.BlockSpec((CHUNK, D_MODEL), lambda i, ps: (i, 0)),
                pl.BlockSpec((1, D_MODEL), lambda i, ps: (0, 0)),
                pl.BlockSpec(memory_space=pl.ANY),
            ],
            out_specs=pl.BlockSpec((CHUNK, D_MODEL), lambda i, ps: (i, 0)),
            scratch_shapes=[pltpu.VMEM((2, LOCAL_ROWS, D_MODEL // 2), jnp.uint32), pltpu.SemaphoreType.DMA((2,))],
        ),
        out_shape=jax.ShapeDtypeStruct((t, D_MODEL), F32),
        compiler_params=_params(("arbitrary",)),
        name="combine",
    )(pstart, meta, meta, slot, top_w, h1, norm_w, y_sorted)


def kernel(x, meta_tokens, norm_mix, w_in, conv_w, conv_b, dt_bias, a_log, d_skip, ssm_norm, w_branch, w_out, norm_ffn,
           router_w, router_b, w_gate_up, b_gate_up, w_down, b_down, norm_final):
    bsz, seq, _ = x.shape
    assert seq % CHUNK == 0 and norm_mix.shape[0] == 1
    nc = seq // CHUNK
    t = bsz * seq
    x2d = x.reshape(t, D_MODEL)
    meta_chunk = jnp.concatenate([jnp.zeros((META_PAD, D_MODEL), x.dtype), meta_tokens.astype(x.dtype)], axis=0)

    w_in0 = w_in[0]
    w_main = jnp.concatenate([w_in0[:, :OFF_DT], w_in0[:, OFF_DT + SSM_HEADS:]], axis=1).astype(BF16)
    w_dt = jnp.pad(w_in0[:, OFF_DT:OFF_DT + SSM_HEADS], ((0, 0), (0, DT_PAD - SSM_HEADS))).astype(BF16)
    wgu = _regroup_gate_up(w_gate_up[0])
    bgu = _regroup_bias(b_gate_up[0])
    wd = w_down[0].astype(BF16)
    bd = b_down[0][:, None, :].astype(F32)

    norm_mix_w = norm_mix[0].reshape(1, D_MODEL).astype(F32)
    ssd_params = (conv_w[0], conv_b[0], dt_bias[0], a_log[0], d_skip[0], ssm_norm[0])
    proj_m, dt_m = _in_proj(meta_chunk, norm_mix_w, w_main, w_dt)
    zero_states = tuple(jnp.zeros(s, F32) for s in (RET_STATE, SSD_STATE, CONV_TAIL))
    meta_states = _mixers(proj_m, dt_m, ssd_params, 1, 1, -META_PAD, META_PAD, zero_states, True)[2:]

    proj, dt_raw = _in_proj(x2d, norm_mix_w, w_main, w_dt)
    o_ret, o_ssd = _mixers(proj, dt_raw, ssd_params, bsz, nc, N_META, 0, meta_states, False)
    h1, un, top_e, top_w = _merge(
        o_ret, o_ssd, proj, x2d, w_branch[0, 0].astype(BF16), w_branch[0, 1].astype(BF16), w_out[0].astype(BF16),
        norm_ffn[0].reshape(1, D_MODEL).astype(F32),
        jnp.pad(router_w[0].astype(F32), ((0, 0), (0, CHUNK - N_EXPERTS))),
        jnp.pad(router_b[0].astype(F32), (0, CHUNK - N_EXPERTS)).reshape(1, CHUNK))

    slot, meta, counts = _rank(top_e)
    counts = counts[:, 0]
    slack = MOE_BLOCK - 1 + PIECE - ROW_ALIGN
    padded = (counts + slack) // MOE_BLOCK * MOE_BLOCK
    pend = jnp.cumsum(padded)
    pstart = (pend - padded).astype(I32)
    n_pairs = t * TOP_K
    max_rows = n_pairs + bsz * nc * N_EXPERTS * (ROW_ALIGN - 1)
    n_blocks = (max_rows + N_EXPERTS * slack) // MOE_BLOCK
    blk_row = jnp.arange(n_blocks, dtype=I32) * MOE_BLOCK
    block_e = jnp.minimum(jnp.sum(pend[None, :] <= blk_row[:, None], axis=1), N_EXPERTS - 1).astype(I32)
    n_used = (pend[-1:] // MOE_BLOCK).astype(I32)
    assert MOE_BLOCK % LOCAL_ROWS == 0
    tail = jnp.zeros((N_EXPERTS,), I32).at[0].set(pend[-1]).at[1].set((n_blocks * MOE_BLOCK - pend[-1]) // LOCAL_ROWS)
    fill = jnp.stack([pstart + counts, (padded - counts) // ROW_ALIGN, tail]).astype(I32)

    xs = _dispatch(pstart, fill, meta, slot, un, n_blocks * MOE_BLOCK)
    ys = _experts(block_e, n_used, xs, wgu, wd, bgu, bd)
    out = _combine(pstart, meta, slot, top_w, h1, norm_final.reshape(1, D_MODEL).astype(F32), ys)
    return out.reshape(bsz, seq, D_MODEL)
```
